```python
import jax, jax.numpy as jnp
from jax import lax
import numpy as np

D_MODEL = 1024
BATCH = 4
SEQ = 8192
DEPTH = 2
DEC_BATCH = 16
DEC_SEQ = 32
PAST_LEN = 2048

CHUNK = 64
N_A_LAYERS = DEPTH // 2
N_B_LAYERS = DEPTH - N_A_LAYERS
A_EXPAND = 128
A_HEADS = D_MODEL // A_EXPAND
A_DK = A_EXPAND
A_DV = D_MODEL // A_HEADS
B_HEAD_DIM = 64
B_Q_HEADS = D_MODEL // B_HEAD_DIM
B_KV_HEADS = 4
B_GROUP = B_Q_HEADS // B_KV_HEADS
WINDOW = 128
WIN_CHUNKS = WINDOW // CHUNK
D_FF = -(-8 * D_MODEL // (3 * 256)) * 256
ROPE_THETA = 10000.0
NORM_EPS = 1e-6

kernel_name = "hgrn2_yoco_swa_sink_stream_step"

F32 = jnp.float32


def rms_norm(x, w):
    xf = x.astype(F32)
    y = xf * lax.rsqrt(jnp.mean(xf * xf, axis=-1, keepdims=True) + NORM_EPS)
    return (y * w.astype(F32)).astype(x.dtype)


def rope(x, pos):
    half = x.shape[-1] // 2
    inv = ROPE_THETA ** (-jnp.arange(half, dtype=F32) / half)
    ang = pos.astype(F32)[:, None] * inv[None, :]
    cos = jnp.cos(ang)[:, None, :]
    sin = jnp.sin(ang)[:, None, :]
    xf = x.astype(F32)
    x1, x2 = xf[..., :half], xf[..., half:]
    return jnp.concatenate([x1 * cos - x2 * sin, x2 * cos + x1 * sin], axis=-1).astype(x.dtype)


def hgrn2_block(S, blk):
    q, k, v, lf = blk
    L = q.shape[1]
    b = jnp.cumsum(lf, axis=1)
    o_inter = jnp.einsum('blhk,bhkv->blhv', q * jnp.exp(b), S)
    causal = jnp.tril(jnp.ones((L, L), dtype=bool))
    e = b[:, :, None] - b[:, None, :]
    decay = jnp.exp(jnp.where(causal[None, :, :, None, None], e, -jnp.inf))
    scores = jnp.einsum('btshk,bshk->bhts', decay * q[:, :, None], k)
    o_intra = jnp.einsum('bhts,bshv->bthv', scores, v)
    b_last = b[:, -1]
    S_new = jnp.exp(b_last)[..., None] * S + jnp.einsum(
        'bshk,bshv->bhkv', k * jnp.exp(b_last[:, None] - b), v)
    return S_new, o_inter + o_intra


def hgrn2_scan(q, k, v, lf, S0, block_len):
    B, T = q.shape[:2]
    n = T // block_len

    def split(a):
        return a.reshape(B, n, block_len, *a.shape[2:]).swapaxes(0, 1)

    S, o = lax.scan(hgrn2_block, S0, (split(q), split(k), split(v), split(lf)))
    o = o.swapaxes(0, 1).reshape(B, T, A_HEADS, A_DV)
    return o, S


def hgrn2_mixer(h, w_in, lb, g_norm, w_out, S0, block_len):
    B, T, _ = h.shape
    q, f, i, g = jnp.split(h @ w_in, 4, axis=-1)
    lbf = lb.astype(F32)
    forget = lbf + (1.0 - lbf) * jax.nn.sigmoid(f.astype(F32))
    lf = jnp.log(forget)
    inp = 1.0 - forget
    shp = (B, T, A_HEADS, A_DK)
    o, S = hgrn2_scan(jax.nn.silu(q.astype(F32)).reshape(shp), inp.reshape(shp),
                      i.astype(F32).reshape(B, T, A_HEADS, A_DV), lf.reshape(shp),
                      S0.astype(F32), block_len)
    o = rms_norm(o, g_norm) * jax.nn.silu(g.astype(F32)).reshape(B, T, A_HEADS, A_DV)
    return o.reshape(B, T, D_MODEL).astype(h.dtype) @ w_out, S


def sink_attention(q, k, v, sinks, valid):
    s = jnp.einsum('bnqkgd,bnskd->bnkgqs', q.astype(F32), k.astype(F32)) * (B_HEAD_DIM ** -0.5)
    s = jnp.where(valid[None, :, None, None, None, :], s, -jnp.inf)
    sink = sinks.astype(F32).reshape(1, 1, B_KV_HEADS, B_GROUP, 1, 1)
    m = jnp.maximum(jnp.max(s, axis=-1, keepdims=True), sink)
    p = jnp.exp(s - m)
    denom = jnp.sum(p, axis=-1, keepdims=True) + jnp.exp(sink - m)
    return jnp.einsum('bnkgqs,bnskd->bnqkgd', p / denom, v.astype(F32))


def window_attention_prompt(q, k, v, sinks):
    B, T = q.shape[:2]
    n = T // CHUNK
    qb = q.reshape(B, n, CHUNK, B_KV_HEADS, B_GROUP, B_HEAD_DIM)

    def band(a):
        ap = jnp.pad(a, ((0, 0), (WIN_CHUNKS * CHUNK, 0), (0, 0), (0, 0)))
        ap = ap.reshape(B, n + WIN_CHUNKS, CHUNK, B_KV_HEADS, B_HEAD_DIM)
        return jnp.concatenate([ap[:, j:j + n] for j in range(WIN_CHUNKS + 1)], axis=2)

    key_pos = (jnp.arange(n)[:, None] - WIN_CHUNKS) * CHUNK + jnp.arange((WIN_CHUNKS + 1) * CHUNK)[None, :]
    o = sink_attention(qb, band(k), band(v), sinks, key_pos >= 0)
    return o.reshape(B, T, B_Q_HEADS * B_HEAD_DIM)


def window_attention_sample(q, k_all, v_all, sinks):
    B, T = q.shape[:2]
    Lk = k_all.shape[1]
    o = sink_attention(q.reshape(B, 1, T, B_KV_HEADS, B_GROUP, B_HEAD_DIM),
                       k_all[:, None], v_all[:, None], sinks, jnp.ones((1, Lk), dtype=bool))
    return o.reshape(B, T, B_Q_HEADS * B_HEAD_DIM)


def run_trunk(x, pos, hgrn_state, cache_k, cache_v, block_len,
              norm_mix_pre, norm_mix_post, norm_ffn_pre, norm_ffn_post, w_ffn_in, w_ffn_out,
              w_a_in, a_lower_bound, a_out_norm, w_a_out, kv_norm, w_kv, w_b_q, b_sinks, w_b_out):
    B, T, _ = x.shape
    lower = jnp.cumsum(jax.nn.softmax(a_lower_bound.astype(F32), axis=0), axis=0)
    new_states = []
    k_attn = v_attn = None
    for layer in range(DEPTH):
        h = rms_norm(x, norm_mix_pre[layer])
        if layer < N_A_LAYERS:
            mix, S = hgrn2_mixer(h, w_a_in[layer], lower[layer], a_out_norm[layer], w_a_out[layer],
                                 hgrn_state[layer], block_len)
            new_states.append(S.astype(x.dtype))
        else:
            if layer == N_A_LAYERS:
                k, v = jnp.split(rms_norm(x, kv_norm) @ w_kv, 2, axis=-1)
                k = rope(k.reshape(B, T, B_KV_HEADS, B_HEAD_DIM), pos)
                v = v.reshape(B, T, B_KV_HEADS, B_HEAD_DIM)
                if cache_k is None:
                    k_attn, v_attn = k, v
                else:
                    k_attn = jnp.concatenate([cache_k.astype(k.dtype), k], axis=1)
                    v_attn = jnp.concatenate([cache_v.astype(v.dtype), v], axis=1)
            j = layer - N_A_LAYERS
            q = rope((h @ w_b_q[j]).reshape(B, T, B_Q_HEADS, B_HEAD_DIM), pos)
            if cache_k is None:
                o = window_attention_prompt(q, k_attn, v_attn, b_sinks[j])
            else:
                o = window_attention_sample(q, k_attn, v_attn, b_sinks[j])
            mix = o.astype(x.dtype) @ w_b_out[j]
        x = x + rms_norm(mix, norm_mix_post[layer])
        hf = rms_norm(x, norm_ffn_pre[layer])
        a, b = jnp.split(hf @ w_ffn_in[layer], 2, axis=-1)
        x = x + rms_norm((jax.nn.silu(a) * b) @ w_ffn_out[layer], norm_ffn_post[layer])
    return x, jnp.stack(new_states), k_attn[:, -WINDOW:], v_attn[:, -WINDOW:]


def setup_inputs(seed: int = 0) -> dict:
    key = jax.random.key(seed)
    ks = jax.random.split(key, 24)
    nrm = jax.random.normal
    D = D_MODEL
    QW = B_Q_HEADS * B_HEAD_DIM
    KVW = B_KV_HEADS * B_HEAD_DIM
    return {
        "x_prompt": nrm(ks[0], (BATCH, SEQ, D), F32),
        "x_sample": nrm(ks[1], (DEC_BATCH, DEC_SEQ, D), F32),
        "state_hgrn": 0.5 * nrm(ks[2], (N_A_LAYERS, DEC_BATCH, A_HEADS, A_DK, A_DV), F32),
        "cache_k": nrm(ks[3], (DEC_BATCH, WINDOW, B_KV_HEADS, B_HEAD_DIM), F32),
        "cache_v": nrm(ks[4], (DEC_BATCH, WINDOW, B_KV_HEADS, B_HEAD_DIM), F32),
        "norm_mix_pre": 1.0 + 0.05 * nrm(ks[5], (DEPTH, D), F32),
        "norm_mix_post": 1.0 + 0.05 * nrm(ks[6], (DEPTH, D), F32),
        "norm_ffn_pre": 1.0 + 0.05 * nrm(ks[7], (DEPTH, D), F32),
        "norm_ffn_post": 1.0 + 0.05 * nrm(ks[8], (DEPTH, D), F32),
        "w_ffn_in": nrm(ks[9], (DEPTH, D, 2 * D_FF), F32) * D ** -0.5,
        "w_ffn_out": nrm(ks[10], (DEPTH, D_FF, D), F32) * D_FF ** -0.5,
        "w_a_in": nrm(ks[11], (N_A_LAYERS, D, 4 * D), F32) * D ** -0.5,
        "a_lower_bound": 0.1 * nrm(ks[12], (DEPTH, D), F32),
        "a_out_norm": 1.0 + 0.05 * nrm(ks[13], (N_A_LAYERS, A_HEADS, A_DV), F32),
        "w_a_out": nrm(ks[14], (N_A_LAYERS, D, D), F32) * D ** -0.5,
        "kv_norm": 1.0 + 0.05 * nrm(ks[15], (D,), F32),
        "w_kv": nrm(ks[16], (D, 2 * KVW), F32) * D ** -0.5,
        "w_b_q": nrm(ks[17], (N_B_LAYERS, D, QW), F32) * D ** -0.5,
        "b_sinks": 0.5 * nrm(ks[18], (N_B_LAYERS, B_Q_HEADS), F32),
        "w_b_out": nrm(ks[19], (N_B_LAYERS, QW, D), F32) * QW ** -0.5,
    }


def reference(x_prompt, x_sample, state_hgrn, cache_k, cache_v,
              norm_mix_pre, norm_mix_post, norm_ffn_pre, norm_ffn_post, w_ffn_in, w_ffn_out,
              w_a_in, a_lower_bound, a_out_norm, w_a_out, kv_norm, w_kv, w_b_q, b_sinks, w_b_out):
    weights = (norm_mix_pre, norm_mix_post, norm_ffn_pre, norm_ffn_post, w_ffn_in, w_ffn_out,
               w_a_in, a_lower_bound, a_out_norm, w_a_out, kv_norm, w_kv, w_b_q, b_sinks, w_b_out)
    Bp, Tp, _ = x_prompt.shape
    Ts = x_sample.shape[1]
    pos_prompt = jnp.arange(Tp)
    pos_sample = PAST_LEN + jnp.arange(Ts)
    zero_state = jnp.zeros((N_A_LAYERS, Bp, A_HEADS, A_DK, A_DV), x_prompt.dtype)
    y_prompt, st_p, k_p, v_p = run_trunk(x_prompt, pos_prompt, zero_state, None, None, CHUNK, *weights)
    y_sample, st_s, k_s, v_s = run_trunk(x_sample, pos_sample, state_hgrn, cache_k, cache_v, Ts, *weights)
    return (y_prompt, y_sample, st_p, st_s, k_p, v_p, k_s, v_s)
```

```python
import functools

import jax
import jax.numpy as jnp
from jax import lax
from jax.experimental import pallas as pl
from jax.experimental.pallas import tpu as pltpu

F32 = jnp.float32
BF16 = jnp.bfloat16

D_MODEL = 1024
A_HEADS = 8
A_DK = 128
A_DV = 128
B_HEAD_DIM = 64
B_Q_HEADS = 16
B_KV_HEADS = 4
B_GROUP = B_Q_HEADS // B_KV_HEADS
KV_WIDTH = B_KV_HEADS * B_HEAD_DIM
WINDOW = 128
CHUNK = 64
PAST_LEN = 2048
D_FF = 2816
ROPE_THETA = 10000.0
NORM_EPS = 1e-6

LANES = 128
SUBLANES = 8
VMEM_LIMIT_BYTES = 52 * 1024 * 1024

ROW_TILE = 512
HGRN_CHUNK = 128
FFN_CHUNK = 256


def _rms_scale(x):
    ms = jnp.mean(x * x, axis=-1, keepdims=True)
    return x * lax.rsqrt(ms + NORM_EPS)


def _dot(a, b):
    return jnp.dot(a, b, preferred_element_type=F32)


def _dot_nt(a, b):
    return lax.dot_general(a, b, (((1,), (1,)), ((), ())), preferred_element_type=F32)


def _dot_tn(a, b):
    return lax.dot_general(a, b, (((0,), (0,)), ((), ())), preferred_element_type=F32)


def _const_spec(shape):
    nd = len(shape)
    return pl.BlockSpec(shape, lambda *_: (0,) * nd, pipeline_mode=pl.Buffered(1))


def _ffn_kernel(x_ref, npre_ref, npost_ref, win_ref, wout_ref, o_ref):
    x = x_ref[...]
    h = (_rms_scale(x) * npre_ref[...]).astype(BF16)
    acc = jnp.zeros(x.shape, F32)
    for c0 in range(0, D_FF, FFN_CHUNK):
        a = _dot(h, win_ref[:, c0:c0 + FFN_CHUNK])
        b = _dot(h, win_ref[:, D_FF + c0:D_FF + c0 + FFN_CHUNK])
        g = (a * jax.nn.sigmoid(a) * b).astype(BF16)
        acc = acc + _dot(g, wout_ref[c0:c0 + FFN_CHUNK, :])
    o_ref[...] = x + _rms_scale(acc) * npost_ref[...]


def _ffn(x2d, npre, npost, win, wout):
    rows = x2d.shape[0]
    tile = min(ROW_TILE, rows)
    assert rows % tile == 0 and D_FF % FFN_CHUNK == 0
    return pl.pallas_call(
        _ffn_kernel,
        grid=(rows // tile,),
        in_specs=[
            pl.BlockSpec((tile, D_MODEL), lambda i: (i, 0)),
            _const_spec((1, D_MODEL)),
            _const_spec((1, D_MODEL)),
            _const_spec((D_MODEL, 2 * D_FF)),
            _const_spec((D_FF, D_MODEL)),
        ],
        out_specs=pl.BlockSpec((tile, D_MODEL), lambda i: (i, 0)),
        out_shape=jax.ShapeDtypeStruct((rows, D_MODEL), F32),
        compiler_params=pltpu.CompilerParams(
            dimension_semantics=("parallel",), vmem_limit_bytes=VMEM_LIMIT_BYTES),
        name="ffn",
    )(x2d, npre, npost, win, wout)


def _cumsum_rows(x):
    c, w = x.shape
    groups = c // SUBLANES
    y = x.reshape(groups, SUBLANES, w)
    sub = lax.broadcasted_iota(jnp.int32, y.shape, 1)
    shift = 1
    while shift < SUBLANES:
        y = y + jnp.where(sub >= shift, pltpu.roll(y, shift, axis=1), 0.0)
        shift *= 2
    tot = jnp.broadcast_to(y[:, SUBLANES - 1:SUBLANES, :], y.shape)
    inc = tot
    shift = 1
    while shift < groups:
        inc = inc + jnp.concatenate(
            [jnp.zeros((shift, SUBLANES, w), F32), inc[:groups - shift]], axis=0)
        shift *= 2
    return (y + (inc - tot)).reshape(c, w)


def _hgrn_kernel(x_ref, st_in_ref, npre_ref, win_ref, alb_ref, gnorm_ref, wout_ref, npost_ref,
                 y_ref, st_ref, qs_ref, lf_ref, inp_ref, v_ref, sg_ref, on_ref,
                 *, nb, tt, chunk):
    t = pl.program_id(1)
    rows = nb * tt
    nchunks = tt // chunk

    @pl.when(t == 0)
    def _():
        for n in range(nb):
            for h in range(A_HEADS):
                st_ref[n, h] = st_in_ref[n, h].T

    x = x_ref[...].reshape(rows, D_MODEL)
    hn = (_rms_scale(x) * npre_ref[...]).astype(BF16)

    alb = alb_ref[...]
    e = jnp.exp(alb - jnp.max(alb, axis=0, keepdims=True))
    lb = e[0:1] / jnp.sum(e, axis=0, keepdims=True)

    q = _dot(hn, win_ref[:, 0:D_MODEL])
    qs_ref[...] = q * jax.nn.sigmoid(q)
    f = _dot(hn, win_ref[:, D_MODEL:2 * D_MODEL])
    forget = lb + (1.0 - lb) * jax.nn.sigmoid(f)
    lf_ref[...] = jnp.log(forget)
    inp_ref[...] = 1.0 - forget
    v_ref[...] = _dot(hn, win_ref[:, 2 * D_MODEL:3 * D_MODEL]).astype(BF16)
    g = _dot(hn, win_ref[:, 3 * D_MODEL:4 * D_MODEL])
    sg_ref[...] = g * jax.nn.sigmoid(g)

    half = chunk // 2
    ri = lax.broadcasted_iota(jnp.int32, (chunk, chunk), 0)
    ci = lax.broadcasted_iota(jnp.int32, (chunk, chunk), 1)
    causal = ri >= ci

    def chunk_step(idx, carry):
        n = idx // nchunks
        r0 = pl.multiple_of(idx * chunk, chunk)
        rs = pl.ds(r0, chunk)
        for h in range(A_HEADS):
            sl = slice(h * A_DK, (h + 1) * A_DK)
            b = _cumsum_rows(lf_ref[rs, sl])
            bmid = b[half - 1:half]
            blast = b[chunk - 1:chunk]
            eq = jnp.exp(b - bmid)
            ek = jnp.exp(bmid - b)
            qsc = qs_ref[rs, sl] * eq
            inp = inp_ref[rs, sl] * ek
            qt = qsc.astype(BF16)
            kt = inp.astype(BF16)
            qi = (qsc * jnp.exp(bmid)).astype(BF16)
            kh = (inp * jnp.exp(blast - bmid)).astype(BF16)
            vh = v_ref[rs, sl]
            st = st_ref[n, h]
            sc = _dot_nt(qt, kt)
            p = jnp.where(causal, sc, 0.0).astype(BF16)
            o = _dot(p, vh) + _dot_nt(qi, st.astype(BF16))
            st_ref[n, h] = st * jnp.exp(blast) + _dot_tn(vh, kh)
            on = _rms_scale(o) * gnorm_ref[:, sl] * sg_ref[rs, sl]
            on_ref[rs, sl] = on.astype(BF16)
        return carry

    lax.fori_loop(0, nb * nchunks, chunk_step, 0)

    mix = _dot(on_ref[...], wout_ref[...])
    y = x + _rms_scale(mix) * npost_ref[...]
    y_ref[...] = y.reshape(nb, tt, D_MODEL)

    @pl.when(t == pl.num_programs(1) - 1)
    def _():
        for n in range(nb):
            for h in range(A_HEADS):
                st_ref[n, h] = st_ref[n, h].T


def _hgrn_mixer(x, state, npre, win, alb, gnorm, wout, npost, *, nb, tt, chunk):
    batch, seq, _ = x.shape
    assert batch % nb == 0 and seq % tt == 0 and tt % chunk == 0 and chunk % (2 * SUBLANES) == 0
    rows = nb * tt
    kern = functools.partial(_hgrn_kernel, nb=nb, tt=tt, chunk=chunk)
    st_spec = pl.BlockSpec((nb, A_HEADS, A_DK, A_DV), lambda b, t: (b, 0, 0, 0))
    return pl.pallas_call(
        kern,
        grid=(batch // nb, seq // tt),
        in_specs=[
            pl.BlockSpec((nb, tt, D_MODEL), lambda b, t: (b, t, 0)),
            st_spec,
            _const_spec((1, D_MODEL)),
            _const_spec((D_MODEL, 4 * D_MODEL)),
            _const_spec(alb.shape),
            _const_spec((1, D_MODEL)),
            _const_spec((D_MODEL, D_MODEL)),
            _const_spec((1, D_MODEL)),
        ],
        out_specs=[
            pl.BlockSpec((nb, tt, D_MODEL), lambda b, t: (b, t, 0)),
            st_spec,
        ],
        out_shape=[
            jax.ShapeDtypeStruct(x.shape, F32),
            jax.ShapeDtypeStruct(state.shape, F32),
        ],
        scratch_shapes=[
            pltpu.VMEM((rows, D_MODEL), F32),
            pltpu.VMEM((rows, D_MODEL), F32),
            pltpu.VMEM((rows, D_MODEL), F32),
            pltpu.VMEM((rows, D_MODEL), BF16),
            pltpu.VMEM((rows, D_MODEL), F32),
            pltpu.VMEM((rows, D_MODEL), BF16),
        ],
        compiler_params=pltpu.CompilerParams(
            dimension_semantics=("parallel", "arbitrary"), vmem_limit_bytes=VMEM_LIMIT_BYTES),
        name="hgrn_mixer",
    )(x, state, npre, win, alb, gnorm, wout, npost)


def _rope_cols(x, cos_t, sin_t):
    lane = lax.broadcasted_iota(jnp.int32, (x.shape[0], LANES), 1)
    first_half = (lane % B_HEAD_DIM) < (B_HEAD_DIM // 2)
    cols = []
    for c0 in range(0, x.shape[1], LANES):
        xc = x[:, c0:c0 + LANES]
        partner = jnp.where(first_half,
                            pltpu.roll(xc, LANES - B_HEAD_DIM // 2, axis=1),
                            pltpu.roll(xc, B_HEAD_DIM // 2, axis=1))
        cols.append(xc * cos_t + partner * sin_t)
    return jnp.concatenate(cols, axis=1)


def _attn_kernel(*refs, nb, tt, cq, has_cache):
    if has_cache:
        (x_ref, kc_in_ref, vc_in_ref, cosq_ref, sinq_ref, cosk_ref, sinkt_ref, sinks_ref,
         npre_ref, kvn_ref, wq_ref, wkv_ref, wo_ref, npost_ref,
         y_ref, kc_ref, vc_ref, q_ref, kd_ref, vl_ref, vh_ref, o_ref) = refs
    else:
        (x_ref, cosq_ref, sinq_ref, cosk_ref, sinkt_ref, sinks_ref,
         npre_ref, kvn_ref, wq_ref, wkv_ref, wo_ref, npost_ref,
         y_ref, kc_ref, vc_ref, q_ref, kd_ref, vl_ref, vh_ref, o_ref) = refs
    t = pl.program_id(1)
    rows = nb * tt
    ext = WINDOW + tt
    nkeys = WINDOW + cq
    nchunks = tt // cq

    @pl.when(t == 0)
    def _():
        if has_cache:
            kc_ref[...] = kc_in_ref[...]
            vc_ref[...] = vc_in_ref[...]
        else:
            kc_ref[...] = jnp.zeros(kc_ref.shape, F32)
            vc_ref[...] = jnp.zeros(vc_ref.shape, F32)

    x = x_ref[...].reshape(rows, D_MODEL)
    xs = _rms_scale(x)
    hq = (xs * npre_ref[...]).astype(BF16)
    hk = (xs * kvn_ref[...]).astype(BF16)

    def per_stream(tab_ref):
        tab = tab_ref[...]
        return tab if nb == 1 else jnp.concatenate([tab] * nb, axis=0)

    q = _dot(hq, wq_ref[...])
    q_ref[...] = _rope_cols(q, per_stream(cosq_ref), per_stream(sinq_ref)).astype(BF16)
    kv = _dot(hk, wkv_ref[...])
    k_new = _rope_cols(kv[:, :KV_WIDTH], per_stream(cosk_ref), per_stream(sinkt_ref))
    v_new = kv[:, KV_WIDTH:]

    lane = lax.broadcasted_iota(jnp.int32, (ext, LANES), 1)
    low = lane < B_HEAD_DIM
    for n in range(nb):
        k_ext = jnp.concatenate([kc_ref[n], k_new[n * tt:(n + 1) * tt]], axis=0)
        v_ext = jnp.concatenate([vc_ref[n], v_new[n * tt:(n + 1) * tt]], axis=0)
        kc_ref[n] = k_ext[ext - WINDOW:]
        vc_ref[n] = v_ext[ext - WINDOW:]
        for m in range(KV_WIDTH // LANES):
            ka = k_ext[:, m * LANES:(m + 1) * LANES]
            kr = pltpu.roll(ka, B_HEAD_DIM, axis=1)
            va = v_ext[:, m * LANES:(m + 1) * LANES]
            vr = pltpu.roll(va, B_HEAD_DIM, axis=1)
            es = pl.ds(n * ext, ext)
            kd_ref[2 * m, es, :] = jnp.where(low, ka, kr).astype(BF16)
            kd_ref[2 * m + 1, es, :] = jnp.where(low, kr, ka).astype(BF16)
            vl_ref[2 * m, es, :] = jnp.where(low, va, 0.0).astype(BF16)
            vh_ref[2 * m, es, :] = jnp.where(low, 0.0, vr).astype(BF16)
            vl_ref[2 * m + 1, es, :] = jnp.where(low, vr, 0.0).astype(BF16)
            vh_ref[2 * m + 1, es, :] = jnp.where(low, 0.0, va).astype(BF16)

    qlane_low = lax.broadcasted_iota(jnp.int32, (cq, LANES), 1) < B_HEAD_DIM
    key_idx = lax.broadcasted_iota(jnp.int32, (1, nkeys), 1)

    def chunk_step(idx, carry):
        n = idx // nchunks
        c = idx % nchunks
        r0 = pl.multiple_of(idx * cq, cq)
        k0 = pl.multiple_of(n * ext + c * cq, SUBLANES * 2)
        if has_cache:
            bias = None
        else:
            first_pos = t * tt + c * cq - WINDOW
            bias = jnp.where(key_idx + first_pos >= 0, 0.0, -jnp.inf)
        for j in range(B_KV_HEADS):
            qa = q_ref[pl.ds(r0, cq), 2 * j * LANES:(2 * j + 1) * LANES]
            qb = q_ref[pl.ds(r0, cq), (2 * j + 1) * LANES:(2 * j + 2) * LANES]
            zero = jnp.zeros_like(qa)
            qstack = jnp.concatenate([
                jnp.where(qlane_low, qa, zero), jnp.where(qlane_low, qb, zero),
                jnp.where(qlane_low, zero, qa), jnp.where(qlane_low, zero, qb)], axis=0)
            s = _dot_nt(qstack, kd_ref[j, pl.ds(k0, nkeys), :])
            if bias is not None:
                s = s + bias
            heads = (4 * j, 4 * j + 2, 4 * j + 1, 4 * j + 3)
            sink = jnp.concatenate(
                [jnp.full((cq, 1), sinks_ref[hd], F32) for hd in heads], axis=0)
            mx = jnp.maximum(jnp.max(s, axis=-1, keepdims=True), sink)
            p = jnp.exp(s - mx)
            denom = jnp.sum(p, axis=-1, keepdims=True) + jnp.exp(sink - mx)
            p = (p / denom).astype(BF16)
            out = (_dot(p[:2 * cq], vl_ref[j, pl.ds(k0, nkeys), :])
                   + _dot(p[2 * cq:], vh_ref[j, pl.ds(k0, nkeys), :]))
            o_ref[pl.ds(r0, cq), 2 * j * LANES:(2 * j + 1) * LANES] = out[:cq].astype(BF16)
            o_ref[pl.ds(r0, cq), (2 * j + 1) * LANES:(2 * j + 2) * LANES] = out[cq:].astype(BF16)
        return carry

    lax.fori_loop(0, nb * nchunks, chunk_step, 0)

    mix = _dot(o_ref[...], wo_ref[...])
    y = x + _rms_scale(mix) * npost_ref[...]
    y_ref[...] = y.reshape(nb, tt, D_MODEL)


def _attn_mixer(x, cache_k, cache_v, tables, sinks, npre, kvn, wq, wkv, wo, npost, *, nb, tt, cq):
    batch, seq, _ = x.shape
    has_cache = cache_k is not None
    assert batch % nb == 0 and seq % tt == 0 and tt % cq == 0 and cq % (2 * SUBLANES) == 0
    rows = nb * tt
    ext = WINDOW + tt
    kern = functools.partial(_attn_kernel, nb=nb, tt=tt, cq=cq, has_cache=has_cache)
    x_spec = pl.BlockSpec((nb, tt, D_MODEL), lambda b, t: (b, t, 0))
    c_spec = pl.BlockSpec((nb, WINDOW, KV_WIDTH), lambda b, t: (b, 0, 0))
    tab_spec = pl.BlockSpec((tt, LANES), lambda b, t: (t, 0))
    in_specs = [x_spec]
    args = [x]
    if has_cache:
        in_specs += [c_spec, c_spec]
        args += [cache_k, cache_v]
    in_specs += [tab_spec] * 4 + [
        pl.BlockSpec(memory_space=pltpu.SMEM),
        _const_spec((1, D_MODEL)),
        _const_spec((1, D_MODEL)),
        _const_spec((D_MODEL, D_MODEL)),
        _const_spec((D_MODEL, 2 * KV_WIDTH)),
        _const_spec((D_MODEL, D_MODEL)),
        _const_spec((1, D_MODEL)),
    ]
    args += list(tables) + [sinks, npre, kvn, wq, wkv, wo, npost]
    cache_shape = jax.ShapeDtypeStruct((batch, WINDOW, KV_WIDTH), F32)
    return pl.pallas_call(
        kern,
        grid=(batch // nb, seq // tt),
        in_specs=in_specs,
        out_specs=[x_spec, c_spec, c_spec],
        out_shape=[jax.ShapeDtypeStruct(x.shape, F32), cache_shape, cache_shape],
        scratch_shapes=[
            pltpu.VMEM((rows, D_MODEL), BF16),
            pltpu.VMEM((B_KV_HEADS, nb * ext, LANES), BF16),
            pltpu.VMEM((B_KV_HEADS, nb * ext, LANES), BF16),
            pltpu.VMEM((B_KV_HEADS, nb * ext, LANES), BF16),
            pltpu.VMEM((rows, D_MODEL), BF16),
        ],
        compiler_params=pltpu.CompilerParams(
            dimension_semantics=("parallel", "arbitrary"), vmem_limit_bytes=VMEM_LIMIT_BYTES),
        name="attn_mixer",
    )(*args)


def _rope_tables(pos):
    half = B_HEAD_DIM // 2
    inv = ROPE_THETA ** (-jnp.arange(half, dtype=F32) / half)
    ang = pos.astype(F32)[:, None] * inv[None, :]
    cos = jnp.cos(ang)
    sin = jnp.sin(ang)
    reps = LANES // B_HEAD_DIM
    cos_t = jnp.tile(jnp.concatenate([cos, cos], axis=1), (1, reps))
    sin_t = jnp.tile(jnp.concatenate([-sin, sin], axis=1), (1, reps))
    scale = B_HEAD_DIM ** -0.5
    return cos_t * scale, sin_t * scale, cos_t, sin_t


def _trunk(x, pos, state, cache_k, cache_v, w, *, nb, tt, hgrn_chunk, cq):
    batch, seq, _ = x.shape
    row = lambda a: a.reshape(1, D_MODEL)
    x, st = _hgrn_mixer(x, state, row(w["norm_mix_pre"][0]), w["w_a_in"], w["a_lower_bound"],
                        row(w["a_out_norm"]), w["w_a_out"], row(w["norm_mix_post"][0]),
                        nb=nb, tt=tt, chunk=hgrn_chunk)
    x = _ffn(x.reshape(batch * seq, D_MODEL), row(w["norm_ffn_pre"][0]), row(w["norm_ffn_post"][0]),
             w["w_ffn_in"][0], w["w_ffn_out"][0]).reshape(batch, seq, D_MODEL)
    x, kc, vc = _attn_mixer(x, cache_k, cache_v, _rope_tables(pos), w["b_sinks"],
                            row(w["norm_mix_pre"][1]), row(w["kv_norm"]), w["w_b_q"], w["w_kv"],
                            w["w_b_out"], row(w["norm_mix_post"][1]), nb=nb, tt=tt, cq=cq)
    x = _ffn(x.reshape(batch * seq, D_MODEL), row(w["norm_ffn_pre"][1]), row(w["norm_ffn_post"][1]),
             w["w_ffn_in"][1], w["w_ffn_out"][1]).reshape(batch, seq, D_MODEL)
    return x, st, kc, vc


def kernel(x_prompt, x_sample, state_hgrn, cache_k, cache_v, norm_mix_pre, norm_mix_post, norm_ffn_pre, norm_ffn_post, w_ffn_in, w_ffn_out, w_a_in, a_lower_bound, a_out_norm, w_a_out, kv_norm, w_kv, w_b_q, b_sinks, w_b_out):
    w = dict(
        norm_mix_pre=norm_mix_pre, norm_mix_post=norm_mix_post,
        norm_ffn_pre=norm_ffn_pre, norm_ffn_post=norm_ffn_post,
        w_ffn_in=w_ffn_in.astype(BF16), w_ffn_out=w_ffn_out.astype(BF16),
        w_a_in=w_a_in[0].astype(BF16), a_lower_bound=a_lower_bound,
        a_out_norm=a_out_norm[0], w_a_out=w_a_out[0].astype(BF16),
        kv_norm=kv_norm, w_kv=w_kv.astype(BF16), w_b_q=w_b_q[0].astype(BF16),
        b_sinks=b_sinks[0], w_b_out=w_b_out[0].astype(BF16),
    )
    bp, tp, _ = x_prompt.shape
    bs, ts, _ = x_sample.shape

    zero_state = jnp.zeros((bp, A_HEADS, A_DK, A_DV), F32)
    y_p, st_p, kc_p, vc_p = _trunk(
        x_prompt, jnp.arange(tp), zero_state, None, None, w,
        nb=1, tt=ROW_TILE, hgrn_chunk=HGRN_CHUNK, cq=CHUNK)

    nb_s = ROW_TILE // (2 * ts)
    y_s, st_s, kc_s, vc_s = _trunk(
        x_sample, PAST_LEN + jnp.arange(ts), state_hgrn[0],
        cache_k.reshape(bs, WINDOW, KV_WIDTH), cache_v.reshape(bs, WINDOW, KV_WIDTH), w,
        nb=nb_s, tt=ts, hgrn_chunk=ts, cq=ts)

    cache4 = lambda a: a.reshape(a.shape[0], WINDOW, B_KV_HEADS, B_HEAD_DIM)
    return (y_p, y_s, st_p[None], st_s[None],
            cache4(kc_p), cache4(vc_p), cache4(kc_s), cache4(vc_s))
```

```python
import functools

import jax
import jax.numpy as jnp
from jax import lax
from jax.experimental import pallas as pl
from jax.experimental.pallas import tpu as pltpu

F32 = jnp.float32
BF16 = jnp.bfloat16

D_MODEL = 1024
A_HEADS = 8
A_DK = 128
A_DV = 128
B_HEAD_DIM = 64
B_Q_HEADS = 16
B_KV_HEADS = 4
B_GROUP = B_Q_HEADS // B_KV_HEADS
KV_WIDTH = B_KV_HEADS * B_HEAD_DIM
WINDOW = 128
CHUNK = 64
PAST_LEN = 2048
D_FF = 2816
ROPE_THETA = 10000.0
NORM_EPS = 1e-6

LANES = 128
SUBLANES = 8
VMEM_LIMIT_BYTES = 52 * 1024 * 1024

ROW_TILE = 512
HGRN_CHUNK = 128
FFN_CHUNK = 256


def _rms_scale(x):
    ms = jnp.mean(x * x, axis=-1, keepdims=True)
    return x * lax.rsqrt(ms + NORM_EPS)


def _dot(a, b):
    return jnp.dot(a, b, preferred_element_type=F32)


def _dot_nt(a, b):
    return lax.dot_general(a, b, (((1,), (1,)), ((), ())), preferred_element_type=F32)


def _dot_tn(a, b):
    return lax.dot_general(a, b, (((0,), (0,)), ((), ())), preferred_element_type=F32)


def _const_spec(shape):
    nd = len(shape)
    return pl.BlockSpec(shape, lambda *_: (0,) * nd, pipeline_mode=pl.Buffered(1))


def _ffn_kernel(x_ref, npre_ref, npost_ref, win_ref, wout_ref, o_ref):
    x = x_ref[...]
    h = (_rms_scale(x) * npre_ref[...]).astype(BF16)
    acc = jnp.zeros(x.shape, F32)
    for c0 in range(0, D_FF, FFN_CHUNK):
        a = _dot(h, win_ref[:, c0:c0 + FFN_CHUNK])
        b = _dot(h, win_ref[:, D_FF + c0:D_FF + c0 + FFN_CHUNK])
        g = (a * jax.nn.sigmoid(a) * b).astype(BF16)
        acc = acc + _dot(g, wout_ref[c0:c0 + FFN_CHUNK, :])
    o_ref[...] = x + _rms_scale(acc) * npost_ref[...]


def _ffn(x2d, npre, npost, win, wout):
    rows = x2d.shape[0]
    tile = min(ROW_TILE, rows)
    assert rows % tile == 0 and D_FF % FFN_CHUNK == 0
    return pl.pallas_call(
        _ffn_kernel,
        grid=(rows // tile,),
        in_specs=[
            pl.BlockSpec((tile, D_MODEL), lambda i: (i, 0)),
            _const_spec((1, D_MODEL)),
            _const_spec((1, D_MODEL)),
            _const_spec((D_MODEL, 2 * D_FF)),
            _const_spec((D_FF, D_MODEL)),
        ],
        out_specs=pl.BlockSpec((tile, D_MODEL), lambda i: (i, 0)),
        out_shape=jax.ShapeDtypeStruct((rows, D_MODEL), F32),
        compiler_params=pltpu.CompilerParams(
            dimension_semantics=("parallel",), vmem_limit_bytes=VMEM_LIMIT_BYTES),
        name="ffn",
    )(x2d, npre, npost, win, wout)


def _cumsum_rows(x):
    c, w = x.shape
    groups = c // SUBLANES
    y = x.reshape(groups, SUBLANES, w)
    sub = lax.broadcasted_iota(jnp.int32, y.shape, 1)
    shift = 1
    while shift < SUBLANES:
        y = y + jnp.where(sub >= shift, pltpu.roll(y, shift, axis=1), 0.0)
        shift *= 2
    tot = jnp.broadcast_to(y[:, SUBLANES - 1:SUBLANES, :], y.shape)
    inc = tot
    shift = 1
    while shift < groups:
        inc = inc + jnp.concatenate(
            [jnp.zeros((shift, SUBLANES, w), F32), inc[:groups - shift]], axis=0)
        shift *= 2
    return (y + (inc - tot)).reshape(c, w)


def _hgrn_kernel(x_ref, st_in_ref, npre_ref, win_ref, alb_ref, gnorm_ref, wout_ref, npost_ref,
                 y_ref, st_ref, qs_ref, lf_ref, inp_ref, v_ref, sg_ref, on_ref,
                 *, nb, tt, chunk):
    t = pl.program_id(1)
    rows = nb * tt
    nchunks = tt // chunk

    @pl.when(t == 0)
    def _():
        for n in range(nb):
            for h in range(A_HEADS):
                st_ref[n, h] = st_in_ref[n, h].T

    x = x_ref[...].reshape(rows, D_MODEL)
    hn = (_rms_scale(x) * npre_ref[...]).astype(BF16)

    alb = alb_ref[...]
    e = jnp.exp(alb - jnp.max(alb, axis=0, keepdims=True))
    lb = e[0:1] / jnp.sum(e, axis=0, keepdims=True)

    q = _dot(hn, win_ref[:, 0:D_MODEL])
    qs_ref[...] = q * jax.nn.sigmoid(q)
    f = _dot(hn, win_ref[:, D_MODEL:2 * D_MODEL])
    forget = lb + (1.0 - lb) * jax.nn.sigmoid(f)
    lf_ref[...] = jnp.log(forget)
    inp_ref[...] = 1.0 - forget
    v_ref[...] = _dot(hn, win_ref[:, 2 * D_MODEL:3 * D_MODEL]).astype(BF16)
    g = _dot(hn, win_ref[:, 3 * D_MODEL:4 * D_MODEL])
    sg_ref[...] = g * jax.nn.sigmoid(g)

    half = chunk // 2
    ri = lax.broadcasted_iota(jnp.int32, (chunk, chunk), 0)
    ci = lax.broadcasted_iota(jnp.int32, (chunk, chunk), 1)
    causal = ri >= ci

    def chunk_step(idx, carry):
        n = idx // nchunks
        r0 = pl.multiple_of(idx * chunk, chunk)
        rs = pl.ds(r0, chunk)
        for h in range(A_HEADS):
            sl = slice(h * A_DK, (h + 1) * A_DK)
            b = _cumsum_rows(lf_ref[rs, sl])
            bmid = b[half - 1:half]
            blast = b[chunk - 1:chunk]
            eq = jnp.exp(b - bmid)
            ek = jnp.exp(bmid - b)
            qsc = qs_ref[rs, sl] * eq
            inp = inp_ref[rs, sl] * ek
            qt = qsc.astype(BF16)
            kt = inp.astype(BF16)
            qi = (qsc * jnp.exp(bmid)).astype(BF16)
            kh = (inp * jnp.exp(blast - bmid)).astype(BF16)
            vh = v_ref[rs, sl]
            st = st_ref[n, h]
            sc = _dot_nt(qt, kt)
            p = jnp.where(causal, sc, 0.0).astype(BF16)
            o = _dot(p, vh) + _dot_nt(qi, st.astype(BF16))
            st_ref[n, h] = st * jnp.exp(blast) + _dot_tn(vh, kh)
            on = _rms_scale(o) * gnorm_ref[:, sl] * sg_ref[rs, sl]
            on_ref[rs, sl] = on.astype(BF16)
        return carry

    lax.fori_loop(0, nb * nchunks, chunk_step, 0)

    mix = _dot(on_ref[...], wout_ref[...])
    y = x + _rms_scale(mix) * npost_ref[...]
    y_ref[...] = y.reshape(nb, tt, D_MODEL)

    @pl.when(t == pl.num_programs(1) - 1)
    def _():
        for n in range(nb):
            for h in range(A_HEADS):
                st_ref[n, h] = st_ref[n, h].T


def _hgrn_mixer(x, state, npre, win, alb, gnorm, wout, npost, *, nb, tt, chunk):
    batch, seq, _ = x.shape
    assert batch % nb == 0 and seq % tt == 0 and tt % chunk == 0 and chunk % (2 * SUBLANES) == 0
    rows = nb * tt
    kern = functools.partial(_hgrn_kernel, nb=nb, tt=tt, chunk=chunk)
    st_spec = pl.BlockSpec((nb, A_HEADS, A_DK, A_DV), lambda b, t: (b, 0, 0, 0))
    return pl.pallas_call(
        kern,
        grid=(batch // nb, seq // tt),
        in_specs=[
            pl.BlockSpec((nb, tt, D_MODEL), lambda b, t: (b, t, 0)),
            st_spec,
            _const_spec((1, D_MODEL)),
            _const_spec((D_MODEL, 4 * D_MODEL)),
            _const_spec(alb.shape),
            _const_spec((1, D_MODEL)),
            _const_spec((D_MODEL, D_MODEL)),
            _const_spec((1, D_MODEL)),
        ],
        out_specs=[
            pl.BlockSpec((nb, tt, D_MODEL), lambda b, t: (b, t, 0)),
            st_spec,
        ],
        out_shape=[
            jax.ShapeDtypeStruct(x.shape, F32),
            jax.ShapeDtypeStruct(state.shape, F32),
        ],
        scratch_shapes=[
            pltpu.VMEM((rows, D_MODEL), F32),
            pltpu.VMEM((rows, D_MODEL), F32),
            pltpu.VMEM((rows, D_MODEL), F32),
            pltpu.VMEM((rows, D_MODEL), BF16),
            pltpu.VMEM((rows, D_MODEL), F32),
            pltpu.VMEM((rows, D_MODEL), BF16),
        ],
        compiler_params=pltpu.CompilerParams(
            dimension_semantics=("parallel", "arbitrary"), vmem_limit_bytes=VMEM_LIMIT_BYTES),
        name="hgrn_mixer",
    )(x, state, npre, win, alb, gnorm, wout, npost)


def _rope_cols(x, cos_t, sin_t):
    lane = lax.broadcasted_iota(jnp.int32, (x.shape[0], LANES), 1)
    first_half = (lane % B_HEAD_DIM) < (B_HEAD_DIM // 2)
    cols = []
    for c0 in range(0, x.shape[1], LANES):
        xc = x[:, c0:c0 + LANES]
        partner = jnp.where(first_half,
                            pltpu.roll(xc, LANES - B_HEAD_DIM // 2, axis=1),
                            pltpu.roll(xc, B_HEAD_DIM // 2, axis=1))
        cols.append(xc * cos_t + partner * sin_t)
    return jnp.concatenate(cols, axis=1)


def _attn_kernel(*refs, nb, tt, cq, has_cache):
    if has_cache:
        (x_ref, kc_in_ref, vc_in_ref, cosq_ref, sinq_ref, cosk_ref, sinkt_ref, sinks_ref,
         npre_ref, kvn_ref, wq_ref, wkv_ref, wo_ref, npost_ref,
         y_ref, kc_ref, vc_ref, q_ref, kd_ref, vl_ref, vh_ref, o_ref) = refs
    else:
        (x_ref, cosq_ref, sinq_ref, cosk_ref, sinkt_ref, sinks_ref,
         npre_ref, kvn_ref, wq_ref, wkv_ref, wo_ref, npost_ref,
         y_ref, kc_ref, vc_ref, q_ref, kd_ref, vl_ref, vh_ref, o_ref) = refs
    t = pl.program_id(1)
    rows = nb * tt
    ext = WINDOW + tt
    nkeys = WINDOW + cq
    nchunks = tt // cq

    @pl.when(t == 0)
    def _():
        if has_cache:
            kc_ref[...] = kc_in_ref[...]
            vc_ref[...] = vc_in_ref[...]
        else:
            kc_ref[...] = jnp.zeros(kc_ref.shape, F32)
            vc_ref[...] = jnp.zeros(vc_ref.shape, F32)

    x = x_ref[...].reshape(rows, D_MODEL)
    xs = _rms_scale(x)
    hq = (xs * npre_ref[...]).astype(BF16)
    hk = (xs * kvn_ref[...]).astype(BF16)

    def per_stream(tab_ref):
        tab = tab_ref[...]
        return tab if nb == 1 else jnp.concatenate([tab] * nb, axis=0)

    q = _dot(hq, wq_ref[...])
    q_ref[...] = _rope_cols(q, per_stream(cosq_ref), per_stream(sinq_ref)).astype(BF16)
    kv = _dot(hk, wkv_ref[...])
    k_new = _rope_cols(kv[:, :KV_WIDTH], per_stream(cosk_ref), per_stream(sinkt_ref))
    v_new = kv[:, KV_WIDTH:]

    lane = lax.broadcasted_iota(jnp.int32, (ext, LANES), 1)
    low = lane < B_HEAD_DIM
    for n in range(nb):
        k_ext = jnp.concatenate([kc_ref[n], k_new[n * tt:(n + 1) * tt]], axis=0)
        v_ext = jnp.concatenate([vc_ref[n], v_new[n * tt:(n + 1) * tt]], axis=0)
        kc_ref[n] = k_ext[ext - WINDOW:]
        vc_ref[n] = v_ext[ext - WINDOW:]
        for m in range(KV_WIDTH // LANES):
            ka = k_ext[:, m * LANES:(m + 1) * LANES]
            kr = pltpu.roll(ka, B_HEAD_DIM, axis=1)
            va = v_ext[:, m * LANES:(m + 1) * LANES]
            vr = pltpu.roll(va, B_HEAD_DIM, axis=1)
            es = pl.ds(n * ext, ext)
            kd_ref[2 * m, es, :] = jnp.where(low, ka, kr).astype(BF16)
            kd_ref[2 * m + 1, es, :] = jnp.where(low, kr, ka).astype(BF16)
            vl_ref[2 * m, es, :] = jnp.where(low, va, 0.0).astype(BF16)
            vh_ref[2 * m, es, :] = jnp.where(low, 0.0, vr).astype(BF16)
            vl_ref[2 * m + 1, es, :] = jnp.where(low, vr, 0.0).astype(BF16)
            vh_ref[2 * m + 1, es, :] = jnp.where(low, 0.0, va).astype(BF16)

    qlane_low = lax.broadcasted_iota(jnp.int32, (cq, LANES), 1) < B_HEAD_DIM
    key_idx = lax.broadcasted_iota(jnp.int32, (1, nkeys), 1)

    def chunk_step(idx, carry):
        n = idx // nchunks
        c = idx % nchunks
        r0 = pl.multiple_of(idx * cq, cq)
        k0 = pl.multiple_of(n * ext + c * cq, SUBLANES * 2)
        if has_cache:
            bias = None
        else:
            first_pos = t * tt + c * cq - WINDOW
            bias = jnp.where(key_idx + first_pos >= 0, 0.0, -jnp.inf)
        for j in range(B_KV_HEADS):
            qa = q_ref[pl.ds(r0, cq), 2 * j * LANES:(2 * j + 1) * LANES]
            qb = q_ref[pl.ds(r0, cq), (2 * j + 1) * LANES:(2 * j + 2) * LANES]
            zero = jnp.zeros_like(qa)
            qstack = jnp.concatenate([
                jnp.where(qlane_low, qa, zero), jnp.where(qlane_low, qb, zero),
                jnp.where(qlane_low, zero, qa), jnp.where(qlane_low, zero, qb)], axis=0)
            s = _dot_nt(qstack, kd_ref[j, pl.ds(k0, nkeys), :])
            if bias is not None:
                s = s + bias
            heads = (4 * j, 4 * j + 2, 4 * j + 1, 4 * j + 3)
            sink = jnp.concatenate(
                [jnp.full((cq, 1), sinks_ref[hd], F32) for hd in heads], axis=0)
            mx = jnp.maximum(jnp.max(s, axis=-1, keepdims=True), sink)
            p = jnp.exp(s - mx)
            denom = jnp.sum(p, axis=-1, keepdims=True) + jnp.exp(sink - mx)
            p = (p / denom).astype(BF16)
            out = (_dot(p[:2 * cq], vl_ref[j, pl.ds(k0, nkeys), :])
                   + _dot(p[2 * cq:], vh_ref[j, pl.ds(k0, nkeys), :]))
            o_ref[pl.ds(r0, cq), 2 * j * LANES:(2 * j + 1) * LANES] = out[:cq].astype(BF16)
            o_ref[pl.ds(r0, cq), (2 * j + 1) * LANES:(2 * j + 2) * LANES] = out[cq:].astype(BF16)
        return carry

    lax.fori_loop(0, nb * nchunks, chunk_step, 0)

    mix = _dot(o_ref[...], wo_ref[...])
    y = x + _rms_scale(mix) * npost_ref[...]
    y_ref[...] = y.reshape(nb, tt, D_MODEL)


def _attn_mixer(x, cache_k, cache_v, tables, sinks, npre, kvn, wq, wkv, wo, npost, *, nb, tt, cq):
    batch, seq, _ = x.shape
    has_cache = cache_k is not None
    assert batch % nb == 0 and seq % tt == 0 and tt % cq == 0 and cq % (2 * SUBLANES) == 0
    rows = nb * tt
    ext = WINDOW + tt
    kern = functools.partial(_attn_kernel, nb=nb, tt=tt, cq=cq, has_cache=has_cache)
    x_spec = pl.BlockSpec((nb, tt, D_MODEL), lambda b, t: (b, t, 0))
    c_spec = pl.BlockSpec((nb, WINDOW, KV_WIDTH), lambda b, t: (b, 0, 0))
    tab_spec = pl.BlockSpec((tt, LANES), lambda b, t: (t, 0))
    in_specs = [x_spec]
    args = [x]
    if has_cache:
        in_specs += [c_spec, c_spec]
        args += [cache_k, cache_v]
    in_specs += [tab_spec] * 4 + [
        pl.BlockSpec(memory_space=pltpu.SMEM),
        _const_spec((1, D_MODEL)),
        _const_spec((1, D_MODEL)),
        _const_spec((D_MODEL, D_MODEL)),
        _const_spec((D_MODEL, 2 * KV_WIDTH)),
        _const_spec((D_MODEL, D_MODEL)),
        _const_spec((1, D_MODEL)),
    ]
    args += list(tables) + [sinks, npre, kvn, wq, wkv, wo, npost]
    cache_shape = jax.ShapeDtypeStruct((batch, WINDOW, KV_WIDTH), F32)
    return pl.pallas_call(
        kern,
        grid=(batch // nb, seq // tt),
        in_specs=in_specs,
        out_specs=[x_spec, c_spec, c_spec],
        out_shape=[jax.ShapeDtypeStruct(x.shape, F32), cache_shape, cache_shape],
        scratch_shapes=[
            pltpu.VMEM((rows, D_MODEL), BF16),
            pltpu.VMEM((B_KV_HEADS, nb * ext, LANES), BF16),
            pltpu.VMEM((B_KV_HEADS, nb * ext, LANES), BF16),
            pltpu.VMEM((B_KV_HEADS, nb * ext, LANES), BF16),
            pltpu.VMEM((rows, D_MODEL), BF16),
        ],
        compiler_params=pltpu.CompilerParams(
            dimension_semantics=("parallel", "arbitrary"), vmem_limit_bytes=VMEM_LIMIT_BYTES),
        name="attn_mixer",
    )(*args)


def _attn_pair_kernel(x_ref, cosq_ref, sinq_ref, cosk_ref, sinkt_ref, sinks_ref,
                      npre_ref, kvn_ref, wq_ref, wkv_ref, wo_ref, npost_ref,
                      y_ref, kc_ref, vc_ref, q_ref, kl_ref, kh_ref, vt_ref, ot_ref, *, tt):
    t = pl.program_id(1)
    ext = WINDOW + tt
    pair = 2 * CHUNK
    nkeys = WINDOW + pair

    @pl.when(t == 0)
    def _():
        kc_ref[...] = jnp.zeros(kc_ref.shape, F32)
        vc_ref[...] = jnp.zeros(vc_ref.shape, F32)

    x = x_ref[0]
    xs = _rms_scale(x)
    hq = (xs * npre_ref[...]).astype(BF16)
    hk = (xs * kvn_ref[...]).astype(BF16)

    q = _dot(hq, wq_ref[...])
    q_ref[...] = _rope_cols(q, cosq_ref[...], sinq_ref[...]).astype(BF16)
    kv = _dot(hk, wkv_ref[...])
    k_ext = jnp.concatenate(
        [kc_ref[0], _rope_cols(kv[:, :KV_WIDTH], cosk_ref[...], sinkt_ref[...])], axis=0)
    v_ext = jnp.concatenate([vc_ref[0], kv[:, KV_WIDTH:]], axis=0)
    kc_ref[0] = k_ext[ext - WINDOW:]
    vc_ref[0] = v_ext[ext - WINDOW:]
    vt_ref[...] = v_ext.T.astype(BF16)

    low = lax.broadcasted_iota(jnp.int32, (ext, LANES), 1) < B_HEAD_DIM
    for m in range(KV_WIDTH // LANES):
        ka = k_ext[:, m * LANES:(m + 1) * LANES]
        kr = pltpu.roll(ka, B_HEAD_DIM, axis=1)
        kl_ref[2 * m] = jnp.where(low, ka, 0.0).astype(BF16)
        kh_ref[2 * m] = jnp.where(low, 0.0, kr).astype(BF16)
        kl_ref[2 * m + 1] = jnp.where(low, kr, 0.0).astype(BF16)
        kh_ref[2 * m + 1] = jnp.where(low, 0.0, ka).astype(BF16)

    key_row = lax.broadcasted_iota(jnp.int32, (nkeys, 2 * pair), 0)
    col = lax.broadcasted_iota(jnp.int32, (nkeys, 2 * pair), 1)
    rel = key_row // CHUNK - (col // CHUNK) % 2
    band_bias = jnp.where(rel >= 0, jnp.where(rel <= WINDOW // CHUNK, 0.0, -jnp.inf), -jnp.inf)
    first_bias = jnp.where(key_row + (t * tt - WINDOW) >= 0, band_bias, -jnp.inf)
    first_head = col < pair

    units = [(j, kx_ref, ha, hb) for j in range(B_KV_HEADS)
             for kx_ref, ha, hb in ((kl_ref, 4 * j, 4 * j + 2), (kh_ref, 4 * j + 1, 4 * j + 3))]

    def scores(e):
        r0 = e * pair
        bias = first_bias if e == 0 else band_bias
        out = []
        for j, kx_ref, _, _ in units:
            qq = jnp.concatenate([q_ref[r0:r0 + pair, 2 * j * LANES:(2 * j + 1) * LANES],
                                  q_ref[r0:r0 + pair, (2 * j + 1) * LANES:(2 * j + 2) * LANES]], axis=0)
            out.append(_dot_nt(kx_ref[j, r0:r0 + nkeys, :], qq) + bias)
        return out

    def softmax(s_list):
        out = []
        for (_, _, ha, hb), s in zip(units, s_list):
            sink = jnp.where(first_head[0:1], sinks_ref[ha], sinks_ref[hb])
            mx = jnp.maximum(jnp.max(s, axis=0, keepdims=True), sink)
            p = jnp.exp(s - mx)
            denom = jnp.sum(p, axis=0, keepdims=True) + jnp.exp(sink - mx)
            out.append((p * (1.0 / denom)).astype(BF16))
        return out

    def weighted_values(e, p_list):
        r0 = e * pair
        for (j, _, ha, hb), p in zip(units, p_list):
            o = _dot(vt_ref[j * B_HEAD_DIM:(j + 1) * B_HEAD_DIM, r0:r0 + nkeys], p).astype(BF16)
            ot_ref[ha * B_HEAD_DIM:(ha + 1) * B_HEAD_DIM, r0:r0 + pair] = o[:, :pair]
            ot_ref[hb * B_HEAD_DIM:(hb + 1) * B_HEAD_DIM, r0:r0 + pair] = o[:, pair:]

    npairs = tt // pair
    s_next = scores(0)
    for e in range(npairs):
        s_cur = s_next
        if e + 1 < npairs:
            s_next = scores(e + 1)
        weighted_values(e, softmax(s_cur))

    mix = _dot_tn(ot_ref[...], wo_ref[...])
    y_ref[0] = x + _rms_scale(mix) * npost_ref[...]


def _attn_pair_mixer(x, tables, sinks, npre, kvn, wq, wkv, wo, npost, *, tt):
    batch, seq, _ = x.shape
    assert seq % tt == 0 and tt % (2 * CHUNK) == 0 and WINDOW == 2 * CHUNK
    ext = WINDOW + tt
    kern = functools.partial(_attn_pair_kernel, tt=tt)
    x_spec = pl.BlockSpec((1, tt, D_MODEL), lambda b, t: (b, t, 0))
    c_spec = pl.BlockSpec((1, WINDOW, KV_WIDTH), lambda b, t: (b, 0, 0))
    tab_spec = pl.BlockSpec((tt, LANES), lambda b, t: (t, 0))
    cache_shape = jax.ShapeDtypeStruct((batch, WINDOW, KV_WIDTH), F32)
    return pl.pallas_call(
        kern,
        grid=(batch, seq // tt),
        in_specs=[x_spec] + [tab_spec] * 4 + [
            pl.BlockSpec(memory_space=pltpu.SMEM),
            _const_spec((1, D_MODEL)),
            _const_spec((1, D_MODEL)),
            _const_spec((D_MODEL, D_MODEL)),
            _const_spec((D_MODEL, 2 * KV_WIDTH)),
            _const_spec((D_MODEL, D_MODEL)),
            _const_spec((1, D_MODEL)),
        ],
        out_specs=[x_spec, c_spec, c_spec],
        out_shape=[jax.ShapeDtypeStruct(x.shape, F32), cache_shape, cache_shape],
        scratch_shapes=[
            pltpu.VMEM((tt, D_MODEL), BF16),
            pltpu.VMEM((B_KV_HEADS, ext, LANES), BF16),
            pltpu.VMEM((B_KV_HEADS, ext, LANES), BF16),
            pltpu.VMEM((KV_WIDTH, ext), BF16),
            pltpu.VMEM((D_MODEL, tt), BF16),
        ],
        compiler_params=pltpu.CompilerParams(
            dimension_semantics=("parallel", "arbitrary"), vmem_limit_bytes=VMEM_LIMIT_BYTES),
        name="attn_pair_mixer",
    )(x, *tables, sinks, npre, kvn, wq, wkv, wo, npost)


def _rope_tables(pos):
    half = B_HEAD_DIM // 2
    inv = ROPE_THETA ** (-jnp.arange(half, dtype=F32) / half)
    ang = pos.astype(F32)[:, None] * inv[None, :]
    cos = jnp.cos(ang)
    sin = jnp.sin(ang)
    reps = LANES // B_HEAD_DIM
    cos_t = jnp.tile(jnp.concatenate([cos, cos], axis=1), (1, reps))
    sin_t = jnp.tile(jnp.concatenate([-sin, sin], axis=1), (1, reps))
    scale = B_HEAD_DIM ** -0.5
    return cos_t * scale, sin_t * scale, cos_t, sin_t


def _trunk(x, pos, state, cache_k, cache_v, w, *, nb, tt, hgrn_chunk, cq):
    batch, seq, _ = x.shape
    row = lambda a: a.reshape(1, D_MODEL)
    x, st = _hgrn_mixer(x, state, row(w["norm_mix_pre"][0]), w["w_a_in"], w["a_lower_bound"],
                        row(w["a_out_norm"]), w["w_a_out"], row(w["norm_mix_post"][0]),
                        nb=nb, tt=tt, chunk=hgrn_chunk)
    x = _ffn(x.reshape(batch * seq, D_MODEL), row(w["norm_ffn_pre"][0]), row(w["norm_ffn_post"][0]),
             w["w_ffn_in"][0], w["w_ffn_out"][0]).reshape(batch, seq, D_MODEL)
    attn_w = (w["b_sinks"], row(w["norm_mix_pre"][1]), row(w["kv_norm"]), w["w_b_q"], w["w_kv"],
              w["w_b_out"], row(w["norm_mix_post"][1]))
    if cache_k is None and nb == 1 and cq == CHUNK:
        x, kc, vc = _attn_pair_mixer(x, _rope_tables(pos), *attn_w, tt=tt)
    else:
        x, kc, vc = _attn_mixer(x, cache_k, cache_v, _rope_tables(pos), *attn_w, nb=nb, tt=tt, cq=cq)
    x = _ffn(x.reshape(batch * seq, D_MODEL), row(w["norm_ffn_pre"][1]), row(w["norm_ffn_post"][1]),
             w["w_ffn_in"][1], w["w_ffn_out"][1]).reshape(batch, seq, D_MODEL)
    return x, st, kc, vc


def kernel(x_prompt, x_sample, state_hgrn, cache_k, cache_v, norm_mix_pre, norm_mix_post, norm_ffn_pre, norm_ffn_post, w_ffn_in, w_ffn_out, w_a_in, a_lower_bound, a_out_norm, w_a_out, kv_norm, w_kv, w_b_q, b_sinks, w_b_out):
    w = dict(
        norm_mix_pre=norm_mix_pre, norm_mix_post=norm_mix_post,
        norm_ffn_pre=norm_ffn_pre, norm_ffn_post=norm_ffn_post,
        w_ffn_in=w_ffn_in.astype(BF16), w_ffn_out=w_ffn_out.astype(BF16),
        w_a_in=w_a_in[0].astype(BF16), a_lower_bound=a_lower_bound,
        a_out_norm=a_out_norm[0], w_a_out=w_a_out[0].astype(BF16),
        kv_norm=kv_norm, w_kv=w_kv.astype(BF16), w_b_q=w_b_q[0].astype(BF16),
        b_sinks=b_sinks[0], w_b_out=w_b_out[0].astype(BF16),
    )
    bp, tp, _ = x_prompt.shape
    bs, ts, _ = x_sample.shape

    zero_state = jnp.zeros((bp, A_HEADS, A_DK, A_DV), F32)
    y_p, st_p, kc_p, vc_p = _trunk(
        x_prompt, jnp.arange(tp), zero_state, None, None, w,
        nb=1, tt=ROW_TILE, hgrn_chunk=HGRN_CHUNK, cq=CHUNK)

    nb_s = ROW_TILE // (2 * ts)
    y_s, st_s, kc_s, vc_s = _trunk(
        x_sample, PAST_LEN + jnp.arange(ts), state_hgrn[0],
        cache_k.reshape(bs, WINDOW, KV_WIDTH), cache_v.reshape(bs, WINDOW, KV_WIDTH), w,
        nb=nb_s, tt=ts, hgrn_chunk=ts, cq=ts)

    cache4 = lambda a: a.reshape(a.shape[0], WINDOW, B_KV_HEADS, B_HEAD_DIM)
    return (y_p, y_s, st_p[None], st_s[None],
            cache4(kc_p), cache4(vc_p), cache4(kc_s), cache4(vc_s))
```

```python
import functools

import jax
import jax.numpy as jnp
from jax import lax
from jax.experimental import pallas as pl
from jax.experimental.pallas import tpu as pltpu

F32 = jnp.float32
BF16 = jnp.bfloat16

D_MODEL = 1024
A_HEADS = 8
A_DK = 128
A_DV = 128
B_HEAD_DIM = 64
B_Q_HEADS = 16
B_KV_HEADS = 4
B_GROUP = B_Q_HEADS // B_KV_HEADS
KV_WIDTH = B_KV_HEADS * B_HEAD_DIM
WINDOW = 128
CHUNK = 64
PAST_LEN = 2048
D_FF = 2816
ROPE_THETA = 10000.0
NORM_EPS = 1e-6

LANES = 128
SUBLANES = 8
VMEM_LIMIT_BYTES = 52 * 1024 * 1024

ROW_TILE = 512
HGRN_CHUNK = 128
FFN_CHUNK = 256
MAX_FACTOR_EXPONENT = 60.0
EXACT_BLOCK = 16


def _rms_scale(x):
    ms = jnp.mean(x * x, axis=-1, keepdims=True)
    return x * lax.rsqrt(ms + NORM_EPS)


def _dot(a, b):
    return jnp.dot(a, b, preferred_element_type=F32)


def _dot_nt(a, b):
    return lax.dot_general(a, b, (((1,), (1,)), ((), ())), preferred_element_type=F32)


def _dot_tn(a, b):
    return lax.dot_general(a, b, (((0,), (0,)), ((), ())), preferred_element_type=F32)


def _const_spec(shape):
    nd = len(shape)
    return pl.BlockSpec(shape, lambda *_: (0,) * nd, pipeline_mode=pl.Buffered(1))


def _ffn_kernel(x_ref, npre_ref, npost_ref, win_ref, wout_ref, o_ref):
    x = x_ref[...]
    h = (_rms_scale(x) * npre_ref[...]).astype(BF16)
    acc = jnp.zeros(x.shape, F32)
    for c0 in range(0, D_FF, FFN_CHUNK):
        a = _dot(h, win_ref[:, c0:c0 + FFN_CHUNK])
        b = _dot(h, win_ref[:, D_FF + c0:D_FF + c0 + FFN_CHUNK])
        g = (a * jax.nn.sigmoid(a) * b).astype(BF16)
        acc = acc + _dot(g, wout_ref[c0:c0 + FFN_CHUNK, :])
    o_ref[...] = x + _rms_scale(acc) * npost_ref[...]


def _ffn(x2d, npre, npost, win, wout):
    rows = x2d.shape[0]
    tile = min(ROW_TILE, rows)
    assert rows % tile == 0 and D_FF % FFN_CHUNK == 0
    return pl.pallas_call(
        _ffn_kernel,
        grid=(rows // tile,),
        in_specs=[
            pl.BlockSpec((tile, D_MODEL), lambda i: (i, 0)),
            _const_spec((1, D_MODEL)),
            _const_spec((1, D_MODEL)),
            _const_spec((D_MODEL, 2 * D_FF)),
            _const_spec((D_FF, D_MODEL)),
        ],
        out_specs=pl.BlockSpec((tile, D_MODEL), lambda i: (i, 0)),
        out_shape=jax.ShapeDtypeStruct((rows, D_MODEL), F32),
        compiler_params=pltpu.CompilerParams(
            dimension_semantics=("parallel",), vmem_limit_bytes=VMEM_LIMIT_BYTES),
        name="ffn",
    )(x2d, npre, npost, win, wout)


def _cumsum_rows(x):
    c, w = x.shape
    groups = c // SUBLANES
    y = x.reshape(groups, SUBLANES, w)
    sub = lax.broadcasted_iota(jnp.int32, y.shape, 1)
    shift = 1
    while shift < SUBLANES:
        y = y + jnp.where(sub >= shift, pltpu.roll(y, shift, axis=1), 0.0)
        shift *= 2
    tot = jnp.broadcast_to(y[:, SUBLANES - 1:SUBLANES, :], y.shape)
    inc = tot
    shift = 1
    while shift < groups:
        inc = inc + jnp.concatenate(
            [jnp.zeros((shift, SUBLANES, w), F32), inc[:groups - shift]], axis=0)
        shift *= 2
    return (y + (inc - tot)).reshape(c, w)


def _hgrn_kernel(x_ref, st_in_ref, npre_ref, win_ref, alb_ref, gnorm_ref, wout_ref, npost_ref,
                 y_ref, st_ref, qs_ref, lf_ref, inp_ref, v_ref, sg_ref, on_ref,
                 *, nb, tt, chunk):
    t = pl.program_id(1)
    rows = nb * tt
    nchunks = tt // chunk

    @pl.when(t == 0)
    def _():
        for n in range(nb):
            for h in range(A_HEADS):
                st_ref[n, h] = st_in_ref[n, h].T

    x = x_ref[...].reshape(rows, D_MODEL)
    hn = (_rms_scale(x) * npre_ref[...]).astype(BF16)

    alb = alb_ref[...]
    e = jnp.exp(alb - jnp.max(alb, axis=0, keepdims=True))
    lb = e[0:1] / jnp.sum(e, axis=0, keepdims=True)

    q = _dot(hn, win_ref[:, 0:D_MODEL])
    qs_ref[...] = q * jax.nn.sigmoid(q)
    f = _dot(hn, win_ref[:, D_MODEL:2 * D_MODEL])
    forget = lb + (1.0 - lb) * jax.nn.sigmoid(f)
    lf_ref[...] = jnp.log(forget)
    inp_ref[...] = 1.0 - forget
    v_ref[...] = _dot(hn, win_ref[:, 2 * D_MODEL:3 * D_MODEL]).astype(BF16)
    g = _dot(hn, win_ref[:, 3 * D_MODEL:4 * D_MODEL])
    sg_ref[...] = g * jax.nn.sigmoid(g)

    half = chunk // 2
    ri = lax.broadcasted_iota(jnp.int32, (chunk, chunk), 0)
    ci = lax.broadcasted_iota(jnp.int32, (chunk, chunk), 1)
    causal = ri >= ci

    def chunk_step(idx, carry):
        n = idx // nchunks
        r0 = pl.multiple_of(idx * chunk, chunk)
        rs = pl.ds(r0, chunk)
        for h in range(A_HEADS):
            sl = slice(h * A_DK, (h + 1) * A_DK)
            b = _cumsum_rows(lf_ref[rs, sl])
            bmid = b[half - 1:half]
            blast = b[chunk - 1:chunk]
            eq = jnp.exp(b - bmid)
            ek = jnp.exp(bmid - b)
            qsc = qs_ref[rs, sl] * eq
            inp = inp_ref[rs, sl] * ek
            qt = qsc.astype(BF16)
            kt = inp.astype(BF16)
            qi = (qsc * jnp.exp(bmid)).astype(BF16)
            kh = (inp * jnp.exp(blast - bmid)).astype(BF16)
            vh = v_ref[rs, sl]
            st = st_ref[n, h]
            sc = _dot_nt(qt, kt)
            p = jnp.where(causal, sc, 0.0).astype(BF16)
            o = _dot(p, vh) + _dot_nt(qi, st.astype(BF16))
            st_ref[n, h] = st * jnp.exp(blast) + _dot_tn(vh, kh)
            on = _rms_scale(o) * gnorm_ref[:, sl] * sg_ref[rs, sl]
            on_ref[rs, sl] = on.astype(BF16)
        return carry

    def exact_step(idx, carry):
        n = (idx * EXACT_BLOCK) // tt
        rs = pl.ds(pl.multiple_of(idx * EXACT_BLOCK, EXACT_BLOCK), EXACT_BLOCK)
        row = lax.broadcasted_iota(jnp.int32, (EXACT_BLOCK, A_DK), 0)
        for h in range(A_HEADS):
            sl = slice(h * A_DK, (h + 1) * A_DK)
            b = _cumsum_rows(lf_ref[rs, sl])
            blast = b[EXACT_BLOCK - 1:EXACT_BLOCK]
            qs = qs_ref[rs, sl]
            inp = inp_ref[rs, sl]
            vh = v_ref[rs, sl]
            vf = vh.astype(F32)
            st = st_ref[n, h]
            o = _dot_nt((qs * jnp.exp(b)).astype(BF16), st.astype(BF16))
            for s in range(EXACT_BLOCK):
                decay = jnp.exp(jnp.where(row >= s, b - b[s:s + 1], -jnp.inf))
                score = jnp.sum(qs * decay * inp[s:s + 1], axis=-1, keepdims=True)
                o = o + score * vf[s:s + 1]
            kh = (inp * jnp.exp(blast - b)).astype(BF16)
            st_ref[n, h] = st * jnp.exp(blast) + _dot_tn(vh, kh)
            on = _rms_scale(o) * gnorm_ref[:, sl] * sg_ref[rs, sl]
            on_ref[rs, sl] = on.astype(BF16)
        return carry

    max_step = -jnp.min(lf_ref[...])
    bounded = max_step * half <= MAX_FACTOR_EXPONENT

    @pl.when(bounded)
    def _():
        lax.fori_loop(0, nb * nchunks, chunk_step, 0)

    @pl.when(jnp.logical_not(bounded))
    def _():
        lax.fori_loop(0, rows // EXACT_BLOCK, exact_step, 0)

    mix = _dot(on_ref[...], wout_ref[...])
    y = x + _rms_scale(mix) * npost_ref[...]
    y_ref[...] = y.reshape(nb, tt, D_MODEL)

    @pl.when(t == pl.num_programs(1) - 1)
    def _():
        for n in range(nb):
            for h in range(A_HEADS):
                st_ref[n, h] = st_ref[n, h].T


def _hgrn_mixer(x, state, npre, win, alb, gnorm, wout, npost, *, nb, tt, chunk):
    batch, seq, _ = x.shape
    assert batch % nb == 0 and seq % tt == 0 and tt % chunk == 0 and chunk % (2 * SUBLANES) == 0
    rows = nb * tt
    kern = functools.partial(_hgrn_kernel, nb=nb, tt=tt, chunk=chunk)
    st_spec = pl.BlockSpec((nb, A_HEADS, A_DK, A_DV), lambda b, t: (b, 0, 0, 0))
    return pl.pallas_call(
        kern,
        grid=(batch // nb, seq // tt),
        in_specs=[
            pl.BlockSpec((nb, tt, D_MODEL), lambda b, t: (b, t, 0)),
            st_spec,
            _const_spec((1, D_MODEL)),
            _const_spec((D_MODEL, 4 * D_MODEL)),
            _const_spec(alb.shape),
            _const_spec((1, D_MODEL)),
            _const_spec((D_MODEL, D_MODEL)),
            _const_spec((1, D_MODEL)),
        ],
        out_specs=[
            pl.BlockSpec((nb, tt, D_MODEL), lambda b, t: (b, t, 0)),
            st_spec,
        ],
        out_shape=[
            jax.ShapeDtypeStruct(x.shape, F32),
            jax.ShapeDtypeStruct(state.shape, F32),
        ],
        scratch_shapes=[
            pltpu.VMEM((rows, D_MODEL), F32),
            pltpu.VMEM((rows, D_MODEL), F32),
            pltpu.VMEM((rows, D_MODEL), F32),
            pltpu.VMEM((rows, D_MODEL), BF16),
            pltpu.VMEM((rows, D_MODEL), F32),
            pltpu.VMEM((rows, D_MODEL), BF16),
        ],
        compiler_params=pltpu.CompilerParams(
            dimension_semantics=("parallel", "arbitrary"), vmem_limit_bytes=VMEM_LIMIT_BYTES),
        name="hgrn_mixer",
    )(x, state, npre, win, alb, gnorm, wout, npost)


def _rope_cols(x, cos_t, sin_t):
    lane = lax.broadcasted_iota(jnp.int32, (x.shape[0], LANES), 1)
    first_half = (lane % B_HEAD_DIM) < (B_HEAD_DIM // 2)
    cols = []
    for c0 in range(0, x.shape[1], LANES):
        xc = x[:, c0:c0 + LANES]
        partner = jnp.where(first_half,
                            pltpu.roll(xc, LANES - B_HEAD_DIM // 2, axis=1),
                            pltpu.roll(xc, B_HEAD_DIM // 2, axis=1))
        cols.append(xc * cos_t + partner * sin_t)
    return jnp.concatenate(cols, axis=1)


def _attn_kernel(*refs, nb, tt, cq, has_cache):
    if has_cache:
        (x_ref, kc_in_ref, vc_in_ref, cosq_ref, sinq_ref, cosk_ref, sinkt_ref, sinks_ref,
         npre_ref, kvn_ref, wq_ref, wkv_ref, wo_ref, npost_ref,
         y_ref, kc_ref, vc_ref, q_ref, kd_ref, vl_ref, vh_ref, o_ref) = refs
    else:
        (x_ref, cosq_ref, sinq_ref, cosk_ref, sinkt_ref, sinks_ref,
         npre_ref, kvn_ref, wq_ref, wkv_ref, wo_ref, npost_ref,
         y_ref, kc_ref, vc_ref, q_ref, kd_ref, vl_ref, vh_ref, o_ref) = refs
    t = pl.program_id(1)
    rows = nb * tt
    ext = WINDOW + tt
    nkeys = WINDOW + cq
    nchunks = tt // cq

    @pl.when(t == 0)
    def _():
        if has_cache:
            kc_ref[...] = kc_in_ref[...]
            vc_ref[...] = vc_in_ref[...]
        else:
            kc_ref[...] = jnp.zeros(kc_ref.shape, F32)
            vc_ref[...] = jnp.zeros(vc_ref.shape, F32)

    x = x_ref[...].reshape(rows, D_MODEL)
    xs = _rms_scale(x)
    hq = (xs * npre_ref[...]).astype(BF16)
    hk = (xs * kvn_ref[...]).astype(BF16)

    def per_stream(tab_ref):
        tab = tab_ref[...]
        return tab if nb == 1 else jnp.concatenate([tab] * nb, axis=0)

    q = _dot(hq, wq_ref[...])
    q_ref[...] = _rope_cols(q, per_stream(cosq_ref), per_stream(sinq_ref)).astype(BF16)
    kv = _dot(hk, wkv_ref[...])
    k_new = _rope_cols(kv[:, :KV_WIDTH], per_stream(cosk_ref), per_stream(sinkt_ref))
    v_new = kv[:, KV_WIDTH:]

    lane = lax.broadcasted_iota(jnp.int32, (ext, LANES), 1)
    low = lane < B_HEAD_DIM
    for n in range(nb):
        k_ext = jnp.concatenate([kc_ref[n], k_new[n * tt:(n + 1) * tt]], axis=0)
        v_ext = jnp.concatenate([vc_ref[n], v_new[n * tt:(n + 1) * tt]], axis=0)
        kc_ref[n] = k_ext[ext - WINDOW:]
        vc_ref[n] = v_ext[ext - WINDOW:]
        for m in range(KV_WIDTH // LANES):
            ka = k_ext[:, m * LANES:(m + 1) * LANES]
            kr = pltpu.roll(ka, B_HEAD_DIM, axis=1)
            va = v_ext[:, m * LANES:(m + 1) * LANES]
            vr = pltpu.roll(va, B_HEAD_DIM, axis=1)
            es = pl.ds(n * ext, ext)
            kd_ref[2 * m, es, :] = jnp.where(low, ka, kr).astype(BF16)
            kd_ref[2 * m + 1, es, :] = jnp.where(low, kr, ka).astype(BF16)
            vl_ref[2 * m, es, :] = jnp.where(low, va, 0.0).astype(BF16)
            vh_ref[2 * m, es, :] = jnp.where(low, 0.0, vr).astype(BF16)
            vl_ref[2 * m + 1, es, :] = jnp.where(low, vr, 0.0).astype(BF16)
            vh_ref[2 * m + 1, es, :] = jnp.where(low, 0.0, va).astype(BF16)

    qlane_low = lax.broadcasted_iota(jnp.int32, (cq, LANES), 1) < B_HEAD_DIM
    key_idx = lax.broadcasted_iota(jnp.int32, (1, nkeys), 1)

    def chunk_step(idx, carry):
        n = idx // nchunks
        c = idx % nchunks
        r0 = pl.multiple_of(idx * cq, cq)
        k0 = pl.multiple_of(n * ext + c * cq, SUBLANES * 2)
        if has_cache:
            bias = None
        else:
            first_pos = t * tt + c * cq - WINDOW
            bias = jnp.where(key_idx + first_pos >= 0, 0.0, -jnp.inf)
        for j in range(B_KV_HEADS):
            qa = q_ref[pl.ds(r0, cq), 2 * j * LANES:(2 * j + 1) * LANES]
            qb = q_ref[pl.ds(r0, cq), (2 * j + 1) * LANES:(2 * j + 2) * LANES]
            zero = jnp.zeros_like(qa)
            qstack = jnp.concatenate([
                jnp.where(qlane_low, qa, zero), jnp.where(qlane_low, qb, zero),
                jnp.where(qlane_low, zero, qa), jnp.where(qlane_low, zero, qb)], axis=0)
            s = _dot_nt(qstack, kd_ref[j, pl.ds(k0, nkeys), :])
            if bias is not None:
                s = s + bias
            heads = (4 * j, 4 * j + 2, 4 * j + 1, 4 * j + 3)
            sink = jnp.concatenate(
                [jnp.full((cq, 1), sinks_ref[hd], F32) for hd in heads], axis=0)
            mx = jnp.maximum(jnp.max(s, axis=-1, keepdims=True), sink)
            p = jnp.exp(s - mx)
            denom = jnp.sum(p, axis=-1, keepdims=True) + jnp.exp(sink - mx)
            p = (p / denom).astype(BF16)
            out = (_dot(p[:2 * cq], vl_ref[j, pl.ds(k0, nkeys), :])
                   + _dot(p[2 * cq:], vh_ref[j, pl.ds(k0, nkeys), :]))
            o_ref[pl.ds(r0, cq), 2 * j * LANES:(2 * j + 1) * LANES] = out[:cq].astype(BF16)
            o_ref[pl.ds(r0, cq), (2 * j + 1) * LANES:(2 * j + 2) * LANES] = out[cq:].astype(BF16)
        return carry

    lax.fori_loop(0, nb * nchunks, chunk_step, 0)

    mix = _dot(o_ref[...], wo_ref[...])
    y = x + _rms_scale(mix) * npost_ref[...]
    y_ref[...] = y.reshape(nb, tt, D_MODEL)


def _attn_mixer(x, cache_k, cache_v, tables, sinks, npre, kvn, wq, wkv, wo, npost, *, nb, tt, cq):
    batch, seq, _ = x.shape
    has_cache = cache_k is not None
    assert batch % nb == 0 and seq % tt == 0 and tt % cq == 0 and cq % (2 * SUBLANES) == 0
    rows = nb * tt
    ext = WINDOW + tt
    kern = functools.partial(_attn_kernel, nb=nb, tt=tt, cq=cq, has_cache=has_cache)
    x_spec = pl.BlockSpec((nb, tt, D_MODEL), lambda b, t: (b, t, 0))
    c_spec = pl.BlockSpec((nb, WINDOW, KV_WIDTH), lambda b, t: (b, 0, 0))
    tab_spec = pl.BlockSpec((tt, LANES), lambda b, t: (t, 0))
    in_specs = [x_spec]
    args = [x]
    if has_cache:
        in_specs += [c_spec, c_spec]
        args += [cache_k, cache_v]
    in_specs += [tab_spec] * 4 + [
        pl.BlockSpec(memory_space=pltpu.SMEM),
        _const_spec((1, D_MODEL)),
        _const_spec((1, D_MODEL)),
        _const_spec((D_MODEL, D_MODEL)),
        _const_spec((D_MODEL, 2 * KV_WIDTH)),
        _const_spec((D_MODEL, D_MODEL)),
        _const_spec((1, D_MODEL)),
    ]
    args += list(tables) + [sinks, npre, kvn, wq, wkv, wo, npost]
    cache_shape = jax.ShapeDtypeStruct((batch, WINDOW, KV_WIDTH), F32)
    return pl.pallas_call(
        kern,
        grid=(batch // nb, seq // tt),
        in_specs=in_specs,
        out_specs=[x_spec, c_spec, c_spec],
        out_shape=[jax.ShapeDtypeStruct(x.shape, F32), cache_shape, cache_shape],
        scratch_shapes=[
            pltpu.VMEM((rows, D_MODEL), BF16),
            pltpu.VMEM((B_KV_HEADS, nb * ext, LANES), BF16),
            pltpu.VMEM((B_KV_HEADS, nb * ext, LANES), BF16),
            pltpu.VMEM((B_KV_HEADS, nb * ext, LANES), BF16),
            pltpu.VMEM((rows, D_MODEL), BF16),
        ],
        compiler_params=pltpu.CompilerParams(
            dimension_semantics=("parallel", "arbitrary"), vmem_limit_bytes=VMEM_LIMIT_BYTES),
        name="attn_mixer",
    )(*args)


def _attn_pair_kernel(x_ref, cosq_ref, sinq_ref, cosk_ref, sinkt_ref, sinks_ref,
                      npre_ref, kvn_ref, wq_ref, wkv_ref, wo_ref, npost_ref,
                      y_ref, kc_ref, vc_ref, q_ref, kl_ref, kh_ref, vt_ref, ot_ref, *, tt):
    t = pl.program_id(1)
    ext = WINDOW + tt
    pair = 2 * CHUNK
    nkeys = WINDOW + pair

    @pl.when(t == 0)
    def _():
        kc_ref[...] = jnp.zeros(kc_ref.shape, F32)
        vc_ref[...] = jnp.zeros(vc_ref.shape, F32)

    x = x_ref[0]
    xs = _rms_scale(x)
    hq = (xs * npre_ref[...]).astype(BF16)
    hk = (xs * kvn_ref[...]).astype(BF16)

    q = _dot(hq, wq_ref[...])
    q_ref[...] = _rope_cols(q, cosq_ref[...], sinq_ref[...]).astype(BF16)
    kv = _dot(hk, wkv_ref[...])
    k_ext = jnp.concatenate(
        [kc_ref[0], _rope_cols(kv[:, :KV_WIDTH], cosk_ref[...], sinkt_ref[...])], axis=0)
    v_ext = jnp.concatenate([vc_ref[0], kv[:, KV_WIDTH:]], axis=0)
    kc_ref[0] = k_ext[ext - WINDOW:]
    vc_ref[0] = v_ext[ext - WINDOW:]
    vt_ref[...] = v_ext.T.astype(BF16)

    low = lax.broadcasted_iota(jnp.int32, (ext, LANES), 1) < B_HEAD_DIM
    for m in range(KV_WIDTH // LANES):
        ka = k_ext[:, m * LANES:(m + 1) * LANES]
        kr = pltpu.roll(ka, B_HEAD_DIM, axis=1)
        kl_ref[2 * m] = jnp.where(low, ka, 0.0).astype(BF16)
        kh_ref[2 * m] = jnp.where(low, 0.0, kr).astype(BF16)
        kl_ref[2 * m + 1] = jnp.where(low, kr, 0.0).astype(BF16)
        kh_ref[2 * m + 1] = jnp.where(low, 0.0, ka).astype(BF16)

    key_row = lax.broadcasted_iota(jnp.int32, (nkeys, 2 * pair), 0)
    col = lax.broadcasted_iota(jnp.int32, (nkeys, 2 * pair), 1)
    rel = key_row // CHUNK - (col // CHUNK) % 2
    band_bias = jnp.where(rel >= 0, jnp.where(rel <= WINDOW // CHUNK, 0.0, -jnp.inf), -jnp.inf)
    first_bias = jnp.where(key_row + (t * tt - WINDOW) >= 0, band_bias, -jnp.inf)
    first_head = col < pair

    units = [(j, kx_ref, ha, hb) for j in range(B_KV_HEADS)
             for kx_ref, ha, hb in ((kl_ref, 4 * j, 4 * j + 2), (kh_ref, 4 * j + 1, 4 * j + 3))]

    def scores(e):
        r0 = e * pair
        bias = first_bias if e == 0 else band_bias
        out = []
        for j, kx_ref, _, _ in units:
            qq = jnp.concatenate([q_ref[r0:r0 + pair, 2 * j * LANES:(2 * j + 1) * LANES],
                                  q_ref[r0:r0 + pair, (2 * j + 1) * LANES:(2 * j + 2) * LANES]], axis=0)
            out.append(_dot_nt(kx_ref[j, r0:r0 + nkeys, :], qq) + bias)
        return out

    def softmax(s_list):
        out = []
        for (_, _, ha, hb), s in zip(units, s_list):
            sink = jnp.where(first_head[0:1], sinks_ref[ha], sinks_ref[hb])
            mx = jnp.maximum(jnp.max(s, axis=0, keepdims=True), sink)
            p = jnp.exp(s - mx)
            denom = jnp.sum(p, axis=0, keepdims=True) + jnp.exp(sink - mx)
            out.append((p * (1.0 / denom)).astype(BF16))
        return out

    def weighted_values(e, p_list):
        r0 = e * pair
        for (j, _, ha, hb), p in zip(units, p_list):
            o = _dot(vt_ref[j * B_HEAD_DIM:(j + 1) * B_HEAD_DIM, r0:r0 + nkeys], p).astype(BF16)
            ot_ref[ha * B_HEAD_DIM:(ha + 1) * B_HEAD_DIM, r0:r0 + pair] = o[:, :pair]
            ot_ref[hb * B_HEAD_DIM:(hb + 1) * B_HEAD_DIM, r0:r0 + pair] = o[:, pair:]

    npairs = tt // pair
    s_next = scores(0)
    for e in range(npairs):
        s_cur = s_next
        if e + 1 < npairs:
            s_next = scores(e + 1)
        weighted_values(e, softmax(s_cur))

    mix = _dot_tn(ot_ref[...], wo_ref[...])
    y_ref[0] = x + _rms_scale(mix) * npost_ref[...]


def _attn_pair_mixer(x, tables, sinks, npre, kvn, wq, wkv, wo, npost, *, tt):
    batch, seq, _ = x.shape
    assert seq % tt == 0 and tt % (2 * CHUNK) == 0 and WINDOW == 2 * CHUNK
    ext = WINDOW + tt
    kern = functools.partial(_attn_pair_kernel, tt=tt)
    x_spec = pl.BlockSpec((1, tt, D_MODEL), lambda b, t: (b, t, 0))
    c_spec = pl.BlockSpec((1, WINDOW, KV_WIDTH), lambda b, t: (b, 0, 0))
    tab_spec = pl.BlockSpec((tt, LANES), lambda b, t: (t, 0))
    cache_shape = jax.ShapeDtypeStruct((batch, WINDOW, KV_WIDTH), F32)
    return pl.pallas_call(
        kern,
        grid=(batch, seq // tt),
        in_specs=[x_spec] + [tab_spec] * 4 + [
            pl.BlockSpec(memory_space=pltpu.SMEM),
            _const_spec((1, D_MODEL)),
            _const_spec((1, D_MODEL)),
            _const_spec((D_MODEL, D_MODEL)),
            _const_spec((D_MODEL, 2 * KV_WIDTH)),
            _const_spec((D_MODEL, D_MODEL)),
            _const_spec((1, D_MODEL)),
        ],
        out_specs=[x_spec, c_spec, c_spec],
        out_shape=[jax.ShapeDtypeStruct(x.shape, F32), cache_shape, cache_shape],
        scratch_shapes=[
            pltpu.VMEM((tt, D_MODEL), BF16),
            pltpu.VMEM((B_KV_HEADS, ext, LANES), BF16),
            pltpu.VMEM((B_KV_HEADS, ext, LANES), BF16),
            pltpu.VMEM((KV_WIDTH, ext), BF16),
            pltpu.VMEM((D_MODEL, tt), BF16),
        ],
        compiler_params=pltpu.CompilerParams(
            dimension_semantics=("parallel", "arbitrary"), vmem_limit_bytes=VMEM_LIMIT_BYTES),
        name="attn_pair_mixer",
    )(x, *tables, sinks, npre, kvn, wq, wkv, wo, npost)


def _rope_tables(pos):
    half = B_HEAD_DIM // 2
    inv = ROPE_THETA ** (-jnp.arange(half, dtype=F32) / half)
    ang = pos.astype(F32)[:, None] * inv[None, :]
    cos = jnp.cos(ang)
    sin = jnp.sin(ang)
    reps = LANES // B_HEAD_DIM
    cos_t = jnp.tile(jnp.concatenate([cos, cos], axis=1), (1, reps))
    sin_t = jnp.tile(jnp.concatenate([-sin, sin], axis=1), (1, reps))
    scale = B_HEAD_DIM ** -0.5
    return cos_t * scale, sin_t * scale, cos_t, sin_t


def _trunk(x, pos, state, cache_k, cache_v, w, *, nb, tt, hgrn_chunk, cq):
    batch, seq, _ = x.shape
    row = lambda a: a.reshape(1, D_MODEL)
    x, st = _hgrn_mixer(x, state, row(w["norm_mix_pre"][0]), w["w_a_in"], w["a_lower_bound"],
                        row(w["a_out_norm"]), w["w_a_out"], row(w["norm_mix_post"][0]),
                        nb=nb, tt=tt, chunk=hgrn_chunk)
    x = _ffn(x.reshape(batch * seq, D_MODEL), row(w["norm_ffn_pre"][0]), row(w["norm_ffn_post"][0]),
             w["w_ffn_in"][0], w["w_ffn_out"][0]).reshape(batch, seq, D_MODEL)
    attn_w = (w["b_sinks"], row(w["norm_mix_pre"][1]), row(w["kv_norm"]), w["w_b_q"], w["w_kv"],
              w["w_b_out"], row(w["norm_mix_post"][1]))
    if cache_k is None and nb == 1 and cq == CHUNK:
        x, kc, vc = _attn_pair_mixer(x, _rope_tables(pos), *attn_w, tt=tt)
    else:
        x, kc, vc = _attn_mixer(x, cache_k, cache_v, _rope_tables(pos), *attn_w, nb=nb, tt=tt, cq=cq)
    x = _ffn(x.reshape(batch * seq, D_MODEL), row(w["norm_ffn_pre"][1]), row(w["norm_ffn_post"][1]),
             w["w_ffn_in"][1], w["w_ffn_out"][1]).reshape(batch, seq, D_MODEL)
    return x, st, kc, vc


def kernel(x_prompt, x_sample, state_hgrn, cache_k, cache_v, norm_mix_pre, norm_mix_post, norm_ffn_pre, norm_ffn_post, w_ffn_in, w_ffn_out, w_a_in, a_lower_bound, a_out_norm, w_a_out, kv_norm, w_kv, w_b_q, b_sinks, w_b_out):
    w = dict(
        norm_mix_pre=norm_mix_pre, norm_mix_post=norm_mix_post,
        norm_ffn_pre=norm_ffn_pre, norm_ffn_post=norm_ffn_post,
        w_ffn_in=w_ffn_in.astype(BF16), w_ffn_out=w_ffn_out.astype(BF16),
        w_a_in=w_a_in[0].astype(BF16), a_lower_bound=a_lower_bound,
        a_out_norm=a_out_norm[0], w_a_out=w_a_out[0].astype(BF16),
        kv_norm=kv_norm, w_kv=w_kv.astype(BF16), w_b_q=w_b_q[0].astype(BF16),
        b_sinks=b_sinks[0], w_b_out=w_b_out[0].astype(BF16),
    )
    bp, tp, _ = x_prompt.shape
    bs, ts, _ = x_sample.shape

    zero_state = jnp.zeros((bp, A_HEADS, A_DK, A_DV), F32)
    y_p, st_p, kc_p, vc_p = _trunk(
        x_prompt, jnp.arange(tp), zero_state, None, None, w,
        nb=1, tt=ROW_TILE, hgrn_chunk=HGRN_CHUNK, cq=CHUNK)

    nb_s = ROW_TILE // (2 * ts)
    y_s, st_s, kc_s, vc_s = _trunk(
        x_sample, PAST_LEN + jnp.arange(ts), state_hgrn[0],
        cache_k.reshape(bs, WINDOW, KV_WIDTH), cache_v.reshape(bs, WINDOW, KV_WIDTH), w,
        nb=nb_s, tt=ts, hgrn_chunk=ts, cq=ts)

    cache4 = lambda a: a.reshape(a.shape[0], WINDOW, B_KV_HEADS, B_HEAD_DIM)
    return (y_p, y_s, st_p[None], st_s[None],
            cache4(kc_p), cache4(vc_p), cache4(kc_s), cache4(vc_s))
```

```python
import functools

import jax
import jax.numpy as jnp
from jax import lax
from jax.experimental import pallas as pl
from jax.experimental.pallas import tpu as pltpu

F32 = jnp.float32
BF16 = jnp.bfloat16

D_MODEL = 1024
A_HEADS = 8
A_DK = 128
A_DV = 128
B_HEAD_DIM = 64
B_Q_HEADS = 16
B_KV_HEADS = 4
B_GROUP = B_Q_HEADS // B_KV_HEADS
KV_WIDTH = B_KV_HEADS * B_HEAD_DIM
WINDOW = 128
CHUNK = 64
PAST_LEN = 2048
D_FF = 2816
ROPE_THETA = 10000.0
NORM_EPS = 1e-6
SOFTMAX_SCALE = B_HEAD_DIM ** -0.5
LOG2_E = 1.4426950408889634

LANES = 128
SUBLANES = 8
VMEM_LIMIT_BYTES = 52 * 1024 * 1024

ROW_TILE = 512
HGRN_CHUNK = 128
FFN_CHUNK = 256
MAX_FACTOR_EXPONENT = 60.0
EXACT_BLOCK = 16


def _rms_scale(x):
    ms = jnp.mean(x * x, axis=-1, keepdims=True)
    return x * lax.rsqrt(ms + NORM_EPS)


def _dot(a, b):
    return jnp.dot(a, b, preferred_element_type=F32)


def _dot_nt(a, b):
    return lax.dot_general(a, b, (((1,), (1,)), ((), ())), preferred_element_type=F32)


def _dot_tn(a, b):
    return lax.dot_general(a, b, (((0,), (0,)), ((), ())), preferred_element_type=F32)


def _const_spec(shape):
    nd = len(shape)
    return pl.BlockSpec(shape, lambda *_: (0,) * nd, pipeline_mode=pl.Buffered(1))


def _ffn_kernel(x_ref, npre_ref, npost_ref, win_ref, wout_ref, o_ref):
    x = x_ref[...]
    h = (_rms_scale(x) * npre_ref[...]).astype(BF16)
    acc = jnp.zeros(x.shape, F32)
    for c0 in range(0, D_FF, FFN_CHUNK):
        a = _dot(h, win_ref[:, c0:c0 + FFN_CHUNK])
        b = _dot(h, win_ref[:, D_FF + c0:D_FF + c0 + FFN_CHUNK])
        g = (a * jax.nn.sigmoid(a) * b).astype(BF16)
        acc = acc + _dot(g, wout_ref[c0:c0 + FFN_CHUNK, :])
    o_ref[...] = x + _rms_scale(acc) * npost_ref[...]


def _ffn(x2d, npre, npost, win, wout):
    rows = x2d.shape[0]
    tile = min(ROW_TILE, rows)
    assert rows % tile == 0 and D_FF % FFN_CHUNK == 0
    return pl.pallas_call(
        _ffn_kernel,
        grid=(rows // tile,),
        in_specs=[
            pl.BlockSpec((tile, D_MODEL), lambda i: (i, 0)),
            _const_spec((1, D_MODEL)),
            _const_spec((1, D_MODEL)),
            _const_spec((D_MODEL, 2 * D_FF)),
            _const_spec((D_FF, D_MODEL)),
        ],
        out_specs=pl.BlockSpec((tile, D_MODEL), lambda i: (i, 0)),
        out_shape=jax.ShapeDtypeStruct((rows, D_MODEL), F32),
        compiler_params=pltpu.CompilerParams(
            dimension_semantics=("parallel",), vmem_limit_bytes=VMEM_LIMIT_BYTES),
        name="ffn",
    )(x2d, npre, npost, win, wout)


def _cumsum_rows(x):
    c, w = x.shape
    groups = c // SUBLANES
    y = x.reshape(groups, SUBLANES, w)
    sub = lax.broadcasted_iota(jnp.int32, y.shape, 1)
    shift = 1
    while shift < SUBLANES:
        y = y + jnp.where(sub >= shift, pltpu.roll(y, shift, axis=1), 0.0)
        shift *= 2
    tot = jnp.broadcast_to(y[:, SUBLANES - 1:SUBLANES, :], y.shape)
    inc = tot
    shift = 1
    while shift < groups:
        inc = inc + jnp.concatenate(
            [jnp.zeros((shift, SUBLANES, w), F32), inc[:groups - shift]], axis=0)
        shift *= 2
    return (y + (inc - tot)).reshape(c, w)


def _hgrn_kernel(x_ref, st_in_ref, npre_ref, win_ref, alb_ref, gnorm_ref, wout_ref, npost_ref,
                 y_ref, st_ref, qs_ref, lf_ref, inp_ref, v_ref, sg_ref, on_ref,
                 *, nb, tt, chunk):
    t = pl.program_id(1)
    rows = nb * tt
    nchunks = tt // chunk

    @pl.when(t == 0)
    def _():
        for n in range(nb):
            for h in range(A_HEADS):
                st_ref[n, h] = st_in_ref[n, h].T

    x = x_ref[...].reshape(rows, D_MODEL)
    hn = (_rms_scale(x) * npre_ref[...]).astype(BF16)

    alb = alb_ref[...]
    e = jnp.exp(alb - jnp.max(alb, axis=0, keepdims=True))
    lb = e[0:1] / jnp.sum(e, axis=0, keepdims=True)

    q = _dot(hn, win_ref[:, 0:D_MODEL])
    qs_ref[...] = q * jax.nn.sigmoid(q)
    f = _dot(hn, win_ref[:, D_MODEL:2 * D_MODEL])
    forget = lb + (1.0 - lb) * jax.nn.sigmoid(f)
    lf_ref[...] = jnp.log(forget)
    inp_ref[...] = 1.0 - forget
    v_ref[...] = _dot(hn, win_ref[:, 2 * D_MODEL:3 * D_MODEL]).astype(BF16)
    g = _dot(hn, win_ref[:, 3 * D_MODEL:4 * D_MODEL])
    sg_ref[...] = g * jax.nn.sigmoid(g)

    half = chunk // 2
    ri = lax.broadcasted_iota(jnp.int32, (chunk, chunk), 0)
    ci = lax.broadcasted_iota(jnp.int32, (chunk, chunk), 1)
    causal = ri >= ci

    def chunk_step(idx, carry):
        n = idx // nchunks
        r0 = pl.multiple_of(idx * chunk, chunk)
        rs = pl.ds(r0, chunk)
        for h in range(A_HEADS):
            sl = slice(h * A_DK, (h + 1) * A_DK)
            b = _cumsum_rows(lf_ref[rs, sl])
            bmid = b[half - 1:half]
            blast = b[chunk - 1:chunk]
            eq = jnp.exp(b - bmid)
            ek = jnp.exp(bmid - b)
            qsc = qs_ref[rs, sl] * eq
            inp = inp_ref[rs, sl] * ek
            qt = qsc.astype(BF16)
            kt = inp.astype(BF16)
            qi = (qsc * jnp.exp(bmid)).astype(BF16)
            kh = (inp * jnp.exp(blast - bmid)).astype(BF16)
            vh = v_ref[rs, sl]
            st = st_ref[n, h]
            sc = _dot_nt(qt, kt)
            p = jnp.where(causal, sc, 0.0).astype(BF16)
            o = _dot(p, vh) + _dot_nt(qi, st.astype(BF16))
            st_ref[n, h] = st * jnp.exp(blast) + _dot_tn(vh, kh)
            on = _rms_scale(o) * gnorm_ref[:, sl] * sg_ref[rs, sl]
            on_ref[rs, sl] = on.astype(BF16)
        return carry

    def exact_step(idx, carry):
        n = (idx * EXACT_BLOCK) // tt
        rs = pl.ds(pl.multiple_of(idx * EXACT_BLOCK, EXACT_BLOCK), EXACT_BLOCK)
        row = lax.broadcasted_iota(jnp.int32, (EXACT_BLOCK, A_DK), 0)
        for h in range(A_HEADS):
            sl = slice(h * A_DK, (h + 1) * A_DK)
            b = _cumsum_rows(lf_ref[rs, sl])
            blast = b[EXACT_BLOCK - 1:EXACT_BLOCK]
            qs = qs_ref[rs, sl]
            inp = inp_ref[rs, sl]
            vh = v_ref[rs, sl]
            vf = vh.astype(F32)
            st = st_ref[n, h]
            o = _dot_nt((qs * jnp.exp(b)).astype(BF16), st.astype(BF16))
            for s in range(EXACT_BLOCK):
                decay = jnp.exp(jnp.where(row >= s, b - b[s:s + 1], -jnp.inf))
                score = jnp.sum(qs * decay * inp[s:s + 1], axis=-1, keepdims=True)
                o = o + score * vf[s:s + 1]
            kh = (inp * jnp.exp(blast - b)).astype(BF16)
            st_ref[n, h] = st * jnp.exp(blast) + _dot_tn(vh, kh)
            on = _rms_scale(o) * gnorm_ref[:, sl] * sg_ref[rs, sl]
            on_ref[rs, sl] = on.astype(BF16)
        return carry

    max_step = -jnp.min(lf_ref[...])
    bounded = max_step * half <= MAX_FACTOR_EXPONENT

    @pl.when(bounded)
    def _():
        lax.fori_loop(0, nb * nchunks, chunk_step, 0)

    @pl.when(jnp.logical_not(bounded))
    def _():
        lax.fori_loop(0, rows // EXACT_BLOCK, exact_step, 0)

    mix = _dot(on_ref[...], wout_ref[...])
    y = x + _rms_scale(mix) * npost_ref[...]
    y_ref[...] = y.reshape(nb, tt, D_MODEL)

    @pl.when(t == pl.num_programs(1) - 1)
    def _():
        for n in range(nb):
            for h in range(A_HEADS):
                st_ref[n, h] = st_ref[n, h].T


def _hgrn_mixer(x, state, npre, win, alb, gnorm, wout, npost, *, nb, tt, chunk):
    batch, seq, _ = x.shape
    assert batch % nb == 0 and seq % tt == 0 and tt % chunk == 0 and chunk % (2 * SUBLANES) == 0
    rows = nb * tt
    kern = functools.partial(_hgrn_kernel, nb=nb, tt=tt, chunk=chunk)
    st_spec = pl.BlockSpec((nb, A_HEADS, A_DK, A_DV), lambda b, t: (b, 0, 0, 0))
    return pl.pallas_call(
        kern,
        grid=(batch // nb, seq // tt),
        in_specs=[
            pl.BlockSpec((nb, tt, D_MODEL), lambda b, t: (b, t, 0)),
            st_spec,
            _const_spec((1, D_MODEL)),
            _const_spec((D_MODEL, 4 * D_MODEL)),
            _const_spec(alb.shape),
            _const_spec((1, D_MODEL)),
            _const_spec((D_MODEL, D_MODEL)),
            _const_spec((1, D_MODEL)),
        ],
        out_specs=[
            pl.BlockSpec((nb, tt, D_MODEL), lambda b, t: (b, t, 0)),
            st_spec,
        ],
        out_shape=[
            jax.ShapeDtypeStruct(x.shape, F32),
            jax.ShapeDtypeStruct(state.shape, F32),
        ],
        scratch_shapes=[
            pltpu.VMEM((rows, D_MODEL), F32),
            pltpu.VMEM((rows, D_MODEL), F32),
            pltpu.VMEM((rows, D_MODEL), F32),
            pltpu.VMEM((rows, D_MODEL), BF16),
            pltpu.VMEM((rows, D_MODEL), F32),
            pltpu.VMEM((rows, D_MODEL), BF16),
        ],
        compiler_params=pltpu.CompilerParams(
            dimension_semantics=("parallel", "arbitrary"), vmem_limit_bytes=VMEM_LIMIT_BYTES),
        name="hgrn_mixer",
    )(x, state, npre, win, alb, gnorm, wout, npost)


def _rope_cols(x, cos_t, sin_t):
    lane = lax.broadcasted_iota(jnp.int32, (x.shape[0], LANES), 1)
    first_half = (lane % B_HEAD_DIM) < (B_HEAD_DIM // 2)
    cols = []
    for c0 in range(0, x.shape[1], LANES):
        xc = x[:, c0:c0 + LANES]
        partner = jnp.where(first_half,
                            pltpu.roll(xc, LANES - B_HEAD_DIM // 2, axis=1),
                            pltpu.roll(xc, B_HEAD_DIM // 2, axis=1))
        cols.append(xc * cos_t + partner * sin_t)
    return jnp.concatenate(cols, axis=1)


def _attn_kernel(*refs, nb, tt, cq, has_cache):
    if has_cache:
        (x_ref, kc_in_ref, vc_in_ref, cosq_ref, sinq_ref, cosk_ref, sinkt_ref, sinks_ref,
         npre_ref, kvn_ref, wq_ref, wkv_ref, wo_ref, npost_ref,
         y_ref, kc_ref, vc_ref, q_ref, kd_ref, vl_ref, vh_ref, o_ref) = refs
    else:
        (x_ref, cosq_ref, sinq_ref, cosk_ref, sinkt_ref, sinks_ref,
         npre_ref, kvn_ref, wq_ref, wkv_ref, wo_ref, npost_ref,
         y_ref, kc_ref, vc_ref, q_ref, kd_ref, vl_ref, vh_ref, o_ref) = refs
    t = pl.program_id(1)
    rows = nb * tt
    ext = WINDOW + tt
    nkeys = WINDOW + cq
    nchunks = tt // cq

    @pl.when(t == 0)
    def _():
        if has_cache:
            kc_ref[...] = kc_in_ref[...]
            vc_ref[...] = vc_in_ref[...]
        else:
            kc_ref[...] = jnp.zeros(kc_ref.shape, F32)
            vc_ref[...] = jnp.zeros(vc_ref.shape, F32)

    x = x_ref[...].reshape(rows, D_MODEL)
    xs = _rms_scale(x)
    hq = (xs * npre_ref[...]).astype(BF16)
    hk = (xs * kvn_ref[...]).astype(BF16)

    def per_stream(tab_ref):
        tab = tab_ref[...]
        return tab if nb == 1 else jnp.concatenate([tab] * nb, axis=0)

    q = _dot(hq, wq_ref[...])
    q_ref[...] = _rope_cols(q, per_stream(cosq_ref), per_stream(sinq_ref)).astype(BF16)
    kv = _dot(hk, wkv_ref[...])
    k_new = _rope_cols(kv[:, :KV_WIDTH], per_stream(cosk_ref), per_stream(sinkt_ref))
    v_new = kv[:, KV_WIDTH:]

    lane = lax.broadcasted_iota(jnp.int32, (ext, LANES), 1)
    low = lane < B_HEAD_DIM
    for n in range(nb):
        k_ext = jnp.concatenate([kc_ref[n], k_new[n * tt:(n + 1) * tt]], axis=0)
        v_ext = jnp.concatenate([vc_ref[n], v_new[n * tt:(n + 1) * tt]], axis=0)
        kc_ref[n] = k_ext[ext - WINDOW:]
        vc_ref[n] = v_ext[ext - WINDOW:]
        for m in range(KV_WIDTH // LANES):
            ka = k_ext[:, m * LANES:(m + 1) * LANES]
            kr = pltpu.roll(ka, B_HEAD_DIM, axis=1)
            va = v_ext[:, m * LANES:(m + 1) * LANES]
            vr = pltpu.roll(va, B_HEAD_DIM, axis=1)
            es = pl.ds(n * ext, ext)
            kd_ref[2 * m, es, :] = jnp.where(low, ka, kr).astype(BF16)
            kd_ref[2 * m + 1, es, :] = jnp.where(low, kr, ka).astype(BF16)
            vl_ref[2 * m, es, :] = jnp.where(low, va, 0.0).astype(BF16)
            vh_ref[2 * m, es, :] = jnp.where(low, 0.0, vr).astype(BF16)
            vl_ref[2 * m + 1, es, :] = jnp.where(low, vr, 0.0).astype(BF16)
            vh_ref[2 * m + 1, es, :] = jnp.where(low, 0.0, va).astype(BF16)

    qlane_low = lax.broadcasted_iota(jnp.int32, (cq, LANES), 1) < B_HEAD_DIM
    key_idx = lax.broadcasted_iota(jnp.int32, (1, nkeys), 1)

    def chunk_step(idx, carry):
        n = idx // nchunks
        c = idx % nchunks
        r0 = pl.multiple_of(idx * cq, cq)
        k0 = pl.multiple_of(n * ext + c * cq, SUBLANES * 2)
        if has_cache:
            bias = None
        else:
            first_pos = t * tt + c * cq - WINDOW
            bias = jnp.where(key_idx + first_pos >= 0, 0.0, -jnp.inf)
        for j in range(B_KV_HEADS):
            qa = q_ref[pl.ds(r0, cq), 2 * j * LANES:(2 * j + 1) * LANES]
            qb = q_ref[pl.ds(r0, cq), (2 * j + 1) * LANES:(2 * j + 2) * LANES]
            zero = jnp.zeros_like(qa)
            qstack = jnp.concatenate([
                jnp.where(qlane_low, qa, zero), jnp.where(qlane_low, qb, zero),
                jnp.where(qlane_low, zero, qa), jnp.where(qlane_low, zero, qb)], axis=0)
            s = _dot_nt(qstack, kd_ref[j, pl.ds(k0, nkeys), :])
            if bias is not None:
                s = s + bias
            heads = (4 * j, 4 * j + 2, 4 * j + 1, 4 * j + 3)
            sink = jnp.concatenate(
                [jnp.full((cq, 1), sinks_ref[hd], F32) for hd in heads], axis=0)
            mx = jnp.maximum(jnp.max(s, axis=-1, keepdims=True), sink)
            p = jnp.exp(s - mx)
            denom = jnp.sum(p, axis=-1, keepdims=True) + jnp.exp(sink - mx)
            p = (p / denom).astype(BF16)
            out = (_dot(p[:2 * cq], vl_ref[j, pl.ds(k0, nkeys), :])
                   + _dot(p[2 * cq:], vh_ref[j, pl.ds(k0, nkeys), :]))
            o_ref[pl.ds(r0, cq), 2 * j * LANES:(2 * j + 1) * LANES] = out[:cq].astype(BF16)
            o_ref[pl.ds(r0, cq), (2 * j + 1) * LANES:(2 * j + 2) * LANES] = out[cq:].astype(BF16)
        return carry

    lax.fori_loop(0, nb * nchunks, chunk_step, 0)

    mix = _dot(o_ref[...], wo_ref[...])
    y = x + _rms_scale(mix) * npost_ref[...]
    y_ref[...] = y.reshape(nb, tt, D_MODEL)


def _attn_mixer(x, cache_k, cache_v, tables, sinks, npre, kvn, wq, wkv, wo, npost, *, nb, tt, cq):
    batch, seq, _ = x.shape
    has_cache = cache_k is not None
    assert batch % nb == 0 and seq % tt == 0 and tt % cq == 0 and cq % (2 * SUBLANES) == 0
    rows = nb * tt
    ext = WINDOW + tt
    kern = functools.partial(_attn_kernel, nb=nb, tt=tt, cq=cq, has_cache=has_cache)
    x_spec = pl.BlockSpec((nb, tt, D_MODEL), lambda b, t: (b, t, 0))
    c_spec = pl.BlockSpec((nb, WINDOW, KV_WIDTH), lambda b, t: (b, 0, 0))
    tab_spec = pl.BlockSpec((tt, LANES), lambda b, t: (t, 0))
    in_specs = [x_spec]
    args = [x]
    if has_cache:
        in_specs += [c_spec, c_spec]
        args += [cache_k, cache_v]
    in_specs += [tab_spec] * 4 + [
        pl.BlockSpec(memory_space=pltpu.SMEM),
        _const_spec((1, D_MODEL)),
        _const_spec((1, D_MODEL)),
        _const_spec((D_MODEL, D_MODEL)),
        _const_spec((D_MODEL, 2 * KV_WIDTH)),
        _const_spec((D_MODEL, D_MODEL)),
        _const_spec((1, D_MODEL)),
    ]
    args += list(tables) + [sinks, npre, kvn, wq, wkv, wo, npost]
    cache_shape = jax.ShapeDtypeStruct((batch, WINDOW, KV_WIDTH), F32)
    return pl.pallas_call(
        kern,
        grid=(batch // nb, seq // tt),
        in_specs=in_specs,
        out_specs=[x_spec, c_spec, c_spec],
        out_shape=[jax.ShapeDtypeStruct(x.shape, F32), cache_shape, cache_shape],
        scratch_shapes=[
            pltpu.VMEM((rows, D_MODEL), BF16),
            pltpu.VMEM((B_KV_HEADS, nb * ext, LANES), BF16),
            pltpu.VMEM((B_KV_HEADS, nb * ext, LANES), BF16),
            pltpu.VMEM((B_KV_HEADS, nb * ext, LANES), BF16),
            pltpu.VMEM((rows, D_MODEL), BF16),
        ],
        compiler_params=pltpu.CompilerParams(
            dimension_semantics=("parallel", "arbitrary"), vmem_limit_bytes=VMEM_LIMIT_BYTES),
        name="attn_mixer",
    )(*args)


def _attn_pair_kernel(x_ref, cosq_ref, sinq_ref, cosk_ref, sinkt_ref, sinks_ref,
                      npre_ref, kvn_ref, wq_ref, wkv_ref, wo_ref, npost_ref,
                      y_ref, kc_ref, vc_ref, q_ref, kl_ref, kh_ref, vt_ref, ot_ref, *, tt):
    t = pl.program_id(1)
    ext = WINDOW + tt
    pair = 2 * CHUNK
    nkeys = WINDOW + pair
    nvis = WINDOW + CHUNK

    @pl.when(t == 0)
    def _():
        kc_ref[...] = jnp.zeros(kc_ref.shape, F32)
        vc_ref[...] = jnp.zeros(vc_ref.shape, F32)

    x = x_ref[0]
    xs = _rms_scale(x)
    hq = (xs * npre_ref[...]).astype(BF16)
    hk = (xs * kvn_ref[...]).astype(BF16)

    q = _dot(hq, wq_ref[...])
    q_ref[...] = _rope_cols(q, cosq_ref[...], sinq_ref[...]).astype(BF16)
    kv = _dot(hk, wkv_ref[...])
    k_ext = jnp.concatenate(
        [kc_ref[0], _rope_cols(kv[:, :KV_WIDTH], cosk_ref[...], sinkt_ref[...])], axis=0)
    v_ext = jnp.concatenate([vc_ref[0], kv[:, KV_WIDTH:]], axis=0)
    kc_ref[0] = k_ext[ext - WINDOW:]
    vc_ref[0] = v_ext[ext - WINDOW:]
    vt_ref[...] = v_ext.T.astype(BF16)

    low = lax.broadcasted_iota(jnp.int32, (ext, LANES), 1) < B_HEAD_DIM
    for m in range(KV_WIDTH // LANES):
        ka = k_ext[:, m * LANES:(m + 1) * LANES]
        kr = pltpu.roll(ka, B_HEAD_DIM, axis=1)
        kl_ref[2 * m] = jnp.where(low, ka, 0.0).astype(BF16)
        kh_ref[2 * m] = jnp.where(low, 0.0, kr).astype(BF16)
        kl_ref[2 * m + 1] = jnp.where(low, kr, 0.0).astype(BF16)
        kh_ref[2 * m + 1] = jnp.where(low, 0.0, ka).astype(BF16)

    vis_row = lax.broadcasted_iota(jnp.int32, (nvis, LANES), 0)
    first_bias = [jnp.where(vis_row + (t * tt - WINDOW + qc * CHUNK) >= 0, 0.0, -jnp.inf)
                  for qc in range(2)]
    first_head = lax.broadcasted_iota(jnp.int32, (1, LANES), 1) < CHUNK
    no_keys = jnp.zeros((CHUNK, LANES), BF16)

    units = [(j, kx_ref, ha, hb) for j in range(B_KV_HEADS)
             for kx_ref, ha, hb in ((kl_ref, 4 * j, 4 * j + 2), (kh_ref, 4 * j + 1, 4 * j + 3))]

    def scores(e):
        r0 = e * pair
        out = []
        for j, kx_ref, _, _ in units:
            ca = slice(2 * j * LANES, (2 * j + 1) * LANES)
            cb = slice((2 * j + 1) * LANES, (2 * j + 2) * LANES)
            qq = jnp.concatenate([q_ref[r0:r0 + CHUNK, ca], q_ref[r0:r0 + CHUNK, cb],
                                  q_ref[r0 + CHUNK:r0 + pair, ca], q_ref[r0 + CHUNK:r0 + pair, cb]], axis=0)
            out.append(_dot_nt(kx_ref[j, r0:r0 + nkeys, :], qq))
        return out

    def softmax(e, s_list):
        out = []
        for (_, _, ha, hb), s in zip(units, s_list):
            sink = jnp.where(first_head, sinks_ref[ha], sinks_ref[hb]) * LOG2_E
            ps, dens = [], []
            for qc in range(2):
                sq = s[qc * CHUNK:qc * CHUNK + nvis, qc * LANES:(qc + 1) * LANES]
                if e == 0:
                    sq = sq + first_bias[qc]
                mx = jnp.maximum(jnp.max(sq, axis=0, keepdims=True), sink)
                p = jnp.exp2(sq - mx)
                dens.append(jnp.sum(p, axis=0, keepdims=True) + jnp.exp2(sink - mx))
                ps.append(p.astype(BF16))
            pmat = jnp.concatenate([jnp.concatenate([ps[0], no_keys], axis=0),
                                    jnp.concatenate([no_keys, ps[1]], axis=0)], axis=1)
            out.append((pmat, jnp.concatenate(dens, axis=1)))
        return out

    def weighted_values(e, pd_list):
        r0 = e * pair
        for (j, _, ha, hb), (pmat, den) in zip(units, pd_list):
            o = _dot(vt_ref[j * B_HEAD_DIM:(j + 1) * B_HEAD_DIM, r0:r0 + nkeys], pmat)
            o = (o * (1.0 / den)).astype(BF16)
            for qc in range(2):
                c0 = r0 + qc * CHUNK
                ot_ref[ha * B_HEAD_DIM:(ha + 1) * B_HEAD_DIM, c0:c0 + CHUNK] = (
                    o[:, qc * LANES:qc * LANES + CHUNK])
                ot_ref[hb * B_HEAD_DIM:(hb + 1) * B_HEAD_DIM, c0:c0 + CHUNK] = (
                    o[:, qc * LANES + CHUNK:(qc + 1) * LANES])

    npairs = tt // pair
    s_next = scores(0)
    for e in range(npairs):
        s_cur = s_next
        if e + 1 < npairs:
            s_next = scores(e + 1)
        weighted_values(e, softmax(e, s_cur))

    mix = _dot_tn(ot_ref[...], wo_ref[...])
    y_ref[0] = x + _rms_scale(mix) * npost_ref[...]


def _attn_pair_mixer(x, tables, sinks, npre, kvn, wq, wkv, wo, npost, *, tt):
    batch, seq, _ = x.shape
    assert seq % tt == 0 and tt % (2 * CHUNK) == 0 and WINDOW == 2 * CHUNK
    ext = WINDOW + tt
    kern = functools.partial(_attn_pair_kernel, tt=tt)
    x_spec = pl.BlockSpec((1, tt, D_MODEL), lambda b, t: (b, t, 0))
    c_spec = pl.BlockSpec((1, WINDOW, KV_WIDTH), lambda b, t: (b, 0, 0))
    tab_spec = pl.BlockSpec((tt, LANES), lambda b, t: (t, 0))
    cache_shape = jax.ShapeDtypeStruct((batch, WINDOW, KV_WIDTH), F32)
    return pl.pallas_call(
        kern,
        grid=(batch, seq // tt),
        in_specs=[x_spec] + [tab_spec] * 4 + [
            pl.BlockSpec(memory_space=pltpu.SMEM),
            _const_spec((1, D_MODEL)),
            _const_spec((1, D_MODEL)),
            _const_spec((D_MODEL, D_MODEL)),
            _const_spec((D_MODEL, 2 * KV_WIDTH)),
            _const_spec((D_MODEL, D_MODEL)),
            _const_spec((1, D_MODEL)),
        ],
        out_specs=[x_spec, c_spec, c_spec],
        out_shape=[jax.ShapeDtypeStruct(x.shape, F32), cache_shape, cache_shape],
        scratch_shapes=[
            pltpu.VMEM((tt, D_MODEL), BF16),
            pltpu.VMEM((B_KV_HEADS, ext, LANES), BF16),
            pltpu.VMEM((B_KV_HEADS, ext, LANES), BF16),
            pltpu.VMEM((KV_WIDTH, ext), BF16),
            pltpu.VMEM((D_MODEL, tt), BF16),
        ],
        compiler_params=pltpu.CompilerParams(
            dimension_semantics=("parallel", "arbitrary"), vmem_limit_bytes=VMEM_LIMIT_BYTES),
        name="attn_pair_mixer",
    )(x, *tables, sinks, npre, kvn, wq, wkv, wo, npost)


def _rope_tables(pos, q_scale):
    half = B_HEAD_DIM // 2
    inv = ROPE_THETA ** (-jnp.arange(half, dtype=F32) / half)
    ang = pos.astype(F32)[:, None] * inv[None, :]
    cos = jnp.cos(ang)
    sin = jnp.sin(ang)
    reps = LANES // B_HEAD_DIM
    cos_t = jnp.tile(jnp.concatenate([cos, cos], axis=1), (1, reps))
    sin_t = jnp.tile(jnp.concatenate([-sin, sin], axis=1), (1, reps))
    return cos_t * q_scale, sin_t * q_scale, cos_t, sin_t


def _trunk(x, pos, state, cache_k, cache_v, w, *, nb, tt, hgrn_chunk, cq):
    batch, seq, _ = x.shape
    row = lambda a: a.reshape(1, D_MODEL)
    x, st = _hgrn_mixer(x, state, row(w["norm_mix_pre"][0]), w["w_a_in"], w["a_lower_bound"],
                        row(w["a_out_norm"]), w["w_a_out"], row(w["norm_mix_post"][0]),
                        nb=nb, tt=tt, chunk=hgrn_chunk)
    x = _ffn(x.reshape(batch * seq, D_MODEL), row(w["norm_ffn_pre"][0]), row(w["norm_ffn_post"][0]),
             w["w_ffn_in"][0], w["w_ffn_out"][0]).reshape(batch, seq, D_MODEL)
    attn_w = (w["b_sinks"], row(w["norm_mix_pre"][1]), row(w["kv_norm"]), w["w_b_q"], w["w_kv"],
              w["w_b_out"], row(w["norm_mix_post"][1]))
    if cache_k is None and nb == 1 and cq == CHUNK:
        x, kc, vc = _attn_pair_mixer(x, _rope_tables(pos, SOFTMAX_SCALE * LOG2_E), *attn_w, tt=tt)
    else:
        x, kc, vc = _attn_mixer(x, cache_k, cache_v, _rope_tables(pos, SOFTMAX_SCALE), *attn_w,
                                nb=nb, tt=tt, cq=cq)
    x = _ffn(x.reshape(batch * seq, D_MODEL), row(w["norm_ffn_pre"][1]), row(w["norm_ffn_post"][1]),
             w["w_ffn_in"][1], w["w_ffn_out"][1]).reshape(batch, seq, D_MODEL)
    return x, st, kc, vc


def kernel(x_prompt, x_sample, state_hgrn, cache_k, cache_v, norm_mix_pre, norm_mix_post, norm_ffn_pre, norm_ffn_post, w_ffn_in, w_ffn_out, w_a_in, a_lower_bound, a_out_norm, w_a_out, kv_norm, w_kv, w_b_q, b_sinks, w_b_out):
    w = dict(
        norm_mix_pre=norm_mix_pre, norm_mix_post=norm_mix_post,
        norm_ffn_pre=norm_ffn_pre, norm_ffn_post=norm_ffn_post,
        w_ffn_in=w_ffn_in.astype(BF16), w_ffn_out=w_ffn_out.astype(BF16),
        w_a_in=w_a_in[0].astype(BF16), a_lower_bound=a_lower_bound,
        a_out_norm=a_out_norm[0], w_a_out=w_a_out[0].astype(BF16),
        kv_norm=kv_norm, w_kv=w_kv.astype(BF16), w_b_q=w_b_q[0].astype(BF16),
        b_sinks=b_sinks[0], w_b_out=w_b_out[0].astype(BF16),
    )
    bp, tp, _ = x_prompt.shape
    bs, ts, _ = x_sample.shape

    zero_state = jnp.zeros((bp, A_HEADS, A_DK, A_DV), F32)
    y_p, st_p, kc_p, vc_p = _trunk(
        x_prompt, jnp.arange(tp), zero_state, None, None, w,
        nb=1, tt=ROW_TILE, hgrn_chunk=HGRN_CHUNK, cq=CHUNK)

    nb_s = ROW_TILE // (2 * ts)
    y_s, st_s, kc_s, vc_s = _trunk(
        x_sample, PAST_LEN + jnp.arange(ts), state_hgrn[0],
        cache_k.reshape(bs, WINDOW, KV_WIDTH), cache_v.reshape(bs, WINDOW, KV_WIDTH), w,
        nb=nb_s, tt=ts, hgrn_chunk=ts, cq=ts)

    cache4 = lambda a: a.reshape(a.shape[0], WINDOW, B_KV_HEADS, B_HEAD_DIM)
    return (y_p, y_s, st_p[None], st_s[None],
            cache4(kc_p), cache4(vc_p), cache4(kc_s), cache4(vc_s))
```

```python
import functools

import jax
import jax.numpy as jnp
from jax import lax
from jax.experimental import pallas as pl
from jax.experimental.pallas import tpu as pltpu

F32 = jnp.float32
BF16 = jnp.bfloat16

D_MODEL = 1024
A_HEADS = 8
A_DK = 128
A_DV = 128
B_HEAD_DIM = 64
B_Q_HEADS = 16
B_KV_HEADS = 4
B_GROUP = B_Q_HEADS // B_KV_HEADS
KV_WIDTH = B_KV_HEADS * B_HEAD_DIM
WINDOW = 128
CHUNK = 64
PAST_LEN = 2048
D_FF = 2816
ROPE_THETA = 10000.0
NORM_EPS = 1e-6
SOFTMAX_SCALE = B_HEAD_DIM ** -0.5
LOG2_E = 1.4426950408889634

LANES = 128
SUBLANES = 8
VMEM_LIMIT_BYTES = 52 * 1024 * 1024

ROW_TILE = 512
HGRN_CHUNK = 128
FFN_CHUNK = 256
MAX_FACTOR_EXPONENT = 60.0
EXACT_BLOCK = 16
STAGE_ROWS = 128


def _rms_scale(x):
    ms = jnp.mean(x * x, axis=-1, keepdims=True)
    return x * lax.rsqrt(ms + NORM_EPS)


def _dot(a, b):
    return jnp.dot(a, b, preferred_element_type=F32)


def _dot_nt(a, b):
    return lax.dot_general(a, b, (((1,), (1,)), ((), ())), preferred_element_type=F32)


def _dot_tn(a, b):
    return lax.dot_general(a, b, (((0,), (0,)), ((), ())), preferred_element_type=F32)


def _const_spec(shape):
    nd = len(shape)
    return pl.BlockSpec(shape, lambda *_: (0,) * nd, pipeline_mode=pl.Buffered(1))


def _ffn_kernel(x_ref, npre_ref, npost_ref, win_ref, wout_ref, o_ref):
    x = x_ref[...]
    h = (_rms_scale(x) * npre_ref[...]).astype(BF16)
    acc = jnp.zeros(x.shape, F32)
    for c0 in range(0, D_FF, FFN_CHUNK):
        a = _dot(h, win_ref[:, c0:c0 + FFN_CHUNK])
        b = _dot(h, win_ref[:, D_FF + c0:D_FF + c0 + FFN_CHUNK])
        g = (a * jax.nn.sigmoid(a) * b).astype(BF16)
        acc = acc + _dot(g, wout_ref[c0:c0 + FFN_CHUNK, :])
    o_ref[...] = x + _rms_scale(acc) * npost_ref[...]


def _ffn(x2d, npre, npost, win, wout):
    rows = x2d.shape[0]
    tile = min(ROW_TILE, rows)
    assert rows % tile == 0 and D_FF % FFN_CHUNK == 0
    return pl.pallas_call(
        _ffn_kernel,
        grid=(rows // tile,),
        in_specs=[
            pl.BlockSpec((tile, D_MODEL), lambda i: (i, 0)),
            _const_spec((1, D_MODEL)),
            _const_spec((1, D_MODEL)),
            _const_spec((D_MODEL, 2 * D_FF)),
            _const_spec((D_FF, D_MODEL)),
        ],
        out_specs=pl.BlockSpec((tile, D_MODEL), lambda i: (i, 0)),
        out_shape=jax.ShapeDtypeStruct((rows, D_MODEL), F32),
        compiler_params=pltpu.CompilerParams(
            dimension_semantics=("parallel",), vmem_limit_bytes=VMEM_LIMIT_BYTES),
        name="ffn",
    )(x2d, npre, npost, win, wout)


def _cumsum_rows(x):
    c, w = x.shape
    groups = c // SUBLANES
    y = x.reshape(groups, SUBLANES, w)
    sub = lax.broadcasted_iota(jnp.int32, y.shape, 1)
    shift = 1
    while shift < SUBLANES:
        y = y + jnp.where(sub >= shift, pltpu.roll(y, shift, axis=1), 0.0)
        shift *= 2
    tot = jnp.broadcast_to(y[:, SUBLANES - 1:SUBLANES, :], y.shape)
    inc = tot
    shift = 1
    while shift < groups:
        inc = inc + jnp.concatenate(
            [jnp.zeros((shift, SUBLANES, w), F32), inc[:groups - shift]], axis=0)
        shift *= 2
    return (y + (inc - tot)).reshape(c, w)


def _hgrn_kernel(x_ref, st_in_ref, npre_ref, win_ref, alb_ref, gnorm_ref, wout_ref, npost_ref,
                 y_ref, st_ref, qs_ref, lf_ref, inp_ref, v_ref, sg_ref, on_ref,
                 *, nb, tt, chunk):
    t = pl.program_id(1)
    rows = nb * tt

    @pl.when(t == 0)
    def _():
        for n in range(nb):
            for h in range(A_HEADS):
                st_ref[n, h] = st_in_ref[n, h].T

    x = x_ref[...].reshape(rows, D_MODEL)
    hn = (_rms_scale(x) * npre_ref[...]).astype(BF16)

    alb = alb_ref[...]
    e = jnp.exp(alb - jnp.max(alb, axis=0, keepdims=True))
    lb = e[0:1] / jnp.sum(e, axis=0, keepdims=True)

    f = _dot(hn, win_ref[:, D_MODEL:2 * D_MODEL])
    forget = lb + (1.0 - lb) * jax.nn.sigmoid(f)
    lf_ref[...] = jnp.log(forget)
    inp_ref[...] = 1.0 - forget

    def project_query(c0, width):
        q = _dot(hn, win_ref[:, c0:c0 + width])
        qs_ref[:, c0:c0 + width] = q * jax.nn.sigmoid(q)

    def project_value(c0, width):
        v_ref[:, c0:c0 + width] = _dot(
            hn, win_ref[:, 2 * D_MODEL + c0:2 * D_MODEL + c0 + width]).astype(BF16)

    def project_gate(c0, width):
        g = _dot(hn, win_ref[:, 3 * D_MODEL + c0:3 * D_MODEL + c0 + width])
        sg_ref[:, c0:c0 + width] = g * jax.nn.sigmoid(g)

    def write_rows(mix):
        y = x + _rms_scale(mix) * npost_ref[...]
        y_ref[...] = y.reshape(nb, tt, D_MODEL)

    half = chunk // 2
    ri = lax.broadcasted_iota(jnp.int32, (chunk, chunk), 0)
    ci = lax.broadcasted_iota(jnp.int32, (chunk, chunk), 1)
    causal = ri >= ci

    def prepare(blk, h):
        rs = slice(blk * chunk, (blk + 1) * chunk)
        sl = slice(h * A_DK, (h + 1) * A_DK)
        b = _cumsum_rows(lf_ref[rs, sl])
        bmid = b[half - 1:half]
        blast = b[chunk - 1:chunk]
        qsc = qs_ref[rs, sl] * jnp.exp(b - bmid)
        inp = inp_ref[rs, sl] * jnp.exp(bmid - b)
        return dict(
            rs=rs, sl=sl, qt=qsc.astype(BF16), kt=inp.astype(BF16),
            qi=(qsc * jnp.exp(bmid)).astype(BF16),
            kh=(inp * jnp.exp(blast - bmid)).astype(BF16),
            vh=v_ref[rs, sl], decay=jnp.exp(blast))

    def emit_output(u, sc, st):
        p = jnp.where(causal, sc, 0.0).astype(BF16)
        o = _dot(p, u["vh"]) + _dot_nt(u["qi"], st.astype(BF16))
        on = _rms_scale(o) * gnorm_ref[:, u["sl"]] * sg_ref[u["rs"], u["sl"]]
        on_ref[u["rs"], u["sl"]] = on.astype(BF16)

    def recurrence(heads, jobs):
        blocks_per_stage = STAGE_ROWS // chunk
        nstages = rows // STAGE_ROWS
        slots = 2 * nstages
        jobs = list(jobs)

        def run_jobs(slot):
            for _ in range(-(-len(jobs) // (slots - slot))):
                jobs.pop(0)()

        states = {}
        for stage in range(nstages):
            blocks = range(stage * blocks_per_stage, (stage + 1) * blocks_per_stage)
            units = [(blk, h, prepare(blk, h)) for blk in blocks for h in heads]
            scores = [_dot_nt(u["qt"], u["kt"]) for _, _, u in units]
            run_jobs(2 * stage)
            updated = {}
            for blk, h, u in units:
                key = ((blk * chunk) // tt, h)
                if key not in states:
                    states[key] = st_ref[key[0], h]
                updated[key] = states[key] * u["decay"] + _dot_tn(u["vh"], u["kh"])
            run_jobs(2 * stage + 1)
            for (blk, h, u), sc in zip(units, scores):
                emit_output(u, sc, states[((blk * chunk) // tt, h)])
            states.update(updated)
            if (blocks[-1] + 1) * chunk % tt == 0:
                for (n, h) in list(states):
                    st_ref[n, h] = states.pop((n, h))

    def factorised_path():
        gw = D_MODEL // 2
        piece = 2 * LANES
        project_query(0, gw)
        project_value(0, gw)
        project_gate(0, gw)
        jobs = [functools.partial(proj, gw + c0, piece)
                for proj in (project_query, project_value, project_gate)
                for c0 in range(0, gw, piece)]
        recurrence(range(0, A_HEADS // 2), jobs)
        partial_mix = []
        jobs = [lambda c0=c0: partial_mix.append(_dot(on_ref[:, :gw], wout_ref[:gw, c0:c0 + piece]))
                for c0 in range(0, D_MODEL, piece)]
        recurrence(range(A_HEADS // 2, A_HEADS), jobs)
        write_rows(jnp.concatenate(partial_mix, axis=1) + _dot(on_ref[:, gw:], wout_ref[gw:, :]))

    def exact_step(idx, carry):
        n = (idx * EXACT_BLOCK) // tt
        rs = pl.ds(pl.multiple_of(idx * EXACT_BLOCK, EXACT_BLOCK), EXACT_BLOCK)
        row = lax.broadcasted_iota(jnp.int32, (EXACT_BLOCK, A_DK), 0)
        for h in range(A_HEADS):
            sl = slice(h * A_DK, (h + 1) * A_DK)
            b = _cumsum_rows(lf_ref[rs, sl])
            blast = b[EXACT_BLOCK - 1:EXACT_BLOCK]
            qs = qs_ref[rs, sl]
            inp = inp_ref[rs, sl]
            vh = v_ref[rs, sl]
            vf = vh.astype(F32)
            st = st_ref[n, h]
            o = _dot_nt((qs * jnp.exp(b)).astype(BF16), st.astype(BF16))
            for s in range(EXACT_BLOCK):
                decay = jnp.exp(jnp.where(row >= s, b - b[s:s + 1], -jnp.inf))
                score = jnp.sum(qs * decay * inp[s:s + 1], axis=-1, keepdims=True)
                o = o + score * vf[s:s + 1]
            kh = (inp * jnp.exp(blast - b)).astype(BF16)
            st_ref[n, h] = st * jnp.exp(blast) + _dot_tn(vh, kh)
            on = _rms_scale(o) * gnorm_ref[:, sl] * sg_ref[rs, sl]
            on_ref[rs, sl] = on.astype(BF16)
        return carry

    max_step = -jnp.min(lf_ref[...])
    bounded = max_step * half <= MAX_FACTOR_EXPONENT

    pl.when(bounded)(factorised_path)

    @pl.when(jnp.logical_not(bounded))
    def _():
        project_query(0, D_MODEL)
        project_value(0, D_MODEL)
        project_gate(0, D_MODEL)
        lax.fori_loop(0, rows // EXACT_BLOCK, exact_step, 0)
        write_rows(_dot(on_ref[...], wout_ref[...]))

    @pl.when(t == pl.num_programs(1) - 1)
    def _():
        for n in range(nb):
            for h in range(A_HEADS):
                st_ref[n, h] = st_ref[n, h].T


def _hgrn_mixer(x, state, npre, win, alb, gnorm, wout, npost, *, nb, tt, chunk):
    batch, seq, _ = x.shape
    assert batch % nb == 0 and seq % tt == 0 and tt % chunk == 0 and chunk % (2 * SUBLANES) == 0
    rows = nb * tt
    assert rows % STAGE_ROWS == 0 and STAGE_ROWS % chunk == 0
    assert (nb == 1 and tt % STAGE_ROWS == 0) or STAGE_ROWS % tt == 0
    kern = functools.partial(_hgrn_kernel, nb=nb, tt=tt, chunk=chunk)
    st_spec = pl.BlockSpec((nb, A_HEADS, A_DK, A_DV), lambda b, t: (b, 0, 0, 0))
    return pl.pallas_call(
        kern,
        grid=(batch // nb, seq // tt),
        in_specs=[
            pl.BlockSpec((nb, tt, D_MODEL), lambda b, t: (b, t, 0)),
            st_spec,
            _const_spec((1, D_MODEL)),
            _const_spec((D_MODEL, 4 * D_MODEL)),
            _const_spec(alb.shape),
            _const_spec((1, D_MODEL)),
            _const_spec((D_MODEL, D_MODEL)),
            _const_spec((1, D_MODEL)),
        ],
        out_specs=[
            pl.BlockSpec((nb, tt, D_MODEL), lambda b, t: (b, t, 0)),
            st_spec,
        ],
        out_shape=[
            jax.ShapeDtypeStruct(x.shape, F32),
            jax.ShapeDtypeStruct(state.shape, F32),
        ],
        scratch_shapes=[
            pltpu.VMEM((rows, D_MODEL), F32),
            pltpu.VMEM((rows, D_MODEL), F32),
            pltpu.VMEM((rows, D_MODEL), F32),
            pltpu.VMEM((rows, D_MODEL), BF16),
            pltpu.VMEM((rows, D_MODEL), F32),
            pltpu.VMEM((rows, D_MODEL), BF16),
        ],
        compiler_params=pltpu.CompilerParams(
            dimension_semantics=("parallel", "arbitrary"), vmem_limit_bytes=VMEM_LIMIT_BYTES),
        name="hgrn_mixer",
    )(x, state, npre, win, alb, gnorm, wout, npost)


def _rope_cols(x, cos_t, sin_t):
    lane = lax.broadcasted_iota(jnp.int32, (x.shape[0], LANES), 1)
    first_half = (lane % B_HEAD_DIM) < (B_HEAD_DIM // 2)
    cols = []
    for c0 in range(0, x.shape[1], LANES):
        xc = x[:, c0:c0 + LANES]
        partner = jnp.where(first_half,
                            pltpu.roll(xc, LANES - B_HEAD_DIM // 2, axis=1),
                            pltpu.roll(xc, B_HEAD_DIM // 2, axis=1))
        cols.append(xc * cos_t + partner * sin_t)
    return jnp.concatenate(cols, axis=1)


def _attn_kernel(*refs, nb, tt, cq, has_cache):
    if has_cache:
        (x_ref, kc_in_ref, vc_in_ref, cosq_ref, sinq_ref, cosk_ref, sinkt_ref, sinks_ref,
         npre_ref, kvn_ref, wq_ref, wkv_ref, wo_ref, npost_ref,
         y_ref, kc_ref, vc_ref, q_ref, kd_ref, vl_ref, vh_ref, o_ref) = refs
    else:
        (x_ref, cosq_ref, sinq_ref, cosk_ref, sinkt_ref, sinks_ref,
         npre_ref, kvn_ref, wq_ref, wkv_ref, wo_ref, npost_ref,
         y_ref, kc_ref, vc_ref, q_ref, kd_ref, vl_ref, vh_ref, o_ref) = refs
    t = pl.program_id(1)
    rows = nb * tt
    ext = WINDOW + tt
    nkeys = WINDOW + cq
    nchunks = tt // cq

    @pl.when(t == 0)
    def _():
        if has_cache:
            kc_ref[...] = kc_in_ref[...]
            vc_ref[...] = vc_in_ref[...]
        else:
            kc_ref[...] = jnp.zeros(kc_ref.shape, F32)
            vc_ref[...] = jnp.zeros(vc_ref.shape, F32)

    x = x_ref[...].reshape(rows, D_MODEL)
    xs = _rms_scale(x)
    hq = (xs * npre_ref[...]).astype(BF16)
    hk = (xs * kvn_ref[...]).astype(BF16)

    def per_stream(tab_ref):
        tab = tab_ref[...]
        return tab if nb == 1 else jnp.concatenate([tab] * nb, axis=0)

    q = _dot(hq, wq_ref[...])
    q_ref[...] = _rope_cols(q, per_stream(cosq_ref), per_stream(sinq_ref)).astype(BF16)
    kv = _dot(hk, wkv_ref[...])
    k_new = _rope_cols(kv[:, :KV_WIDTH], per_stream(cosk_ref), per_stream(sinkt_ref))
    v_new = kv[:, KV_WIDTH:]

    lane = lax.broadcasted_iota(jnp.int32, (ext, LANES), 1)
    low = lane < B_HEAD_DIM
    for n in range(nb):
        k_ext = jnp.concatenate([kc_ref[n], k_new[n * tt:(n + 1) * tt]], axis=0)
        v_ext = jnp.concatenate([vc_ref[n], v_new[n * tt:(n + 1) * tt]], axis=0)
        kc_ref[n] = k_ext[ext - WINDOW:]
        vc_ref[n] = v_ext[ext - WINDOW:]
        for m in range(KV_WIDTH // LANES):
            ka = k_ext[:, m * LANES:(m + 1) * LANES]
            kr = pltpu.roll(ka, B_HEAD_DIM, axis=1)
            va = v_ext[:, m * LANES:(m + 1) * LANES]
            vr = pltpu.roll(va, B_HEAD_DIM, axis=1)
            es = pl.ds(n * ext, ext)
            kd_ref[2 * m, es, :] = jnp.where(low, ka, kr).astype(BF16)
            kd_ref[2 * m + 1, es, :] = jnp.where(low, kr, ka).astype(BF16)
            vl_ref[2 * m, es, :] = jnp.where(low, va, 0.0).astype(BF16)
            vh_ref[2 * m, es, :] = jnp.where(low, 0.0, vr).astype(BF16)
            vl_ref[2 * m + 1, es, :] = jnp.where(low, vr, 0.0).astype(BF16)
            vh_ref[2 * m + 1, es, :] = jnp.where(low, 0.0, va).astype(BF16)

    qlane_low = lax.broadcasted_iota(jnp.int32, (cq, LANES), 1) < B_HEAD_DIM
    key_idx = lax.broadcasted_iota(jnp.int32, (1, nkeys), 1)

    def chunk_step(idx, carry):
        n = idx // nchunks
        c = idx % nchunks
        r0 = pl.multiple_of(idx * cq, cq)
        k0 = pl.multiple_of(n * ext + c * cq, SUBLANES * 2)
        if has_cache:
            bias = None
        else:
            first_pos = t * tt + c * cq - WINDOW
            bias = jnp.where(key_idx + first_pos >= 0, 0.0, -jnp.inf)
        for j in range(B_KV_HEADS):
            qa = q_ref[pl.ds(r0, cq), 2 * j * LANES:(2 * j + 1) * LANES]
            qb = q_ref[pl.ds(r0, cq), (2 * j + 1) * LANES:(2 * j + 2) * LANES]
            zero = jnp.zeros_like(qa)
            qstack = jnp.concatenate([
                jnp.where(qlane_low, qa, zero), jnp.where(qlane_low, qb, zero),
                jnp.where(qlane_low, zero, qa), jnp.where(qlane_low, zero, qb)], axis=0)
            s = _dot_nt(qstack, kd_ref[j, pl.ds(k0, nkeys), :])
            if bias is not None:
                s = s + bias
            heads = (4 * j, 4 * j + 2, 4 * j + 1, 4 * j + 3)
            sink = jnp.concatenate(
                [jnp.full((cq, 1), sinks_ref[hd], F32) for hd in heads], axis=0)
            mx = jnp.maximum(jnp.max(s, axis=-1, keepdims=True), sink)
            p = jnp.exp(s - mx)
            denom = jnp.sum(p, axis=-1, keepdims=True) + jnp.exp(sink - mx)
            p = (p / denom).astype(BF16)
            out = (_dot(p[:2 * cq], vl_ref[j, pl.ds(k0, nkeys), :])
                   + _dot(p[2 * cq:], vh_ref[j, pl.ds(k0, nkeys), :]))
            o_ref[pl.ds(r0, cq), 2 * j * LANES:(2 * j + 1) * LANES] = out[:cq].astype(BF16)
            o_ref[pl.ds(r0, cq), (2 * j + 1) * LANES:(2 * j + 2) * LANES] = out[cq:].astype(BF16)
        return carry

    lax.fori_loop(0, nb * nchunks, chunk_step, 0)

    mix = _dot(o_ref[...], wo_ref[...])
    y = x + _rms_scale(mix) * npost_ref[...]
    y_ref[...] = y.reshape(nb, tt, D_MODEL)


def _attn_mixer(x, cache_k, cache_v, tables, sinks, npre, kvn, wq, wkv, wo, npost, *, nb, tt, cq):
    batch, seq, _ = x.shape
    has_cache = cache_k is not None
    assert batch % nb == 0 and seq % tt == 0 and tt % cq == 0 and cq % (2 * SUBLANES) == 0
    rows = nb * tt
    ext = WINDOW + tt
    kern = functools.partial(_attn_kernel, nb=nb, tt=tt, cq=cq, has_cache=has_cache)
    x_spec = pl.BlockSpec((nb, tt, D_MODEL), lambda b, t: (b, t, 0))
    c_spec = pl.BlockSpec((nb, WINDOW, KV_WIDTH), lambda b, t: (b, 0, 0))
    tab_spec = pl.BlockSpec((tt, LANES), lambda b, t: (t, 0))
    in_specs = [x_spec]
    args = [x]
    if has_cache:
        in_specs += [c_spec, c_spec]
        args += [cache_k, cache_v]
    in_specs += [tab_spec] * 4 + [
        pl.BlockSpec(memory_space=pltpu.SMEM),
        _const_spec((1, D_MODEL)),
        _const_spec((1, D_MODEL)),
        _const_spec((D_MODEL, D_MODEL)),
        _const_spec((D_MODEL, 2 * KV_WIDTH)),
        _const_spec((D_MODEL, D_MODEL)),
        _const_spec((1, D_MODEL)),
    ]
    args += list(tables) + [sinks, npre, kvn, wq, wkv, wo, npost]
    cache_shape = jax.ShapeDtypeStruct((batch, WINDOW, KV_WIDTH), F32)
    return pl.pallas_call(
        kern,
        grid=(batch // nb, seq // tt),
        in_specs=in_specs,
        out_specs=[x_spec, c_spec, c_spec],
        out_shape=[jax.ShapeDtypeStruct(x.shape, F32), cache_shape, cache_shape],
        scratch_shapes=[
            pltpu.VMEM((rows, D_MODEL), BF16),
            pltpu.VMEM((B_KV_HEADS, nb * ext, LANES), BF16),
            pltpu.VMEM((B_KV_HEADS, nb * ext, LANES), BF16),
            pltpu.VMEM((B_KV_HEADS, nb * ext, LANES), BF16),
            pltpu.VMEM((rows, D_MODEL), BF16),
        ],
        compiler_params=pltpu.CompilerParams(
            dimension_semantics=("parallel", "arbitrary"), vmem_limit_bytes=VMEM_LIMIT_BYTES),
        name="attn_mixer",
    )(*args)


def _attn_pair_kernel(x_ref, cosq_ref, sinq_ref, cosk_ref, sinkt_ref, sinks_ref,
                      npre_ref, kvn_ref, wq_ref, wkv_ref, wo_ref, npost_ref,
                      y_ref, kc_ref, vc_ref, q_ref, kl_ref, kh_ref, vt_ref, ot_ref, *, tt):
    t = pl.program_id(1)
    ext = WINDOW + tt
    pair = 2 * CHUNK
    nkeys = WINDOW + pair
    nvis = WINDOW + CHUNK

    @pl.when(t == 0)
    def _():
        kc_ref[...] = jnp.zeros(kc_ref.shape, F32)
        vc_ref[...] = jnp.zeros(vc_ref.shape, F32)

    x = x_ref[0]
    xs = _rms_scale(x)
    hq = (xs * npre_ref[...]).astype(BF16)
    hk = (xs * kvn_ref[...]).astype(BF16)

    q = _dot(hq, wq_ref[...])
    q_ref[...] = _rope_cols(q, cosq_ref[...], sinq_ref[...]).astype(BF16)
    kv = _dot(hk, wkv_ref[...])
    k_ext = jnp.concatenate(
        [kc_ref[0], _rope_cols(kv[:, :KV_WIDTH], cosk_ref[...], sinkt_ref[...])], axis=0)
    v_ext = jnp.concatenate([vc_ref[0], kv[:, KV_WIDTH:]], axis=0)
    kc_ref[0] = k_ext[ext - WINDOW:]
    vc_ref[0] = v_ext[ext - WINDOW:]
    vt_ref[...] = v_ext.T.astype(BF16)

    low = lax.broadcasted_iota(jnp.int32, (ext, LANES), 1) < B_HEAD_DIM
    for m in range(KV_WIDTH // LANES):
        ka = k_ext[:, m * LANES:(m + 1) * LANES]
        kr = pltpu.roll(ka, B_HEAD_DIM, axis=1)
        kl_ref[2 * m] = jnp.where(low, ka, 0.0).astype(BF16)
        kh_ref[2 * m] = jnp.where(low, 0.0, kr).astype(BF16)
        kl_ref[2 * m + 1] = jnp.where(low, kr, 0.0).astype(BF16)
        kh_ref[2 * m + 1] = jnp.where(low, 0.0, ka).astype(BF16)

    vis_row = lax.broadcasted_iota(jnp.int32, (nvis, LANES), 0)
    first_bias = [jnp.where(vis_row + (t * tt - WINDOW + qc * CHUNK) >= 0, 0.0, -jnp.inf)
                  for qc in range(2)]
    first_head = lax.broadcasted_iota(jnp.int32, (1, LANES), 1) < CHUNK
    no_keys = jnp.zeros((CHUNK, LANES), BF16)

    units = [(j, kx_ref, ha, hb) for j in range(B_KV_HEADS)
             for kx_ref, ha, hb in ((kl_ref, 4 * j, 4 * j + 2), (kh_ref, 4 * j + 1, 4 * j + 3))]

    def scores(e):
        r0 = e * pair
        out = []
        for j, kx_ref, _, _ in units:
            ca = slice(2 * j * LANES, (2 * j + 1) * LANES)
            cb = slice((2 * j + 1) * LANES, (2 * j + 2) * LANES)
            qq = jnp.concatenate([q_ref[r0:r0 + CHUNK, ca], q_ref[r0:r0 + CHUNK, cb],
                                  q_ref[r0 + CHUNK:r0 + pair, ca], q_ref[r0 + CHUNK:r0 + pair, cb]], axis=0)
            out.append(_dot_nt(kx_ref[j, r0:r0 + nkeys, :], qq))
        return out

    def softmax(e, s_list):
        out = []
        for (_, _, ha, hb), s in zip(units, s_list):
            sink = jnp.where(first_head, sinks_ref[ha], sinks_ref[hb]) * LOG2_E
            ps, dens = [], []
            for qc in range(2):
                sq = s[qc * CHUNK:qc * CHUNK + nvis, qc * LANES:(qc + 1) * LANES]
                if e == 0:
                    sq = sq + first_bias[qc]
                mx = jnp.maximum(jnp.max(sq, axis=0, keepdims=True), sink)
                p = jnp.exp2(sq - mx)
                dens.append(jnp.sum(p, axis=0, keepdims=True) + jnp.exp2(sink - mx))
                ps.append(p.astype(BF16))
            pmat = jnp.concatenate([jnp.concatenate([ps[0], no_keys], axis=0),
                                    jnp.concatenate([no_keys, ps[1]], axis=0)], axis=1)
            out.append((pmat, jnp.concatenate(dens, axis=1)))
        return out

    def weighted_values(e, pd_list):
        r0 = e * pair
        for (j, _, ha, hb), (pmat, den) in zip(units, pd_list):
            o = _dot(vt_ref[j * B_HEAD_DIM:(j + 1) * B_HEAD_DIM, r0:r0 + nkeys], pmat)
            o = (o * (1.0 / den)).astype(BF16)
            for qc in range(2):
                c0 = r0 + qc * CHUNK
                ot_ref[ha * B_HEAD_DIM:(ha + 1) * B_HEAD_DIM, c0:c0 + CHUNK] = (
                    o[:, qc * LANES:qc * LANES + CHUNK])
                ot_ref[hb * B_HEAD_DIM:(hb + 1) * B_HEAD_DIM, c0:c0 + CHUNK] = (
                    o[:, qc * LANES + CHUNK:(qc + 1) * LANES])

    npairs = tt // pair
    s_next = scores(0)
    for e in range(npairs):
        s_cur = s_next
        if e + 1 < npairs:
            s_next = scores(e + 1)
        weighted_values(e, softmax(e, s_cur))

    mix = _dot_tn(ot_ref[...], wo_ref[...])
    y_ref[0] = x + _rms_scale(mix) * npost_ref[...]


def _attn_pair_mixer(x, tables, sinks, npre, kvn, wq, wkv, wo, npost, *, tt):
    batch, seq, _ = x.shape
    assert seq % tt == 0 and tt % (2 * CHUNK) == 0 and WINDOW == 2 * CHUNK
    ext = WINDOW + tt
    kern = functools.partial(_attn_pair_kernel, tt=tt)
    x_spec = pl.BlockSpec((1, tt, D_MODEL), lambda b, t: (b, t, 0))
    c_spec = pl.BlockSpec((1, WINDOW, KV_WIDTH), lambda b, t: (b, 0, 0))
    tab_spec = pl.BlockSpec((tt, LANES), lambda b, t: (t, 0))
    cache_shape = jax.ShapeDtypeStruct((batch, WINDOW, KV_WIDTH), F32)
    return pl.pallas_call(
        kern,
        grid=(batch, seq // tt),
        in_specs=[x_spec] + [tab_spec] * 4 + [
            pl.BlockSpec(memory_space=pltpu.SMEM),
            _const_spec((1, D_MODEL)),
            _const_spec((1, D_MODEL)),
            _const_spec((D_MODEL, D_MODEL)),
            _const_spec((D_MODEL, 2 * KV_WIDTH)),
            _const_spec((D_MODEL, D_MODEL)),
            _const_spec((1, D_MODEL)),
        ],
        out_specs=[x_spec, c_spec, c_spec],
        out_shape=[jax.ShapeDtypeStruct(x.shape, F32), cache_shape, cache_shape],
        scratch_shapes=[
            pltpu.VMEM((tt, D_MODEL), BF16),
            pltpu.VMEM((B_KV_HEADS, ext, LANES), BF16),
            pltpu.VMEM((B_KV_HEADS, ext, LANES), BF16),
            pltpu.VMEM((KV_WIDTH, ext), BF16),
            pltpu.VMEM((D_MODEL, tt), BF16),
        ],
        compiler_params=pltpu.CompilerParams(
            dimension_semantics=("parallel", "arbitrary"), vmem_limit_bytes=VMEM_LIMIT_BYTES),
        name="attn_pair_mixer",
    )(x, *tables, sinks, npre, kvn, wq, wkv, wo, npost)


def _rope_tables(pos, q_scale):
    half = B_HEAD_DIM // 2
    inv = ROPE_THETA ** (-jnp.arange(half, dtype=F32) / half)
    ang = pos.astype(F32)[:, None] * inv[None, :]
    cos = jnp.cos(ang)
    sin = jnp.sin(ang)
    reps = LANES // B_HEAD_DIM
    cos_t = jnp.tile(jnp.concatenate([cos, cos], axis=1), (1, reps))
    sin_t = jnp.tile(jnp.concatenate([-sin, sin], axis=1), (1, reps))
    return cos_t * q_scale, sin_t * q_scale, cos_t, sin_t


def _trunk(x, pos, state, cache_k, cache_v, w, *, nb, tt, hgrn_chunk, cq):
    batch, seq, _ = x.shape
    row = lambda a: a.reshape(1, D_MODEL)
    x, st = _hgrn_mixer(x, state, row(w["norm_mix_pre"][0]), w["w_a_in"], w["a_lower_bound"],
                        row(w["a_out_norm"]), w["w_a_out"], row(w["norm_mix_post"][0]),
                        nb=nb, tt=tt, chunk=hgrn_chunk)
    x = _ffn(x.reshape(batch * seq, D_MODEL), row(w["norm_ffn_pre"][0]), row(w["norm_ffn_post"][0]),
             w["w_ffn_in"][0], w["w_ffn_out"][0]).reshape(batch, seq, D_MODEL)
    attn_w = (w["b_sinks"], row(w["norm_mix_pre"][1]), row(w["kv_norm"]), w["w_b_q"], w["w_kv"],
              w["w_b_out"], row(w["norm_mix_post"][1]))
    if cache_k is None and nb == 1 and cq == CHUNK:
        x, kc, vc = _attn_pair_mixer(x, _rope_tables(pos, SOFTMAX_SCALE * LOG2_E), *attn_w, tt=tt)
    else:
        x, kc, vc = _attn_mixer(x, cache_k, cache_v, _rope_tables(pos, SOFTMAX_SCALE), *attn_w,
                                nb=nb, tt=tt, cq=cq)
    x = _ffn(x.reshape(batch * seq, D_MODEL), row(w["norm_ffn_pre"][1]), row(w["norm_ffn_post"][1]),
             w["w_ffn_in"][1], w["w_ffn_out"][1]).reshape(batch, seq, D_MODEL)
    return x, st, kc, vc


def kernel(x_prompt, x_sample, state_hgrn, cache_k, cache_v, norm_mix_pre, norm_mix_post, norm_ffn_pre, norm_ffn_post, w_ffn_in, w_ffn_out, w_a_in, a_lower_bound, a_out_norm, w_a_out, kv_norm, w_kv, w_b_q, b_sinks, w_b_out):
    w = dict(
        norm_mix_pre=norm_mix_pre, norm_mix_post=norm_mix_post,
        norm_ffn_pre=norm_ffn_pre, norm_ffn_post=norm_ffn_post,
        w_ffn_in=w_ffn_in.astype(BF16), w_ffn_out=w_ffn_out.astype(BF16),
        w_a_in=w_a_in[0].astype(BF16), a_lower_bound=a_lower_bound,
        a_out_norm=a_out_norm[0], w_a_out=w_a_out[0].astype(BF16),
        kv_norm=kv_norm, w_kv=w_kv.astype(BF16), w_b_q=w_b_q[0].astype(BF16),
        b_sinks=b_sinks[0], w_b_out=w_b_out[0].astype(BF16),
    )
    bp, tp, _ = x_prompt.shape
    bs, ts, _ = x_sample.shape

    zero_state = jnp.zeros((bp, A_HEADS, A_DK, A_DV), F32)
    y_p, st_p, kc_p, vc_p = _trunk(
        x_prompt, jnp.arange(tp), zero_state, None, None, w,
        nb=1, tt=ROW_TILE, hgrn_chunk=HGRN_CHUNK, cq=CHUNK)

    nb_s = ROW_TILE // (2 * ts)
    y_s, st_s, kc_s, vc_s = _trunk(
        x_sample, PAST_LEN + jnp.arange(ts), state_hgrn[0],
        cache_k.reshape(bs, WINDOW, KV_WIDTH), cache_v.reshape(bs, WINDOW, KV_WIDTH), w,
        nb=nb_s, tt=ts, hgrn_chunk=ts, cq=ts)

    cache4 = lambda a: a.reshape(a.shape[0], WINDOW, B_KV_HEADS, B_HEAD_DIM)
    return (y_p, y_s, st_p[None], st_s[None],
            cache4(kc_p), cache4(vc_p), cache4(kc_s), cache4(vc_s))
```

```python
import functools

import jax
import jax.numpy as jnp
from jax import lax
from jax.experimental import pallas as pl
from jax.experimental.pallas import tpu as pltpu

F32 = jnp.float32
BF16 = jnp.bfloat16

D_MODEL = 1024
A_HEADS = 8
A_DK = 128
A_DV = 128
B_HEAD_DIM = 64
B_Q_HEADS = 16
B_KV_HEADS = 4
B_GROUP = B_Q_HEADS // B_KV_HEADS
KV_WIDTH = B_KV_HEADS * B_HEAD_DIM
WINDOW = 128
CHUNK = 64
PAST_LEN = 2048
D_FF = 2816
ROPE_THETA = 10000.0
NORM_EPS = 1e-6
SOFTMAX_SCALE = B_HEAD_DIM ** -0.5
LOG2_E = 1.4426950408889634

LANES = 128
SUBLANES = 8
VMEM_LIMIT_BYTES = 52 * 1024 * 1024

ROW_TILE = 512
HGRN_CHUNK = 128
FFN_CHUNK = 256
MAX_FACTOR_EXPONENT = 60.0
EXACT_BLOCK = 16
STAGE_ROWS = 128


def _rms_scale(x):
    ms = jnp.mean(x * x, axis=-1, keepdims=True)
    return x * lax.rsqrt(ms + NORM_EPS)


def _dot(a, b):
    return jnp.dot(a, b, preferred_element_type=F32)


def _dot_nt(a, b):
    return lax.dot_general(a, b, (((1,), (1,)), ((), ())), preferred_element_type=F32)


def _dot_tn(a, b):
    return lax.dot_general(a, b, (((0,), (0,)), ((), ())), preferred_element_type=F32)


def _const_spec(shape):
    nd = len(shape)
    return pl.BlockSpec(shape, lambda *_: (0,) * nd, pipeline_mode=pl.Buffered(1))


def _ffn_kernel(x_ref, npre_ref, npost_ref, win_ref, wout_ref, o_ref):
    x = x_ref[...]
    h = (_rms_scale(x) * npre_ref[...]).astype(BF16)
    acc = jnp.zeros(x.shape, F32)
    for c0 in range(0, D_FF, FFN_CHUNK):
        a = _dot(h, win_ref[:, c0:c0 + FFN_CHUNK])
        b = _dot(h, win_ref[:, D_FF + c0:D_FF + c0 + FFN_CHUNK])
        g = (a * jax.nn.sigmoid(a) * b).astype(BF16)
        acc = acc + _dot(g, wout_ref[c0:c0 + FFN_CHUNK, :])
    o_ref[...] = x + _rms_scale(acc) * npost_ref[...]


def _ffn(x2d, npre, npost, win, wout):
    rows = x2d.shape[0]
    tile = min(ROW_TILE, rows)
    assert rows % tile == 0 and D_FF % FFN_CHUNK == 0
    return pl.pallas_call(
        _ffn_kernel,
        grid=(rows // tile,),
        in_specs=[
            pl.BlockSpec((tile, D_MODEL), lambda i: (i, 0)),
            _const_spec((1, D_MODEL)),
            _const_spec((1, D_MODEL)),
            _const_spec((D_MODEL, 2 * D_FF)),
            _const_spec((D_FF, D_MODEL)),
        ],
        out_specs=pl.BlockSpec((tile, D_MODEL), lambda i: (i, 0)),
        out_shape=jax.ShapeDtypeStruct((rows, D_MODEL), F32),
        compiler_params=pltpu.CompilerParams(
            dimension_semantics=("parallel",), vmem_limit_bytes=VMEM_LIMIT_BYTES),
        name="ffn",
    )(x2d, npre, npost, win, wout)


def _cumsum_rows(x):
    c, w = x.shape
    groups = c // SUBLANES
    y = x.reshape(groups, SUBLANES, w)
    sub = lax.broadcasted_iota(jnp.int32, y.shape, 1)
    shift = 1
    while shift < SUBLANES:
        y = y + jnp.where(sub >= shift, pltpu.roll(y, shift, axis=1), 0.0)
        shift *= 2
    tot = jnp.broadcast_to(y[:, SUBLANES - 1:SUBLANES, :], y.shape)
    inc = tot
    shift = 1
    while shift < groups:
        inc = inc + jnp.concatenate(
            [jnp.zeros((shift, SUBLANES, w), F32), inc[:groups - shift]], axis=0)
        shift *= 2
    return (y + (inc - tot)).reshape(c, w)


def _hgrn_kernel(x_ref, st_in_ref, npre_ref, win_ref, alb_ref, gnorm_ref, wout_ref, npost_ref,
                 y_ref, st_ref, qs_ref, lf_ref, inp_ref, v_ref, sg_ref, on_ref,
                 *, nb, tt, chunk):
    t = pl.program_id(1)
    rows = nb * tt

    @pl.when(t == 0)
    def _():
        for n in range(nb):
            for h in range(A_HEADS):
                st_ref[n, h] = st_in_ref[n, h].T

    x = x_ref[...].reshape(rows, D_MODEL)
    hn = (_rms_scale(x) * npre_ref[...]).astype(BF16)

    alb = alb_ref[...]
    e = jnp.exp(alb - jnp.max(alb, axis=0, keepdims=True))
    lb = e[0:1] / jnp.sum(e, axis=0, keepdims=True)

    f = _dot(hn, win_ref[:, D_MODEL:2 * D_MODEL])
    forget = lb + (1.0 - lb) * jax.nn.sigmoid(f)
    lf_ref[...] = jnp.log(forget)
    inp_ref[...] = 1.0 - forget

    def project_query(c0, width):
        q = _dot(hn, win_ref[:, c0:c0 + width])
        qs_ref[:, c0:c0 + width] = q * jax.nn.sigmoid(q)

    def project_value(c0, width):
        v_ref[:, c0:c0 + width] = _dot(
            hn, win_ref[:, 2 * D_MODEL + c0:2 * D_MODEL + c0 + width]).astype(BF16)

    def project_gate(c0, width):
        g = _dot(hn, win_ref[:, 3 * D_MODEL + c0:3 * D_MODEL + c0 + width])
        sg_ref[:, c0:c0 + width] = g * jax.nn.sigmoid(g)

    def write_rows(mix):
        y = x + _rms_scale(mix) * npost_ref[...]
        y_ref[...] = y.reshape(nb, tt, D_MODEL)

    half = chunk // 2
    ri = lax.broadcasted_iota(jnp.int32, (chunk, chunk), 0)
    ci = lax.broadcasted_iota(jnp.int32, (chunk, chunk), 1)
    causal = ri >= ci

    def prepare(blk, h):
        rs = slice(blk * chunk, (blk + 1) * chunk)
        sl = slice(h * A_DK, (h + 1) * A_DK)
        b = _cumsum_rows(lf_ref[rs, sl])
        bmid = b[half - 1:half]
        blast = b[chunk - 1:chunk]
        qsc = qs_ref[rs, sl] * jnp.exp(b - bmid)
        inp = inp_ref[rs, sl] * jnp.exp(bmid - b)
        return dict(
            rs=rs, sl=sl, qt=qsc.astype(BF16), kt=inp.astype(BF16),
            qi=(qsc * jnp.exp(bmid)).astype(BF16),
            kh=(inp * jnp.exp(blast - bmid)).astype(BF16),
            vh=v_ref[rs, sl], decay=jnp.exp(blast))

    def emit_output(u, sc, st):
        p = jnp.where(causal, sc, 0.0).astype(BF16)
        o = _dot(p, u["vh"]) + _dot_nt(u["qi"], st.astype(BF16))
        on = _rms_scale(o) * gnorm_ref[:, u["sl"]] * sg_ref[u["rs"], u["sl"]]
        on_ref[u["rs"], u["sl"]] = on.astype(BF16)

    def recurrence(heads, jobs):
        blocks_per_stage = STAGE_ROWS // chunk
        nstages = rows // STAGE_ROWS
        slots = 2 * nstages
        jobs = list(jobs)

        def run_jobs(slot):
            for _ in range(-(-len(jobs) // (slots - slot))):
                jobs.pop(0)()

        states = {}
        for stage in range(nstages):
            blocks = range(stage * blocks_per_stage, (stage + 1) * blocks_per_stage)
            units = [(blk, h, prepare(blk, h)) for blk in blocks for h in heads]
            scores = [_dot_nt(u["qt"], u["kt"]) for _, _, u in units]
            run_jobs(2 * stage)
            updated = {}
            for blk, h, u in units:
                key = ((blk * chunk) // tt, h)
                if key not in states:
                    states[key] = st_ref[key[0], h]
                updated[key] = states[key] * u["decay"] + _dot_tn(u["vh"], u["kh"])
            run_jobs(2 * stage + 1)
            for (blk, h, u), sc in zip(units, scores):
                emit_output(u, sc, states[((blk * chunk) // tt, h)])
            states.update(updated)
            if (blocks[-1] + 1) * chunk % tt == 0:
                for (n, h) in list(states):
                    st_ref[n, h] = states.pop((n, h))

    def factorised_path():
        piece = 2 * LANES
        jobs = [functools.partial(proj, gw + c0, piece)
                for proj in (project_query, project_value, project_gate)
                for c0 in range(0, gw, piece)]
        recurrence(range(0, A_HEADS // 2), jobs)
        partial_mix = []
        jobs = [lambda c0=c0: partial_mix.append(_dot(on_ref[:, :gw], wout_ref[:gw, c0:c0 + piece]))
                for c0 in range(0, D_MODEL, piece)]
        recurrence(range(A_HEADS // 2, A_HEADS), jobs)
        write_rows(jnp.concatenate(partial_mix, axis=1) + _dot(on_ref[:, gw:], wout_ref[gw:, :]))

    def exact_step(idx, carry):
        n = (idx * EXACT_BLOCK) // tt
        rs = pl.ds(pl.multiple_of(idx * EXACT_BLOCK, EXACT_BLOCK), EXACT_BLOCK)
        row = lax.broadcasted_iota(jnp.int32, (EXACT_BLOCK, A_DK), 0)
        for h in range(A_HEADS):
            sl = slice(h * A_DK, (h + 1) * A_DK)
            b = _cumsum_rows(lf_ref[rs, sl])
            blast = b[EXACT_BLOCK - 1:EXACT_BLOCK]
            qs = qs_ref[rs, sl]
            inp = inp_ref[rs, sl]
            vh = v_ref[rs, sl]
            vf = vh.astype(F32)
            st = st_ref[n, h]
            o = _dot_nt((qs * jnp.exp(b)).astype(BF16), st.astype(BF16))
            for s in range(EXACT_BLOCK):
                decay = jnp.exp(jnp.where(row >= s, b - b[s:s + 1], -jnp.inf))
                score = jnp.sum(qs * decay * inp[s:s + 1], axis=-1, keepdims=True)
                o = o + score * vf[s:s + 1]
            kh = (inp * jnp.exp(blast - b)).astype(BF16)
            st_ref[n, h] = st * jnp.exp(blast) + _dot_tn(vh, kh)
            on = _rms_scale(o) * gnorm_ref[:, sl] * sg_ref[rs, sl]
            on_ref[rs, sl] = on.astype(BF16)
        return carry

    gw = D_MODEL // 2
    project_query(0, gw)
    project_value(0, gw)
    project_gate(0, gw)

    max_step = -jnp.min(lf_ref[...])
    bounded = max_step * half <= MAX_FACTOR_EXPONENT

    pl.when(bounded)(factorised_path)

    @pl.when(jnp.logical_not(bounded))
    def _():
        project_query(gw, gw)
        project_value(gw, gw)
        project_gate(gw, gw)
        lax.fori_loop(0, rows // EXACT_BLOCK, exact_step, 0)
        write_rows(_dot(on_ref[...], wout_ref[...]))

    @pl.when(t == pl.num_programs(1) - 1)
    def _():
        for n in range(nb):
            for h in range(A_HEADS):
                st_ref[n, h] = st_ref[n, h].T


def _hgrn_mixer(x, state, npre, win, alb, gnorm, wout, npost, *, nb, tt, chunk):
    batch, seq, _ = x.shape
    assert batch % nb == 0 and seq % tt == 0 and tt % chunk == 0 and chunk % (2 * SUBLANES) == 0
    rows = nb * tt
    assert rows % STAGE_ROWS == 0 and STAGE_ROWS % chunk == 0
    assert (nb == 1 and tt % STAGE_ROWS == 0) or STAGE_ROWS % tt == 0
    kern = functools.partial(_hgrn_kernel, nb=nb, tt=tt, chunk=chunk)
    st_spec = pl.BlockSpec((nb, A_HEADS, A_DK, A_DV), lambda b, t: (b, 0, 0, 0))
    return pl.pallas_call(
        kern,
        grid=(batch // nb, seq // tt),
        in_specs=[
            pl.BlockSpec((nb, tt, D_MODEL), lambda b, t: (b, t, 0)),
            st_spec,
            _const_spec((1, D_MODEL)),
            _const_spec((D_MODEL, 4 * D_MODEL)),
            _const_spec(alb.shape),
            _const_spec((1, D_MODEL)),
            _const_spec((D_MODEL, D_MODEL)),
            _const_spec((1, D_MODEL)),
        ],
        out_specs=[
            pl.BlockSpec((nb, tt, D_MODEL), lambda b, t: (b, t, 0)),
            st_spec,
        ],
        out_shape=[
            jax.ShapeDtypeStruct(x.shape, F32),
            jax.ShapeDtypeStruct(state.shape, F32),
        ],
        scratch_shapes=[
            pltpu.VMEM((rows, D_MODEL), F32),
            pltpu.VMEM((rows, D_MODEL), F32),
            pltpu.VMEM((rows, D_MODEL), F32),
            pltpu.VMEM((rows, D_MODEL), BF16),
            pltpu.VMEM((rows, D_MODEL), F32),
            pltpu.VMEM((rows, D_MODEL), BF16),
        ],
        compiler_params=pltpu.CompilerParams(
            dimension_semantics=("parallel", "arbitrary"), vmem_limit_bytes=VMEM_LIMIT_BYTES),
        name="hgrn_mixer",
    )(x, state, npre, win, alb, gnorm, wout, npost)


def _rope_cols(x, cos_t, sin_t):
    lane = lax.broadcasted_iota(jnp.int32, (x.shape[0], LANES), 1)
    first_half = (lane % B_HEAD_DIM) < (B_HEAD_DIM // 2)
    cols = []
    for c0 in range(0, x.shape[1], LANES):
        xc = x[:, c0:c0 + LANES]
        partner = jnp.where(first_half,
                            pltpu.roll(xc, LANES - B_HEAD_DIM // 2, axis=1),
                            pltpu.roll(xc, B_HEAD_DIM // 2, axis=1))
        cols.append(xc * cos_t + partner * sin_t)
    return jnp.concatenate(cols, axis=1)


def _attn_kernel(*refs, nb, tt, cq, has_cache):
    if has_cache:
        (x_ref, kc_in_ref, vc_in_ref, cosq_ref, sinq_ref, cosk_ref, sinkt_ref, sinks_ref,
         npre_ref, kvn_ref, wq_ref, wkv_ref, wo_ref, npost_ref,
         y_ref, kc_ref, vc_ref, q_ref, kd_ref, vl_ref, vh_ref, o_ref) = refs
    else:
        (x_ref, cosq_ref, sinq_ref, cosk_ref, sinkt_ref, sinks_ref,
         npre_ref, kvn_ref, wq_ref, wkv_ref, wo_ref, npost_ref,
         y_ref, kc_ref, vc_ref, q_ref, kd_ref, vl_ref, vh_ref, o_ref) = refs
    t = pl.program_id(1)
    rows = nb * tt
    ext = WINDOW + tt
    nkeys = WINDOW + cq
    nchunks = tt // cq

    @pl.when(t == 0)
    def _():
        if has_cache:
            kc_ref[...] = kc_in_ref[...]
            vc_ref[...] = vc_in_ref[...]
        else:
            kc_ref[...] = jnp.zeros(kc_ref.shape, F32)
            vc_ref[...] = jnp.zeros(vc_ref.shape, F32)

    x = x_ref[...].reshape(rows, D_MODEL)
    xs = _rms_scale(x)
    hq = (xs * npre_ref[...]).astype(BF16)
    hk = (xs * kvn_ref[...]).astype(BF16)

    def per_stream(tab_ref):
        tab = tab_ref[...]
        return tab if nb == 1 else jnp.concatenate([tab] * nb, axis=0)

    q = _dot(hq, wq_ref[...])
    q_ref[...] = _rope_cols(q, per_stream(cosq_ref), per_stream(sinq_ref)).astype(BF16)
    kv = _dot(hk, wkv_ref[...])
    k_new = _rope_cols(kv[:, :KV_WIDTH], per_stream(cosk_ref), per_stream(sinkt_ref))
    v_new = kv[:, KV_WIDTH:]

    lane = lax.broadcasted_iota(jnp.int32, (ext, LANES), 1)
    low = lane < B_HEAD_DIM
    for n in range(nb):
        k_ext = jnp.concatenate([kc_ref[n], k_new[n * tt:(n + 1) * tt]], axis=0)
        v_ext = jnp.concatenate([vc_ref[n], v_new[n * tt:(n + 1) * tt]], axis=0)
        kc_ref[n] = k_ext[ext - WINDOW:]
        vc_ref[n] = v_ext[ext - WINDOW:]
        for m in range(KV_WIDTH // LANES):
            ka = k_ext[:, m * LANES:(m + 1) * LANES]
            kr = pltpu.roll(ka, B_HEAD_DIM, axis=1)
            va = v_ext[:, m * LANES:(m + 1) * LANES]
            vr = pltpu.roll(va, B_HEAD_DIM, axis=1)
            es = pl.ds(n * ext, ext)
            kd_ref[2 * m, es, :] = jnp.where(low, ka, kr).astype(BF16)
            kd_ref[2 * m + 1, es, :] = jnp.where(low, kr, ka).astype(BF16)
            vl_ref[2 * m, es, :] = jnp.where(low, va, 0.0).astype(BF16)
            vh_ref[2 * m, es, :] = jnp.where(low, 0.0, vr).astype(BF16)
            vl_ref[2 * m + 1, es, :] = jnp.where(low, vr, 0.0).astype(BF16)
            vh_ref[2 * m + 1, es, :] = jnp.where(low, 0.0, va).astype(BF16)

    qlane_low = lax.broadcasted_iota(jnp.int32, (cq, LANES), 1) < B_HEAD_DIM
    key_idx = lax.broadcasted_iota(jnp.int32, (1, nkeys), 1)

    def chunk_step(idx, carry):
        n = idx // nchunks
        c = idx % nchunks
        r0 = pl.multiple_of(idx * cq, cq)
        k0 = pl.multiple_of(n * ext + c * cq, SUBLANES * 2)
        if has_cache:
            bias = None
        else:
            first_pos = t * tt + c * cq - WINDOW
            bias = jnp.where(key_idx + first_pos >= 0, 0.0, -jnp.inf)
        for j in range(B_KV_HEADS):
            qa = q_ref[pl.ds(r0, cq), 2 * j * LANES:(2 * j + 1) * LANES]
            qb = q_ref[pl.ds(r0, cq), (2 * j + 1) * LANES:(2 * j + 2) * LANES]
            zero = jnp.zeros_like(qa)
            qstack = jnp.concatenate([
                jnp.where(qlane_low, qa, zero), jnp.where(qlane_low, qb, zero),
                jnp.where(qlane_low, zero, qa), jnp.where(qlane_low, zero, qb)], axis=0)
            s = _dot_nt(qstack, kd_ref[j, pl.ds(k0, nkeys), :])
            if bias is not None:
                s = s + bias
            heads = (4 * j, 4 * j + 2, 4 * j + 1, 4 * j + 3)
            sink = jnp.concatenate(
                [jnp.full((cq, 1), sinks_ref[hd], F32) for hd in heads], axis=0)
            mx = jnp.maximum(jnp.max(s, axis=-1, keepdims=True), sink)
            p = jnp.exp(s - mx)
            denom = jnp.sum(p, axis=-1, keepdims=True) + jnp.exp(sink - mx)
            p = (p / denom).astype(BF16)
            out = (_dot(p[:2 * cq], vl_ref[j, pl.ds(k0, nkeys), :])
                   + _dot(p[2 * cq:], vh_ref[j, pl.ds(k0, nkeys), :]))
            o_ref[pl.ds(r0, cq), 2 * j * LANES:(2 * j + 1) * LANES] = out[:cq].astype(BF16)
            o_ref[pl.ds(r0, cq), (2 * j + 1) * LANES:(2 * j + 2) * LANES] = out[cq:].astype(BF16)
        return carry

    lax.fori_loop(0, nb * nchunks, chunk_step, 0)

    mix = _dot(o_ref[...], wo_ref[...])
    y = x + _rms_scale(mix) * npost_ref[...]
    y_ref[...] = y.reshape(nb, tt, D_MODEL)


def _attn_mixer(x, cache_k, cache_v, tables, sinks, npre, kvn, wq, wkv, wo, npost, *, nb, tt, cq):
    batch, seq, _ = x.shape
    has_cache = cache_k is not None
    assert batch % nb == 0 and seq % tt == 0 and tt % cq == 0 and cq % (2 * SUBLANES) == 0
    rows = nb * tt
    ext = WINDOW + tt
    kern = functools.partial(_attn_kernel, nb=nb, tt=tt, cq=cq, has_cache=has_cache)
    x_spec = pl.BlockSpec((nb, tt, D_MODEL), lambda b, t: (b, t, 0))
    c_spec = pl.BlockSpec((nb, WINDOW, KV_WIDTH), lambda b, t: (b, 0, 0))
    tab_spec = pl.BlockSpec((tt, LANES), lambda b, t: (t, 0))
    in_specs = [x_spec]
    args = [x]
    if has_cache:
        in_specs += [c_spec, c_spec]
        args += [cache_k, cache_v]
    in_specs += [tab_spec] * 4 + [
        pl.BlockSpec(memory_space=pltpu.SMEM),
        _const_spec((1, D_MODEL)),
        _const_spec((1, D_MODEL)),
        _const_spec((D_MODEL, D_MODEL)),
        _const_spec((D_MODEL, 2 * KV_WIDTH)),
        _const_spec((D_MODEL, D_MODEL)),
        _const_spec((1, D_MODEL)),
    ]
    args += list(tables) + [sinks, npre, kvn, wq, wkv, wo, npost]
    cache_shape = jax.ShapeDtypeStruct((batch, WINDOW, KV_WIDTH), F32)
    return pl.pallas_call(
        kern,
        grid=(batch // nb, seq // tt),
        in_specs=in_specs,
        out_specs=[x_spec, c_spec, c_spec],
        out_shape=[jax.ShapeDtypeStruct(x.shape, F32), cache_shape, cache_shape],
        scratch_shapes=[
            pltpu.VMEM((rows, D_MODEL), BF16),
            pltpu.VMEM((B_KV_HEADS, nb * ext, LANES), BF16),
            pltpu.VMEM((B_KV_HEADS, nb * ext, LANES), BF16),
            pltpu.VMEM((B_KV_HEADS, nb * ext, LANES), BF16),
            pltpu.VMEM((rows, D_MODEL), BF16),
        ],
        compiler_params=pltpu.CompilerParams(
            dimension_semantics=("parallel", "arbitrary"), vmem_limit_bytes=VMEM_LIMIT_BYTES),
        name="attn_mixer",
    )(*args)


def _attn_pair_kernel(x_ref, cosq_ref, sinq_ref, cosk_ref, sinkt_ref, sinks_ref,
                      npre_ref, kvn_ref, wq_ref, wkv_ref, wo_ref, npost_ref,
                      y_ref, kc_ref, vc_ref, q_ref, kl_ref, kh_ref, vt_ref, ot_ref, *, tt):
    t = pl.program_id(1)
    ext = WINDOW + tt
    pair = 2 * CHUNK
    nkeys = WINDOW + pair
    nvis = WINDOW + CHUNK

    @pl.when(t == 0)
    def _():
        kc_ref[...] = jnp.zeros(kc_ref.shape, F32)
        vc_ref[...] = jnp.zeros(vc_ref.shape, F32)

    x = x_ref[0]
    xs = _rms_scale(x)
    hq = (xs * npre_ref[...]).astype(BF16)
    hk = (xs * kvn_ref[...]).astype(BF16)

    q = _dot(hq, wq_ref[...])
    q_ref[...] = _rope_cols(q, cosq_ref[...], sinq_ref[...]).astype(BF16)
    kv = _dot(hk, wkv_ref[...])
    k_ext = jnp.concatenate(
        [kc_ref[0], _rope_cols(kv[:, :KV_WIDTH], cosk_ref[...], sinkt_ref[...])], axis=0)
    v_ext = jnp.concatenate([vc_ref[0], kv[:, KV_WIDTH:]], axis=0)
    kc_ref[0] = k_ext[ext - WINDOW:]
    vc_ref[0] = v_ext[ext - WINDOW:]
    vt_ref[...] = v_ext.T.astype(BF16)

    low = lax.broadcasted_iota(jnp.int32, (ext, LANES), 1) < B_HEAD_DIM
    for m in range(KV_WIDTH // LANES):
        ka = k_ext[:, m * LANES:(m + 1) * LANES]
        kr = pltpu.roll(ka, B_HEAD_DIM, axis=1)
        kl_ref[2 * m] = jnp.where(low, ka, 0.0).astype(BF16)
        kh_ref[2 * m] = jnp.where(low, 0.0, kr).astype(BF16)
        kl_ref[2 * m + 1] = jnp.where(low, kr, 0.0).astype(BF16)
        kh_ref[2 * m + 1] = jnp.where(low, 0.0, ka).astype(BF16)

    vis_row = lax.broadcasted_iota(jnp.int32, (nvis, LANES), 0)
    first_bias = [jnp.where(vis_row + (t * tt - WINDOW + qc * CHUNK) >= 0, 0.0, -jnp.inf)
                  for qc in range(2)]
    first_head = lax.broadcasted_iota(jnp.int32, (1, LANES), 1) < CHUNK
    no_keys = jnp.zeros((CHUNK, LANES), BF16)

    units = [(j, kx_ref, ha, hb) for j in range(B_KV_HEADS)
             for kx_ref, ha, hb in ((kl_ref, 4 * j, 4 * j + 2), (kh_ref, 4 * j + 1, 4 * j + 3))]

    def scores(e):
        r0 = e * pair
        out = []
        for j, kx_ref, _, _ in units:
            ca = slice(2 * j * LANES, (2 * j + 1) * LANES)
            cb = slice((2 * j + 1) * LANES, (2 * j + 2) * LANES)
            qq = jnp.concatenate([q_ref[r0:r0 + CHUNK, ca], q_ref[r0:r0 + CHUNK, cb],
                                  q_ref[r0 + CHUNK:r0 + pair, ca], q_ref[r0 + CHUNK:r0 + pair, cb]], axis=0)
            out.append(_dot_nt(kx_ref[j, r0:r0 + nkeys, :], qq))
        return out

    def softmax(e, s_list):
        out = []
        for (_, _, ha, hb), s in zip(units, s_list):
            sink = jnp.where(first_head, sinks_ref[ha], sinks_ref[hb]) * LOG2_E
            ps, dens = [], []
            for qc in range(2):
                sq = s[qc * CHUNK:qc * CHUNK + nvis, qc * LANES:(qc + 1) * LANES]
                if e == 0:
                    sq = sq + first_bias[qc]
                mx = jnp.maximum(jnp.max(sq, axis=0, keepdims=True), sink)
                p = jnp.exp2(sq - mx)
                dens.append(jnp.sum(p, axis=0, keepdims=True) + jnp.exp2(sink - mx))
                ps.append(p.astype(BF16))
            pmat = jnp.concatenate([jnp.concatenate([ps[0], no_keys], axis=0),
                                    jnp.concatenate([no_keys, ps[1]], axis=0)], axis=1)
            out.append((pmat, jnp.concatenate(dens, axis=1)))
        return out

    def weighted_values(e, pd_list):
        r0 = e * pair
        for (j, _, ha, hb), (pmat, den) in zip(units, pd_list):
            o = _dot(vt_ref[j * B_HEAD_DIM:(j + 1) * B_HEAD_DIM, r0:r0 + nkeys], pmat)
            o = (o * (1.0 / den)).astype(BF16)
            for qc in range(2):
                c0 = r0 + qc * CHUNK
                ot_ref[ha * B_HEAD_DIM:(ha + 1) * B_HEAD_DIM, c0:c0 + CHUNK] = (
                    o[:, qc * LANES:qc * LANES + CHUNK])
                ot_ref[hb * B_HEAD_DIM:(hb + 1) * B_HEAD_DIM, c0:c0 + CHUNK] = (
                    o[:, qc * LANES + CHUNK:(qc + 1) * LANES])

    npairs = tt // pair
    s_next = scores(0)
    for e in range(npairs):
        s_cur = s_next
        if e + 1 < npairs:
            s_next = scores(e + 1)
        weighted_values(e, softmax(e, s_cur))

    mix = _dot_tn(ot_ref[...], wo_ref[...])
    y_ref[0] = x + _rms_scale(mix) * npost_ref[...]


def _attn_pair_mixer(x, tables, sinks, npre, kvn, wq, wkv, wo, npost, *, tt):
    batch, seq, _ = x.shape
    assert seq % tt == 0 and tt % (2 * CHUNK) == 0 and WINDOW == 2 * CHUNK
    ext = WINDOW + tt
    kern = functools.partial(_attn_pair_kernel, tt=tt)
    x_spec = pl.BlockSpec((1, tt, D_MODEL), lambda b, t: (b, t, 0))
    c_spec = pl.BlockSpec((1, WINDOW, KV_WIDTH), lambda b, t: (b, 0, 0))
    tab_spec = pl.BlockSpec((tt, LANES), lambda b, t: (t, 0))
    cache_shape = jax.ShapeDtypeStruct((batch, WINDOW, KV_WIDTH), F32)
    return pl.pallas_call(
        kern,
        grid=(batch, seq // tt),
        in_specs=[x_spec] + [tab_spec] * 4 + [
            pl.BlockSpec(memory_space=pltpu.SMEM),
            _const_spec((1, D_MODEL)),
            _const_spec((1, D_MODEL)),
            _const_spec((D_MODEL, D_MODEL)),
            _const_spec((D_MODEL, 2 * KV_WIDTH)),
            _const_spec((D_MODEL, D_MODEL)),
            _const_spec((1, D_MODEL)),
        ],
        out_specs=[x_spec, c_spec, c_spec],
        out_shape=[jax.ShapeDtypeStruct(x.shape, F32), cache_shape, cache_shape],
        scratch_shapes=[
            pltpu.VMEM((tt, D_MODEL), BF16),
            pltpu.VMEM((B_KV_HEADS, ext, LANES), BF16),
            pltpu.VMEM((B_KV_HEADS, ext, LANES), BF16),
            pltpu.VMEM((KV_WIDTH, ext), BF16),
            pltpu.VMEM((D_MODEL, tt), BF16),
        ],
        compiler_params=pltpu.CompilerParams(
            dimension_semantics=("parallel", "arbitrary"), vmem_limit_bytes=VMEM_LIMIT_BYTES),
        name="attn_pair_mixer",
    )(x, *tables, sinks, npre, kvn, wq, wkv, wo, npost)


def _rope_tables(pos, q_scale):
    half = B_HEAD_DIM // 2
    inv = ROPE_THETA ** (-jnp.arange(half, dtype=F32) / half)
    ang = pos.astype(F32)[:, None] * inv[None, :]
    cos = jnp.cos(ang)
    sin = jnp.sin(ang)
    reps = LANES // B_HEAD_DIM
    cos_t = jnp.tile(jnp.concatenate([cos, cos], axis=1), (1, reps))
    sin_t = jnp.tile(jnp.concatenate([-sin, sin], axis=1), (1, reps))
    return cos_t * q_scale, sin_t * q_scale, cos_t, sin_t


def _trunk(x, pos, state, cache_k, cache_v, w, *, nb, tt, hgrn_chunk, cq):
    batch, seq, _ = x.shape
    row = lambda a: a.reshape(1, D_MODEL)
    x, st = _hgrn_mixer(x, state, row(w["norm_mix_pre"][0]), w["w_a_in"], w["a_lower_bound"],
                        row(w["a_out_norm"]), w["w_a_out"], row(w["norm_mix_post"][0]),
                        nb=nb, tt=tt, chunk=hgrn_chunk)
    x = _ffn(x.reshape(batch * seq, D_MODEL), row(w["norm_ffn_pre"][0]), row(w["norm_ffn_post"][0]),
             w["w_ffn_in"][0], w["w_ffn_out"][0]).reshape(batch, seq, D_MODEL)
    attn_w = (w["b_sinks"], row(w["norm_mix_pre"][1]), row(w["kv_norm"]), w["w_b_q"], w["w_kv"],
              w["w_b_out"], row(w["norm_mix_post"][1]))
    if cache_k is None and nb == 1 and cq == CHUNK:
        x, kc, vc = _attn_pair_mixer(x, _rope_tables(pos, SOFTMAX_SCALE * LOG2_E), *attn_w, tt=tt)
    else:
        x, kc, vc = _attn_mixer(x, cache_k, cache_v, _rope_tables(pos, SOFTMAX_SCALE), *attn_w,
                                nb=nb, tt=tt, cq=cq)
    x = _ffn(x.reshape(batch * seq, D_MODEL), row(w["norm_ffn_pre"][1]), row(w["norm_ffn_post"][1]),
             w["w_ffn_in"][1], w["w_ffn_out"][1]).reshape(batch, seq, D_MODEL)
    return x, st, kc, vc


def kernel(x_prompt, x_sample, state_hgrn, cache_k, cache_v, norm_mix_pre, norm_mix_post, norm_ffn_pre, norm_ffn_post, w_ffn_in, w_ffn_out, w_a_in, a_lower_bound, a_out_norm, w_a_out, kv_norm, w_kv, w_b_q, b_sinks, w_b_out):
    w = dict(
        norm_mix_pre=norm_mix_pre, norm_mix_post=norm_mix_post,
        norm_ffn_pre=norm_ffn_pre, norm_ffn_post=norm_ffn_post,
        w_ffn_in=[w_ffn_in[l].astype(BF16) for l in range(w_ffn_in.shape[0])],
        w_ffn_out=[w_ffn_out[l].astype(BF16) for l in range(w_ffn_out.shape[0])],
        w_a_in=w_a_in[0].astype(BF16), a_lower_bound=a_lower_bound,
        a_out_norm=a_out_norm[0], w_a_out=w_a_out[0].astype(BF16),
        kv_norm=kv_norm, w_kv=w_kv.astype(BF16), w_b_q=w_b_q[0].astype(BF16),
        b_sinks=b_sinks[0], w_b_out=w_b_out[0].astype(BF16),
    )
    bp, tp, _ = x_prompt.shape
    bs, ts, _ = x_sample.shape

    zero_state = jnp.zeros((bp, A_HEADS, A_DK, A_DV), F32)
    y_p, st_p, kc_p, vc_p = _trunk(
        x_prompt, jnp.arange(tp), zero_state, None, None, w,
        nb=1, tt=ROW_TILE, hgrn_chunk=HGRN_CHUNK, cq=CHUNK)

    nb_s = ROW_TILE // (2 * ts)
    y_s, st_s, kc_s, vc_s = _trunk(
        x_sample, PAST_LEN + jnp.arange(ts), state_hgrn[0],
        cache_k.reshape(bs, WINDOW, KV_WIDTH), cache_v.reshape(bs, WINDOW, KV_WIDTH), w,
        nb=nb_s, tt=ts, hgrn_chunk=ts, cq=ts)

    cache4 = lambda a: a.reshape(a.shape[0], WINDOW, B_KV_HEADS, B_HEAD_DIM)
    return (y_p, y_s, st_p[None], st_s[None],
            cache4(kc_p), cache4(vc_p), cache4(kc_s), cache4(vc_s))
```

```python
import functools

import jax
import jax.numpy as jnp
from jax import lax
from jax.experimental import pallas as pl
from jax.experimental.pallas import tpu as pltpu

F32 = jnp.float32
BF16 = jnp.bfloat16

D_MODEL = 1024
A_HEADS = 8
A_DK = 128
A_DV = 128
B_HEAD_DIM = 64
B_Q_HEADS = 16
B_KV_HEADS = 4
B_GROUP = B_Q_HEADS // B_KV_HEADS
KV_WIDTH = B_KV_HEADS * B_HEAD_DIM
WINDOW = 128
CHUNK = 64
PAST_LEN = 2048
D_FF = 2816
ROPE_THETA = 10000.0
NORM_EPS = 1e-6
SOFTMAX_SCALE = B_HEAD_DIM ** -0.5
LOG2_E = 1.4426950408889634

LANES = 128
SUBLANES = 8
VMEM_LIMIT_BYTES = 52 * 1024 * 1024

ROW_TILE = 512
HGRN_CHUNK = 128
FFN_CHUNK = 256
MAX_FACTOR_EXPONENT = 60.0
EXACT_BLOCK = 16
STAGE_ROWS = 128
HEAD_GROUPS = 2


def _rms_scale(x):
    ms = jnp.mean(x * x, axis=-1, keepdims=True)
    return x * lax.rsqrt(ms + NORM_EPS)


def _dot(a, b):
    return jnp.dot(a, b, preferred_element_type=F32)


def _dot_nt(a, b):
    return lax.dot_general(a, b, (((1,), (1,)), ((), ())), preferred_element_type=F32)


def _dot_tn(a, b):
    return lax.dot_general(a, b, (((0,), (0,)), ((), ())), preferred_element_type=F32)


def _const_spec(shape):
    nd = len(shape)
    return pl.BlockSpec(shape, lambda *_: (0,) * nd, pipeline_mode=pl.Buffered(1))


def _ffn_kernel(x_ref, npre_ref, npost_ref, win_ref, wout_ref, o_ref):
    x = x_ref[...]
    h = (_rms_scale(x) * npre_ref[...]).astype(BF16)
    acc = jnp.zeros(x.shape, F32)
    for c0 in range(0, D_FF, FFN_CHUNK):
        a = _dot(h, win_ref[:, c0:c0 + FFN_CHUNK])
        b = _dot(h, win_ref[:, D_FF + c0:D_FF + c0 + FFN_CHUNK])
        g = (a * jax.nn.sigmoid(a) * b).astype(BF16)
        acc = acc + _dot(g, wout_ref[c0:c0 + FFN_CHUNK, :])
    o_ref[...] = x + _rms_scale(acc) * npost_ref[...]


def _ffn(x2d, npre, npost, win, wout, layer):
    rows = x2d.shape[0]
    tile = min(ROW_TILE, rows)
    assert rows % tile == 0 and D_FF % FFN_CHUNK == 0

    def layer_spec(shape):
        return pl.BlockSpec((None,) + shape, lambda i: (layer, 0, 0), pipeline_mode=pl.Buffered(1))

    return pl.pallas_call(
        _ffn_kernel,
        grid=(rows // tile,),
        in_specs=[
            pl.BlockSpec((tile, D_MODEL), lambda i: (i, 0)),
            _const_spec((1, D_MODEL)),
            _const_spec((1, D_MODEL)),
            layer_spec((D_MODEL, 2 * D_FF)),
            layer_spec((D_FF, D_MODEL)),
        ],
        out_specs=pl.BlockSpec((tile, D_MODEL), lambda i: (i, 0)),
        out_shape=jax.ShapeDtypeStruct((rows, D_MODEL), F32),
        compiler_params=pltpu.CompilerParams(
            dimension_semantics=("parallel",), vmem_limit_bytes=VMEM_LIMIT_BYTES),
        name="ffn",
    )(x2d, npre, npost, win, wout)


def _cumsum_rows(x):
    c, w = x.shape
    groups = c // SUBLANES
    y = x.reshape(groups, SUBLANES, w)
    sub = lax.broadcasted_iota(jnp.int32, y.shape, 1)
    shift = 1
    while shift < SUBLANES:
        y = y + jnp.where(sub >= shift, pltpu.roll(y, shift, axis=1), 0.0)
        shift *= 2
    tot = jnp.broadcast_to(y[:, SUBLANES - 1:SUBLANES, :], y.shape)
    inc = tot
    shift = 1
    while shift < groups:
        inc = inc + jnp.concatenate(
            [jnp.zeros((shift, SUBLANES, w), F32), inc[:groups - shift]], axis=0)
        shift *= 2
    return (y + (inc - tot)).reshape(c, w)


def _hgrn_kernel(x_ref, st_in_ref, npre_ref, win_ref, alb_ref, gnorm_ref, wout_ref, npost_ref,
                 y_ref, st_ref, qs_ref, lf_ref, inp_ref, v_ref, sg_ref, on_ref,
                 *, nb, tt, chunk):
    t = pl.program_id(1)
    rows = nb * tt

    @pl.when(t == 0)
    def _():
        for n in range(nb):
            for h in range(A_HEADS):
                st_ref[n, h] = st_in_ref[n, h].T

    x = x_ref[...].reshape(rows, D_MODEL)
    hn = (_rms_scale(x) * npre_ref[...]).astype(BF16)

    alb = alb_ref[...]
    e = jnp.exp(alb - jnp.max(alb, axis=0, keepdims=True))
    lb = e[0:1] / jnp.sum(e, axis=0, keepdims=True)

    f = _dot(hn, win_ref[:, D_MODEL:2 * D_MODEL])
    forget = lb + (1.0 - lb) * jax.nn.sigmoid(f)
    lf_ref[...] = jnp.log(forget)
    inp_ref[...] = 1.0 - forget

    def project_query(c0, width):
        q = _dot(hn, win_ref[:, c0:c0 + width])
        qs_ref[:, c0:c0 + width] = q * jax.nn.sigmoid(q)

    def project_value(c0, width):
        v_ref[:, c0:c0 + width] = _dot(
            hn, win_ref[:, 2 * D_MODEL + c0:2 * D_MODEL + c0 + width]).astype(BF16)

    def project_gate(c0, width):
        g = _dot(hn, win_ref[:, 3 * D_MODEL + c0:3 * D_MODEL + c0 + width])
        sg_ref[:, c0:c0 + width] = g * jax.nn.sigmoid(g)

    def write_rows(mix):
        y = x + _rms_scale(mix) * npost_ref[...]
        y_ref[...] = y.reshape(nb, tt, D_MODEL)

    half = chunk // 2
    ri = lax.broadcasted_iota(jnp.int32, (chunk, chunk), 0)
    ci = lax.broadcasted_iota(jnp.int32, (chunk, chunk), 1)
    causal = ri >= ci

    def prepare(blk, h):
        rs = slice(blk * chunk, (blk + 1) * chunk)
        sl = slice(h * A_DK, (h + 1) * A_DK)
        b = _cumsum_rows(lf_ref[rs, sl])
        bmid = b[half - 1:half]
        blast = b[chunk - 1:chunk]
        qsc = qs_ref[rs, sl] * jnp.exp(b - bmid)
        inp = inp_ref[rs, sl] * jnp.exp(bmid - b)
        return dict(
            rs=rs, sl=sl, qt=qsc.astype(BF16), kt=inp.astype(BF16),
            qi=(qsc * jnp.exp(bmid)).astype(BF16),
            kh=(inp * jnp.exp(blast - bmid)).astype(BF16),
            vh=v_ref[rs, sl], decay=jnp.exp(blast))

    def emit_output(u, sc, st):
        p = jnp.where(causal, sc, 0.0).astype(BF16)
        o = _dot(p, u["vh"]) + _dot_nt(u["qi"], st.astype(BF16))
        on = _rms_scale(o) * gnorm_ref[:, u["sl"]] * sg_ref[u["rs"], u["sl"]]
        on_ref[u["rs"], u["sl"]] = on.astype(BF16)

    def recurrence(heads, jobs):
        blocks_per_stage = STAGE_ROWS // chunk
        nstages = rows // STAGE_ROWS
        slots = 2 * nstages
        jobs = list(jobs)

        def run_jobs(slot):
            for _ in range(-(-len(jobs) // (slots - slot))):
                jobs.pop(0)()

        states = {}
        for stage in range(nstages):
            blocks = range(stage * blocks_per_stage, (stage + 1) * blocks_per_stage)
            units = [(blk, h, prepare(blk, h)) for blk in blocks for h in heads]
            scores = [_dot_nt(u["qt"], u["kt"]) for _, _, u in units]
            run_jobs(2 * stage)
            updated = {}
            for blk, h, u in units:
                key = ((blk * chunk) // tt, h)
                if key not in states:
                    states[key] = st_ref[key[0], h]
                updated[key] = states[key] * u["decay"] + _dot_tn(u["vh"], u["kh"])
            run_jobs(2 * stage + 1)
            for (blk, h, u), sc in zip(units, scores):
                emit_output(u, sc, states[((blk * chunk) // tt, h)])
            states.update(updated)
            if (blocks[-1] + 1) * chunk % tt == 0:
                for (n, h) in list(states):
                    st_ref[n, h] = states.pop((n, h))

    def factorised_path():
        heads_per_group = A_HEADS // HEAD_GROUPS
        piece = 2 * LANES
        partial_mix = [[] for _ in range(HEAD_GROUPS)]

        def project_out(g, c0, width):
            partial_mix[g].append(
                _dot(on_ref[:, g * gw:(g + 1) * gw], wout_ref[g * gw:(g + 1) * gw, c0:c0 + width]))

        for g in range(HEAD_GROUPS):
            jobs = []
            if g + 1 < HEAD_GROUPS:
                jobs += [functools.partial(proj, (g + 1) * gw + c0, piece)
                         for proj in (project_query, project_value, project_gate)
                         for c0 in range(0, gw, piece)]
            if g > 0:
                jobs += [functools.partial(project_out, g - 1, c0, piece)
                         for c0 in range(0, D_MODEL, piece)]
            recurrence(range(g * heads_per_group, (g + 1) * heads_per_group), jobs)
        project_out(HEAD_GROUPS - 1, 0, D_MODEL)
        write_rows(functools.reduce(
            jnp.add, [jnp.concatenate(parts, axis=1) for parts in partial_mix]))

    def exact_step(idx, carry):
        n = (idx * EXACT_BLOCK) // tt
        rs = pl.ds(pl.multiple_of(idx * EXACT_BLOCK, EXACT_BLOCK), EXACT_BLOCK)
        row = lax.broadcasted_iota(jnp.int32, (EXACT_BLOCK, A_DK), 0)
        for h in range(A_HEADS):
            sl = slice(h * A_DK, (h + 1) * A_DK)
            b = _cumsum_rows(lf_ref[rs, sl])
            blast = b[EXACT_BLOCK - 1:EXACT_BLOCK]
            qs = qs_ref[rs, sl]
            inp = inp_ref[rs, sl]
            vh = v_ref[rs, sl]
            vf = vh.astype(F32)
            st = st_ref[n, h]
            o = _dot_nt((qs * jnp.exp(b)).astype(BF16), st.astype(BF16))
            for s in range(EXACT_BLOCK):
                decay = jnp.exp(jnp.where(row >= s, b - b[s:s + 1], -jnp.inf))
                score = jnp.sum(qs * decay * inp[s:s + 1], axis=-1, keepdims=True)
                o = o + score * vf[s:s + 1]
            kh = (inp * jnp.exp(blast - b)).astype(BF16)
            st_ref[n, h] = st * jnp.exp(blast) + _dot_tn(vh, kh)
            on = _rms_scale(o) * gnorm_ref[:, sl] * sg_ref[rs, sl]
            on_ref[rs, sl] = on.astype(BF16)
        return carry

    gw = D_MODEL // HEAD_GROUPS
    project_query(0, gw)
    project_value(0, gw)
    project_gate(0, gw)

    max_step = -jnp.min(lf_ref[...])
    bounded = max_step * half <= MAX_FACTOR_EXPONENT

    pl.when(bounded)(factorised_path)

    @pl.when(jnp.logical_not(bounded))
    def _():
        project_query(gw, D_MODEL - gw)
        project_value(gw, D_MODEL - gw)
        project_gate(gw, D_MODEL - gw)
        lax.fori_loop(0, rows // EXACT_BLOCK, exact_step, 0)
        write_rows(_dot(on_ref[...], wout_ref[...]))

    @pl.when(t == pl.num_programs(1) - 1)
    def _():
        for n in range(nb):
            for h in range(A_HEADS):
                st_ref[n, h] = st_ref[n, h].T


def _hgrn_mixer(x, state, npre, win, alb, gnorm, wout, npost, *, nb, tt, chunk):
    batch, seq, _ = x.shape
    assert batch % nb == 0 and seq % tt == 0 and tt % chunk == 0 and chunk % (2 * SUBLANES) == 0
    rows = nb * tt
    assert rows % STAGE_ROWS == 0 and STAGE_ROWS % chunk == 0
    assert (nb == 1 and tt % STAGE_ROWS == 0) or STAGE_ROWS % tt == 0
    kern = functools.partial(_hgrn_kernel, nb=nb, tt=tt, chunk=chunk)
    st_spec = pl.BlockSpec((nb, A_HEADS, A_DK, A_DV), lambda b, t: (b, 0, 0, 0))
    return pl.pallas_call(
        kern,
        grid=(batch // nb, seq // tt),
        in_specs=[
            pl.BlockSpec((nb, tt, D_MODEL), lambda b, t: (b, t, 0)),
            st_spec,
            _const_spec((1, D_MODEL)),
            _const_spec((D_MODEL, 4 * D_MODEL)),
            _const_spec(alb.shape),
            _const_spec((1, D_MODEL)),
            _const_spec((D_MODEL, D_MODEL)),
            _const_spec((1, D_MODEL)),
        ],
        out_specs=[
            pl.BlockSpec((nb, tt, D_MODEL), lambda b, t: (b, t, 0)),
            st_spec,
        ],
        out_shape=[
            jax.ShapeDtypeStruct(x.shape, F32),
            jax.ShapeDtypeStruct(state.shape, F32),
        ],
        scratch_shapes=[
            pltpu.VMEM((rows, D_MODEL), F32),
            pltpu.VMEM((rows, D_MODEL), F32),
            pltpu.VMEM((rows, D_MODEL), F32),
            pltpu.VMEM((rows, D_MODEL), BF16),
            pltpu.VMEM((rows, D_MODEL), F32),
            pltpu.VMEM((rows, D_MODEL), BF16),
        ],
        compiler_params=pltpu.CompilerParams(
            dimension_semantics=("parallel", "arbitrary"), vmem_limit_bytes=VMEM_LIMIT_BYTES),
        name="hgrn_mixer",
    )(x, state, npre, win, alb, gnorm, wout, npost)


def _rope_cols(x, cos_t, sin_t):
    lane = lax.broadcasted_iota(jnp.int32, (x.shape[0], LANES), 1)
    first_half = (lane % B_HEAD_DIM) < (B_HEAD_DIM // 2)
    cols = []
    for c0 in range(0, x.shape[1], LANES):
        xc = x[:, c0:c0 + LANES]
        partner = jnp.where(first_half,
                            pltpu.roll(xc, LANES - B_HEAD_DIM // 2, axis=1),
                            pltpu.roll(xc, B_HEAD_DIM // 2, axis=1))
        cols.append(xc * cos_t + partner * sin_t)
    return jnp.concatenate(cols, axis=1)


def _attn_kernel(*refs, nb, tt, cq, has_cache):
    if has_cache:
        (x_ref, kc_in_ref, vc_in_ref, cosq_ref, sinq_ref, cosk_ref, sinkt_ref, sinks_ref,
         npre_ref, kvn_ref, wq_ref, wkv_ref, wo_ref, npost_ref,
         y_ref, kc_ref, vc_ref, q_ref, kd_ref, vl_ref, vh_ref, o_ref) = refs
    else:
        (x_ref, cosq_ref, sinq_ref, cosk_ref, sinkt_ref, sinks_ref,
         npre_ref, kvn_ref, wq_ref, wkv_ref, wo_ref, npost_ref,
         y_ref, kc_ref, vc_ref, q_ref, kd_ref, vl_ref, vh_ref, o_ref) = refs
    t = pl.program_id(1)
    rows = nb * tt
    ext = WINDOW + tt
    nkeys = WINDOW + cq
    nchunks = tt // cq

    @pl.when(t == 0)
    def _():
        if has_cache:
            kc_ref[...] = kc_in_ref[...]
            vc_ref[...] = vc_in_ref[...]
        else:
            kc_ref[...] = jnp.zeros(kc_ref.shape, F32)
            vc_ref[...] = jnp.zeros(vc_ref.shape, F32)

    x = x_ref[...].reshape(rows, D_MODEL)
    xs = _rms_scale(x)
    hq = (xs * npre_ref[...]).astype(BF16)
    hk = (xs * kvn_ref[...]).astype(BF16)

    def per_stream(tab_ref):
        tab = tab_ref[...]
        return tab if nb == 1 else jnp.concatenate([tab] * nb, axis=0)

    q = _dot(hq, wq_ref[...])
    q_ref[...] = _rope_cols(q, per_stream(cosq_ref), per_stream(sinq_ref)).astype(BF16)
    kv = _dot(hk, wkv_ref[...])
    k_new = _rope_cols(kv[:, :KV_WIDTH], per_stream(cosk_ref), per_stream(sinkt_ref))
    v_new = kv[:, KV_WIDTH:]

    lane = lax.broadcasted_iota(jnp.int32, (ext, LANES), 1)
    low = lane < B_HEAD_DIM
    for n in range(nb):
        k_ext = jnp.concatenate([kc_ref[n], k_new[n * tt:(n + 1) * tt]], axis=0)
        v_ext = jnp.concatenate([vc_ref[n], v_new[n * tt:(n + 1) * tt]], axis=0)
        kc_ref[n] = k_ext[ext - WINDOW:]
        vc_ref[n] = v_ext[ext - WINDOW:]
        for m in range(KV_WIDTH // LANES):
            ka = k_ext[:, m * LANES:(m + 1) * LANES]
            kr = pltpu.roll(ka, B_HEAD_DIM, axis=1)
            va = v_ext[:, m * LANES:(m + 1) * LANES]
            vr = pltpu.roll(va, B_HEAD_DIM, axis=1)
            es = pl.ds(n * ext, ext)
            kd_ref[2 * m, es, :] = jnp.where(low, ka, kr).astype(BF16)
            kd_ref[2 * m + 1, es, :] = jnp.where(low, kr, ka).astype(BF16)
            vl_ref[2 * m, es, :] = jnp.where(low, va, 0.0).astype(BF16)
            vh_ref[2 * m, es, :] = jnp.where(low, 0.0, vr).astype(BF16)
            vl_ref[2 * m + 1, es, :] = jnp.where(low, vr, 0.0).astype(BF16)
            vh_ref[2 * m + 1, es, :] = jnp.where(low, 0.0, va).astype(BF16)

    qlane_low = lax.broadcasted_iota(jnp.int32, (cq, LANES), 1) < B_HEAD_DIM
    key_idx = lax.broadcasted_iota(jnp.int32, (1, nkeys), 1)

    def chunk_step(idx, carry):
        n = idx // nchunks
        c = idx % nchunks
        r0 = pl.multiple_of(idx * cq, cq)
        k0 = pl.multiple_of(n * ext + c * cq, SUBLANES * 2)
        if has_cache:
            bias = None
        else:
            first_pos = t * tt + c * cq - WINDOW
            bias = jnp.where(key_idx + first_pos >= 0, 0.0, -jnp.inf)
        for j in range(B_KV_HEADS):
            qa = q_ref[pl.ds(r0, cq), 2 * j * LANES:(2 * j + 1) * LANES]
            qb = q_ref[pl.ds(r0, cq), (2 * j + 1) * LANES:(2 * j + 2) * LANES]
            zero = jnp.zeros_like(qa)
            qstack = jnp.concatenate([
                jnp.where(qlane_low, qa, zero), jnp.where(qlane_low, qb, zero),
                jnp.where(qlane_low, zero, qa), jnp.where(qlane_low, zero, qb)], axis=0)
            s = _dot_nt(qstack, kd_ref[j, pl.ds(k0, nkeys), :])
            if bias is not None:
                s = s + bias
            heads = (4 * j, 4 * j + 2, 4 * j + 1, 4 * j + 3)
            sink = jnp.concatenate(
                [jnp.full((cq, 1), sinks_ref[hd], F32) for hd in heads], axis=0)
            mx = jnp.maximum(jnp.max(s, axis=-1, keepdims=True), sink)
            p = jnp.exp(s - mx)
            denom = jnp.sum(p, axis=-1, keepdims=True) + jnp.exp(sink - mx)
            p = (p / denom).astype(BF16)
            out = (_dot(p[:2 * cq], vl_ref[j, pl.ds(k0, nkeys), :])
                   + _dot(p[2 * cq:], vh_ref[j, pl.ds(k0, nkeys), :]))
            o_ref[pl.ds(r0, cq), 2 * j * LANES:(2 * j + 1) * LANES] = out[:cq].astype(BF16)
            o_ref[pl.ds(r0, cq), (2 * j + 1) * LANES:(2 * j + 2) * LANES] = out[cq:].astype(BF16)
        return carry

    lax.fori_loop(0, nb * nchunks, chunk_step, 0)

    mix = _dot(o_ref[...], wo_ref[...])
    y = x + _rms_scale(mix) * npost_ref[...]
    y_ref[...] = y.reshape(nb, tt, D_MODEL)


def _attn_mixer(x, cache_k, cache_v, tables, sinks, npre, kvn, wq, wkv, wo, npost, *, nb, tt, cq):
    batch, seq, _ = x.shape
    has_cache = cache_k is not None
    assert batch % nb == 0 and seq % tt == 0 and tt % cq == 0 and cq % (2 * SUBLANES) == 0
    rows = nb * tt
    ext = WINDOW + tt
    kern = functools.partial(_attn_kernel, nb=nb, tt=tt, cq=cq, has_cache=has_cache)
    x_spec = pl.BlockSpec((nb, tt, D_MODEL), lambda b, t: (b, t, 0))
    c_spec = pl.BlockSpec((nb, WINDOW, KV_WIDTH), lambda b, t: (b, 0, 0))
    tab_spec = pl.BlockSpec((tt, LANES), lambda b, t: (t, 0))
    in_specs = [x_spec]
    args = [x]
    if has_cache:
        in_specs += [c_spec, c_spec]
        args += [cache_k, cache_v]
    in_specs += [tab_spec] * 4 + [
        pl.BlockSpec(memory_space=pltpu.SMEM),
        _const_spec((1, D_MODEL)),
        _const_spec((1, D_MODEL)),
        _const_spec((D_MODEL, D_MODEL)),
        _const_spec((D_MODEL, 2 * KV_WIDTH)),
        _const_spec((D_MODEL, D_MODEL)),
        _const_spec((1, D_MODEL)),
    ]
    args += list(tables) + [sinks, npre, kvn, wq, wkv, wo, npost]
    cache_shape = jax.ShapeDtypeStruct((batch, WINDOW, KV_WIDTH), F32)
    return pl.pallas_call(
        kern,
        grid=(batch // nb, seq // tt),
        in_specs=in_specs,
        out_specs=[x_spec, c_spec, c_spec],
        out_shape=[jax.ShapeDtypeStruct(x.shape, F32), cache_shape, cache_shape],
        scratch_shapes=[
            pltpu.VMEM((rows, D_MODEL), BF16),
            pltpu.VMEM((B_KV_HEADS, nb * ext, LANES), BF16),
            pltpu.VMEM((B_KV_HEADS, nb * ext, LANES), BF16),
            pltpu.VMEM((B_KV_HEADS, nb * ext, LANES), BF16),
            pltpu.VMEM((rows, D_MODEL), BF16),
        ],
        compiler_params=pltpu.CompilerParams(
            dimension_semantics=("parallel", "arbitrary"), vmem_limit_bytes=VMEM_LIMIT_BYTES),
        name="attn_mixer",
    )(*args)


def _attn_pair_kernel(x_ref, cosq_ref, sinq_ref, cosk_ref, sinkt_ref, sinks_ref,
                      npre_ref, kvn_ref, wq_ref, wkv_ref, wo_ref, npost_ref,
                      y_ref, kc_ref, vc_ref, q_ref, kl_ref, kh_ref, vt_ref, ot_ref, *, tt):
    t = pl.program_id(1)
    ext = WINDOW + tt
    pair = 2 * CHUNK
    nkeys = WINDOW + pair
    nvis = WINDOW + CHUNK

    @pl.when(t == 0)
    def _():
        kc_ref[...] = jnp.zeros(kc_ref.shape, F32)
        vc_ref[...] = jnp.zeros(vc_ref.shape, F32)

    x = x_ref[0]
    xs = _rms_scale(x)
    hq = (xs * npre_ref[...]).astype(BF16)
    hk = (xs * kvn_ref[...]).astype(BF16)

    q = _dot(hq, wq_ref[...])
    q_ref[...] = _rope_cols(q, cosq_ref[...], sinq_ref[...]).astype(BF16)
    kv = _dot(hk, wkv_ref[...])
    k_ext = jnp.concatenate(
        [kc_ref[0], _rope_cols(kv[:, :KV_WIDTH], cosk_ref[...], sinkt_ref[...])], axis=0)
    v_ext = jnp.concatenate([vc_ref[0], kv[:, KV_WIDTH:]], axis=0)
    kc_ref[0] = k_ext[ext - WINDOW:]
    vc_ref[0] = v_ext[ext - WINDOW:]
    vt_ref[...] = v_ext.T.astype(BF16)

    low = lax.broadcasted_iota(jnp.int32, (ext, LANES), 1) < B_HEAD_DIM
    for m in range(KV_WIDTH // LANES):
        ka = k_ext[:, m * LANES:(m + 1) * LANES]
        kr = pltpu.roll(ka, B_HEAD_DIM, axis=1)
        kl_ref[2 * m] = jnp.where(low, ka, 0.0).astype(BF16)
        kh_ref[2 * m] = jnp.where(low, 0.0, kr).astype(BF16)
        kl_ref[2 * m + 1] = jnp.where(low, kr, 0.0).astype(BF16)
        kh_ref[2 * m + 1] = jnp.where(low, 0.0, ka).astype(BF16)

    vis_row = lax.broadcasted_iota(jnp.int32, (nvis, LANES), 0)
    first_bias = [jnp.where(vis_row + (t * tt - WINDOW + qc * CHUNK) >= 0, 0.0, -jnp.inf)
                  for qc in range(2)]
    first_head = lax.broadcasted_iota(jnp.int32, (1, LANES), 1) < CHUNK
    no_keys = jnp.zeros((CHUNK, LANES), BF16)

    units = [(j, kx_ref, ha, hb) for j in range(B_KV_HEADS)
             for kx_ref, ha, hb in ((kl_ref, 4 * j, 4 * j + 2), (kh_ref, 4 * j + 1, 4 * j + 3))]

    def scores(e):
        r0 = e * pair
        out = []
        for j, kx_ref, _, _ in units:
            ca = slice(2 * j * LANES, (2 * j + 1) * LANES)
            cb = slice((2 * j + 1) * LANES, (2 * j + 2) * LANES)
            qq = jnp.concatenate([q_ref[r0:r0 + CHUNK, ca], q_ref[r0:r0 + CHUNK, cb],
                                  q_ref[r0 + CHUNK:r0 + pair, ca], q_ref[r0 + CHUNK:r0 + pair, cb]], axis=0)
            out.append(_dot_nt(kx_ref[j, r0:r0 + nkeys, :], qq))
        return out

    def softmax(e, s_list):
        out = []
        for (_, _, ha, hb), s in zip(units, s_list):
            sink = jnp.where(first_head, sinks_ref[ha], sinks_ref[hb]) * LOG2_E
            ps, dens = [], []
            for qc in range(2):
                sq = s[qc * CHUNK:qc * CHUNK + nvis, qc * LANES:(qc + 1) * LANES]
                if e == 0:
                    sq = sq + first_bias[qc]
                mx = jnp.maximum(jnp.max(sq, axis=0, keepdims=True), sink)
                p = jnp.exp2(sq - mx)
                dens.append(jnp.sum(p, axis=0, keepdims=True) + jnp.exp2(sink - mx))
                ps.append(p.astype(BF16))
            pmat = jnp.concatenate([jnp.concatenate([ps[0], no_keys], axis=0),
                                    jnp.concatenate([no_keys, ps[1]], axis=0)], axis=1)
            out.append((pmat, jnp.concatenate(dens, axis=1)))
        return out

    def weighted_values(e, pd_list):
        r0 = e * pair
        for (j, _, ha, hb), (pmat, den) in zip(units, pd_list):
            o = _dot(vt_ref[j * B_HEAD_DIM:(j + 1) * B_HEAD_DIM, r0:r0 + nkeys], pmat)
            o = (o * (1.0 / den)).astype(BF16)
            for qc in range(2):
                c0 = r0 + qc * CHUNK
                ot_ref[ha * B_HEAD_DIM:(ha + 1) * B_HEAD_DIM, c0:c0 + CHUNK] = (
                    o[:, qc * LANES:qc * LANES + CHUNK])
                ot_ref[hb * B_HEAD_DIM:(hb + 1) * B_HEAD_DIM, c0:c0 + CHUNK] = (
                    o[:, qc * LANES + CHUNK:(qc + 1) * LANES])

    npairs = tt // pair
    s_next = scores(0)
    for e in range(npairs):
        s_cur = s_next
        if e + 1 < npairs:
            s_next = scores(e + 1)
        weighted_values(e, softmax(e, s_cur))

    mix = _dot_tn(ot_ref[...], wo_ref[...])
    y_ref[0] = x + _rms_scale(mix) * npost_ref[...]


def _attn_pair_mixer(x, tables, sinks, npre, kvn, wq, wkv, wo, npost, *, tt):
    batch, seq, _ = x.shape
    assert seq % tt == 0 and tt % (2 * CHUNK) == 0 and WINDOW == 2 * CHUNK
    ext = WINDOW + tt
    kern = functools.partial(_attn_pair_kernel, tt=tt)
    x_spec = pl.BlockSpec((1, tt, D_MODEL), lambda b, t: (b, t, 0))
    c_spec = pl.BlockSpec((1, WINDOW, KV_WIDTH), lambda b, t: (b, 0, 0))
    tab_spec = pl.BlockSpec((tt, LANES), lambda b, t: (t, 0))
    cache_shape = jax.ShapeDtypeStruct((batch, WINDOW, KV_WIDTH), F32)
    return pl.pallas_call(
        kern,
        grid=(batch, seq // tt),
        in_specs=[x_spec] + [tab_spec] * 4 + [
            pl.BlockSpec(memory_space=pltpu.SMEM),
            _const_spec((1, D_MODEL)),
            _const_spec((1, D_MODEL)),
            _const_spec((D_MODEL, D_MODEL)),
            _const_spec((D_MODEL, 2 * KV_WIDTH)),
            _const_spec((D_MODEL, D_MODEL)),
            _const_spec((1, D_MODEL)),
        ],
        out_specs=[x_spec, c_spec, c_spec],
        out_shape=[jax.ShapeDtypeStruct(x.shape, F32), cache_shape, cache_shape],
        scratch_shapes=[
            pltpu.VMEM((tt, D_MODEL), BF16),
            pltpu.VMEM((B_KV_HEADS, ext, LANES), BF16),
            pltpu.VMEM((B_KV_HEADS, ext, LANES), BF16),
            pltpu.VMEM((KV_WIDTH, ext), BF16),
            pltpu.VMEM((D_MODEL, tt), BF16),
        ],
        compiler_params=pltpu.CompilerParams(
            dimension_semantics=("parallel", "arbitrary"), vmem_limit_bytes=VMEM_LIMIT_BYTES),
        name="attn_pair_mixer",
    )(x, *tables, sinks, npre, kvn, wq, wkv, wo, npost)


def _rope_tables(pos, q_scale):
    half = B_HEAD_DIM // 2
    inv = ROPE_THETA ** (-jnp.arange(half, dtype=F32) / half)
    ang = pos.astype(F32)[:, None] * inv[None, :]
    cos = jnp.cos(ang)
    sin = jnp.sin(ang)
    reps = LANES // B_HEAD_DIM
    cos_t = jnp.tile(jnp.concatenate([cos, cos], axis=1), (1, reps))
    sin_t = jnp.tile(jnp.concatenate([-sin, sin], axis=1), (1, reps))
    return cos_t * q_scale, sin_t * q_scale, cos_t, sin_t


def _trunk(x, pos, state, cache_k, cache_v, w, *, nb, tt, hgrn_chunk, cq):
    batch, seq, _ = x.shape
    row = lambda a: a.reshape(1, D_MODEL)
    x, st = _hgrn_mixer(x, state, row(w["norm_mix_pre"][0]), w["w_a_in"], w["a_lower_bound"],
                        row(w["a_out_norm"]), w["w_a_out"], row(w["norm_mix_post"][0]),
                        nb=nb, tt=tt, chunk=hgrn_chunk)
    x = _ffn(x.reshape(batch * seq, D_MODEL), row(w["norm_ffn_pre"][0]), row(w["norm_ffn_post"][0]),
             w["w_ffn_in"], w["w_ffn_out"], 0).reshape(batch, seq, D_MODEL)
    attn_w = (w["b_sinks"], row(w["norm_mix_pre"][1]), row(w["kv_norm"]), w["w_b_q"], w["w_kv"],
              w["w_b_out"], row(w["norm_mix_post"][1]))
    if cache_k is None and nb == 1 and cq == CHUNK:
        x, kc, vc = _attn_pair_mixer(x, _rope_tables(pos, SOFTMAX_SCALE * LOG2_E), *attn_w, tt=tt)
    else:
        x, kc, vc = _attn_mixer(x, cache_k, cache_v, _rope_tables(pos, SOFTMAX_SCALE), *attn_w,
                                nb=nb, tt=tt, cq=cq)
    x = _ffn(x.reshape(batch * seq, D_MODEL), row(w["norm_ffn_pre"][1]), row(w["norm_ffn_post"][1]),
             w["w_ffn_in"], w["w_ffn_out"], 1).reshape(batch, seq, D_MODEL)
    return x, st, kc, vc


def kernel(x_prompt, x_sample, state_hgrn, cache_k, cache_v, norm_mix_pre, norm_mix_post, norm_ffn_pre, norm_ffn_post, w_ffn_in, w_ffn_out, w_a_in, a_lower_bound, a_out_norm, w_a_out, kv_norm, w_kv, w_b_q, b_sinks, w_b_out):
    w = dict(
        norm_mix_pre=norm_mix_pre, norm_mix_post=norm_mix_post,
        norm_ffn_pre=norm_ffn_pre, norm_ffn_post=norm_ffn_post,
        w_ffn_in=w_ffn_in.astype(BF16), w_ffn_out=w_ffn_out.astype(BF16),
        w_a_in=w_a_in[0].astype(BF16), a_lower_bound=a_lower_bound,
        a_out_norm=a_out_norm[0], w_a_out=w_a_out[0].astype(BF16),
        kv_norm=kv_norm, w_kv=w_kv.astype(BF16), w_b_q=w_b_q[0].astype(BF16),
        b_sinks=b_sinks[0], w_b_out=w_b_out[0].astype(BF16),
    )
    bp, tp, _ = x_prompt.shape
    bs, ts, _ = x_sample.shape

    zero_state = jnp.zeros((bp, A_HEADS, A_DK, A_DV), F32)
    y_p, st_p, kc_p, vc_p = _trunk(
        x_prompt, jnp.arange(tp), zero_state, None, None, w,
        nb=1, tt=ROW_TILE, hgrn_chunk=HGRN_CHUNK, cq=CHUNK)

    nb_s = ROW_TILE // (2 * ts)
    y_s, st_s, kc_s, vc_s = _trunk(
        x_sample, PAST_LEN + jnp.arange(ts), state_hgrn[0],
        cache_k.reshape(bs, WINDOW, KV_WIDTH), cache_v.reshape(bs, WINDOW, KV_WIDTH), w,
        nb=nb_s, tt=ts, hgrn_chunk=ts, cq=ts)

    cache4 = lambda a: a.reshape(a.shape[0], WINDOW, B_KV_HEADS, B_HEAD_DIM)
    return (y_p, y_s, st_p[None], st_s[None],
            cache4(kc_p), cache4(vc_p), cache4(kc_s), cache4(vc_s))
```

```python
import functools

import jax
import jax.numpy as jnp
from jax import lax
from jax.experimental import pallas as pl
from jax.experimental.pallas import tpu as pltpu

F32 = jnp.float32
BF16 = jnp.bfloat16

D_MODEL = 1024
A_HEADS = 8
A_DK = 128
A_DV = 128
B_HEAD_DIM = 64
B_Q_HEADS = 16
B_KV_HEADS = 4
B_GROUP = B_Q_HEADS // B_KV_HEADS
KV_WIDTH = B_KV_HEADS * B_HEAD_DIM
WINDOW = 128
CHUNK = 64
PAST_LEN = 2048
D_FF = 2816
ROPE_THETA = 10000.0
NORM_EPS = 1e-6
SOFTMAX_SCALE = B_HEAD_DIM ** -0.5
LOG2_E = 1.4426950408889634

LANES = 128
SUBLANES = 8
VMEM_LIMIT_BYTES = 52 * 1024 * 1024

ROW_TILE = 512
FFN_ROW_TILE = 1024
HGRN_CHUNK = 128
FFN_CHUNK = 256
MAX_FACTOR_EXPONENT = 60.0
EXACT_BLOCK = 16
STAGE_ROWS = 128
HEAD_GROUPS = 2


def _rms_scale(x):
    ms = jnp.mean(x * x, axis=-1, keepdims=True)
    return x * lax.rsqrt(ms + NORM_EPS)


def _dot(a, b):
    return jnp.dot(a, b, preferred_element_type=F32)


def _dot_nt(a, b):
    return lax.dot_general(a, b, (((1,), (1,)), ((), ())), preferred_element_type=F32)


def _dot_tn(a, b):
    return lax.dot_general(a, b, (((0,), (0,)), ((), ())), preferred_element_type=F32)


def _const_spec(shape):
    nd = len(shape)
    return pl.BlockSpec(shape, lambda *_: (0,) * nd, pipeline_mode=pl.Buffered(1))


def _ffn_kernel(x_ref, npre_ref, npost_ref, win_ref, wout_ref, o_ref, *, sub_rows):
    starts = range(0, x_ref.shape[0], sub_rows)
    normed = [(_rms_scale(x_ref[r0:r0 + sub_rows, :]) * npre_ref[...]).astype(BF16) for r0 in starts]
    for r0, h in zip(starts, normed):
        x = x_ref[r0:r0 + sub_rows, :]
        acc = jnp.zeros(x.shape, F32)
        for c0 in range(0, D_FF, FFN_CHUNK):
            a = _dot(h, win_ref[:, c0:c0 + FFN_CHUNK])
            b = _dot(h, win_ref[:, D_FF + c0:D_FF + c0 + FFN_CHUNK])
            g = (a * jax.nn.sigmoid(a) * b).astype(BF16)
            acc = acc + _dot(g, wout_ref[c0:c0 + FFN_CHUNK, :])
        o_ref[r0:r0 + sub_rows, :] = x + _rms_scale(acc) * npost_ref[...]


def _ffn(x2d, npre, npost, win, wout, layer):
    rows = x2d.shape[0]
    tile = min(FFN_ROW_TILE, rows)
    sub_rows = min(ROW_TILE, tile)
    assert rows % tile == 0 and tile % sub_rows == 0 and D_FF % FFN_CHUNK == 0

    def layer_spec(shape):
        return pl.BlockSpec((None,) + shape, lambda i: (layer, 0, 0), pipeline_mode=pl.Buffered(1))

    return pl.pallas_call(
        functools.partial(_ffn_kernel, sub_rows=sub_rows),
        grid=(rows // tile,),
        in_specs=[
            pl.BlockSpec((tile, D_MODEL), lambda i: (i, 0)),
            _const_spec((1, D_MODEL)),
            _const_spec((1, D_MODEL)),
            layer_spec((D_MODEL, 2 * D_FF)),
            layer_spec((D_FF, D_MODEL)),
        ],
        out_specs=pl.BlockSpec((tile, D_MODEL), lambda i: (i, 0)),
        out_shape=jax.ShapeDtypeStruct((rows, D_MODEL), F32),
        compiler_params=pltpu.CompilerParams(
            dimension_semantics=("parallel",), vmem_limit_bytes=VMEM_LIMIT_BYTES),
        name="ffn",
    )(x2d, npre, npost, win, wout)


def _cumsum_rows(x):
    c, w = x.shape
    groups = c // SUBLANES
    y = x.reshape(groups, SUBLANES, w)
    sub = lax.broadcasted_iota(jnp.int32, y.shape, 1)
    shift = 1
    while shift < SUBLANES:
        y = y + jnp.where(sub >= shift, pltpu.roll(y, shift, axis=1), 0.0)
        shift *= 2
    tot = jnp.broadcast_to(y[:, SUBLANES - 1:SUBLANES, :], y.shape)
    inc = tot
    shift = 1
    while shift < groups:
        inc = inc + jnp.concatenate(
            [jnp.zeros((shift, SUBLANES, w), F32), inc[:groups - shift]], axis=0)
        shift *= 2
    return (y + (inc - tot)).reshape(c, w)


def _hgrn_kernel(x_ref, st_in_ref, npre_ref, win_ref, alb_ref, gnorm_ref, wout_ref, npost_ref,
                 y_ref, st_ref, qs_ref, lf_ref, inp_ref, v_ref, sg_ref, on_ref,
                 *, nb, tt, chunk):
    t = pl.program_id(1)
    rows = nb * tt

    @pl.when(t == 0)
    def _():
        for n in range(nb):
            for h in range(A_HEADS):
                st_ref[n, h] = st_in_ref[n, h].T

    x = x_ref[...].reshape(rows, D_MODEL)
    hn = (_rms_scale(x) * npre_ref[...]).astype(BF16)

    alb = alb_ref[...]
    e = jnp.exp(alb - jnp.max(alb, axis=0, keepdims=True))
    lb = e[0:1] / jnp.sum(e, axis=0, keepdims=True)

    f = _dot(hn, win_ref[:, D_MODEL:2 * D_MODEL])
    forget = lb + (1.0 - lb) * jax.nn.sigmoid(f)
    lf_ref[...] = jnp.log(forget)
    inp_ref[...] = 1.0 - forget

    def project_query(c0, width):
        q = _dot(hn, win_ref[:, c0:c0 + width])
        qs_ref[:, c0:c0 + width] = q * jax.nn.sigmoid(q)

    def project_value(c0, width):
        v_ref[:, c0:c0 + width] = _dot(
            hn, win_ref[:, 2 * D_MODEL + c0:2 * D_MODEL + c0 + width]).astype(BF16)

    def project_gate(c0, width):
        g = _dot(hn, win_ref[:, 3 * D_MODEL + c0:3 * D_MODEL + c0 + width])
        sg_ref[:, c0:c0 + width] = g * jax.nn.sigmoid(g)

    def write_rows(mix):
        y = x + _rms_scale(mix) * npost_ref[...]
        y_ref[...] = y.reshape(nb, tt, D_MODEL)

    half = chunk // 2
    ri = lax.broadcasted_iota(jnp.int32, (chunk, chunk), 0)
    ci = lax.broadcasted_iota(jnp.int32, (chunk, chunk), 1)
    causal = ri >= ci

    def prepare(blk, h):
        rs = slice(blk * chunk, (blk + 1) * chunk)
        sl = slice(h * A_DK, (h + 1) * A_DK)
        b = _cumsum_rows(lf_ref[rs, sl])
        bmid = b[half - 1:half]
        blast = b[chunk - 1:chunk]
        qsc = qs_ref[rs, sl] * jnp.exp(b - bmid)
        inp = inp_ref[rs, sl] * jnp.exp(bmid - b)
        return dict(
            rs=rs, sl=sl, qt=qsc.astype(BF16), kt=inp.astype(BF16),
            qi=(qsc * jnp.exp(bmid)).astype(BF16),
            kh=(inp * jnp.exp(blast - bmid)).astype(BF16),
            vh=v_ref[rs, sl], decay=jnp.exp(blast))

    def emit_output(u, sc, st):
        p = jnp.where(causal, sc, 0.0).astype(BF16)
        o = _dot(p, u["vh"]) + _dot_nt(u["qi"], st.astype(BF16))
        on = _rms_scale(o) * gnorm_ref[:, u["sl"]] * sg_ref[u["rs"], u["sl"]]
        on_ref[u["rs"], u["sl"]] = on.astype(BF16)

    def recurrence(heads, jobs):
        blocks_per_stage = STAGE_ROWS // chunk
        nstages = rows // STAGE_ROWS
        slots = 2 * nstages
        jobs = list(jobs)

        def run_jobs(slot):
            for _ in range(-(-len(jobs) // (slots - slot))):
                jobs.pop(0)()

        states = {}
        for stage in range(nstages):
            blocks = range(stage * blocks_per_stage, (stage + 1) * blocks_per_stage)
            units = [(blk, h, prepare(blk, h)) for blk in blocks for h in heads]
            scores = [_dot_nt(u["qt"], u["kt"]) for _, _, u in units]
            run_jobs(2 * stage)
            updated = {}
            for blk, h, u in units:
                key = ((blk * chunk) // tt, h)
                if key not in states:
                    states[key] = st_ref[key[0], h]
                updated[key] = states[key] * u["decay"] + _dot_tn(u["vh"], u["kh"])
            run_jobs(2 * stage + 1)
            for (blk, h, u), sc in zip(units, scores):
                emit_output(u, sc, states[((blk * chunk) // tt, h)])
            states.update(updated)
            if (blocks[-1] + 1) * chunk % tt == 0:
                for (n, h) in list(states):
                    st_ref[n, h] = states.pop((n, h))

    def factorised_path():
        heads_per_group = A_HEADS // HEAD_GROUPS
        piece = 2 * LANES
        partial_mix = [[] for _ in range(HEAD_GROUPS)]

        def project_out(g, c0, width):
            partial_mix[g].append(
                _dot(on_ref[:, g * gw:(g + 1) * gw], wout_ref[g * gw:(g + 1) * gw, c0:c0 + width]))

        for g in range(HEAD_GROUPS):
            jobs = []
            if g + 1 < HEAD_GROUPS:
                jobs += [functools.partial(proj, (g + 1) * gw + c0, piece)
                         for proj in (project_query, project_value, project_gate)
                         for c0 in range(0, gw, piece)]
            if g > 0:
                jobs += [functools.partial(project_out, g - 1, c0, piece)
                         for c0 in range(0, D_MODEL, piece)]
            recurrence(range(g * heads_per_group, (g + 1) * heads_per_group), jobs)
        project_out(HEAD_GROUPS - 1, 0, D_MODEL)
        write_rows(functools.reduce(
            jnp.add, [jnp.concatenate(parts, axis=1) for parts in partial_mix]))

    def exact_step(idx, carry):
        n = (idx * EXACT_BLOCK) // tt
        rs = pl.ds(pl.multiple_of(idx * EXACT_BLOCK, EXACT_BLOCK), EXACT_BLOCK)
        row = lax.broadcasted_iota(jnp.int32, (EXACT_BLOCK, A_DK), 0)
        for h in range(A_HEADS):
            sl = slice(h * A_DK, (h + 1) * A_DK)
            b = _cumsum_rows(lf_ref[rs, sl])
            blast = b[EXACT_BLOCK - 1:EXACT_BLOCK]
            qs = qs_ref[rs, sl]
            inp = inp_ref[rs, sl]
            vh = v_ref[rs, sl]
            vf = vh.astype(F32)
            st = st_ref[n, h]
            o = _dot_nt((qs * jnp.exp(b)).astype(BF16), st.astype(BF16))
            for s in range(EXACT_BLOCK):
                decay = jnp.exp(jnp.where(row >= s, b - b[s:s + 1], -jnp.inf))
                score = jnp.sum(qs * decay * inp[s:s + 1], axis=-1, keepdims=True)
                o = o + score * vf[s:s + 1]
            kh = (inp * jnp.exp(blast - b)).astype(BF16)
            st_ref[n, h] = st * jnp.exp(blast) + _dot_tn(vh, kh)
            on = _rms_scale(o) * gnorm_ref[:, sl] * sg_ref[rs, sl]
            on_ref[rs, sl] = on.astype(BF16)
        return carry

    gw = D_MODEL // HEAD_GROUPS
    project_query(0, gw)
    project_value(0, gw)
    project_gate(0, gw)

    max_step = -jnp.min(lf_ref[...])
    bounded = max_step * half <= MAX_FACTOR_EXPONENT

    pl.when(bounded)(factorised_path)

    @pl.when(jnp.logical_not(bounded))
    def _():
        project_query(gw, D_MODEL - gw)
        project_value(gw, D_MODEL - gw)
        project_gate(gw, D_MODEL - gw)
        lax.fori_loop(0, rows // EXACT_BLOCK, exact_step, 0)
        write_rows(_dot(on_ref[...], wout_ref[...]))

    @pl.when(t == pl.num_programs(1) - 1)
    def _():
        for n in range(nb):
            for h in range(A_HEADS):
                st_ref[n, h] = st_ref[n, h].T


def _hgrn_mixer(x, state, npre, win, alb, gnorm, wout, npost, *, nb, tt, chunk):
    batch, seq, _ = x.shape
    assert batch % nb == 0 and seq % tt == 0 and tt % chunk == 0 and chunk % (2 * SUBLANES) == 0
    rows = nb * tt
    assert rows % STAGE_ROWS == 0 and STAGE_ROWS % chunk == 0
    assert (nb == 1 and tt % STAGE_ROWS == 0) or STAGE_ROWS % tt == 0
    kern = functools.partial(_hgrn_kernel, nb=nb, tt=tt, chunk=chunk)
    st_spec = pl.BlockSpec((nb, A_HEADS, A_DK, A_DV), lambda b, t: (b, 0, 0, 0))
    return pl.pallas_call(
        kern,
        grid=(batch // nb, seq // tt),
        in_specs=[
            pl.BlockSpec((nb, tt, D_MODEL), lambda b, t: (b, t, 0)),
            st_spec,
            _const_spec((1, D_MODEL)),
            _const_spec((D_MODEL, 4 * D_MODEL)),
            _const_spec(alb.shape),
            _const_spec((1, D_MODEL)),
            _const_spec((D_MODEL, D_MODEL)),
            _const_spec((1, D_MODEL)),
        ],
        out_specs=[
            pl.BlockSpec((nb, tt, D_MODEL), lambda b, t: (b, t, 0)),
            st_spec,
        ],
        out_shape=[
            jax.ShapeDtypeStruct(x.shape, F32),
            jax.ShapeDtypeStruct(state.shape, F32),
        ],
        scratch_shapes=[
            pltpu.VMEM((rows, D_MODEL), F32),
            pltpu.VMEM((rows, D_MODEL), F32),
            pltpu.VMEM((rows, D_MODEL), F32),
            pltpu.VMEM((rows, D_MODEL), BF16),
            pltpu.VMEM((rows, D_MODEL), F32),
            pltpu.VMEM((rows, D_MODEL), BF16),
        ],
        compiler_params=pltpu.CompilerParams(
            dimension_semantics=("parallel", "arbitrary"), vmem_limit_bytes=VMEM_LIMIT_BYTES),
        name="hgrn_mixer",
    )(x, state, npre, win, alb, gnorm, wout, npost)


def _rope_cols(x, cos_t, sin_t):
    lane = lax.broadcasted_iota(jnp.int32, (x.shape[0], LANES), 1)
    first_half = (lane % B_HEAD_DIM) < (B_HEAD_DIM // 2)
    cols = []
    for c0 in range(0, x.shape[1], LANES):
        xc = x[:, c0:c0 + LANES]
        partner = jnp.where(first_half,
                            pltpu.roll(xc, LANES - B_HEAD_DIM // 2, axis=1),
                            pltpu.roll(xc, B_HEAD_DIM // 2, axis=1))
        cols.append(xc * cos_t + partner * sin_t)
    return jnp.concatenate(cols, axis=1)


def _attn_kernel(*refs, nb, tt, cq, has_cache):
    if has_cache:
        (x_ref, kc_in_ref, vc_in_ref, cosq_ref, sinq_ref, cosk_ref, sinkt_ref, sinks_ref,
         npre_ref, kvn_ref, wq_ref, wkv_ref, wo_ref, npost_ref,
         y_ref, kc_ref, vc_ref, q_ref, kd_ref, vl_ref, vh_ref, o_ref) = refs
    else:
        (x_ref, cosq_ref, sinq_ref, cosk_ref, sinkt_ref, sinks_ref,
         npre_ref, kvn_ref, wq_ref, wkv_ref, wo_ref, npost_ref,
         y_ref, kc_ref, vc_ref, q_ref, kd_ref, vl_ref, vh_ref, o_ref) = refs
    t = pl.program_id(1)
    rows = nb * tt
    ext = WINDOW + tt
    nkeys = WINDOW + cq
    nchunks = tt // cq

    @pl.when(t == 0)
    def _():
        if has_cache:
            kc_ref[...] = kc_in_ref[...]
            vc_ref[...] = vc_in_ref[...]
        else:
            kc_ref[...] = jnp.zeros(kc_ref.shape, F32)
            vc_ref[...] = jnp.zeros(vc_ref.shape, F32)

    x = x_ref[...].reshape(rows, D_MODEL)
    xs = _rms_scale(x)
    hq = (xs * npre_ref[...]).astype(BF16)
    hk = (xs * kvn_ref[...]).astype(BF16)

    def per_stream(tab_ref):
        tab = tab_ref[...]
        return tab if nb == 1 else jnp.concatenate([tab] * nb, axis=0)

    q = _dot(hq, wq_ref[...])
    q_ref[...] = _rope_cols(q, per_stream(cosq_ref), per_stream(sinq_ref)).astype(BF16)
    kv = _dot(hk, wkv_ref[...])
    k_new = _rope_cols(kv[:, :KV_WIDTH], per_stream(cosk_ref), per_stream(sinkt_ref))
    v_new = kv[:, KV_WIDTH:]

    lane = lax.broadcasted_iota(jnp.int32, (ext, LANES), 1)
    low = lane < B_HEAD_DIM
    for n in range(nb):
        k_ext = jnp.concatenate([kc_ref[n], k_new[n * tt:(n + 1) * tt]], axis=0)
        v_ext = jnp.concatenate([vc_ref[n], v_new[n * tt:(n + 1) * tt]], axis=0)
        kc_ref[n] = k_ext[ext - WINDOW:]
        vc_ref[n] = v_ext[ext - WINDOW:]
        for m in range(KV_WIDTH // LANES):
            ka = k_ext[:, m * LANES:(m + 1) * LANES]
            kr = pltpu.roll(ka, B_HEAD_DIM, axis=1)
            va = v_ext[:, m * LANES:(m + 1) * LANES]
            vr = pltpu.roll(va, B_HEAD_DIM, axis=1)
            es = pl.ds(n * ext, ext)
            kd_ref[2 * m, es, :] = jnp.where(low, ka, kr).astype(BF16)
            kd_ref[2 * m + 1, es, :] = jnp.where(low, kr, ka).astype(BF16)
            vl_ref[2 * m, es, :] = jnp.where(low, va, 0.0).astype(BF16)
            vh_ref[2 * m, es, :] = jnp.where(low, 0.0, vr).astype(BF16)
            vl_ref[2 * m + 1, es, :] = jnp.where(low, vr, 0.0).astype(BF16)
            vh_ref[2 * m + 1, es, :] = jnp.where(low, 0.0, va).astype(BF16)

    qlane_low = lax.broadcasted_iota(jnp.int32, (cq, LANES), 1) < B_HEAD_DIM
    key_idx = lax.broadcasted_iota(jnp.int32, (1, nkeys), 1)

    def chunk_step(idx, carry):
        n = idx // nchunks
        c = idx % nchunks
        r0 = pl.multiple_of(idx * cq, cq)
        k0 = pl.multiple_of(n * ext + c * cq, SUBLANES * 2)
        if has_cache:
            bias = None
        else:
            first_pos = t * tt + c * cq - WINDOW
            bias = jnp.where(key_idx + first_pos >= 0, 0.0, -jnp.inf)
        for j in range(B_KV_HEADS):
            qa = q_ref[pl.ds(r0, cq), 2 * j * LANES:(2 * j + 1) * LANES]
            qb = q_ref[pl.ds(r0, cq), (2 * j + 1) * LANES:(2 * j + 2) * LANES]
            zero = jnp.zeros_like(qa)
            qstack = jnp.concatenate([
                jnp.where(qlane_low, qa, zero), jnp.where(qlane_low, qb, zero),
                jnp.where(qlane_low, zero, qa), jnp.where(qlane_low, zero, qb)], axis=0)
            s = _dot_nt(qstack, kd_ref[j, pl.ds(k0, nkeys), :])
            if bias is not None:
                s = s + bias
            heads = (4 * j, 4 * j + 2, 4 * j + 1, 4 * j + 3)
            sink = jnp.concatenate(
                [jnp.full((cq, 1), sinks_ref[hd], F32) for hd in heads], axis=0)
            mx = jnp.maximum(jnp.max(s, axis=-1, keepdims=True), sink)
            p = jnp.exp(s - mx)
            denom = jnp.sum(p, axis=-1, keepdims=True) + jnp.exp(sink - mx)
            p = (p / denom).astype(BF16)
            out = (_dot(p[:2 * cq], vl_ref[j, pl.ds(k0, nkeys), :])
                   + _dot(p[2 * cq:], vh_ref[j, pl.ds(k0, nkeys), :]))
            o_ref[pl.ds(r0, cq), 2 * j * LANES:(2 * j + 1) * LANES] = out[:cq].astype(BF16)
            o_ref[pl.ds(r0, cq), (2 * j + 1) * LANES:(2 * j + 2) * LANES] = out[cq:].astype(BF16)
        return carry

    lax.fori_loop(0, nb * nchunks, chunk_step, 0)

    mix = _dot(o_ref[...], wo_ref[...])
    y = x + _rms_scale(mix) * npost_ref[...]
    y_ref[...] = y.reshape(nb, tt, D_MODEL)


def _attn_mixer(x, cache_k, cache_v, tables, sinks, npre, kvn, wq, wkv, wo, npost, *, nb, tt, cq):
    batch, seq, _ = x.shape
    has_cache = cache_k is not None
    assert batch % nb == 0 and seq % tt == 0 and tt % cq == 0 and cq % (2 * SUBLANES) == 0
    rows = nb * tt
    ext = WINDOW + tt
    kern = functools.partial(_attn_kernel, nb=nb, tt=tt, cq=cq, has_cache=has_cache)
    x_spec = pl.BlockSpec((nb, tt, D_MODEL), lambda b, t: (b, t, 0))
    c_spec = pl.BlockSpec((nb, WINDOW, KV_WIDTH), lambda b, t: (b, 0, 0))
    tab_spec = pl.BlockSpec((tt, LANES), lambda b, t: (t, 0))
    in_specs = [x_spec]
    args = [x]
    if has_cache:
        in_specs += [c_spec, c_spec]
        args += [cache_k, cache_v]
    in_specs += [tab_spec] * 4 + [
        pl.BlockSpec(memory_space=pltpu.SMEM),
        _const_spec((1, D_MODEL)),
        _const_spec((1, D_MODEL)),
        _const_spec((D_MODEL, D_MODEL)),
        _const_spec((D_MODEL, 2 * KV_WIDTH)),
        _const_spec((D_MODEL, D_MODEL)),
        _const_spec((1, D_MODEL)),
    ]
    args += list(tables) + [sinks, npre, kvn, wq, wkv, wo, npost]
    cache_shape = jax.ShapeDtypeStruct((batch, WINDOW, KV_WIDTH), F32)
    return pl.pallas_call(
        kern,
        grid=(batch // nb, seq // tt),
        in_specs=in_specs,
        out_specs=[x_spec, c_spec, c_spec],
        out_shape=[jax.ShapeDtypeStruct(x.shape, F32), cache_shape, cache_shape],
        scratch_shapes=[
            pltpu.VMEM((rows, D_MODEL), BF16),
            pltpu.VMEM((B_KV_HEADS, nb * ext, LANES), BF16),
            pltpu.VMEM((B_KV_HEADS, nb * ext, LANES), BF16),
            pltpu.VMEM((B_KV_HEADS, nb * ext, LANES), BF16),
            pltpu.VMEM((rows, D_MODEL), BF16),
        ],
        compiler_params=pltpu.CompilerParams(
            dimension_semantics=("parallel", "arbitrary"), vmem_limit_bytes=VMEM_LIMIT_BYTES),
        name="attn_mixer",
    )(*args)


def _attn_pair_kernel(x_ref, cosq_ref, sinq_ref, cosk_ref, sinkt_ref, sinks_ref,
                      npre_ref, kvn_ref, wq_ref, wkv_ref, wo_ref, npost_ref,
                      y_ref, kc_ref, vc_ref, q_ref, kl_ref, kh_ref, vt_ref, ot_ref, *, tt):
    t = pl.program_id(1)
    ext = WINDOW + tt
    pair = 2 * CHUNK
    nkeys = WINDOW + pair
    nvis = WINDOW + CHUNK

    @pl.when(t == 0)
    def _():
        kc_ref[...] = jnp.zeros(kc_ref.shape, F32)
        vc_ref[...] = jnp.zeros(vc_ref.shape, F32)

    x = x_ref[0]
    xs = _rms_scale(x)
    hq = (xs * npre_ref[...]).astype(BF16)
    hk = (xs * kvn_ref[...]).astype(BF16)

    q = _dot(hq, wq_ref[...])
    q_ref[...] = _rope_cols(q, cosq_ref[...], sinq_ref[...]).astype(BF16)
    kv = _dot(hk, wkv_ref[...])
    k_ext = jnp.concatenate(
        [kc_ref[0], _rope_cols(kv[:, :KV_WIDTH], cosk_ref[...], sinkt_ref[...])], axis=0)
    v_ext = jnp.concatenate([vc_ref[0], kv[:, KV_WIDTH:]], axis=0)
    kc_ref[0] = k_ext[ext - WINDOW:]
    vc_ref[0] = v_ext[ext - WINDOW:]
    vt_ref[...] = v_ext.T.astype(BF16)

    low = lax.broadcasted_iota(jnp.int32, (ext, LANES), 1) < B_HEAD_DIM
    for m in range(KV_WIDTH // LANES):
        ka = k_ext[:, m * LANES:(m + 1) * LANES]
        kr = pltpu.roll(ka, B_HEAD_DIM, axis=1)
        kl_ref[2 * m] = jnp.where(low, ka, 0.0).astype(BF16)
        kh_ref[2 * m] = jnp.where(low, 0.0, kr).astype(BF16)
        kl_ref[2 * m + 1] = jnp.where(low, kr, 0.0).astype(BF16)
        kh_ref[2 * m + 1] = jnp.where(low, 0.0, ka).astype(BF16)

    vis_row = lax.broadcasted_iota(jnp.int32, (nvis, LANES), 0)
    first_bias = [jnp.where(vis_row + (t * tt - WINDOW + qc * CHUNK) >= 0, 0.0, -jnp.inf)
                  for qc in range(2)]
    first_head = lax.broadcasted_iota(jnp.int32, (1, LANES), 1) < CHUNK
    no_keys = jnp.zeros((CHUNK, LANES), BF16)

    units = [(j, kx_ref, ha, hb) for j in range(B_KV_HEADS)
             for kx_ref, ha, hb in ((kl_ref, 4 * j, 4 * j + 2), (kh_ref, 4 * j + 1, 4 * j + 3))]

    def scores(e):
        r0 = e * pair
        out = []
        for j, kx_ref, _, _ in units:
            ca = slice(2 * j * LANES, (2 * j + 1) * LANES)
            cb = slice((2 * j + 1) * LANES, (2 * j + 2) * LANES)
            qq = jnp.concatenate([q_ref[r0:r0 + CHUNK, ca], q_ref[r0:r0 + CHUNK, cb],
                                  q_ref[r0 + CHUNK:r0 + pair, ca], q_ref[r0 + CHUNK:r0 + pair, cb]], axis=0)
            out.append(_dot_nt(kx_ref[j, r0:r0 + nkeys, :], qq))
        return out

    def softmax(e, s_list):
        out = []
        for (_, _, ha, hb), s in zip(units, s_list):
            sink = jnp.where(first_head, sinks_ref[ha], sinks_ref[hb]) * LOG2_E
            ps, dens = [], []
            for qc in range(2):
                sq = s[qc * CHUNK:qc * CHUNK + nvis, qc * LANES:(qc + 1) * LANES]
                if e == 0:
                    sq = sq + first_bias[qc]
                mx = jnp.maximum(jnp.max(sq, axis=0, keepdims=True), sink)
                p = jnp.exp2(sq - mx)
                dens.append(jnp.sum(p, axis=0, keepdims=True) + jnp.exp2(sink - mx))
                ps.append(p.astype(BF16))
            pmat = jnp.concatenate([jnp.concatenate([ps[0], no_keys], axis=0),
                                    jnp.concatenate([no_keys, ps[1]], axis=0)], axis=1)
            out.append((pmat, jnp.concatenate(dens, axis=1)))
        return out

    def weighted_values(e, pd_list):
        r0 = e * pair
        for (j, _, ha, hb), (pmat, den) in zip(units, pd_list):
            o = _dot(vt_ref[j * B_HEAD_DIM:(j + 1) * B_HEAD_DIM, r0:r0 + nkeys], pmat)
            o = (o * (1.0 / den)).astype(BF16)
            for qc in range(2):
                c0 = r0 + qc * CHUNK
                ot_ref[ha * B_HEAD_DIM:(ha + 1) * B_HEAD_DIM, c0:c0 + CHUNK] = (
                    o[:, qc * LANES:qc * LANES + CHUNK])
                ot_ref[hb * B_HEAD_DIM:(hb + 1) * B_HEAD_DIM, c0:c0 + CHUNK] = (
                    o[:, qc * LANES + CHUNK:(qc + 1) * LANES])

    npairs = tt // pair
    s_next = scores(0)
    for e in range(npairs):
        s_cur = s_next
        if e + 1 < npairs:
            s_next = scores(e + 1)
        weighted_values(e, softmax(e, s_cur))

    mix = _dot_tn(ot_ref[...], wo_ref[...])
    y_ref[0] = x + _rms_scale(mix) * npost_ref[...]


def _attn_pair_mixer(x, tables, sinks, npre, kvn, wq, wkv, wo, npost, *, tt):
    batch, seq, _ = x.shape
    assert seq % tt == 0 and tt % (2 * CHUNK) == 0 and WINDOW == 2 * CHUNK
    ext = WINDOW + tt
    kern = functools.partial(_attn_pair_kernel, tt=tt)
    x_spec = pl.BlockSpec((1, tt, D_MODEL), lambda b, t: (b, t, 0))
    c_spec = pl.BlockSpec((1, WINDOW, KV_WIDTH), lambda b, t: (b, 0, 0))
    tab_spec = pl.BlockSpec((tt, LANES), lambda b, t: (t, 0))
    cache_shape = jax.ShapeDtypeStruct((batch, WINDOW, KV_WIDTH), F32)
    return pl.pallas_call(
        kern,
        grid=(batch, seq // tt),
        in_specs=[x_spec] + [tab_spec] * 4 + [
            pl.BlockSpec(memory_space=pltpu.SMEM),
            _const_spec((1, D_MODEL)),
            _const_spec((1, D_MODEL)),
            _const_spec((D_MODEL, D_MODEL)),
            _const_spec((D_MODEL, 2 * KV_WIDTH)),
            _const_spec((D_MODEL, D_MODEL)),
            _const_spec((1, D_MODEL)),
        ],
        out_specs=[x_spec, c_spec, c_spec],
        out_shape=[jax.ShapeDtypeStruct(x.shape, F32), cache_shape, cache_shape],
        scratch_shapes=[
            pltpu.VMEM((tt, D_MODEL), BF16),
            pltpu.VMEM((B_KV_HEADS, ext, LANES), BF16),
            pltpu.VMEM((B_KV_HEADS, ext, LANES), BF16),
            pltpu.VMEM((KV_WIDTH, ext), BF16),
            pltpu.VMEM((D_MODEL, tt), BF16),
        ],
        compiler_params=pltpu.CompilerParams(
            dimension_semantics=("parallel", "arbitrary"), vmem_limit_bytes=VMEM_LIMIT_BYTES),
        name="attn_pair_mixer",
    )(x, *tables, sinks, npre, kvn, wq, wkv, wo, npost)


def _rope_tables(pos, q_scale):
    half = B_HEAD_DIM // 2
    inv = ROPE_THETA ** (-jnp.arange(half, dtype=F32) / half)
    ang = pos.astype(F32)[:, None] * inv[None, :]
    cos = jnp.cos(ang)
    sin = jnp.sin(ang)
    reps = LANES // B_HEAD_DIM
    cos_t = jnp.tile(jnp.concatenate([cos, cos], axis=1), (1, reps))
    sin_t = jnp.tile(jnp.concatenate([-sin, sin], axis=1), (1, reps))
    return cos_t * q_scale, sin_t * q_scale, cos_t, sin_t


def _trunk(x, pos, state, cache_k, cache_v, w, *, nb, tt, hgrn_chunk, cq):
    batch, seq, _ = x.shape
    row = lambda a: a.reshape(1, D_MODEL)
    x, st = _hgrn_mixer(x, state, row(w["norm_mix_pre"][0]), w["w_a_in"], w["a_lower_bound"],
                        row(w["a_out_norm"]), w["w_a_out"], row(w["norm_mix_post"][0]),
                        nb=nb, tt=tt, chunk=hgrn_chunk)
    x = _ffn(x.reshape(batch * seq, D_MODEL), row(w["norm_ffn_pre"][0]), row(w["norm_ffn_post"][0]),
             w["w_ffn_in"], w["w_ffn_out"], 0).reshape(batch, seq, D_MODEL)
    attn_w = (w["b_sinks"], row(w["norm_mix_pre"][1]), row(w["kv_norm"]), w["w_b_q"], w["w_kv"],
              w["w_b_out"], row(w["norm_mix_post"][1]))
    if cache_k is None and nb == 1 and cq == CHUNK:
        x, kc, vc = _attn_pair_mixer(x, _rope_tables(pos, SOFTMAX_SCALE * LOG2_E), *attn_w, tt=tt)
    else:
        x, kc, vc = _attn_mixer(x, cache_k, cache_v, _rope_tables(pos, SOFTMAX_SCALE), *attn_w,
                                nb=nb, tt=tt, cq=cq)
    x = _ffn(x.reshape(batch * seq, D_MODEL), row(w["norm_ffn_pre"][1]), row(w["norm_ffn_post"][1]),
             w["w_ffn_in"], w["w_ffn_out"], 1).reshape(batch, seq, D_MODEL)
    return x, st, kc, vc


def kernel(x_prompt, x_sample, state_hgrn, cache_k, cache_v, norm_mix_pre, norm_mix_post, norm_ffn_pre, norm_ffn_post, w_ffn_in, w_ffn_out, w_a_in, a_lower_bound, a_out_norm, w_a_out, kv_norm, w_kv, w_b_q, b_sinks, w_b_out):
    w = dict(
        norm_mix_pre=norm_mix_pre, norm_mix_post=norm_mix_post,
        norm_ffn_pre=norm_ffn_pre, norm_ffn_post=norm_ffn_post,
        w_ffn_in=w_ffn_in.astype(BF16), w_ffn_out=w_ffn_out.astype(BF16),
        w_a_in=w_a_in[0].astype(BF16), a_lower_bound=a_lower_bound,
        a_out_norm=a_out_norm[0], w_a_out=w_a_out[0].astype(BF16),
        kv_norm=kv_norm, w_kv=w_kv.astype(BF16), w_b_q=w_b_q[0].astype(BF16),
        b_sinks=b_sinks[0], w_b_out=w_b_out[0].astype(BF16),
    )
    bp, tp, _ = x_prompt.shape
    bs, ts, _ = x_sample.shape

    zero_state = jnp.zeros((bp, A_HEADS, A_DK, A_DV), F32)
    y_p, st_p, kc_p, vc_p = _trunk(
        x_prompt, jnp.arange(tp), zero_state, None, None, w,
        nb=1, tt=ROW_TILE, hgrn_chunk=HGRN_CHUNK, cq=CHUNK)

    nb_s = ROW_TILE // (2 * ts)
    y_s, st_s, kc_s, vc_s = _trunk(
        x_sample, PAST_LEN + jnp.arange(ts), state_hgrn[0],
        cache_k.reshape(bs, WINDOW, KV_WIDTH), cache_v.reshape(bs, WINDOW, KV_WIDTH), w,
        nb=nb_s, tt=ts, hgrn_chunk=ts, cq=ts)

    cache4 = lambda a: a.reshape(a.shape[0], WINDOW, B_KV_HEADS, B_HEAD_DIM)
    return (y_p, y_s, st_p[None], st_s[None],
            cache4(kc_p), cache4(vc_p), cache4(kc_s), cache4(vc_s))
```

```python
import functools

import jax
import jax.numpy as jnp
from jax import lax
from jax.experimental import pallas as pl
from jax.experimental.pallas import tpu as pltpu

F32 = jnp.float32
BF16 = jnp.bfloat16

D_MODEL = 1024
A_HEADS = 8
A_DK = 128
A_DV = 128
B_HEAD_DIM = 64
B_Q_HEADS = 16
B_KV_HEADS = 4
B_GROUP = B_Q_HEADS // B_KV_HEADS
KV_WIDTH = B_KV_HEADS * B_HEAD_DIM
WINDOW = 128
CHUNK = 64
PAST_LEN = 2048
D_FF = 2816
ROPE_THETA = 10000.0
NORM_EPS = 1e-6
SOFTMAX_SCALE = B_HEAD_DIM ** -0.5
LOG2_E = 1.4426950408889634

LANES = 128
SUBLANES = 8
VMEM_LIMIT_BYTES = 52 * 1024 * 1024

ROW_TILE = 512
FFN_ROW_TILE = 1024
HGRN_CHUNK = 128
FFN_CHUNK = 256
MAX_FACTOR_EXPONENT = 60.0
EXACT_BLOCK = 16
STAGE_ROWS = 128
HEAD_GROUPS = 2
JOB_SHARES = (0.18, 0.09, 0.09, 0.09, 0.09)


def _rms_scale(x):
    ms = jnp.mean(x * x, axis=-1, keepdims=True)
    return x * lax.rsqrt(ms + NORM_EPS)


def _dot(a, b):
    return jnp.dot(a, b, preferred_element_type=F32)


def _dot_nt(a, b):
    return lax.dot_general(a, b, (((1,), (1,)), ((), ())), preferred_element_type=F32)


def _dot_tn(a, b):
    return lax.dot_general(a, b, (((0,), (0,)), ((), ())), preferred_element_type=F32)


def _const_spec(shape):
    nd = len(shape)
    return pl.BlockSpec(shape, lambda *_: (0,) * nd, pipeline_mode=pl.Buffered(1))


def _ffn_kernel(x_ref, npre_ref, npost_ref, win_ref, wout_ref, o_ref, *, sub_rows):
    starts = range(0, x_ref.shape[0], sub_rows)
    normed = [(_rms_scale(x_ref[r0:r0 + sub_rows, :]) * npre_ref[...]).astype(BF16) for r0 in starts]
    for r0, h in zip(starts, normed):
        x = x_ref[r0:r0 + sub_rows, :]
        acc = jnp.zeros(x.shape, F32)
        for c0 in range(0, D_FF, FFN_CHUNK):
            a = _dot(h, win_ref[:, c0:c0 + FFN_CHUNK])
            b = _dot(h, win_ref[:, D_FF + c0:D_FF + c0 + FFN_CHUNK])
            g = (a * jax.nn.sigmoid(a) * b).astype(BF16)
            acc = acc + _dot(g, wout_ref[c0:c0 + FFN_CHUNK, :])
        o_ref[r0:r0 + sub_rows, :] = x + _rms_scale(acc) * npost_ref[...]


def _ffn(x2d, npre, npost, win, wout, layer):
    rows = x2d.shape[0]
    tile = min(FFN_ROW_TILE, rows)
    sub_rows = min(ROW_TILE, tile)
    assert rows % tile == 0 and tile % sub_rows == 0 and D_FF % FFN_CHUNK == 0

    def layer_spec(shape):
        return pl.BlockSpec((None,) + shape, lambda i: (layer, 0, 0), pipeline_mode=pl.Buffered(1))

    return pl.pallas_call(
        functools.partial(_ffn_kernel, sub_rows=sub_rows),
        grid=(rows // tile,),
        in_specs=[
            pl.BlockSpec((tile, D_MODEL), lambda i: (i, 0)),
            _const_spec((1, D_MODEL)),
            _const_spec((1, D_MODEL)),
            layer_spec((D_MODEL, 2 * D_FF)),
            layer_spec((D_FF, D_MODEL)),
        ],
        out_specs=pl.BlockSpec((tile, D_MODEL), lambda i: (i, 0)),
        out_shape=jax.ShapeDtypeStruct((rows, D_MODEL), F32),
        compiler_params=pltpu.CompilerParams(
            dimension_semantics=("parallel",), vmem_limit_bytes=VMEM_LIMIT_BYTES),
        name="ffn",
    )(x2d, npre, npost, win, wout)


def _cumsum_rows(x):
    c, w = x.shape
    groups = c // SUBLANES
    y = x.reshape(groups, SUBLANES, w)
    sub = lax.broadcasted_iota(jnp.int32, y.shape, 1)
    shift = 1
    while shift < SUBLANES:
        y = y + jnp.where(sub >= shift, pltpu.roll(y, shift, axis=1), 0.0)
        shift *= 2
    tot = jnp.broadcast_to(y[:, SUBLANES - 1:SUBLANES, :], y.shape)
    inc = tot
    shift = 1
    while shift < groups:
        inc = inc + jnp.concatenate(
            [jnp.zeros((shift, SUBLANES, w), F32), inc[:groups - shift]], axis=0)
        shift *= 2
    return (y + (inc - tot)).reshape(c, w)


def _hgrn_kernel(x_ref, st_in_ref, npre_ref, win_ref, alb_ref, gnorm_ref, wout_ref, npost_ref,
                 y_ref, st_ref, qs_ref, lf_ref, inp_ref, v_ref, sg_ref, on_ref,
                 *, nb, tt, chunk):
    t = pl.program_id(1)
    rows = nb * tt

    @pl.when(t == 0)
    def _():
        for n in range(nb):
            for h in range(A_HEADS):
                st_ref[n, h] = st_in_ref[n, h].T

    x = x_ref[...].reshape(rows, D_MODEL)
    hn = (_rms_scale(x) * npre_ref[...]).astype(BF16)

    alb = alb_ref[...]
    e = jnp.exp(alb - jnp.max(alb, axis=0, keepdims=True))
    lb = e[0:1] / jnp.sum(e, axis=0, keepdims=True)

    f = _dot(hn, win_ref[:, D_MODEL:2 * D_MODEL])
    forget = lb + (1.0 - lb) * jax.nn.sigmoid(f)
    lf_ref[...] = jnp.log(forget)
    inp_ref[...] = 1.0 - forget

    def project_query(c0, width):
        q = _dot(hn, win_ref[:, c0:c0 + width])
        qs_ref[:, c0:c0 + width] = q * jax.nn.sigmoid(q)

    def project_value(c0, width):
        v_ref[:, c0:c0 + width] = _dot(
            hn, win_ref[:, 2 * D_MODEL + c0:2 * D_MODEL + c0 + width]).astype(BF16)

    def project_gate(c0, width):
        g = _dot(hn, win_ref[:, 3 * D_MODEL + c0:3 * D_MODEL + c0 + width])
        sg_ref[:, c0:c0 + width] = g * jax.nn.sigmoid(g)

    def write_rows(mix):
        y = x + _rms_scale(mix) * npost_ref[...]
        y_ref[...] = y.reshape(nb, tt, D_MODEL)

    half = chunk // 2
    ri = lax.broadcasted_iota(jnp.int32, (chunk, chunk), 0)
    ci = lax.broadcasted_iota(jnp.int32, (chunk, chunk), 1)
    causal = ri >= ci

    def prepare(blk, h):
        rs = slice(blk * chunk, (blk + 1) * chunk)
        sl = slice(h * A_DK, (h + 1) * A_DK)
        b = _cumsum_rows(lf_ref[rs, sl])
        bmid = b[half - 1:half]
        blast = b[chunk - 1:chunk]
        qsc = qs_ref[rs, sl] * jnp.exp(b - bmid)
        inp = inp_ref[rs, sl] * jnp.exp(bmid - b)
        return dict(
            rs=rs, sl=sl, qt=qsc.astype(BF16), kt=inp.astype(BF16),
            qi=(qsc * jnp.exp(bmid)).astype(BF16),
            kh=(inp * jnp.exp(blast - bmid)).astype(BF16),
            vh=v_ref[rs, sl], decay=jnp.exp(blast))

    def emit_output(u, sc, st):
        p = jnp.where(causal, sc, 0.0).astype(BF16)
        o = _dot(p, u["vh"]) + _dot_nt(u["qi"], st.astype(BF16))
        on = _rms_scale(o) * gnorm_ref[:, u["sl"]] * sg_ref[u["rs"], u["sl"]]
        on_ref[u["rs"], u["sl"]] = on.astype(BF16)

    def recurrence(heads, jobs):
        blocks_per_stage = STAGE_ROWS // chunk
        nstages = rows // STAGE_ROWS
        slots = 2 * nstages
        jobs = list(jobs)

        def run_jobs(slot):
            for _ in range(-(-len(jobs) // (slots - slot))):
                jobs.pop(0)()

        states = {}
        for stage in range(nstages):
            blocks = range(stage * blocks_per_stage, (stage + 1) * blocks_per_stage)
            units = [(blk, h, prepare(blk, h)) for blk in blocks for h in heads]
            scores = [_dot_nt(u["qt"], u["kt"]) for _, _, u in units]
            run_jobs(2 * stage)
            updated = {}
            for blk, h, u in units:
                key = ((blk * chunk) // tt, h)
                if key not in states:
                    states[key] = st_ref[key[0], h]
                updated[key] = states[key] * u["decay"] + _dot_tn(u["vh"], u["kh"])
            run_jobs(2 * stage + 1)
            for (blk, h, u), sc in zip(units, scores):
                emit_output(u, sc, states[((blk * chunk) // tt, h)])
            states.update(updated)
            if (blocks[-1] + 1) * chunk % tt == 0:
                for (n, h) in list(states):
                    st_ref[n, h] = states.pop((n, h))

    def factorised_path():
        heads_per_group = A_HEADS // HEAD_GROUPS
        piece = 2 * LANES
        partial_mix = [[] for _ in range(HEAD_GROUPS)]

        def project_out(g, c0, width):
            partial_mix[g].append(
                _dot(on_ref[:, g * gw:(g + 1) * gw], wout_ref[g * gw:(g + 1) * gw, c0:c0 + width]))

        for g in range(HEAD_GROUPS):
            jobs = []
            if g + 1 < HEAD_GROUPS:
                jobs += [functools.partial(proj, (g + 1) * gw + c0, piece)
                         for proj in (project_query, project_value, project_gate)
                         for c0 in range(0, gw, piece)]
            if g > 0:
                jobs += [functools.partial(project_out, g - 1, c0, piece)
                         for c0 in range(0, D_MODEL, piece)]
            recurrence(range(g * heads_per_group, (g + 1) * heads_per_group), jobs)
        project_out(HEAD_GROUPS - 1, 0, D_MODEL)
        write_rows(functools.reduce(
            jnp.add, [jnp.concatenate(parts, axis=1) for parts in partial_mix]))

    def exact_step(idx, carry):
        n = (idx * EXACT_BLOCK) // tt
        rs = pl.ds(pl.multiple_of(idx * EXACT_BLOCK, EXACT_BLOCK), EXACT_BLOCK)
        row = lax.broadcasted_iota(jnp.int32, (EXACT_BLOCK, A_DK), 0)
        for h in range(A_HEADS):
            sl = slice(h * A_DK, (h + 1) * A_DK)
            b = _cumsum_rows(lf_ref[rs, sl])
            blast = b[EXACT_BLOCK - 1:EXACT_BLOCK]
            qs = qs_ref[rs, sl]
            inp = inp_ref[rs, sl]
            vh = v_ref[rs, sl]
            vf = vh.astype(F32)
            st = st_ref[n, h]
            o = _dot_nt((qs * jnp.exp(b)).astype(BF16), st.astype(BF16))
            for s in range(EXACT_BLOCK):
                decay = jnp.exp(jnp.where(row >= s, b - b[s:s + 1], -jnp.inf))
                score = jnp.sum(qs * decay * inp[s:s + 1], axis=-1, keepdims=True)
                o = o + score * vf[s:s + 1]
            kh = (inp * jnp.exp(blast - b)).astype(BF16)
            st_ref[n, h] = st * jnp.exp(blast) + _dot_tn(vh, kh)
            on = _rms_scale(o) * gnorm_ref[:, sl] * sg_ref[rs, sl]
            on_ref[rs, sl] = on.astype(BF16)
        return carry

    gw = D_MODEL // HEAD_GROUPS
    project_query(0, gw)
    project_value(0, gw)
    project_gate(0, gw)

    max_step = -jnp.min(lf_ref[...])
    bounded = max_step * half <= MAX_FACTOR_EXPONENT

    pl.when(bounded)(factorised_path)

    @pl.when(jnp.logical_not(bounded))
    def _():
        project_query(gw, D_MODEL - gw)
        project_value(gw, D_MODEL - gw)
        project_gate(gw, D_MODEL - gw)
        lax.fori_loop(0, rows // EXACT_BLOCK, exact_step, 0)
        write_rows(_dot(on_ref[...], wout_ref[...]))

    @pl.when(t == pl.num_programs(1) - 1)
    def _():
        for n in range(nb):
            for h in range(A_HEADS):
                st_ref[n, h] = st_ref[n, h].T


def _hgrn_mixer(x, state, npre, win, alb, gnorm, wout, npost, *, nb, tt, chunk):
    batch, seq, _ = x.shape
    assert batch % nb == 0 and seq % tt == 0 and tt % chunk == 0 and chunk % (2 * SUBLANES) == 0
    rows = nb * tt
    assert rows % STAGE_ROWS == 0 and STAGE_ROWS % chunk == 0
    assert (nb == 1 and tt % STAGE_ROWS == 0) or STAGE_ROWS % tt == 0
    kern = functools.partial(_hgrn_kernel, nb=nb, tt=tt, chunk=chunk)
    st_spec = pl.BlockSpec((nb, A_HEADS, A_DK, A_DV), lambda b, t: (b, 0, 0, 0))
    return pl.pallas_call(
        kern,
        grid=(batch // nb, seq // tt),
        in_specs=[
            pl.BlockSpec((nb, tt, D_MODEL), lambda b, t: (b, t, 0)),
            st_spec,
            _const_spec((1, D_MODEL)),
            _const_spec((D_MODEL, 4 * D_MODEL)),
            _const_spec(alb.shape),
            _const_spec((1, D_MODEL)),
            _const_spec((D_MODEL, D_MODEL)),
            _const_spec((1, D_MODEL)),
        ],
        out_specs=[
            pl.BlockSpec((nb, tt, D_MODEL), lambda b, t: (b, t, 0)),
            st_spec,
        ],
        out_shape=[
            jax.ShapeDtypeStruct(x.shape, F32),
            jax.ShapeDtypeStruct(state.shape, F32),
        ],
        scratch_shapes=[
            pltpu.VMEM((rows, D_MODEL), F32),
            pltpu.VMEM((rows, D_MODEL), F32),
            pltpu.VMEM((rows, D_MODEL), F32),
            pltpu.VMEM((rows, D_MODEL), BF16),
            pltpu.VMEM((rows, D_MODEL), F32),
            pltpu.VMEM((rows, D_MODEL), BF16),
        ],
        compiler_params=pltpu.CompilerParams(
            dimension_semantics=("parallel", "arbitrary"), vmem_limit_bytes=VMEM_LIMIT_BYTES),
        name="hgrn_mixer",
    )(x, state, npre, win, alb, gnorm, wout, npost)


def _rope_cols(x, cos_t, sin_t):
    lane = lax.broadcasted_iota(jnp.int32, (x.shape[0], LANES), 1)
    first_half = (lane % B_HEAD_DIM) < (B_HEAD_DIM // 2)
    cols = []
    for c0 in range(0, x.shape[1], LANES):
        xc = x[:, c0:c0 + LANES]
        partner = jnp.where(first_half,
                            pltpu.roll(xc, LANES - B_HEAD_DIM // 2, axis=1),
                            pltpu.roll(xc, B_HEAD_DIM // 2, axis=1))
        cols.append(xc * cos_t + partner * sin_t)
    return jnp.concatenate(cols, axis=1)


def _attn_kernel(*refs, nb, tt, cq, has_cache):
    if has_cache:
        (x_ref, kc_in_ref, vc_in_ref, cosq_ref, sinq_ref, cosk_ref, sinkt_ref, sinks_ref,
         npre_ref, kvn_ref, wq_ref, wkv_ref, wo_ref, npost_ref,
         y_ref, kc_ref, vc_ref, q_ref, kd_ref, vl_ref, vh_ref, o_ref) = refs
    else:
        (x_ref, cosq_ref, sinq_ref, cosk_ref, sinkt_ref, sinks_ref,
         npre_ref, kvn_ref, wq_ref, wkv_ref, wo_ref, npost_ref,
         y_ref, kc_ref, vc_ref, q_ref, kd_ref, vl_ref, vh_ref, o_ref) = refs
    t = pl.program_id(1)
    rows = nb * tt
    ext = WINDOW + tt
    nkeys = WINDOW + cq
    nchunks = tt // cq

    @pl.when(t == 0)
    def _():
        if has_cache:
            kc_ref[...] = kc_in_ref[...]
            vc_ref[...] = vc_in_ref[...]
        else:
            kc_ref[...] = jnp.zeros(kc_ref.shape, F32)
            vc_ref[...] = jnp.zeros(vc_ref.shape, F32)

    x = x_ref[...].reshape(rows, D_MODEL)
    xs = _rms_scale(x)
    hq = (xs * npre_ref[...]).astype(BF16)
    hk = (xs * kvn_ref[...]).astype(BF16)

    def per_stream(tab_ref):
        tab = tab_ref[...]
        return tab if nb == 1 else jnp.concatenate([tab] * nb, axis=0)

    q = _dot(hq, wq_ref[...])
    q_ref[...] = _rope_cols(q, per_stream(cosq_ref), per_stream(sinq_ref)).astype(BF16)
    kv = _dot(hk, wkv_ref[...])
    k_new = _rope_cols(kv[:, :KV_WIDTH], per_stream(cosk_ref), per_stream(sinkt_ref))
    v_new = kv[:, KV_WIDTH:]

    lane = lax.broadcasted_iota(jnp.int32, (ext, LANES), 1)
    low = lane < B_HEAD_DIM
    for n in range(nb):
        k_ext = jnp.concatenate([kc_ref[n], k_new[n * tt:(n + 1) * tt]], axis=0)
        v_ext = jnp.concatenate([vc_ref[n], v_new[n * tt:(n + 1) * tt]], axis=0)
        kc_ref[n] = k_ext[ext - WINDOW:]
        vc_ref[n] = v_ext[ext - WINDOW:]
        for m in range(KV_WIDTH // LANES):
            ka = k_ext[:, m * LANES:(m + 1) * LANES]
            kr = pltpu.roll(ka, B_HEAD_DIM, axis=1)
            va = v_ext[:, m * LANES:(m + 1) * LANES]
            vr = pltpu.roll(va, B_HEAD_DIM, axis=1)
            es = pl.ds(n * ext, ext)
            kd_ref[2 * m, es, :] = jnp.where(low, ka, kr).astype(BF16)
            kd_ref[2 * m + 1, es, :] = jnp.where(low, kr, ka).astype(BF16)
            vl_ref[2 * m, es, :] = jnp.where(low, va, 0.0).astype(BF16)
            vh_ref[2 * m, es, :] = jnp.where(low, 0.0, vr).astype(BF16)
            vl_ref[2 * m + 1, es, :] = jnp.where(low, vr, 0.0).astype(BF16)
            vh_ref[2 * m + 1, es, :] = jnp.where(low, 0.0, va).astype(BF16)

    qlane_low = lax.broadcasted_iota(jnp.int32, (cq, LANES), 1) < B_HEAD_DIM
    key_idx = lax.broadcasted_iota(jnp.int32, (1, nkeys), 1)

    def chunk_step(idx, carry):
        n = idx // nchunks
        c = idx % nchunks
        r0 = pl.multiple_of(idx * cq, cq)
        k0 = pl.multiple_of(n * ext + c * cq, SUBLANES * 2)
        if has_cache:
            bias = None
        else:
            first_pos = t * tt + c * cq - WINDOW
            bias = jnp.where(key_idx + first_pos >= 0, 0.0, -jnp.inf)
        for j in range(B_KV_HEADS):
            qa = q_ref[pl.ds(r0, cq), 2 * j * LANES:(2 * j + 1) * LANES]
            qb = q_ref[pl.ds(r0, cq), (2 * j + 1) * LANES:(2 * j + 2) * LANES]
            zero = jnp.zeros_like(qa)
            qstack = jnp.concatenate([
                jnp.where(qlane_low, qa, zero), jnp.where(qlane_low, qb, zero),
                jnp.where(qlane_low, zero, qa), jnp.where(qlane_low, zero, qb)], axis=0)
            s = _dot_nt(qstack, kd_ref[j, pl.ds(k0, nkeys), :])
            if bias is not None:
                s = s + bias
            heads = (4 * j, 4 * j + 2, 4 * j + 1, 4 * j + 3)
            sink = jnp.concatenate(
                [jnp.full((cq, 1), sinks_ref[hd], F32) for hd in heads], axis=0)
            mx = jnp.maximum(jnp.max(s, axis=-1, keepdims=True), sink)
            p = jnp.exp(s - mx)
            denom = jnp.sum(p, axis=-1, keepdims=True) + jnp.exp(sink - mx)
            p = (p / denom).astype(BF16)
            out = (_dot(p[:2 * cq], vl_ref[j, pl.ds(k0, nkeys), :])
                   + _dot(p[2 * cq:], vh_ref[j, pl.ds(k0, nkeys), :]))
            o_ref[pl.ds(r0, cq), 2 * j * LANES:(2 * j + 1) * LANES] = out[:cq].astype(BF16)
            o_ref[pl.ds(r0, cq), (2 * j + 1) * LANES:(2 * j + 2) * LANES] = out[cq:].astype(BF16)
        return carry

    lax.fori_loop(0, nb * nchunks, chunk_step, 0)

    mix = _dot(o_ref[...], wo_ref[...])
    y = x + _rms_scale(mix) * npost_ref[...]
    y_ref[...] = y.reshape(nb, tt, D_MODEL)


def _attn_mixer(x, cache_k, cache_v, tables, sinks, npre, kvn, wq, wkv, wo, npost, *, nb, tt, cq):
    batch, seq, _ = x.shape
    has_cache = cache_k is not None
    assert batch % nb == 0 and seq % tt == 0 and tt % cq == 0 and cq % (2 * SUBLANES) == 0
    rows = nb * tt
    ext = WINDOW + tt
    kern = functools.partial(_attn_kernel, nb=nb, tt=tt, cq=cq, has_cache=has_cache)
    x_spec = pl.BlockSpec((nb, tt, D_MODEL), lambda b, t: (b, t, 0))
    c_spec = pl.BlockSpec((nb, WINDOW, KV_WIDTH), lambda b, t: (b, 0, 0))
    tab_spec = pl.BlockSpec((tt, LANES), lambda b, t: (t, 0))
    in_specs = [x_spec]
    args = [x]
    if has_cache:
        in_specs += [c_spec, c_spec]
        args += [cache_k, cache_v]
    in_specs += [tab_spec] * 4 + [
        pl.BlockSpec(memory_space=pltpu.SMEM),
        _const_spec((1, D_MODEL)),
        _const_spec((1, D_MODEL)),
        _const_spec((D_MODEL, D_MODEL)),
        _const_spec((D_MODEL, 2 * KV_WIDTH)),
        _const_spec((D_MODEL, D_MODEL)),
        _const_spec((1, D_MODEL)),
    ]
    args += list(tables) + [sinks, npre, kvn, wq, wkv, wo, npost]
    cache_shape = jax.ShapeDtypeStruct((batch, WINDOW, KV_WIDTH), F32)
    return pl.pallas_call(
        kern,
        grid=(batch // nb, seq // tt),
        in_specs=in_specs,
        out_specs=[x_spec, c_spec, c_spec],
        out_shape=[jax.ShapeDtypeStruct(x.shape, F32), cache_shape, cache_shape],
        scratch_shapes=[
            pltpu.VMEM((rows, D_MODEL), BF16),
            pltpu.VMEM((B_KV_HEADS, nb * ext, LANES), BF16),
            pltpu.VMEM((B_KV_HEADS, nb * ext, LANES), BF16),
            pltpu.VMEM((B_KV_HEADS, nb * ext, LANES), BF16),
            pltpu.VMEM((rows, D_MODEL), BF16),
        ],
        compiler_params=pltpu.CompilerParams(
            dimension_semantics=("parallel", "arbitrary"), vmem_limit_bytes=VMEM_LIMIT_BYTES),
        name="attn_mixer",
    )(*args)


def _ffn_steps(src_ref, dst_ref, npre_ref, npost_ref, win_ref, wout_ref):
    state = {}

    def chunk(c0):
        if not state:
            state["h"] = (_rms_scale(src_ref[...]) * npre_ref[...]).astype(BF16)
            state["acc"] = jnp.zeros(src_ref.shape, F32)
        h = state["h"]
        a = _dot(h, win_ref[:, c0:c0 + FFN_CHUNK])
        b = _dot(h, win_ref[:, D_FF + c0:D_FF + c0 + FFN_CHUNK])
        g = (a * jax.nn.sigmoid(a) * b).astype(BF16)
        state["acc"] = state["acc"] + _dot(g, wout_ref[c0:c0 + FFN_CHUNK, :])
        if c0 + FFN_CHUNK >= D_FF:
            dst_ref[...] = (src_ref[...] + _rms_scale(state["acc"]) * npost_ref[...]).reshape(dst_ref.shape)

    return [functools.partial(chunk, c0) for c0 in range(0, D_FF, FFN_CHUNK)]


def _attn_pair_kernel(*refs, tt, fused):
    if fused:
        (x_ref, cosq_ref, sinq_ref, cosk_ref, sinkt_ref, sinks_ref,
         npre_ref, kvn_ref, wq_ref, wkv_ref, wo_ref, npost_ref,
         fpre_ref, fpost_ref, win_ref, wout_ref,
         y_ref, kc_ref, vc_ref, q_ref, kl_ref, kh_ref, vt_ref, ot_ref, mid_ref) = refs
    else:
        (x_ref, cosq_ref, sinq_ref, cosk_ref, sinkt_ref, sinks_ref,
         npre_ref, kvn_ref, wq_ref, wkv_ref, wo_ref, npost_ref,
         y_ref, kc_ref, vc_ref, q_ref, kl_ref, kh_ref, vt_ref, ot_ref) = refs
    t = pl.program_id(1)
    if not fused:
        _attention_tile(t, tt, x_ref, cosq_ref, sinq_ref, cosk_ref, sinkt_ref, sinks_ref,
                        npre_ref, kvn_ref, wq_ref, wkv_ref, wo_ref, npost_ref, kc_ref, vc_ref,
                        q_ref, kl_ref, kh_ref, vt_ref, ot_ref, [], y_ref.at[0])
        return

    last = pl.num_programs(1) - 1

    @pl.when((pl.program_id(0) == 0) & (t == 0))
    def _():
        mid_ref[...] = jnp.zeros(mid_ref.shape, F32)

    @pl.when(t < last)
    def _():
        jobs = _ffn_steps(mid_ref, y_ref, fpre_ref, fpost_ref, win_ref, wout_ref)
        _attention_tile(t, tt, x_ref, cosq_ref, sinq_ref, cosk_ref, sinkt_ref, sinks_ref,
                        npre_ref, kvn_ref, wq_ref, wkv_ref, wo_ref, npost_ref, kc_ref, vc_ref,
                        q_ref, kl_ref, kh_ref, vt_ref, ot_ref, jobs, mid_ref)

    @pl.when(t == last)
    def _():
        for job in _ffn_steps(mid_ref, y_ref, fpre_ref, fpost_ref, win_ref, wout_ref):
            job()


def _attention_tile(t, tt, x_ref, cosq_ref, sinq_ref, cosk_ref, sinkt_ref, sinks_ref,
                    npre_ref, kvn_ref, wq_ref, wkv_ref, wo_ref, npost_ref, kc_ref, vc_ref,
                    q_ref, kl_ref, kh_ref, vt_ref, ot_ref, jobs, out_ref):
    ext = WINDOW + tt
    pair = 2 * CHUNK
    nkeys = WINDOW + pair
    nvis = WINDOW + CHUNK
    jobs = list(jobs)
    njobs = len(jobs)

    def run_jobs(share):
        for _ in range(min(round(share * njobs), len(jobs))):
            jobs.pop(0)()

    @pl.when(t == 0)
    def _():
        kc_ref[...] = jnp.zeros(kc_ref.shape, F32)
        vc_ref[...] = jnp.zeros(vc_ref.shape, F32)

    x = x_ref[0]
    xs = _rms_scale(x)
    hq = (xs * npre_ref[...]).astype(BF16)
    hk = (xs * kvn_ref[...]).astype(BF16)

    run_jobs(JOB_SHARES[0])
    q = _dot(hq, wq_ref[...])
    q_ref[...] = _rope_cols(q, cosq_ref[...], sinq_ref[...]).astype(BF16)
    run_jobs(JOB_SHARES[1])
    kv = _dot(hk, wkv_ref[...])
    run_jobs(JOB_SHARES[2])
    k_ext = jnp.concatenate(
        [kc_ref[0], _rope_cols(kv[:, :KV_WIDTH], cosk_ref[...], sinkt_ref[...])], axis=0)
    v_ext = jnp.concatenate([vc_ref[0], kv[:, KV_WIDTH:]], axis=0)
    kc_ref[0] = k_ext[ext - WINDOW:]
    vc_ref[0] = v_ext[ext - WINDOW:]
    vt_ref[...] = v_ext.T.astype(BF16)

    low = lax.broadcasted_iota(jnp.int32, (ext, LANES), 1) < B_HEAD_DIM
    for m in range(KV_WIDTH // LANES):
        ka = k_ext[:, m * LANES:(m + 1) * LANES]
        kr = pltpu.roll(ka, B_HEAD_DIM, axis=1)
        kl_ref[2 * m] = jnp.where(low, ka, 0.0).astype(BF16)
        kh_ref[2 * m] = jnp.where(low, 0.0, kr).astype(BF16)
        kl_ref[2 * m + 1] = jnp.where(low, kr, 0.0).astype(BF16)
        kh_ref[2 * m + 1] = jnp.where(low, 0.0, ka).astype(BF16)

    vis_row = lax.broadcasted_iota(jnp.int32, (nvis, LANES), 0)
    first_bias = [jnp.where(vis_row + (t * tt - WINDOW + qc * CHUNK) >= 0, 0.0, -jnp.inf)
                  for qc in range(2)]
    first_head = lax.broadcasted_iota(jnp.int32, (1, LANES), 1) < CHUNK
    no_keys = jnp.zeros((CHUNK, LANES), BF16)

    units = [(j, kx_ref, ha, hb) for j in range(B_KV_HEADS)
             for kx_ref, ha, hb in ((kl_ref, 4 * j, 4 * j + 2), (kh_ref, 4 * j + 1, 4 * j + 3))]

    def scores(e):
        r0 = e * pair
        out = []
        for j, kx_ref, _, _ in units:
            ca = slice(2 * j * LANES, (2 * j + 1) * LANES)
            cb = slice((2 * j + 1) * LANES, (2 * j + 2) * LANES)
            qq = jnp.concatenate([q_ref[r0:r0 + CHUNK, ca], q_ref[r0:r0 + CHUNK, cb],
                                  q_ref[r0 + CHUNK:r0 + pair, ca], q_ref[r0 + CHUNK:r0 + pair, cb]], axis=0)
            out.append(_dot_nt(kx_ref[j, r0:r0 + nkeys, :], qq))
        return out

    def softmax(e, s_list):
        out = []
        for (_, _, ha, hb), s in zip(units, s_list):
            sink = jnp.where(first_head, sinks_ref[ha], sinks_ref[hb]) * LOG2_E
            ps, dens = [], []
            for qc in range(2):
                sq = s[qc * CHUNK:qc * CHUNK + nvis, qc * LANES:(qc + 1) * LANES]
                if e == 0:
                    sq = sq + first_bias[qc]
                mx = jnp.maximum(jnp.max(sq, axis=0, keepdims=True), sink)
                p = jnp.exp2(sq - mx)
                dens.append(jnp.sum(p, axis=0, keepdims=True) + jnp.exp2(sink - mx))
                ps.append(p.astype(BF16))
            pmat = jnp.concatenate([jnp.concatenate([ps[0], no_keys], axis=0),
                                    jnp.concatenate([no_keys, ps[1]], axis=0)], axis=1)
            out.append((pmat, jnp.concatenate(dens, axis=1)))
        return out

    def weighted_values(e, pd_list):
        r0 = e * pair
        for (j, _, ha, hb), (pmat, den) in zip(units, pd_list):
            o = _dot(vt_ref[j * B_HEAD_DIM:(j + 1) * B_HEAD_DIM, r0:r0 + nkeys], pmat)
            o = (o * (1.0 / den)).astype(BF16)
            for qc in range(2):
                c0 = r0 + qc * CHUNK
                ot_ref[ha * B_HEAD_DIM:(ha + 1) * B_HEAD_DIM, c0:c0 + CHUNK] = (
                    o[:, qc * LANES:qc * LANES + CHUNK])
                ot_ref[hb * B_HEAD_DIM:(hb + 1) * B_HEAD_DIM, c0:c0 + CHUNK] = (
                    o[:, qc * LANES + CHUNK:(qc + 1) * LANES])

    npairs = tt // pair
    s_next = scores(0)
    for e in range(npairs):
        s_cur = s_next
        if e + 1 < npairs:
            s_next = scores(e + 1)
        run_jobs(JOB_SHARES[3])
        weighted_values(e, softmax(e, s_cur))

    run_jobs(JOB_SHARES[4])
    mix = _dot_tn(ot_ref[...], wo_ref[...])
    run_jobs(1.0)
    out_ref[...] = x + _rms_scale(mix) * npost_ref[...]


def _attn_pair_mixer(x, tables, sinks, npre, kvn, wq, wkv, wo, npost, *, tt, ffn=None):
    batch, seq, _ = x.shape
    assert seq % tt == 0 and tt % (2 * CHUNK) == 0 and WINDOW == 2 * CHUNK
    ext = WINDOW + tt
    fused = ffn is not None
    nt = seq // tt
    kern = functools.partial(_attn_pair_kernel, tt=tt, fused=fused)
    if fused:
        in_spec = pl.BlockSpec((1, tt, D_MODEL), lambda b, t: (b, jnp.minimum(t, nt - 1), 0))
        out_spec = pl.BlockSpec((1, tt, D_MODEL), lambda b, t: (b, jnp.maximum(t - 1, 0), 0))
        tab_spec = pl.BlockSpec((tt, LANES), lambda b, t: (jnp.minimum(t, nt - 1), 0))
    else:
        in_spec = out_spec = pl.BlockSpec((1, tt, D_MODEL), lambda b, t: (b, t, 0))
        tab_spec = pl.BlockSpec((tt, LANES), lambda b, t: (t, 0))
    c_spec = pl.BlockSpec((1, WINDOW, KV_WIDTH), lambda b, t: (b, 0, 0))
    cache_shape = jax.ShapeDtypeStruct((batch, WINDOW, KV_WIDTH), F32)
    in_specs = [in_spec] + [tab_spec] * 4 + [
        pl.BlockSpec(memory_space=pltpu.SMEM),
        _const_spec((1, D_MODEL)),
        _const_spec((1, D_MODEL)),
        _const_spec((D_MODEL, D_MODEL)),
        _const_spec((D_MODEL, 2 * KV_WIDTH)),
        _const_spec((D_MODEL, D_MODEL)),
        _const_spec((1, D_MODEL)),
    ]
    args = [x, *tables, sinks, npre, kvn, wq, wkv, wo, npost]
    scratch = [
        pltpu.VMEM((tt, D_MODEL), BF16),
        pltpu.VMEM((B_KV_HEADS, ext, LANES), BF16),
        pltpu.VMEM((B_KV_HEADS, ext, LANES), BF16),
        pltpu.VMEM((KV_WIDTH, ext), BF16),
        pltpu.VMEM((D_MODEL, tt), BF16),
    ]
    if fused:
        fpre, fpost, win, wout, layer = ffn

        def layer_spec(shape):
            return pl.BlockSpec((None,) + shape, lambda b, t: (layer, 0, 0),
                                pipeline_mode=pl.Buffered(1))

        in_specs += [_const_spec((1, D_MODEL)), _const_spec((1, D_MODEL)),
                     layer_spec((D_MODEL, 2 * D_FF)), layer_spec((D_FF, D_MODEL))]
        args += [fpre, fpost, win, wout]
        scratch.append(pltpu.VMEM((tt, D_MODEL), F32))
    return pl.pallas_call(
        kern,
        grid=(batch, nt + 1 if fused else nt),
        in_specs=in_specs,
        out_specs=[out_spec, c_spec, c_spec],
        out_shape=[jax.ShapeDtypeStruct(x.shape, F32), cache_shape, cache_shape],
        scratch_shapes=scratch,
        compiler_params=pltpu.CompilerParams(
            dimension_semantics=("arbitrary", "arbitrary"), vmem_limit_bytes=VMEM_LIMIT_BYTES),
        name="attn_pair_mixer",
    )(*args)


def _rope_tables(pos, q_scale):
    half = B_HEAD_DIM // 2
    inv = ROPE_THETA ** (-jnp.arange(half, dtype=F32) / half)
    ang = pos.astype(F32)[:, None] * inv[None, :]
    cos = jnp.cos(ang)
    sin = jnp.sin(ang)
    reps = LANES // B_HEAD_DIM
    cos_t = jnp.tile(jnp.concatenate([cos, cos], axis=1), (1, reps))
    sin_t = jnp.tile(jnp.concatenate([-sin, sin], axis=1), (1, reps))
    return cos_t * q_scale, sin_t * q_scale, cos_t, sin_t


def _trunk(x, pos, state, cache_k, cache_v, w, *, nb, tt, hgrn_chunk, cq):
    batch, seq, _ = x.shape
    row = lambda a: a.reshape(1, D_MODEL)
    x, st = _hgrn_mixer(x, state, row(w["norm_mix_pre"][0]), w["w_a_in"], w["a_lower_bound"],
                        row(w["a_out_norm"]), w["w_a_out"], row(w["norm_mix_post"][0]),
                        nb=nb, tt=tt, chunk=hgrn_chunk)
    x = _ffn(x.reshape(batch * seq, D_MODEL), row(w["norm_ffn_pre"][0]), row(w["norm_ffn_post"][0]),
             w["w_ffn_in"], w["w_ffn_out"], 0).reshape(batch, seq, D_MODEL)
    attn_w = (w["b_sinks"], row(w["norm_mix_pre"][1]), row(w["kv_norm"]), w["w_b_q"], w["w_kv"],
              w["w_b_out"], row(w["norm_mix_post"][1]))
    ffn1 = (row(w["norm_ffn_pre"][1]), row(w["norm_ffn_post"][1]), w["w_ffn_in"], w["w_ffn_out"], 1)
    if cache_k is None and nb == 1 and cq == CHUNK:
        x, kc, vc = _attn_pair_mixer(x, _rope_tables(pos, SOFTMAX_SCALE * LOG2_E), *attn_w, tt=tt,
                                     ffn=ffn1)
    else:
        x, kc, vc = _attn_mixer(x, cache_k, cache_v, _rope_tables(pos, SOFTMAX_SCALE), *attn_w,
                                nb=nb, tt=tt, cq=cq)
        x = _ffn(x.reshape(batch * seq, D_MODEL), *ffn1).reshape(batch, seq, D_MODEL)
    return x, st, kc, vc


def kernel(x_prompt, x_sample, state_hgrn, cache_k, cache_v, norm_mix_pre, norm_mix_post, norm_ffn_pre, norm_ffn_post, w_ffn_in, w_ffn_out, w_a_in, a_lower_bound, a_out_norm, w_a_out, kv_norm, w_kv, w_b_q, b_sinks, w_b_out):
    w = dict(
        norm_mix_pre=norm_mix_pre, norm_mix_post=norm_mix_post,
        norm_ffn_pre=norm_ffn_pre, norm_ffn_post=norm_ffn_post,
        w_ffn_in=w_ffn_in.astype(BF16), w_ffn_out=w_ffn_out.astype(BF16),
        w_a_in=w_a_in[0].astype(BF16), a_lower_bound=a_lower_bound,
        a_out_norm=a_out_norm[0], w_a_out=w_a_out[0].astype(BF16),
        kv_norm=kv_norm, w_kv=w_kv.astype(BF16), w_b_q=w_b_q[0].astype(BF16),
        b_sinks=b_sinks[0], w_b_out=w_b_out[0].astype(BF16),
    )
    bp, tp, _ = x_prompt.shape
    bs, ts, _ = x_sample.shape

    zero_state = jnp.zeros((bp, A_HEADS, A_DK, A_DV), F32)
    y_p, st_p, kc_p, vc_p = _trunk(
        x_prompt, jnp.arange(tp), zero_state, None, None, w,
        nb=1, tt=ROW_TILE, hgrn_chunk=HGRN_CHUNK, cq=CHUNK)

    nb_s = ROW_TILE // (2 * ts)
    y_s, st_s, kc_s, vc_s = _trunk(
        x_sample, PAST_LEN + jnp.arange(ts), state_hgrn[0],
        cache_k.reshape(bs, WINDOW, KV_WIDTH), cache_v.reshape(bs, WINDOW, KV_WIDTH), w,
        nb=nb_s, tt=ts, hgrn_chunk=ts, cq=ts)

    cache4 = lambda a: a.reshape(a.shape[0], WINDOW, B_KV_HEADS, B_HEAD_DIM)
    return (y_p, y_s, st_p[None], st_s[None],
            cache4(kc_p), cache4(vc_p), cache4(kc_s), cache4(vc_s))
```

```python
import functools

import jax
import jax.numpy as jnp
from jax import lax
from jax.experimental import pallas as pl
from jax.experimental.pallas import tpu as pltpu

F32 = jnp.float32
BF16 = jnp.bfloat16

D_MODEL = 1024
A_HEADS = 8
A_DK = 128
A_DV = 128
B_HEAD_DIM = 64
B_Q_HEADS = 16
B_KV_HEADS = 4
B_GROUP = B_Q_HEADS // B_KV_HEADS
KV_WIDTH = B_KV_HEADS * B_HEAD_DIM
WINDOW = 128
CHUNK = 64
PAST_LEN = 2048
D_FF = 2816
ROPE_THETA = 10000.0
NORM_EPS = 1e-6
SOFTMAX_SCALE = B_HEAD_DIM ** -0.5
LOG2_E = 1.4426950408889634

LANES = 128
SUBLANES = 8
VMEM_LIMIT_BYTES = 52 * 1024 * 1024

ROW_TILE = 512
FFN_ROW_TILE = 1024
HGRN_CHUNK = 128
FFN_CHUNK = 256
MAX_FACTOR_EXPONENT = 60.0
EXACT_BLOCK = 16
HGRN_STREAMS = 1
HEAD_GROUPS = 2
JOB_SHARES = (0.18, 0.09, 0.09, 0.09, 0.09)


def _rms_scale(x):
    ms = jnp.mean(x * x, axis=-1, keepdims=True)
    return x * lax.rsqrt(ms + NORM_EPS)


def _dot(a, b):
    return jnp.dot(a, b, preferred_element_type=F32)


def _dot_nt(a, b):
    return lax.dot_general(a, b, (((1,), (1,)), ((), ())), preferred_element_type=F32)


def _dot_tn(a, b):
    return lax.dot_general(a, b, (((0,), (0,)), ((), ())), preferred_element_type=F32)


def _const_spec(shape):
    nd = len(shape)
    return pl.BlockSpec(shape, lambda *_: (0,) * nd, pipeline_mode=pl.Buffered(1))


def _ffn_kernel(x_ref, npre_ref, npost_ref, win_ref, wout_ref, o_ref, *, sub_rows):
    starts = range(0, x_ref.shape[0], sub_rows)
    normed = [(_rms_scale(x_ref[r0:r0 + sub_rows, :]) * npre_ref[...]).astype(BF16) for r0 in starts]
    for r0, h in zip(starts, normed):
        x = x_ref[r0:r0 + sub_rows, :]
        acc = jnp.zeros(x.shape, F32)
        for c0 in range(0, D_FF, FFN_CHUNK):
            c1 = min(c0 + FFN_CHUNK, D_FF)
            a = _dot(h, win_ref[:, c0:c1])
            b = _dot(h, win_ref[:, D_FF + c0:D_FF + c1])
            g = (a * jax.nn.sigmoid(a) * b).astype(BF16)
            acc = acc + _dot(g, wout_ref[c0:c1, :])
        o_ref[r0:r0 + sub_rows, :] = x + _rms_scale(acc) * npost_ref[...]


def _ffn(x2d, npre, npost, win, wout, layer):
    rows = x2d.shape[0]
    tile = min(FFN_ROW_TILE, rows)
    sub_rows = min(ROW_TILE, tile)
    assert rows % tile == 0 and tile % sub_rows == 0

    def layer_spec(shape):
        return pl.BlockSpec((None,) + shape, lambda i: (layer, 0, 0), pipeline_mode=pl.Buffered(1))

    return pl.pallas_call(
        functools.partial(_ffn_kernel, sub_rows=sub_rows),
        grid=(rows // tile,),
        in_specs=[
            pl.BlockSpec((tile, D_MODEL), lambda i: (i, 0)),
            _const_spec((1, D_MODEL)),
            _const_spec((1, D_MODEL)),
            layer_spec((D_MODEL, 2 * D_FF)),
            layer_spec((D_FF, D_MODEL)),
        ],
        out_specs=pl.BlockSpec((tile, D_MODEL), lambda i: (i, 0)),
        out_shape=jax.ShapeDtypeStruct((rows, D_MODEL), F32),
        compiler_params=pltpu.CompilerParams(
            dimension_semantics=("parallel",), vmem_limit_bytes=VMEM_LIMIT_BYTES),
        name="ffn",
    )(x2d, npre, npost, win, wout)


def _cumsum_rows(x):
    c, w = x.shape
    groups = c // SUBLANES
    y = x.reshape(groups, SUBLANES, w)
    sub = lax.broadcasted_iota(jnp.int32, y.shape, 1)
    shift = 1
    while shift < SUBLANES:
        y = y + jnp.where(sub >= shift, pltpu.roll(y, shift, axis=1), 0.0)
        shift *= 2
    tot = jnp.broadcast_to(y[:, SUBLANES - 1:SUBLANES, :], y.shape)
    inc = tot
    shift = 1
    while shift < groups:
        inc = inc + jnp.concatenate(
            [jnp.zeros((shift, SUBLANES, w), F32), inc[:groups - shift]], axis=0)
        shift *= 2
    return (y + (inc - tot)).reshape(c, w)


def _hgrn_kernel(x_ref, st_in_ref, npre_ref, win_ref, alb_ref, gnorm_ref, wout_ref, npost_ref,
                 y_ref, st_ref, qs_ref, lf_ref, inp_ref, v_ref, sg_ref, on_ref,
                 *, nb, tt, chunk):
    t = pl.program_id(1)
    rows = nb * tt

    @pl.when(t == 0)
    def _():
        for n in range(nb):
            for h in range(A_HEADS):
                st_ref[n, h] = st_in_ref[n, h].T

    x = x_ref[...].reshape(rows, D_MODEL)
    hn = (_rms_scale(x) * npre_ref[...]).astype(BF16)

    alb = alb_ref[...]
    e = jnp.exp(alb - jnp.max(alb, axis=0, keepdims=True))
    lb = e[0:1] / jnp.sum(e, axis=0, keepdims=True)

    f = _dot(hn, win_ref[:, D_MODEL:2 * D_MODEL])
    forget = lb + (1.0 - lb) * jax.nn.sigmoid(f)
    lf_ref[...] = jnp.log(forget)
    inp_ref[...] = 1.0 - forget

    def project_query(c0, width):
        q = _dot(hn, win_ref[:, c0:c0 + width])
        qs_ref[:, c0:c0 + width] = q * jax.nn.sigmoid(q)

    def project_value(c0, width):
        v_ref[:, c0:c0 + width] = _dot(
            hn, win_ref[:, 2 * D_MODEL + c0:2 * D_MODEL + c0 + width]).astype(BF16)

    def project_gate(c0, width):
        g = _dot(hn, win_ref[:, 3 * D_MODEL + c0:3 * D_MODEL + c0 + width])
        sg_ref[:, c0:c0 + width] = g * jax.nn.sigmoid(g)

    def write_rows(mix):
        y = x + _rms_scale(mix) * npost_ref[...]
        y_ref[...] = y.reshape(nb, tt, D_MODEL)

    half = chunk // 2
    ri = lax.broadcasted_iota(jnp.int32, (chunk, chunk), 0)
    ci = lax.broadcasted_iota(jnp.int32, (chunk, chunk), 1)
    causal = ri >= ci

    def prepare(blk, h):
        rs = slice(blk * chunk, (blk + 1) * chunk)
        sl = slice(h * A_DK, (h + 1) * A_DK)
        b = _cumsum_rows(lf_ref[rs, sl])
        bmid = b[half - 1:half]
        blast = b[chunk - 1:chunk]
        qsc = qs_ref[rs, sl] * jnp.exp(b - bmid)
        inp = inp_ref[rs, sl] * jnp.exp(bmid - b)
        return dict(
            rs=rs, sl=sl, qt=qsc.astype(BF16), kt=inp.astype(BF16),
            qi=(qsc * jnp.exp(bmid)).astype(BF16),
            kh=(inp * jnp.exp(blast - bmid)).astype(BF16),
            vh=v_ref[rs, sl], decay=jnp.exp(blast))

    def emit_output(u, sc, st):
        p = jnp.where(causal, sc, 0.0).astype(BF16)
        o = _dot(p, u["vh"]) + _dot_nt(u["qi"], st.astype(BF16))
        on = _rms_scale(o) * gnorm_ref[:, u["sl"]] * sg_ref[u["rs"], u["sl"]]
        on_ref[u["rs"], u["sl"]] = on.astype(BF16)

    def recurrence(heads, jobs):
        nstages = tt // chunk
        slots = 2 * nstages
        jobs = list(jobs)

        def run_jobs(slot):
            for _ in range(-(-len(jobs) // (slots - slot))):
                jobs.pop(0)()

        states = {}
        for stage in range(nstages):
            blocks = [n * nstages + stage for n in range(nb)]
            units = [(blk, h, prepare(blk, h)) for blk in blocks for h in heads]
            scores = [_dot_nt(u["qt"], u["kt"]) for _, _, u in units]
            run_jobs(2 * stage)
            updated = {}
            for blk, h, u in units:
                key = ((blk * chunk) // tt, h)
                if key not in states:
                    states[key] = st_ref[key[0], h]
                updated[key] = states[key] * u["decay"] + _dot_tn(u["vh"], u["kh"])
            run_jobs(2 * stage + 1)
            for (blk, h, u), sc in zip(units, scores):
                emit_output(u, sc, states[((blk * chunk) // tt, h)])
            states.update(updated)
        for (n, h), st in states.items():
            st_ref[n, h] = st

    def factorised_path():
        heads_per_group = A_HEADS // HEAD_GROUPS
        piece = 2 * LANES
        partial_mix = [[] for _ in range(HEAD_GROUPS)]

        def project_out(g, c0, width):
            partial_mix[g].append(
                _dot(on_ref[:, g * gw:(g + 1) * gw], wout_ref[g * gw:(g + 1) * gw, c0:c0 + width]))

        for g in range(HEAD_GROUPS):
            jobs = []
            if g + 1 < HEAD_GROUPS:
                jobs += [functools.partial(proj, (g + 1) * gw + c0, piece)
                         for proj in (project_query, project_value, project_gate)
                         for c0 in range(0, gw, piece)]
            if g > 0:
                jobs += [functools.partial(project_out, g - 1, c0, piece)
                         for c0 in range(0, D_MODEL, piece)]
            recurrence(range(g * heads_per_group, (g + 1) * heads_per_group), jobs)
        project_out(HEAD_GROUPS - 1, 0, D_MODEL)
        write_rows(functools.reduce(
            jnp.add, [jnp.concatenate(parts, axis=1) for parts in partial_mix]))

    def exact_step(idx, carry):
        n = (idx * EXACT_BLOCK) // tt
        rs = pl.ds(pl.multiple_of(idx * EXACT_BLOCK, EXACT_BLOCK), EXACT_BLOCK)
        row = lax.broadcasted_iota(jnp.int32, (EXACT_BLOCK, A_DK), 0)
        for h in range(A_HEADS):
            sl = slice(h * A_DK, (h + 1) * A_DK)
            b = _cumsum_rows(lf_ref[rs, sl])
            blast = b[EXACT_BLOCK - 1:EXACT_BLOCK]
            qs = qs_ref[rs, sl]
            inp = inp_ref[rs, sl]
            vh = v_ref[rs, sl]
            vf = vh.astype(F32)
            st = st_ref[n, h]
            o = _dot_nt((qs * jnp.exp(b)).astype(BF16), st.astype(BF16))
            for s in range(EXACT_BLOCK):
                decay = jnp.exp(jnp.where(row >= s, b - b[s:s + 1], -jnp.inf))
                score = jnp.sum(qs * decay * inp[s:s + 1], axis=-1, keepdims=True)
                o = o + score * vf[s:s + 1]
            kh = (inp * jnp.exp(blast - b)).astype(BF16)
            st_ref[n, h] = st * jnp.exp(blast) + _dot_tn(vh, kh)
            on = _rms_scale(o) * gnorm_ref[:, sl] * sg_ref[rs, sl]
            on_ref[rs, sl] = on.astype(BF16)
        return carry

    gw = D_MODEL // HEAD_GROUPS
    project_query(0, gw)
    project_value(0, gw)
    project_gate(0, gw)

    max_step = -jnp.min(lf_ref[...])
    bounded = max_step * half <= MAX_FACTOR_EXPONENT

    pl.when(bounded)(factorised_path)

    @pl.when(jnp.logical_not(bounded))
    def _():
        project_query(gw, D_MODEL - gw)
        project_value(gw, D_MODEL - gw)
        project_gate(gw, D_MODEL - gw)
        lax.fori_loop(0, rows // EXACT_BLOCK, exact_step, 0)
        write_rows(_dot(on_ref[...], wout_ref[...]))

    @pl.when(t == pl.num_programs(1) - 1)
    def _():
        for n in range(nb):
            for h in range(A_HEADS):
                st_ref[n, h] = st_ref[n, h].T


def _hgrn_mixer(x, state, npre, win, alb, gnorm, wout, npost, *, nb, tt, chunk):
    batch, seq, _ = x.shape
    assert batch % nb == 0 and seq % tt == 0 and tt % chunk == 0 and chunk % (2 * SUBLANES) == 0
    rows = nb * tt
    kern = functools.partial(_hgrn_kernel, nb=nb, tt=tt, chunk=chunk)
    st_spec = pl.BlockSpec((nb, A_HEADS, A_DK, A_DV), lambda b, t: (b, 0, 0, 0))
    return pl.pallas_call(
        kern,
        grid=(batch // nb, seq // tt),
        in_specs=[
            pl.BlockSpec((nb, tt, D_MODEL), lambda b, t: (b, t, 0)),
            st_spec,
            _const_spec((1, D_MODEL)),
            _const_spec((D_MODEL, 4 * D_MODEL)),
            _const_spec(alb.shape),
            _const_spec((1, D_MODEL)),
            _const_spec((D_MODEL, D_MODEL)),
            _const_spec((1, D_MODEL)),
        ],
        out_specs=[
            pl.BlockSpec((nb, tt, D_MODEL), lambda b, t: (b, t, 0)),
            st_spec,
        ],
        out_shape=[
            jax.ShapeDtypeStruct(x.shape, F32),
            jax.ShapeDtypeStruct(state.shape, F32),
        ],
        scratch_shapes=[
            pltpu.VMEM((rows, D_MODEL), F32),
            pltpu.VMEM((rows, D_MODEL), F32),
            pltpu.VMEM((rows, D_MODEL), F32),
            pltpu.VMEM((rows, D_MODEL), BF16),
            pltpu.VMEM((rows, D_MODEL), F32),
            pltpu.VMEM((rows, D_MODEL), BF16),
        ],
        compiler_params=pltpu.CompilerParams(
            dimension_semantics=("parallel", "arbitrary"), vmem_limit_bytes=VMEM_LIMIT_BYTES),
        name="hgrn_mixer",
    )(x, state, npre, win, alb, gnorm, wout, npost)


def _rope_cols(x, cos_t, sin_t):
    lane = lax.broadcasted_iota(jnp.int32, (x.shape[0], LANES), 1)
    first_half = (lane % B_HEAD_DIM) < (B_HEAD_DIM // 2)
    cols = []
    for c0 in range(0, x.shape[1], LANES):
        xc = x[:, c0:c0 + LANES]
        partner = jnp.where(first_half,
                            pltpu.roll(xc, LANES - B_HEAD_DIM // 2, axis=1),
                            pltpu.roll(xc, B_HEAD_DIM // 2, axis=1))
        cols.append(xc * cos_t + partner * sin_t)
    return jnp.concatenate(cols, axis=1)


def _attn_kernel(*refs, nb, tt, cq, has_cache):
    if has_cache:
        (x_ref, kc_in_ref, vc_in_ref, cosq_ref, sinq_ref, cosk_ref, sinkt_ref, sinks_ref,
         npre_ref, kvn_ref, wq_ref, wkv_ref, wo_ref, npost_ref,
         y_ref, kc_ref, vc_ref, q_ref, kd_ref, vl_ref, vh_ref, o_ref) = refs
    else:
        (x_ref, cosq_ref, sinq_ref, cosk_ref, sinkt_ref, sinks_ref,
         npre_ref, kvn_ref, wq_ref, wkv_ref, wo_ref, npost_ref,
         y_ref, kc_ref, vc_ref, q_ref, kd_ref, vl_ref, vh_ref, o_ref) = refs
    t = pl.program_id(1)
    rows = nb * tt
    ext = WINDOW + tt
    nkeys = WINDOW + cq
    nchunks = tt // cq

    @pl.when(t == 0)
    def _():
        if has_cache:
            kc_ref[...] = kc_in_ref[...]
            vc_ref[...] = vc_in_ref[...]
        else:
            kc_ref[...] = jnp.zeros(kc_ref.shape, F32)
            vc_ref[...] = jnp.zeros(vc_ref.shape, F32)

    x = x_ref[...].reshape(rows, D_MODEL)
    xs = _rms_scale(x)
    hq = (xs * npre_ref[...]).astype(BF16)
    hk = (xs * kvn_ref[...]).astype(BF16)

    def per_stream(tab_ref):
        tab = tab_ref[...]
        return tab if nb == 1 else jnp.concatenate([tab] * nb, axis=0)

    q = _dot(hq, wq_ref[...])
    q_ref[...] = _rope_cols(q, per_stream(cosq_ref), per_stream(sinq_ref)).astype(BF16)
    kv = _dot(hk, wkv_ref[...])
    k_new = _rope_cols(kv[:, :KV_WIDTH], per_stream(cosk_ref), per_stream(sinkt_ref))
    v_new = kv[:, KV_WIDTH:]

    lane = lax.broadcasted_iota(jnp.int32, (ext, LANES), 1)
    low = lane < B_HEAD_DIM
    for n in range(nb):
        k_ext = jnp.concatenate([kc_ref[n], k_new[n * tt:(n + 1) * tt]], axis=0)
        v_ext = jnp.concatenate([vc_ref[n], v_new[n * tt:(n + 1) * tt]], axis=0)
        kc_ref[n] = k_ext[ext - WINDOW:]
        vc_ref[n] = v_ext[ext - WINDOW:]
        for m in range(KV_WIDTH // LANES):
            ka = k_ext[:, m * LANES:(m + 1) * LANES]
            kr = pltpu.roll(ka, B_HEAD_DIM, axis=1)
            va = v_ext[:, m * LANES:(m + 1) * LANES]
            vr = pltpu.roll(va, B_HEAD_DIM, axis=1)
            es = pl.ds(n * ext, ext)
            kd_ref[2 * m, es, :] = jnp.where(low, ka, kr).astype(BF16)
            kd_ref[2 * m + 1, es, :] = jnp.where(low, kr, ka).astype(BF16)
            vl_ref[2 * m, es, :] = jnp.where(low, va, 0.0).astype(BF16)
            vh_ref[2 * m, es, :] = jnp.where(low, 0.0, vr).astype(BF16)
            vl_ref[2 * m + 1, es, :] = jnp.where(low, vr, 0.0).astype(BF16)
            vh_ref[2 * m + 1, es, :] = jnp.where(low, 0.0, va).astype(BF16)

    qlane_low = lax.broadcasted_iota(jnp.int32, (cq, LANES), 1) < B_HEAD_DIM
    key_idx = lax.broadcasted_iota(jnp.int32, (1, nkeys), 1)

    def chunk_step(idx, carry):
        n = idx // nchunks
        c = idx % nchunks
        r0 = pl.multiple_of(idx * cq, cq)
        k0 = pl.multiple_of(n * ext + c * cq, SUBLANES * 2)
        if has_cache:
            bias = None
        else:
            first_pos = t * tt + c * cq - WINDOW
            bias = jnp.where(key_idx + first_pos >= 0, 0.0, -jnp.inf)
        for j in range(B_KV_HEADS):
            qa = q_ref[pl.ds(r0, cq), 2 * j * LANES:(2 * j + 1) * LANES]
            qb = q_ref[pl.ds(r0, cq), (2 * j + 1) * LANES:(2 * j + 2) * LANES]
            zero = jnp.zeros_like(qa)
            qstack = jnp.concatenate([
                jnp.where(qlane_low, qa, zero), jnp.where(qlane_low, qb, zero),
                jnp.where(qlane_low, zero, qa), jnp.where(qlane_low, zero, qb)], axis=0)
            s = _dot_nt(qstack, kd_ref[j, pl.ds(k0, nkeys), :])
            if bias is not None:
                s = s + bias
            heads = (4 * j, 4 * j + 2, 4 * j + 1, 4 * j + 3)
            sink = jnp.concatenate(
                [jnp.full((cq, 1), sinks_ref[hd], F32) for hd in heads], axis=0)
            mx = jnp.maximum(jnp.max(s, axis=-1, keepdims=True), sink)
            p = jnp.exp(s - mx)
            denom = jnp.sum(p, axis=-1, keepdims=True) + jnp.exp(sink - mx)
            p = (p / denom).astype(BF16)
            out = (_dot(p[:2 * cq], vl_ref[j, pl.ds(k0, nkeys), :])
                   + _dot(p[2 * cq:], vh_ref[j, pl.ds(k0, nkeys), :]))
            o_ref[pl.ds(r0, cq), 2 * j * LANES:(2 * j + 1) * LANES] = out[:cq].astype(BF16)
            o_ref[pl.ds(r0, cq), (2 * j + 1) * LANES:(2 * j + 2) * LANES] = out[cq:].astype(BF16)
        return carry

    lax.fori_loop(0, nb * nchunks, chunk_step, 0)

    mix = _dot(o_ref[...], wo_ref[...])
    y = x + _rms_scale(mix) * npost_ref[...]
    y_ref[...] = y.reshape(nb, tt, D_MODEL)


def _attn_mixer(x, cache_k, cache_v, tables, sinks, npre, kvn, wq, wkv, wo, npost, *, nb, tt, cq):
    batch, seq, _ = x.shape
    has_cache = cache_k is not None
    assert batch % nb == 0 and seq % tt == 0 and tt % cq == 0 and cq % (2 * SUBLANES) == 0
    rows = nb * tt
    ext = WINDOW + tt
    kern = functools.partial(_attn_kernel, nb=nb, tt=tt, cq=cq, has_cache=has_cache)
    x_spec = pl.BlockSpec((nb, tt, D_MODEL), lambda b, t: (b, t, 0))
    c_spec = pl.BlockSpec((nb, WINDOW, KV_WIDTH), lambda b, t: (b, 0, 0))
    tab_spec = pl.BlockSpec((tt, LANES), lambda b, t: (t, 0))
    in_specs = [x_spec]
    args = [x]
    if has_cache:
        in_specs += [c_spec, c_spec]
        args += [cache_k, cache_v]
    in_specs += [tab_spec] * 4 + [
        pl.BlockSpec(memory_space=pltpu.SMEM),
        _const_spec((1, D_MODEL)),
        _const_spec((1, D_MODEL)),
        _const_spec((D_MODEL, D_MODEL)),
        _const_spec((D_MODEL, 2 * KV_WIDTH)),
        _const_spec((D_MODEL, D_MODEL)),
        _const_spec((1, D_MODEL)),
    ]
    args += list(tables) + [sinks, npre, kvn, wq, wkv, wo, npost]
    cache_shape = jax.ShapeDtypeStruct((batch, WINDOW, KV_WIDTH), F32)
    return pl.pallas_call(
        kern,
        grid=(batch // nb, seq // tt),
        in_specs=in_specs,
        out_specs=[x_spec, c_spec, c_spec],
        out_shape=[jax.ShapeDtypeStruct(x.shape, F32), cache_shape, cache_shape],
        scratch_shapes=[
            pltpu.VMEM((rows, D_MODEL), BF16),
            pltpu.VMEM((B_KV_HEADS, nb * ext, LANES), BF16),
            pltpu.VMEM((B_KV_HEADS, nb * ext, LANES), BF16),
            pltpu.VMEM((B_KV_HEADS, nb * ext, LANES), BF16),
            pltpu.VMEM((rows, D_MODEL), BF16),
        ],
        compiler_params=pltpu.CompilerParams(
            dimension_semantics=("parallel", "arbitrary"), vmem_limit_bytes=VMEM_LIMIT_BYTES),
        name="attn_mixer",
    )(*args)


def _ffn_steps(src_ref, dst_ref, npre_ref, npost_ref, win_ref, wout_ref):
    state = {}

    def chunk(c0):
        if not state:
            state["h"] = (_rms_scale(src_ref[...]) * npre_ref[...]).astype(BF16)
            state["acc"] = jnp.zeros(src_ref.shape, F32)
        h = state["h"]
        c1 = min(c0 + FFN_CHUNK, D_FF)
        a = _dot(h, win_ref[:, c0:c1])
        b = _dot(h, win_ref[:, D_FF + c0:D_FF + c1])
        g = (a * jax.nn.sigmoid(a) * b).astype(BF16)
        state["acc"] = state["acc"] + _dot(g, wout_ref[c0:c1, :])
        if c1 == D_FF:
            dst_ref[...] = (src_ref[...] + _rms_scale(state["acc"]) * npost_ref[...]).reshape(dst_ref.shape)

    return [functools.partial(chunk, c0) for c0 in range(0, D_FF, FFN_CHUNK)]


def _attn_pair_kernel(*refs, tt, fused):
    if fused:
        (x_ref, cosq_ref, sinq_ref, cosk_ref, sinkt_ref, sinks_ref,
         npre_ref, kvn_ref, wq_ref, wkv_ref, wo_ref, npost_ref,
         fpre_ref, fpost_ref, win_ref, wout_ref,
         y_ref, kc_ref, vc_ref, q_ref, kl_ref, kh_ref, vt_ref, ot_ref, mid_ref) = refs
    else:
        (x_ref, cosq_ref, sinq_ref, cosk_ref, sinkt_ref, sinks_ref,
         npre_ref, kvn_ref, wq_ref, wkv_ref, wo_ref, npost_ref,
         y_ref, kc_ref, vc_ref, q_ref, kl_ref, kh_ref, vt_ref, ot_ref) = refs
    t = pl.program_id(1)
    if not fused:
        _attention_tile(t, tt, x_ref, cosq_ref, sinq_ref, cosk_ref, sinkt_ref, sinks_ref,
                        npre_ref, kvn_ref, wq_ref, wkv_ref, wo_ref, npost_ref, kc_ref, vc_ref,
                        q_ref, kl_ref, kh_ref, vt_ref, ot_ref, [], y_ref.at[0])
        return

    last = pl.num_programs(1) - 1

    @pl.when((pl.program_id(0) == 0) & (t == 0))
    def _():
        mid_ref[...] = jnp.zeros(mid_ref.shape, F32)

    @pl.when(t < last)
    def _():
        jobs = _ffn_steps(mid_ref, y_ref, fpre_ref, fpost_ref, win_ref, wout_ref)
        _attention_tile(t, tt, x_ref, cosq_ref, sinq_ref, cosk_ref, sinkt_ref, sinks_ref,
                        npre_ref, kvn_ref, wq_ref, wkv_ref, wo_ref, npost_ref, kc_ref, vc_ref,
                        q_ref, kl_ref, kh_ref, vt_ref, ot_ref, jobs, mid_ref)

    @pl.when(t == last)
    def _():
        for job in _ffn_steps(mid_ref, y_ref, fpre_ref, fpost_ref, win_ref, wout_ref):
            job()


def _attention_tile(t, tt, x_ref, cosq_ref, sinq_ref, cosk_ref, sinkt_ref, sinks_ref,
                    npre_ref, kvn_ref, wq_ref, wkv_ref, wo_ref, npost_ref, kc_ref, vc_ref,
                    q_ref, kl_ref, kh_ref, vt_ref, ot_ref, jobs, out_ref):
    ext = WINDOW + tt
    pair = 2 * CHUNK
    nkeys = WINDOW + pair
    nvis = WINDOW + CHUNK
    jobs = list(jobs)
    njobs = len(jobs)

    def run_jobs(share):
        for _ in range(min(round(share * njobs), len(jobs))):
            jobs.pop(0)()

    @pl.when(t == 0)
    def _():
        kc_ref[...] = jnp.zeros(kc_ref.shape, F32)
        vc_ref[...] = jnp.zeros(vc_ref.shape, F32)

    x = x_ref[0]
    xs = _rms_scale(x)
    hq = (xs * npre_ref[...]).astype(BF16)
    hk = (xs * kvn_ref[...]).astype(BF16)

    run_jobs(JOB_SHARES[0])
    q = _dot(hq, wq_ref[...])
    q_ref[...] = _rope_cols(q, cosq_ref[...], sinq_ref[...]).astype(BF16)
    run_jobs(JOB_SHARES[1])
    kv = _dot(hk, wkv_ref[...])
    run_jobs(JOB_SHARES[2])
    k_ext = jnp.concatenate(
        [kc_ref[0], _rope_cols(kv[:, :KV_WIDTH], cosk_ref[...], sinkt_ref[...])], axis=0)
    v_ext = jnp.concatenate([vc_ref[0], kv[:, KV_WIDTH:]], axis=0)
    kc_ref[0] = k_ext[ext - WINDOW:]
    vc_ref[0] = v_ext[ext - WINDOW:]
    vt_ref[...] = v_ext.T.astype(BF16)

    low = lax.broadcasted_iota(jnp.int32, (ext, LANES), 1) < B_HEAD_DIM
    for m in range(KV_WIDTH // LANES):
        ka = k_ext[:, m * LANES:(m + 1) * LANES]
        kr = pltpu.roll(ka, B_HEAD_DIM, axis=1)
        kl_ref[2 * m] = jnp.where(low, ka, 0.0).astype(BF16)
        kh_ref[2 * m] = jnp.where(low, 0.0, kr).astype(BF16)
        kl_ref[2 * m + 1] = jnp.where(low, kr, 0.0).astype(BF16)
        kh_ref[2 * m + 1] = jnp.where(low, 0.0, ka).astype(BF16)

    vis_row = lax.broadcasted_iota(jnp.int32, (nvis, LANES), 0)
    first_bias = [jnp.where(vis_row + (t * tt - WINDOW + qc * CHUNK) >= 0, 0.0, -jnp.inf)
                  for qc in range(2)]
    first_head = lax.broadcasted_iota(jnp.int32, (1, LANES), 1) < CHUNK
    no_keys = jnp.zeros((CHUNK, LANES), BF16)

    units = [(j, kx_ref, ha, hb) for j in range(B_KV_HEADS)
             for kx_ref, ha, hb in ((kl_ref, 4 * j, 4 * j + 2), (kh_ref, 4 * j + 1, 4 * j + 3))]

    def scores(e):
        r0 = e * pair
        out = []
        for j, kx_ref, _, _ in units:
            ca = slice(2 * j * LANES, (2 * j + 1) * LANES)
            cb = slice((2 * j + 1) * LANES, (2 * j + 2) * LANES)
            qq = jnp.concatenate([q_ref[r0:r0 + CHUNK, ca], q_ref[r0:r0 + CHUNK, cb],
                                  q_ref[r0 + CHUNK:r0 + pair, ca], q_ref[r0 + CHUNK:r0 + pair, cb]], axis=0)
            out.append(_dot_nt(kx_ref[j, r0:r0 + nkeys, :], qq))
        return out

    def softmax(e, s_list):
        out = []
        for (_, _, ha, hb), s in zip(units, s_list):
            sink = jnp.where(first_head, sinks_ref[ha], sinks_ref[hb]) * LOG2_E
            ps, dens = [], []
            for qc in range(2):
                sq = s[qc * CHUNK:qc * CHUNK + nvis, qc * LANES:(qc + 1) * LANES]
                if e == 0:
                    sq = sq + first_bias[qc]
                mx = jnp.maximum(jnp.max(sq, axis=0, keepdims=True), sink)
                p = jnp.exp2(sq - mx)
                dens.append(jnp.sum(p, axis=0, keepdims=True) + jnp.exp2(sink - mx))
                ps.append(p.astype(BF16))
            pmat = jnp.concatenate([jnp.concatenate([ps[0], no_keys], axis=0),
                                    jnp.concatenate([no_keys, ps[1]], axis=0)], axis=1)
            out.append((pmat, jnp.concatenate(dens, axis=1)))
        return out

    def weighted_values(e, pd_list):
        r0 = e * pair
        for (j, _, ha, hb), (pmat, den) in zip(units, pd_list):
            o = _dot(vt_ref[j * B_HEAD_DIM:(j + 1) * B_HEAD_DIM, r0:r0 + nkeys], pmat)
            o = (o * (1.0 / den)).astype(BF16)
            for qc in range(2):
                c0 = r0 + qc * CHUNK
                ot_ref[ha * B_HEAD_DIM:(ha + 1) * B_HEAD_DIM, c0:c0 + CHUNK] = (
                    o[:, qc * LANES:qc * LANES + CHUNK])
                ot_ref[hb * B_HEAD_DIM:(hb + 1) * B_HEAD_DIM, c0:c0 + CHUNK] = (
                    o[:, qc * LANES + CHUNK:(qc + 1) * LANES])

    npairs = tt // pair
    s_next = scores(0)
    for e in range(npairs):
        s_cur = s_next
        if e + 1 < npairs:
            s_next = scores(e + 1)
        run_jobs(JOB_SHARES[3])
        weighted_values(e, softmax(e, s_cur))

    run_jobs(JOB_SHARES[4])
    mix = _dot_tn(ot_ref[...], wo_ref[...])
    run_jobs(1.0)
    out_ref[...] = x + _rms_scale(mix) * npost_ref[...]


def _attn_pair_mixer(x, tables, sinks, npre, kvn, wq, wkv, wo, npost, *, tt, ffn=None):
    batch, seq, _ = x.shape
    assert seq % tt == 0 and tt % (2 * CHUNK) == 0 and WINDOW == 2 * CHUNK
    ext = WINDOW + tt
    fused = ffn is not None
    nt = seq // tt
    kern = functools.partial(_attn_pair_kernel, tt=tt, fused=fused)
    if fused:
        in_spec = pl.BlockSpec((1, tt, D_MODEL), lambda b, t: (b, jnp.minimum(t, nt - 1), 0))
        out_spec = pl.BlockSpec((1, tt, D_MODEL), lambda b, t: (b, jnp.maximum(t - 1, 0), 0))
        tab_spec = pl.BlockSpec((tt, LANES), lambda b, t: (jnp.minimum(t, nt - 1), 0))
    else:
        in_spec = out_spec = pl.BlockSpec((1, tt, D_MODEL), lambda b, t: (b, t, 0))
        tab_spec = pl.BlockSpec((tt, LANES), lambda b, t: (t, 0))
    c_spec = pl.BlockSpec((1, WINDOW, KV_WIDTH), lambda b, t: (b, 0, 0))
    cache_shape = jax.ShapeDtypeStruct((batch, WINDOW, KV_WIDTH), F32)
    in_specs = [in_spec] + [tab_spec] * 4 + [
        pl.BlockSpec(memory_space=pltpu.SMEM),
        _const_spec((1, D_MODEL)),
        _const_spec((1, D_MODEL)),
        _const_spec((D_MODEL, D_MODEL)),
        _const_spec((D_MODEL, 2 * KV_WIDTH)),
        _const_spec((D_MODEL, D_MODEL)),
        _const_spec((1, D_MODEL)),
    ]
    args = [x, *tables, sinks, npre, kvn, wq, wkv, wo, npost]
    scratch = [
        pltpu.VMEM((tt, D_MODEL), BF16),
        pltpu.VMEM((B_KV_HEADS, ext, LANES), BF16),
        pltpu.VMEM((B_KV_HEADS, ext, LANES), BF16),
        pltpu.VMEM((KV_WIDTH, ext), BF16),
        pltpu.VMEM((D_MODEL, tt), BF16),
    ]
    if fused:
        fpre, fpost, win, wout, layer = ffn

        def layer_spec(shape):
            return pl.BlockSpec((None,) + shape, lambda b, t: (layer, 0, 0),
                                pipeline_mode=pl.Buffered(1))

        in_specs += [_const_spec((1, D_MODEL)), _const_spec((1, D_MODEL)),
                     layer_spec((D_MODEL, 2 * D_FF)), layer_spec((D_FF, D_MODEL))]
        args += [fpre, fpost, win, wout]
        scratch.append(pltpu.VMEM((tt, D_MODEL), F32))
    return pl.pallas_call(
        kern,
        grid=(batch, nt + 1 if fused else nt),
        in_specs=in_specs,
        out_specs=[out_spec, c_spec, c_spec],
        out_shape=[jax.ShapeDtypeStruct(x.shape, F32), cache_shape, cache_shape],
        scratch_shapes=scratch,
        compiler_params=pltpu.CompilerParams(
            dimension_semantics=("arbitrary", "arbitrary"), vmem_limit_bytes=VMEM_LIMIT_BYTES),
        name="attn_pair_mixer",
    )(*args)


def _rope_tables(pos, q_scale):
    half = B_HEAD_DIM // 2
    inv = ROPE_THETA ** (-jnp.arange(half, dtype=F32) / half)
    ang = pos.astype(F32)[:, None] * inv[None, :]
    cos = jnp.cos(ang)
    sin = jnp.sin(ang)
    reps = LANES // B_HEAD_DIM
    cos_t = jnp.tile(jnp.concatenate([cos, cos], axis=1), (1, reps))
    sin_t = jnp.tile(jnp.concatenate([-sin, sin], axis=1), (1, reps))
    return cos_t * q_scale, sin_t * q_scale, cos_t, sin_t


def _trunk(x, pos, state, cache_k, cache_v, w, *, nb, tt, hgrn_nb, hgrn_tt, hgrn_chunk, cq):
    batch, seq, _ = x.shape
    row = lambda a: a.reshape(1, D_MODEL)
    x, st = _hgrn_mixer(x, state, row(w["norm_mix_pre"][0]), w["w_a_in"], w["a_lower_bound"],
                        row(w["a_out_norm"]), w["w_a_out"], row(w["norm_mix_post"][0]),
                        nb=hgrn_nb, tt=hgrn_tt, chunk=hgrn_chunk)
    x = _ffn(x.reshape(batch * seq, D_MODEL), row(w["norm_ffn_pre"][0]), row(w["norm_ffn_post"][0]),
             w["w_ffn_in"], w["w_ffn_out"], 0).reshape(batch, seq, D_MODEL)
    attn_w = (w["b_sinks"], row(w["norm_mix_pre"][1]), row(w["kv_norm"]), w["w_b_q"], w["w_kv"],
              w["w_b_out"], row(w["norm_mix_post"][1]))
    ffn1 = (row(w["norm_ffn_pre"][1]), row(w["norm_ffn_post"][1]), w["w_ffn_in"], w["w_ffn_out"], 1)
    if cache_k is None and nb == 1 and cq == CHUNK:
        x, kc, vc = _attn_pair_mixer(x, _rope_tables(pos, SOFTMAX_SCALE * LOG2_E), *attn_w, tt=tt,
                                     ffn=ffn1)
    else:
        x, kc, vc = _attn_mixer(x, cache_k, cache_v, _rope_tables(pos, SOFTMAX_SCALE), *attn_w,
                                nb=nb, tt=tt, cq=cq)
        x = _ffn(x.reshape(batch * seq, D_MODEL), *ffn1).reshape(batch, seq, D_MODEL)
    return x, st, kc, vc


def kernel(x_prompt, x_sample, state_hgrn, cache_k, cache_v, norm_mix_pre, norm_mix_post, norm_ffn_pre, norm_ffn_post, w_ffn_in, w_ffn_out, w_a_in, a_lower_bound, a_out_norm, w_a_out, kv_norm, w_kv, w_b_q, b_sinks, w_b_out):
    w = dict(
        norm_mix_pre=norm_mix_pre, norm_mix_post=norm_mix_post,
        norm_ffn_pre=norm_ffn_pre, norm_ffn_post=norm_ffn_post,
        w_ffn_in=w_ffn_in.astype(BF16), w_ffn_out=w_ffn_out.astype(BF16),
        w_a_in=w_a_in[0].astype(BF16), a_lower_bound=a_lower_bound,
        a_out_norm=a_out_norm[0], w_a_out=w_a_out[0].astype(BF16),
        kv_norm=kv_norm, w_kv=w_kv.astype(BF16), w_b_q=w_b_q[0].astype(BF16),
        b_sinks=b_sinks[0], w_b_out=w_b_out[0].astype(BF16),
    )
    bp, tp, _ = x_prompt.shape
    bs, ts, _ = x_sample.shape

    zero_state = jnp.zeros((bp, A_HEADS, A_DK, A_DV), F32)
    y_p, st_p, kc_p, vc_p = _trunk(
        x_prompt, jnp.arange(tp), zero_state, None, None, w,
        nb=1, tt=ROW_TILE, hgrn_nb=HGRN_STREAMS, hgrn_tt=ROW_TILE // HGRN_STREAMS,
        hgrn_chunk=HGRN_CHUNK, cq=CHUNK)

    nb_s = ROW_TILE // (2 * ts)
    y_s, st_s, kc_s, vc_s = _trunk(
        x_sample, PAST_LEN + jnp.arange(ts), state_hgrn[0],
        cache_k.reshape(bs, WINDOW, KV_WIDTH), cache_v.reshape(bs, WINDOW, KV_WIDTH), w,
        nb=nb_s, tt=ts, hgrn_nb=nb_s, hgrn_tt=ts, hgrn_chunk=ts, cq=ts)

    cache4 = lambda a: a.reshape(a.shape[0], WINDOW, B_KV_HEADS, B_HEAD_DIM)
    return (y_p, y_s, st_p[None], st_s[None],
            cache4(kc_p), cache4(vc_p), cache4(kc_s), cache4(vc_s))
```

```python
import functools

import jax
import jax.numpy as jnp
from jax import lax
from jax.experimental import pallas as pl
from jax.experimental.pallas import tpu as pltpu

F32 = jnp.float32
BF16 = jnp.bfloat16

D_MODEL = 1024
A_HEADS = 8
A_DK = 128
A_DV = 128
B_HEAD_DIM = 64
B_Q_HEADS = 16
B_KV_HEADS = 4
B_GROUP = B_Q_HEADS // B_KV_HEADS
KV_WIDTH = B_KV_HEADS * B_HEAD_DIM
WINDOW = 128
CHUNK = 64
PAST_LEN = 2048
D_FF = 2816
ROPE_THETA = 10000.0
NORM_EPS = 1e-6
SOFTMAX_SCALE = B_HEAD_DIM ** -0.5
LOG2_E = 1.4426950408889634

LANES = 128
SUBLANES = 8
VMEM_LIMIT_BYTES = 52 * 1024 * 1024

ROW_TILE = 512
FFN_ROW_TILE = 1024
HGRN_CHUNK = 128
FFN_CHUNK = 256
MAX_FACTOR_EXPONENT = 60.0
EXACT_BLOCK = 16
HGRN_STREAMS = 1
HEAD_GROUPS = 2
JOB_SHARES = (0.18, 0.09, 0.09, 0.09, 0.09)


def _rms_scale(x):
    ms = jnp.mean(x * x, axis=-1, keepdims=True)
    return x * lax.rsqrt(ms + NORM_EPS)


def _dot(a, b):
    return jnp.dot(a, b, preferred_element_type=F32)


def _dot_nt(a, b):
    return lax.dot_general(a, b, (((1,), (1,)), ((), ())), preferred_element_type=F32)


def _dot_tn(a, b):
    return lax.dot_general(a, b, (((0,), (0,)), ((), ())), preferred_element_type=F32)


def _const_spec(shape):
    nd = len(shape)
    return pl.BlockSpec(shape, lambda *_: (0,) * nd, pipeline_mode=pl.Buffered(1))


def _ffn_kernel(x_ref, npre_ref, npost_ref, win_ref, wout_ref, o_ref, *, sub_rows):
    starts = range(0, x_ref.shape[0], sub_rows)
    normed = [(_rms_scale(x_ref[r0:r0 + sub_rows, :]) * npre_ref[...]).astype(BF16) for r0 in starts]
    for r0, h in zip(starts, normed):
        x = x_ref[r0:r0 + sub_rows, :]
        acc = jnp.zeros(x.shape, F32)
        for c0 in range(0, D_FF, FFN_CHUNK):
            c1 = min(c0 + FFN_CHUNK, D_FF)
            a = _dot(h, win_ref[:, c0:c1])
            b = _dot(h, win_ref[:, D_FF + c0:D_FF + c1])
            g = (a * jax.nn.sigmoid(a) * b).astype(BF16)
            acc = acc + _dot(g, wout_ref[c0:c1, :])
        o_ref[r0:r0 + sub_rows, :] = x + _rms_scale(acc) * npost_ref[...]


def _ffn(x2d, npre, npost, win, wout, layer):
    rows = x2d.shape[0]
    tile = min(FFN_ROW_TILE, rows)
    sub_rows = min(ROW_TILE, tile)
    assert rows % tile == 0 and tile % sub_rows == 0

    def layer_spec(shape):
        return pl.BlockSpec((None,) + shape, lambda i: (layer, 0, 0), pipeline_mode=pl.Buffered(1))

    return pl.pallas_call(
        functools.partial(_ffn_kernel, sub_rows=sub_rows),
        grid=(rows // tile,),
        in_specs=[
            pl.BlockSpec((tile, D_MODEL), lambda i: (i, 0)),
            _const_spec((1, D_MODEL)),
            _const_spec((1, D_MODEL)),
            layer_spec((D_MODEL, 2 * D_FF)),
            layer_spec((D_FF, D_MODEL)),
        ],
        out_specs=pl.BlockSpec((tile, D_MODEL), lambda i: (i, 0)),
        out_shape=jax.ShapeDtypeStruct((rows, D_MODEL), F32),
        compiler_params=pltpu.CompilerParams(
            dimension_semantics=("parallel",), vmem_limit_bytes=VMEM_LIMIT_BYTES),
        name="ffn",
    )(x2d, npre, npost, win, wout)


def _cumsum_rows(x):
    c, w = x.shape
    groups = c // SUBLANES
    y = x.reshape(groups, SUBLANES, w)
    sub = lax.broadcasted_iota(jnp.int32, y.shape, 1)
    shift = 1
    while shift < SUBLANES:
        y = y + jnp.where(sub >= shift, pltpu.roll(y, shift, axis=1), 0.0)
        shift *= 2
    tot = jnp.broadcast_to(y[:, SUBLANES - 1:SUBLANES, :], y.shape)
    inc = tot
    shift = 1
    while shift < groups:
        inc = inc + jnp.concatenate(
            [jnp.zeros((shift, SUBLANES, w), F32), inc[:groups - shift]], axis=0)
        shift *= 2
    return (y + (inc - tot)).reshape(c, w)


def _hgrn_kernel(x_ref, st_in_ref, npre_ref, win_ref, alb_ref, gnorm_ref, wout_ref, npost_ref,
                 y_ref, st_ref, qs_ref, lf_ref, inp_ref, v_ref, sg_ref, on_ref,
                 *, nb, tt, chunk):
    t = pl.program_id(1)
    rows = nb * tt

    @pl.when(t == 0)
    def _():
        for n in range(nb):
            for h in range(A_HEADS):
                st_ref[n, h] = st_in_ref[n, h].T

    x = x_ref[...].reshape(rows, D_MODEL)
    hn = (_rms_scale(x) * npre_ref[...]).astype(BF16)

    alb = alb_ref[...]
    e = jnp.exp(alb - jnp.max(alb, axis=0, keepdims=True))
    lb = e[0:1] / jnp.sum(e, axis=0, keepdims=True)

    f = _dot(hn, win_ref[:, D_MODEL:2 * D_MODEL])
    forget = lb + (1.0 - lb) * jax.nn.sigmoid(f)
    lf_ref[...] = jnp.log(forget)
    inp_ref[...] = 1.0 - forget

    def project_query(c0, width):
        q = _dot(hn, win_ref[:, c0:c0 + width])
        qs_ref[:, c0:c0 + width] = q * jax.nn.sigmoid(q)

    def project_value(c0, width):
        v_ref[:, c0:c0 + width] = _dot(
            hn, win_ref[:, 2 * D_MODEL + c0:2 * D_MODEL + c0 + width]).astype(BF16)

    def project_gate(c0, width):
        g = _dot(hn, win_ref[:, 3 * D_MODEL + c0:3 * D_MODEL + c0 + width])
        sg_ref[:, c0:c0 + width] = g * jax.nn.sigmoid(g)

    def write_rows(mix):
        y = x + _rms_scale(mix) * npost_ref[...]
        y_ref[...] = y.reshape(nb, tt, D_MODEL)

    half = chunk // 2
    ri = lax.broadcasted_iota(jnp.int32, (chunk, chunk), 0)
    ci = lax.broadcasted_iota(jnp.int32, (chunk, chunk), 1)
    causal = ri >= ci

    def prepare(blk, h):
        rs = slice(blk * chunk, (blk + 1) * chunk)
        sl = slice(h * A_DK, (h + 1) * A_DK)
        b = _cumsum_rows(lf_ref[rs, sl])
        bmid = b[half - 1:half]
        blast = b[chunk - 1:chunk]
        qsc = qs_ref[rs, sl] * jnp.exp(b - bmid)
        inp = inp_ref[rs, sl] * jnp.exp(bmid - b)
        return dict(
            rs=rs, sl=sl, qt=qsc.astype(BF16), kt=inp.astype(BF16),
            qi=(qsc * jnp.exp(bmid)).astype(BF16),
            kh=(inp * jnp.exp(blast - bmid)).astype(BF16),
            vh=v_ref[rs, sl], decay=jnp.exp(blast))

    def emit_output(u, sc, st):
        p = jnp.where(causal, sc, 0.0).astype(BF16)
        o = _dot(p, u["vh"]) + _dot_nt(u["qi"], st.astype(BF16))
        on = _rms_scale(o) * gnorm_ref[:, u["sl"]] * sg_ref[u["rs"], u["sl"]]
        on_ref[u["rs"], u["sl"]] = on.astype(BF16)

    def recurrence(heads, jobs):
        nstages = tt // chunk
        slots = 2 * nstages
        jobs = list(jobs)

        def run_jobs(slot):
            for _ in range(-(-len(jobs) // (slots - slot))):
                jobs.pop(0)()

        states = {}
        for stage in range(nstages):
            blocks = [n * nstages + stage for n in range(nb)]
            units = [(blk, h, prepare(blk, h)) for blk in blocks for h in heads]
            scores = [_dot_nt(u["qt"], u["kt"]) for _, _, u in units]
            run_jobs(2 * stage)
            updated = {}
            for blk, h, u in units:
                key = ((blk * chunk) // tt, h)
                if key not in states:
                    states[key] = st_ref[key[0], h]
                updated[key] = states[key] * u["decay"] + _dot_tn(u["vh"], u["kh"])
            run_jobs(2 * stage + 1)
            for (blk, h, u), sc in zip(units, scores):
                emit_output(u, sc, states[((blk * chunk) // tt, h)])
            states.update(updated)
        for (n, h), st in states.items():
            st_ref[n, h] = st

    def factorised_path():
        heads_per_group = A_HEADS // HEAD_GROUPS
        piece = 2 * LANES
        partial_mix = [[] for _ in range(HEAD_GROUPS)]

        def project_out(g, c0, width):
            partial_mix[g].append(
                _dot(on_ref[:, g * gw:(g + 1) * gw], wout_ref[g * gw:(g + 1) * gw, c0:c0 + width]))

        for g in range(HEAD_GROUPS):
            jobs = []
            if g + 1 < HEAD_GROUPS:
                jobs += [functools.partial(proj, (g + 1) * gw + c0, piece)
                         for proj in (project_query, project_value, project_gate)
                         for c0 in range(0, gw, piece)]
            if g > 0:
                jobs += [functools.partial(project_out, g - 1, c0, piece)
                         for c0 in range(0, D_MODEL, piece)]
            recurrence(range(g * heads_per_group, (g + 1) * heads_per_group), jobs)
        project_out(HEAD_GROUPS - 1, 0, D_MODEL)
        write_rows(functools.reduce(
            jnp.add, [jnp.concatenate(parts, axis=1) for parts in partial_mix]))

    def exact_step(idx, carry):
        n = (idx * EXACT_BLOCK) // tt
        rs = pl.ds(pl.multiple_of(idx * EXACT_BLOCK, EXACT_BLOCK), EXACT_BLOCK)
        row = lax.broadcasted_iota(jnp.int32, (EXACT_BLOCK, A_DK), 0)
        for h in range(A_HEADS):
            sl = slice(h * A_DK, (h + 1) * A_DK)
            b = _cumsum_rows(lf_ref[rs, sl])
            blast = b[EXACT_BLOCK - 1:EXACT_BLOCK]
            qs = qs_ref[rs, sl]
            inp = inp_ref[rs, sl]
            vh = v_ref[rs, sl]
            vf = vh.astype(F32)
            st = st_ref[n, h]
            o = _dot_nt((qs * jnp.exp(b)).astype(BF16), st.astype(BF16))
            for s in range(EXACT_BLOCK):
                decay = jnp.exp(jnp.where(row >= s, b - b[s:s + 1], -jnp.inf))
                score = jnp.sum(qs * decay * inp[s:s + 1], axis=-1, keepdims=True)
                o = o + score * vf[s:s + 1]
            kh = (inp * jnp.exp(blast - b)).astype(BF16)
            st_ref[n, h] = st * jnp.exp(blast) + _dot_tn(vh, kh)
            on = _rms_scale(o) * gnorm_ref[:, sl] * sg_ref[rs, sl]
            on_ref[rs, sl] = on.astype(BF16)
        return carry

    gw = D_MODEL // HEAD_GROUPS
    project_query(0, gw)
    project_value(0, gw)
    project_gate(0, gw)

    max_step = -jnp.min(lf_ref[...])
    bounded = max_step * half <= MAX_FACTOR_EXPONENT

    pl.when(bounded)(factorised_path)

    @pl.when(jnp.logical_not(bounded))
    def _():
        project_query(gw, D_MODEL - gw)
        project_value(gw, D_MODEL - gw)
        project_gate(gw, D_MODEL - gw)
        lax.fori_loop(0, rows // EXACT_BLOCK, exact_step, 0)
        write_rows(_dot(on_ref[...], wout_ref[...]))

    @pl.when(t == pl.num_programs(1) - 1)
    def _():
        for n in range(nb):
            for h in range(A_HEADS):
                st_ref[n, h] = st_ref[n, h].T


def _hgrn_mixer(x, state, npre, win, alb, gnorm, wout, npost, *, nb, tt, chunk):
    batch, seq, _ = x.shape
    assert batch % nb == 0 and seq % tt == 0 and tt % chunk == 0 and chunk % (2 * SUBLANES) == 0
    rows = nb * tt
    kern = functools.partial(_hgrn_kernel, nb=nb, tt=tt, chunk=chunk)
    st_spec = pl.BlockSpec((nb, A_HEADS, A_DK, A_DV), lambda b, t: (b, 0, 0, 0))
    return pl.pallas_call(
        kern,
        grid=(batch // nb, seq // tt),
        in_specs=[
            pl.BlockSpec((nb, tt, D_MODEL), lambda b, t: (b, t, 0)),
            st_spec,
            _const_spec((1, D_MODEL)),
            _const_spec((D_MODEL, 4 * D_MODEL)),
            _const_spec(alb.shape),
            _const_spec((1, D_MODEL)),
            _const_spec((D_MODEL, D_MODEL)),
            _const_spec((1, D_MODEL)),
        ],
        out_specs=[
            pl.BlockSpec((nb, tt, D_MODEL), lambda b, t: (b, t, 0)),
            st_spec,
        ],
        out_shape=[
            jax.ShapeDtypeStruct(x.shape, F32),
            jax.ShapeDtypeStruct(state.shape, F32),
        ],
        scratch_shapes=[
            pltpu.VMEM((rows, D_MODEL), F32),
            pltpu.VMEM((rows, D_MODEL), F32),
            pltpu.VMEM((rows, D_MODEL), F32),
            pltpu.VMEM((rows, D_MODEL), BF16),
            pltpu.VMEM((rows, D_MODEL), F32),
            pltpu.VMEM((rows, D_MODEL), BF16),
        ],
        compiler_params=pltpu.CompilerParams(
            dimension_semantics=("parallel", "arbitrary"), vmem_limit_bytes=VMEM_LIMIT_BYTES),
        name="hgrn_mixer",
    )(x, state, npre, win, alb, gnorm, wout, npost)


def _rope_cols(x, cos_t, sin_t):
    lane = lax.broadcasted_iota(jnp.int32, (x.shape[0], LANES), 1)
    first_half = (lane % B_HEAD_DIM) < (B_HEAD_DIM // 2)
    cols = []
    for c0 in range(0, x.shape[1], LANES):
        xc = x[:, c0:c0 + LANES]
        partner = jnp.where(first_half,
                            pltpu.roll(xc, LANES - B_HEAD_DIM // 2, axis=1),
                            pltpu.roll(xc, B_HEAD_DIM // 2, axis=1))
        cols.append(xc * cos_t + partner * sin_t)
    return jnp.concatenate(cols, axis=1)


def _attn_cached_kernel(x_ref, kc_in_ref, vc_in_ref, cosq_ref, sinq_ref, cosk_ref, sinkt_ref,
                        sinks_ref, npre_ref, kvn_ref, wq_ref, wkv_ref, wo_ref, npost_ref,
                        y_ref, kc_ref, vc_ref, q_ref, kl_ref, kh_ref, vl_ref, ot_ref, *, nb, tt):
    rows = nb * tt
    ext = WINDOW + tt

    x = x_ref[...].reshape(rows, D_MODEL)
    xs = _rms_scale(x)
    hq = (xs * npre_ref[...]).astype(BF16)
    hk = (xs * kvn_ref[...]).astype(BF16)

    def per_stream(tab_ref):
        tab = tab_ref[...]
        return tab if nb == 1 else jnp.concatenate([tab] * nb, axis=0)

    q = _dot(hq, wq_ref[...])
    q_ref[...] = _rope_cols(q, per_stream(cosq_ref), per_stream(sinq_ref)).astype(BF16)
    kv = _dot(hk, wkv_ref[...])
    k_new = _rope_cols(kv[:, :KV_WIDTH], per_stream(cosk_ref), per_stream(sinkt_ref))
    v_new = kv[:, KV_WIDTH:]

    low = lax.broadcasted_iota(jnp.int32, (ext, LANES), 1) < B_HEAD_DIM
    for n in range(nb):
        k_ext = jnp.concatenate([kc_in_ref[n], k_new[n * tt:(n + 1) * tt]], axis=0)
        v_ext = jnp.concatenate([vc_in_ref[n], v_new[n * tt:(n + 1) * tt]], axis=0)
        kc_ref[n] = k_ext[ext - WINDOW:]
        vc_ref[n] = v_ext[ext - WINDOW:]
        for m in range(KV_WIDTH // LANES):
            ka = k_ext[:, m * LANES:(m + 1) * LANES]
            kr = pltpu.roll(ka, B_HEAD_DIM, axis=1)
            va = v_ext[:, m * LANES:(m + 1) * LANES]
            vr = pltpu.roll(va, B_HEAD_DIM, axis=1)
            kl_ref[2 * m, n] = jnp.where(low, ka, 0.0).astype(BF16)
            kh_ref[2 * m, n] = jnp.where(low, 0.0, kr).astype(BF16)
            kl_ref[2 * m + 1, n] = jnp.where(low, kr, 0.0).astype(BF16)
            kh_ref[2 * m + 1, n] = jnp.where(low, 0.0, ka).astype(BF16)
            vl_ref[2 * m, n] = jnp.where(low, va, 0.0).astype(BF16)
            vl_ref[2 * m + 1, n] = jnp.where(low, vr, 0.0).astype(BF16)

    first_head = lax.broadcasted_iota(jnp.int32, (1, 2 * tt), 1) < tt
    pairs = [(j, kx_ref, ha, hb) for j in range(B_KV_HEADS)
             for kx_ref, ha, hb in ((kl_ref, 4 * j, 4 * j + 2), (kh_ref, 4 * j + 1, 4 * j + 3))]

    def scores(n):
        rs = slice(n * tt, (n + 1) * tt)
        out = []
        for j, kx_ref, _, _ in pairs:
            qq = jnp.concatenate([q_ref[rs, 2 * j * LANES:(2 * j + 1) * LANES],
                                  q_ref[rs, (2 * j + 1) * LANES:(2 * j + 2) * LANES]], axis=0)
            out.append(_dot_nt(kx_ref[j, n], qq))
        return out

    def softmax(s_list):
        out = []
        for (_, _, ha, hb), s in zip(pairs, s_list):
            sink = jnp.where(first_head, sinks_ref[ha], sinks_ref[hb])
            mx = jnp.maximum(jnp.max(s, axis=0, keepdims=True), sink)
            p = jnp.exp(s - mx)
            den = jnp.sum(p, axis=0, keepdims=True) + jnp.exp(sink - mx)
            out.append((p.astype(BF16), den))
        return out

    def weighted_values(n, pd_list):
        cs = slice(n * tt, (n + 1) * tt)
        for (j, _, ha, hb), (p, den) in zip(pairs, pd_list):
            o = _dot_tn(vl_ref[j, n], p)[:B_HEAD_DIM]
            o = (o * (1.0 / den)).astype(BF16)
            ot_ref[ha * B_HEAD_DIM:(ha + 1) * B_HEAD_DIM, cs] = o[:, :tt]
            ot_ref[hb * B_HEAD_DIM:(hb + 1) * B_HEAD_DIM, cs] = o[:, tt:]

    s_next = scores(0)
    for n in range(nb):
        s_cur = s_next
        if n + 1 < nb:
            s_next = scores(n + 1)
        weighted_values(n, softmax(s_cur))

    mix = _dot_tn(ot_ref[...], wo_ref[...])
    y = x + _rms_scale(mix) * npost_ref[...]
    y_ref[...] = y.reshape(nb, tt, D_MODEL)


def _attn_cached_mixer(x, cache_k, cache_v, tables, sinks, npre, kvn, wq, wkv, wo, npost, *, nb):
    batch, tt, _ = x.shape
    assert batch % nb == 0 and tt % (2 * SUBLANES) == 0
    rows = nb * tt
    ext = WINDOW + tt
    kern = functools.partial(_attn_cached_kernel, nb=nb, tt=tt)
    x_spec = pl.BlockSpec((nb, tt, D_MODEL), lambda b: (b, 0, 0))
    c_spec = pl.BlockSpec((nb, WINDOW, KV_WIDTH), lambda b: (b, 0, 0))
    tab_spec = _const_spec((tt, LANES))
    cache_shape = jax.ShapeDtypeStruct((batch, WINDOW, KV_WIDTH), F32)
    return pl.pallas_call(
        kern,
        grid=(batch // nb,),
        in_specs=[x_spec, c_spec, c_spec] + [tab_spec] * 4 + [
            pl.BlockSpec(memory_space=pltpu.SMEM),
            _const_spec((1, D_MODEL)),
            _const_spec((1, D_MODEL)),
            _const_spec((D_MODEL, D_MODEL)),
            _const_spec((D_MODEL, 2 * KV_WIDTH)),
            _const_spec((D_MODEL, D_MODEL)),
            _const_spec((1, D_MODEL)),
        ],
        out_specs=[x_spec, c_spec, c_spec],
        out_shape=[jax.ShapeDtypeStruct(x.shape, F32), cache_shape, cache_shape],
        scratch_shapes=[
            pltpu.VMEM((rows, D_MODEL), BF16),
            pltpu.VMEM((B_KV_HEADS, nb, ext, LANES), BF16),
            pltpu.VMEM((B_KV_HEADS, nb, ext, LANES), BF16),
            pltpu.VMEM((B_KV_HEADS, nb, ext, LANES), BF16),
            pltpu.VMEM((D_MODEL, rows), BF16),
        ],
        compiler_params=pltpu.CompilerParams(
            dimension_semantics=("parallel",), vmem_limit_bytes=VMEM_LIMIT_BYTES),
        name="attn_cached_mixer",
    )(x, cache_k, cache_v, *tables, sinks, npre, kvn, wq, wkv, wo, npost)


def _ffn_steps(src_ref, dst_ref, npre_ref, npost_ref, win_ref, wout_ref):
    state = {}

    def chunk(c0):
        if not state:
            state["h"] = (_rms_scale(src_ref[...]) * npre_ref[...]).astype(BF16)
            state["acc"] = jnp.zeros(src_ref.shape, F32)
        h = state["h"]
        c1 = min(c0 + FFN_CHUNK, D_FF)
        a = _dot(h, win_ref[:, c0:c1])
        b = _dot(h, win_ref[:, D_FF + c0:D_FF + c1])
        g = (a * jax.nn.sigmoid(a) * b).astype(BF16)
        state["acc"] = state["acc"] + _dot(g, wout_ref[c0:c1, :])
        if c1 == D_FF:
            dst_ref[...] = (src_ref[...] + _rms_scale(state["acc"]) * npost_ref[...]).reshape(dst_ref.shape)

    return [functools.partial(chunk, c0) for c0 in range(0, D_FF, FFN_CHUNK)]


def _attn_pair_kernel(*refs, tt, fused):
    if fused:
        (x_ref, cosq_ref, sinq_ref, cosk_ref, sinkt_ref, sinks_ref,
         npre_ref, kvn_ref, wq_ref, wkv_ref, wo_ref, npost_ref,
         fpre_ref, fpost_ref, win_ref, wout_ref,
         y_ref, kc_ref, vc_ref, q_ref, kl_ref, kh_ref, vt_ref, ot_ref, mid_ref) = refs
    else:
        (x_ref, cosq_ref, sinq_ref, cosk_ref, sinkt_ref, sinks_ref,
         npre_ref, kvn_ref, wq_ref, wkv_ref, wo_ref, npost_ref,
         y_ref, kc_ref, vc_ref, q_ref, kl_ref, kh_ref, vt_ref, ot_ref) = refs
    t = pl.program_id(1)
    if not fused:
        _attention_tile(t, tt, x_ref, cosq_ref, sinq_ref, cosk_ref, sinkt_ref, sinks_ref,
                        npre_ref, kvn_ref, wq_ref, wkv_ref, wo_ref, npost_ref, kc_ref, vc_ref,
                        q_ref, kl_ref, kh_ref, vt_ref, ot_ref, [], y_ref.at[0])
        return

    last = pl.num_programs(1) - 1

    @pl.when((pl.program_id(0) == 0) & (t == 0))
    def _():
        mid_ref[...] = jnp.zeros(mid_ref.shape, F32)

    @pl.when(t < last)
    def _():
        jobs = _ffn_steps(mid_ref, y_ref, fpre_ref, fpost_ref, win_ref, wout_ref)
        _attention_tile(t, tt, x_ref, cosq_ref, sinq_ref, cosk_ref, sinkt_ref, sinks_ref,
                        npre_ref, kvn_ref, wq_ref, wkv_ref, wo_ref, npost_ref, kc_ref, vc_ref,
                        q_ref, kl_ref, kh_ref, vt_ref, ot_ref, jobs, mid_ref)

    @pl.when(t == last)
    def _():
        for job in _ffn_steps(mid_ref, y_ref, fpre_ref, fpost_ref, win_ref, wout_ref):
            job()


def _attention_tile(t, tt, x_ref, cosq_ref, sinq_ref, cosk_ref, sinkt_ref, sinks_ref,
                    npre_ref, kvn_ref, wq_ref, wkv_ref, wo_ref, npost_ref, kc_ref, vc_ref,
                    q_ref, kl_ref, kh_ref, vt_ref, ot_ref, jobs, out_ref):
    ext = WINDOW + tt
    pair = 2 * CHUNK
    nkeys = WINDOW + pair
    nvis = WINDOW + CHUNK
    jobs = list(jobs)
    njobs = len(jobs)

    def run_jobs(share):
        for _ in range(min(round(share * njobs), len(jobs))):
            jobs.pop(0)()

    @pl.when(t == 0)
    def _():
        kc_ref[...] = jnp.zeros(kc_ref.shape, F32)
        vc_ref[...] = jnp.zeros(vc_ref.shape, F32)

    x = x_ref[0]
    xs = _rms_scale(x)
    hq = (xs * npre_ref[...]).astype(BF16)
    hk = (xs * kvn_ref[...]).astype(BF16)

    run_jobs(JOB_SHARES[0])
    q = _dot(hq, wq_ref[...])
    q_ref[...] = _rope_cols(q, cosq_ref[...], sinq_ref[...]).astype(BF16)
    run_jobs(JOB_SHARES[1])
    kv = _dot(hk, wkv_ref[...])
    run_jobs(JOB_SHARES[2])
    k_ext = jnp.concatenate(
        [kc_ref[0], _rope_cols(kv[:, :KV_WIDTH], cosk_ref[...], sinkt_ref[...])], axis=0)
    v_ext = jnp.concatenate([vc_ref[0], kv[:, KV_WIDTH:]], axis=0)
    kc_ref[0] = k_ext[ext - WINDOW:]
    vc_ref[0] = v_ext[ext - WINDOW:]
    vt_ref[...] = v_ext.T.astype(BF16)

    low = lax.broadcasted_iota(jnp.int32, (ext, LANES), 1) < B_HEAD_DIM
    for m in range(KV_WIDTH // LANES):
        ka = k_ext[:, m * LANES:(m + 1) * LANES]
        kr = pltpu.roll(ka, B_HEAD_DIM, axis=1)
        kl_ref[2 * m] = jnp.where(low, ka, 0.0).astype(BF16)
        kh_ref[2 * m] = jnp.where(low, 0.0, kr).astype(BF16)
        kl_ref[2 * m + 1] = jnp.where(low, kr, 0.0).astype(BF16)
        kh_ref[2 * m + 1] = jnp.where(low, 0.0, ka).astype(BF16)

    vis_row = lax.broadcasted_iota(jnp.int32, (nvis, LANES), 0)
    first_bias = [jnp.where(vis_row + (t * tt - WINDOW + qc * CHUNK) >= 0, 0.0, -jnp.inf)
                  for qc in range(2)]
    first_head = lax.broadcasted_iota(jnp.int32, (1, LANES), 1) < CHUNK
    no_keys = jnp.zeros((CHUNK, LANES), BF16)

    units = [(j, kx_ref, ha, hb) for j in range(B_KV_HEADS)
             for kx_ref, ha, hb in ((kl_ref, 4 * j, 4 * j + 2), (kh_ref, 4 * j + 1, 4 * j + 3))]

    def scores(e):
        r0 = e * pair
        out = []
        for j, kx_ref, _, _ in units:
            ca = slice(2 * j * LANES, (2 * j + 1) * LANES)
            cb = slice((2 * j + 1) * LANES, (2 * j + 2) * LANES)
            qq = jnp.concatenate([q_ref[r0:r0 + CHUNK, ca], q_ref[r0:r0 + CHUNK, cb],
                                  q_ref[r0 + CHUNK:r0 + pair, ca], q_ref[r0 + CHUNK:r0 + pair, cb]], axis=0)
            out.append(_dot_nt(kx_ref[j, r0:r0 + nkeys, :], qq))
        return out

    def softmax(e, s_list):
        out = []
        for (_, _, ha, hb), s in zip(units, s_list):
            sink = jnp.where(first_head, sinks_ref[ha], sinks_ref[hb]) * LOG2_E
            ps, dens = [], []
            for qc in range(2):
                sq = s[qc * CHUNK:qc * CHUNK + nvis, qc * LANES:(qc + 1) * LANES]
                if e == 0:
                    sq = sq + first_bias[qc]
                mx = jnp.maximum(jnp.max(sq, axis=0, keepdims=True), sink)
                p = jnp.exp2(sq - mx)
                dens.append(jnp.sum(p, axis=0, keepdims=True) + jnp.exp2(sink - mx))
                ps.append(p.astype(BF16))
            pmat = jnp.concatenate([jnp.concatenate([ps[0], no_keys], axis=0),
                                    jnp.concatenate([no_keys, ps[1]], axis=0)], axis=1)
            out.append((pmat, jnp.concatenate(dens, axis=1)))
        return out

    def weighted_values(e, pd_list):
        r0 = e * pair
        for (j, _, ha, hb), (pmat, den) in zip(units, pd_list):
            o = _dot(vt_ref[j * B_HEAD_DIM:(j + 1) * B_HEAD_DIM, r0:r0 + nkeys], pmat)
            o = (o * (1.0 / den)).astype(BF16)
            for qc in range(2):
                c0 = r0 + qc * CHUNK
                ot_ref[ha * B_HEAD_DIM:(ha + 1) * B_HEAD_DIM, c0:c0 + CHUNK] = (
                    o[:, qc * LANES:qc * LANES + CHUNK])
                ot_ref[hb * B_HEAD_DIM:(hb + 1) * B_HEAD_DIM, c0:c0 + CHUNK] = (
                    o[:, qc * LANES + CHUNK:(qc + 1) * LANES])

    npairs = tt // pair
    s_next = scores(0)
    for e in range(npairs):
        s_cur = s_next
        if e + 1 < npairs:
            s_next = scores(e + 1)
        run_jobs(JOB_SHARES[3])
        weighted_values(e, softmax(e, s_cur))

    run_jobs(JOB_SHARES[4])
    mix = _dot_tn(ot_ref[...], wo_ref[...])
    run_jobs(1.0)
    out_ref[...] = x + _rms_scale(mix) * npost_ref[...]


def _attn_pair_mixer(x, tables, sinks, npre, kvn, wq, wkv, wo, npost, *, tt, ffn=None):
    batch, seq, _ = x.shape
    assert seq % tt == 0 and tt % (2 * CHUNK) == 0 and WINDOW == 2 * CHUNK
    ext = WINDOW + tt
    fused = ffn is not None
    nt = seq // tt
    kern = functools.partial(_attn_pair_kernel, tt=tt, fused=fused)
    if fused:
        in_spec = pl.BlockSpec((1, tt, D_MODEL), lambda b, t: (b, jnp.minimum(t, nt - 1), 0))
        out_spec = pl.BlockSpec((1, tt, D_MODEL), lambda b, t: (b, jnp.maximum(t - 1, 0), 0))
        tab_spec = pl.BlockSpec((tt, LANES), lambda b, t: (jnp.minimum(t, nt - 1), 0))
    else:
        in_spec = out_spec = pl.BlockSpec((1, tt, D_MODEL), lambda b, t: (b, t, 0))
        tab_spec = pl.BlockSpec((tt, LANES), lambda b, t: (t, 0))
    c_spec = pl.BlockSpec((1, WINDOW, KV_WIDTH), lambda b, t: (b, 0, 0))
    cache_shape = jax.ShapeDtypeStruct((batch, WINDOW, KV_WIDTH), F32)
    in_specs = [in_spec] + [tab_spec] * 4 + [
        pl.BlockSpec(memory_space=pltpu.SMEM),
        _const_spec((1, D_MODEL)),
        _const_spec((1, D_MODEL)),
        _const_spec((D_MODEL, D_MODEL)),
        _const_spec((D_MODEL, 2 * KV_WIDTH)),
        _const_spec((D_MODEL, D_MODEL)),
        _const_spec((1, D_MODEL)),
    ]
    args = [x, *tables, sinks, npre, kvn, wq, wkv, wo, npost]
    scratch = [
        pltpu.VMEM((tt, D_MODEL), BF16),
        pltpu.VMEM((B_KV_HEADS, ext, LANES), BF16),
        pltpu.VMEM((B_KV_HEADS, ext, LANES), BF16),
        pltpu.VMEM((KV_WIDTH, ext), BF16),
        pltpu.VMEM((D_MODEL, tt), BF16),
    ]
    if fused:
        fpre, fpost, win, wout, layer = ffn

        def layer_spec(shape):
            return pl.BlockSpec((None,) + shape, lambda b, t: (layer, 0, 0),
                                pipeline_mode=pl.Buffered(1))

        in_specs += [_const_spec((1, D_MODEL)), _const_spec((1, D_MODEL)),
                     layer_spec((D_MODEL, 2 * D_FF)), layer_spec((D_FF, D_MODEL))]
        args += [fpre, fpost, win, wout]
        scratch.append(pltpu.VMEM((tt, D_MODEL), F32))
    return pl.pallas_call(
        kern,
        grid=(batch, nt + 1 if fused else nt),
        in_specs=in_specs,
        out_specs=[out_spec, c_spec, c_spec],
        out_shape=[jax.ShapeDtypeStruct(x.shape, F32), cache_shape, cache_shape],
        scratch_shapes=scratch,
        compiler_params=pltpu.CompilerParams(
            dimension_semantics=("arbitrary", "arbitrary"), vmem_limit_bytes=VMEM_LIMIT_BYTES),
        name="attn_pair_mixer",
    )(*args)


def _rope_tables(pos, q_scale):
    half = B_HEAD_DIM // 2
    inv = ROPE_THETA ** (-jnp.arange(half, dtype=F32) / half)
    ang = pos.astype(F32)[:, None] * inv[None, :]
    cos = jnp.cos(ang)
    sin = jnp.sin(ang)
    reps = LANES // B_HEAD_DIM
    cos_t = jnp.tile(jnp.concatenate([cos, cos], axis=1), (1, reps))
    sin_t = jnp.tile(jnp.concatenate([-sin, sin], axis=1), (1, reps))
    return cos_t * q_scale, sin_t * q_scale, cos_t, sin_t


def _trunk(x, pos, state, cache_k, cache_v, w, *, nb, tt, hgrn_nb, hgrn_tt, hgrn_chunk, cq):
    batch, seq, _ = x.shape
    row = lambda a: a.reshape(1, D_MODEL)
    x, st = _hgrn_mixer(x, state, row(w["norm_mix_pre"][0]), w["w_a_in"], w["a_lower_bound"],
                        row(w["a_out_norm"]), w["w_a_out"], row(w["norm_mix_post"][0]),
                        nb=hgrn_nb, tt=hgrn_tt, chunk=hgrn_chunk)
    x = _ffn(x.reshape(batch * seq, D_MODEL), row(w["norm_ffn_pre"][0]), row(w["norm_ffn_post"][0]),
             w["w_ffn_in"], w["w_ffn_out"], 0).reshape(batch, seq, D_MODEL)
    attn_w = (w["b_sinks"], row(w["norm_mix_pre"][1]), row(w["kv_norm"]), w["w_b_q"], w["w_kv"],
              w["w_b_out"], row(w["norm_mix_post"][1]))
    ffn1 = (row(w["norm_ffn_pre"][1]), row(w["norm_ffn_post"][1]), w["w_ffn_in"], w["w_ffn_out"], 1)
    if cache_k is None:
        assert nb == 1 and cq == CHUNK
        x, kc, vc = _attn_pair_mixer(x, _rope_tables(pos, SOFTMAX_SCALE * LOG2_E), *attn_w, tt=tt,
                                     ffn=ffn1)
    else:
        assert tt == seq and cq == seq
        x, kc, vc = _attn_cached_mixer(x, cache_k, cache_v, _rope_tables(pos, SOFTMAX_SCALE),
                                       *attn_w, nb=nb)
        x = _ffn(x.reshape(batch * seq, D_MODEL), *ffn1).reshape(batch, seq, D_MODEL)
    return x, st, kc, vc


def kernel(x_prompt, x_sample, state_hgrn, cache_k, cache_v, norm_mix_pre, norm_mix_post, norm_ffn_pre, norm_ffn_post, w_ffn_in, w_ffn_out, w_a_in, a_lower_bound, a_out_norm, w_a_out, kv_norm, w_kv, w_b_q, b_sinks, w_b_out):
    w = dict(
        norm_mix_pre=norm_mix_pre, norm_mix_post=norm_mix_post,
        norm_ffn_pre=norm_ffn_pre, norm_ffn_post=norm_ffn_post,
        w_ffn_in=w_ffn_in.astype(BF16), w_ffn_out=w_ffn_out.astype(BF16),
        w_a_in=w_a_in[0].astype(BF16), a_lower_bound=a_lower_bound,
        a_out_norm=a_out_norm[0], w_a_out=w_a_out[0].astype(BF16),
        kv_norm=kv_norm, w_kv=w_kv.astype(BF16), w_b_q=w_b_q[0].astype(BF16),
        b_sinks=b_sinks[0], w_b_out=w_b_out[0].astype(BF16),
    )
    bp, tp, _ = x_prompt.shape
    bs, ts, _ = x_sample.shape

    zero_state = jnp.zeros((bp, A_HEADS, A_DK, A_DV), F32)
    y_p, st_p, kc_p, vc_p = _trunk(
        x_prompt, jnp.arange(tp), zero_state, None, None, w,
        nb=1, tt=ROW_TILE, hgrn_nb=HGRN_STREAMS, hgrn_tt=ROW_TILE // HGRN_STREAMS,
        hgrn_chunk=HGRN_CHUNK, cq=CHUNK)

    nb_s = ROW_TILE // (2 * ts)
    y_s, st_s, kc_s, vc_s = _trunk(
        x_sample, PAST_LEN + jnp.arange(ts), state_hgrn[0],
        cache_k.reshape(bs, WINDOW, KV_WIDTH), cache_v.reshape(bs, WINDOW, KV_WIDTH), w,
        nb=nb_s, tt=ts, hgrn_nb=nb_s, hgrn_tt=ts, hgrn_chunk=ts, cq=ts)

    cache4 = lambda a: a.reshape(a.shape[0], WINDOW, B_KV_HEADS, B_HEAD_DIM)
    return (y_p, y_s, st_p[None], st_s[None],
            cache4(kc_p), cache4(vc_p), cache4(kc_s), cache4(vc_s))
```

```python
import functools

import jax
import jax.numpy as jnp
from jax import lax
from jax.experimental import pallas as pl
from jax.experimental.pallas import tpu as pltpu

F32 = jnp.float32
BF16 = jnp.bfloat16

D_MODEL = 1024
A_HEADS = 8
A_DK = 128
A_DV = 128
B_HEAD_DIM = 64
B_Q_HEADS = 16
B_KV_HEADS = 4
B_GROUP = B_Q_HEADS // B_KV_HEADS
KV_WIDTH = B_KV_HEADS * B_HEAD_DIM
WINDOW = 128
CHUNK = 64
PAST_LEN = 2048
D_FF = 2816
ROPE_THETA = 10000.0
NORM_EPS = 1e-6
SOFTMAX_SCALE = B_HEAD_DIM ** -0.5
LOG2_E = 1.4426950408889634

LANES = 128
SUBLANES = 8
VMEM_LIMIT_BYTES = 52 * 1024 * 1024

ROW_TILE = 512
FFN_ROW_TILE = 1024
HGRN_CHUNK = 128
FFN_CHUNK = 256
MAX_FACTOR_EXPONENT = 60.0
EXACT_BLOCK = 16
HGRN_STREAMS = 1
HEAD_GROUPS = 2
JOB_SHARES = (0.18, 0.09, 0.09, 0.09, 0.09)


def _rms_scale(x):
    ms = jnp.mean(x * x, axis=-1, keepdims=True)
    return x * lax.rsqrt(ms + NORM_EPS)


def _dot(a, b):
    return jnp.dot(a, b, preferred_element_type=F32)


def _dot_nt(a, b):
    return lax.dot_general(a, b, (((1,), (1,)), ((), ())), preferred_element_type=F32)


def _dot_tn(a, b):
    return lax.dot_general(a, b, (((0,), (0,)), ((), ())), preferred_element_type=F32)


def _const_spec(shape):
    nd = len(shape)
    return pl.BlockSpec(shape, lambda *_: (0,) * nd, pipeline_mode=pl.Buffered(1))


def _ffn_kernel(x_ref, npre_ref, npost_ref, win_ref, wout_ref, o_ref, *, sub_rows):
    starts = range(0, x_ref.shape[0], sub_rows)
    normed = [(_rms_scale(x_ref[r0:r0 + sub_rows, :]) * npre_ref[...]).astype(BF16) for r0 in starts]
    for r0, h in zip(starts, normed):
        x = x_ref[r0:r0 + sub_rows, :]
        acc = jnp.zeros(x.shape, F32)
        for c0 in range(0, D_FF, FFN_CHUNK):
            c1 = min(c0 + FFN_CHUNK, D_FF)
            a = _dot(h, win_ref[:, c0:c1])
            b = _dot(h, win_ref[:, D_FF + c0:D_FF + c1])
            g = (a * jax.nn.sigmoid(a) * b).astype(BF16)
            acc = acc + _dot(g, wout_ref[c0:c1, :])
        o_ref[r0:r0 + sub_rows, :] = x + _rms_scale(acc) * npost_ref[...]


def _ffn(x2d, npre, npost, win, wout, layer):
    rows = x2d.shape[0]
    tile = min(FFN_ROW_TILE, rows)
    sub_rows = min(ROW_TILE, tile)
    assert rows % tile == 0 and tile % sub_rows == 0

    def layer_spec(shape):
        return pl.BlockSpec((None,) + shape, lambda i: (layer, 0, 0), pipeline_mode=pl.Buffered(1))

    return pl.pallas_call(
        functools.partial(_ffn_kernel, sub_rows=sub_rows),
        grid=(rows // tile,),
        in_specs=[
            pl.BlockSpec((tile, D_MODEL), lambda i: (i, 0)),
            _const_spec((1, D_MODEL)),
            _const_spec((1, D_MODEL)),
            layer_spec((D_MODEL, 2 * D_FF)),
            layer_spec((D_FF, D_MODEL)),
        ],
        out_specs=pl.BlockSpec((tile, D_MODEL), lambda i: (i, 0)),
        out_shape=jax.ShapeDtypeStruct((rows, D_MODEL), F32),
        compiler_params=pltpu.CompilerParams(
            dimension_semantics=("parallel",), vmem_limit_bytes=VMEM_LIMIT_BYTES),
        name="ffn",
    )(x2d, npre, npost, win, wout)


def _cumsum_rows(x):
    c, w = x.shape
    groups = c // SUBLANES
    y = x.reshape(groups, SUBLANES, w)
    sub = lax.broadcasted_iota(jnp.int32, y.shape, 1)
    shift = 1
    while shift < SUBLANES:
        y = y + jnp.where(sub >= shift, pltpu.roll(y, shift, axis=1), 0.0)
        shift *= 2
    tot = jnp.broadcast_to(y[:, SUBLANES - 1:SUBLANES, :], y.shape)
    inc = tot
    shift = 1
    while shift < groups:
        inc = inc + jnp.concatenate(
            [jnp.zeros((shift, SUBLANES, w), F32), inc[:groups - shift]], axis=0)
        shift *= 2
    return (y + (inc - tot)).reshape(c, w)


def _hgrn_kernel(*refs, nb, tt, chunk, cast_blocks):
    ncast = len(cast_blocks)
    x_ref, st_in_ref, npre_ref, win_ref, alb_ref, gnorm_ref, wout_ref, npost_ref = refs[:8]
    cast_src = refs[8:8 + ncast]
    y_ref, st_ref = refs[8 + ncast:10 + ncast]
    cast_dst = refs[10 + ncast:10 + 2 * ncast]
    qs_ref, lf_ref, inp_ref, v_ref, sg_ref, on_ref = refs[10 + 2 * ncast:]
    t = pl.program_id(1)
    rows = nb * tt

    step = pl.program_id(0) * pl.num_programs(1) + t
    for src_ref, dst_ref, nblocks in zip(cast_src, cast_dst, cast_blocks):
        @pl.when(step < nblocks)
        def _(src_ref=src_ref, dst_ref=dst_ref):
            dst_ref[...] = src_ref[...].astype(BF16)

    @pl.when(t == 0)
    def _():
        for n in range(nb):
            for h in range(A_HEADS):
                st_ref[n, h] = st_in_ref[n, h].T

    x = x_ref[...].reshape(rows, D_MODEL)
    hn = (_rms_scale(x) * npre_ref[...]).astype(BF16)

    alb = alb_ref[...]
    e = jnp.exp(alb - jnp.max(alb, axis=0, keepdims=True))
    lb = e[0:1] / jnp.sum(e, axis=0, keepdims=True)

    f = _dot(hn, win_ref[:, D_MODEL:2 * D_MODEL])
    forget = lb + (1.0 - lb) * jax.nn.sigmoid(f)
    lf_ref[...] = jnp.log(forget)
    inp_ref[...] = 1.0 - forget

    def project_query(c0, width):
        q = _dot(hn, win_ref[:, c0:c0 + width])
        qs_ref[:, c0:c0 + width] = q * jax.nn.sigmoid(q)

    def project_value(c0, width):
        v_ref[:, c0:c0 + width] = _dot(
            hn, win_ref[:, 2 * D_MODEL + c0:2 * D_MODEL + c0 + width]).astype(BF16)

    def project_gate(c0, width):
        g = _dot(hn, win_ref[:, 3 * D_MODEL + c0:3 * D_MODEL + c0 + width])
        sg_ref[:, c0:c0 + width] = g * jax.nn.sigmoid(g)

    def write_rows(mix):
        y = x + _rms_scale(mix) * npost_ref[...]
        y_ref[...] = y.reshape(nb, tt, D_MODEL)

    half = chunk // 2
    ri = lax.broadcasted_iota(jnp.int32, (chunk, chunk), 0)
    ci = lax.broadcasted_iota(jnp.int32, (chunk, chunk), 1)
    causal = ri >= ci

    def prepare(blk, h):
        rs = slice(blk * chunk, (blk + 1) * chunk)
        sl = slice(h * A_DK, (h + 1) * A_DK)
        b = _cumsum_rows(lf_ref[rs, sl])
        bmid = b[half - 1:half]
        blast = b[chunk - 1:chunk]
        qsc = qs_ref[rs, sl] * jnp.exp(b - bmid)
        inp = inp_ref[rs, sl] * jnp.exp(bmid - b)
        return dict(
            rs=rs, sl=sl, qt=qsc.astype(BF16), kt=inp.astype(BF16),
            qi=(qsc * jnp.exp(bmid)).astype(BF16),
            kh=(inp * jnp.exp(blast - bmid)).astype(BF16),
            vh=v_ref[rs, sl], decay=jnp.exp(blast))

    def emit_output(u, sc, st):
        p = jnp.where(causal, sc, 0.0).astype(BF16)
        o = _dot(p, u["vh"]) + _dot_nt(u["qi"], st.astype(BF16))
        on = _rms_scale(o) * gnorm_ref[:, u["sl"]] * sg_ref[u["rs"], u["sl"]]
        on_ref[u["rs"], u["sl"]] = on.astype(BF16)

    def recurrence(heads, jobs):
        nstages = tt // chunk
        slots = 2 * nstages
        jobs = list(jobs)

        def run_jobs(slot):
            for _ in range(-(-len(jobs) // (slots - slot))):
                jobs.pop(0)()

        states = {}
        for stage in range(nstages):
            blocks = [n * nstages + stage for n in range(nb)]
            units = [(blk, h, prepare(blk, h)) for blk in blocks for h in heads]
            scores = [_dot_nt(u["qt"], u["kt"]) for _, _, u in units]
            run_jobs(2 * stage)
            updated = {}
            for blk, h, u in units:
                key = ((blk * chunk) // tt, h)
                if key not in states:
                    states[key] = st_ref[key[0], h]
                updated[key] = states[key] * u["decay"] + _dot_tn(u["vh"], u["kh"])
            run_jobs(2 * stage + 1)
            for (blk, h, u), sc in zip(units, scores):
                emit_output(u, sc, states[((blk * chunk) // tt, h)])
            states.update(updated)
        for (n, h), st in states.items():
            st_ref[n, h] = st

    def factorised_path():
        heads_per_group = A_HEADS // HEAD_GROUPS
        piece = 2 * LANES
        partial_mix = [[] for _ in range(HEAD_GROUPS)]

        def project_out(g, c0, width):
            partial_mix[g].append(
                _dot(on_ref[:, g * gw:(g + 1) * gw], wout_ref[g * gw:(g + 1) * gw, c0:c0 + width]))

        for g in range(HEAD_GROUPS):
            jobs = []
            if g + 1 < HEAD_GROUPS:
                jobs += [functools.partial(proj, (g + 1) * gw + c0, piece)
                         for proj in (project_query, project_value, project_gate)
                         for c0 in range(0, gw, piece)]
            if g > 0:
                jobs += [functools.partial(project_out, g - 1, c0, piece)
                         for c0 in range(0, D_MODEL, piece)]
            recurrence(range(g * heads_per_group, (g + 1) * heads_per_group), jobs)
        project_out(HEAD_GROUPS - 1, 0, D_MODEL)
        write_rows(functools.reduce(
            jnp.add, [jnp.concatenate(parts, axis=1) for parts in partial_mix]))

    def exact_step(idx, carry):
        n = (idx * EXACT_BLOCK) // tt
        rs = pl.ds(pl.multiple_of(idx * EXACT_BLOCK, EXACT_BLOCK), EXACT_BLOCK)
        row = lax.broadcasted_iota(jnp.int32, (EXACT_BLOCK, A_DK), 0)
        for h in range(A_HEADS):
            sl = slice(h * A_DK, (h + 1) * A_DK)
            b = _cumsum_rows(lf_ref[rs, sl])
            blast = b[EXACT_BLOCK - 1:EXACT_BLOCK]
            qs = qs_ref[rs, sl]
            inp = inp_ref[rs, sl]
            vh = v_ref[rs, sl]
            vf = vh.astype(F32)
            st = st_ref[n, h]
            o = _dot_nt((qs * jnp.exp(b)).astype(BF16), st.astype(BF16))
            for s in range(EXACT_BLOCK):
                decay = jnp.exp(jnp.where(row >= s, b - b[s:s + 1], -jnp.inf))
                score = jnp.sum(qs * decay * inp[s:s + 1], axis=-1, keepdims=True)
                o = o + score * vf[s:s + 1]
            kh = (inp * jnp.exp(blast - b)).astype(BF16)
            st_ref[n, h] = st * jnp.exp(blast) + _dot_tn(vh, kh)
            on = _rms_scale(o) * gnorm_ref[:, sl] * sg_ref[rs, sl]
            on_ref[rs, sl] = on.astype(BF16)
        return carry

    gw = D_MODEL // HEAD_GROUPS
    project_query(0, gw)
    project_value(0, gw)
    project_gate(0, gw)

    max_step = -jnp.min(lf_ref[...])
    bounded = max_step * half <= MAX_FACTOR_EXPONENT

    pl.when(bounded)(factorised_path)

    @pl.when(jnp.logical_not(bounded))
    def _():
        project_query(gw, D_MODEL - gw)
        project_value(gw, D_MODEL - gw)
        project_gate(gw, D_MODEL - gw)
        lax.fori_loop(0, rows // EXACT_BLOCK, exact_step, 0)
        write_rows(_dot(on_ref[...], wout_ref[...]))

    @pl.when(t == pl.num_programs(1) - 1)
    def _():
        for n in range(nb):
            for h in range(A_HEADS):
                st_ref[n, h] = st_ref[n, h].T


def _hgrn_mixer(x, state, npre, win, alb, gnorm, wout, npost, *, nb, tt, chunk, casts=()):
    batch, seq, _ = x.shape
    assert batch % nb == 0 and seq % tt == 0 and tt % chunk == 0 and chunk % (2 * SUBLANES) == 0
    rows = nb * tt
    nt = seq // tt
    nsteps = (batch // nb) * nt
    cast_specs, cast_blocks = [], []
    for a in casts:
        block_rows = next(r for r in range(2 * SUBLANES, a.shape[0] + 1, 2 * SUBLANES)
                          if a.shape[0] % r == 0 and a.shape[0] // r <= nsteps)
        nblocks = a.shape[0] // block_rows
        cast_blocks.append(nblocks)
        cast_specs.append(pl.BlockSpec(
            (block_rows, a.shape[1]),
            lambda b, t, nblocks=nblocks: (jnp.minimum(b * nt + t, nblocks - 1), 0)))
    kern = functools.partial(_hgrn_kernel, nb=nb, tt=tt, chunk=chunk, cast_blocks=tuple(cast_blocks))
    st_spec = pl.BlockSpec((nb, A_HEADS, A_DK, A_DV), lambda b, t: (b, 0, 0, 0))
    return pl.pallas_call(
        kern,
        grid=(batch // nb, nt),
        in_specs=[
            pl.BlockSpec((nb, tt, D_MODEL), lambda b, t: (b, t, 0)),
            st_spec,
            _const_spec((1, D_MODEL)),
            _const_spec((D_MODEL, 4 * D_MODEL)),
            _const_spec(alb.shape),
            _const_spec((1, D_MODEL)),
            _const_spec((D_MODEL, D_MODEL)),
            _const_spec((1, D_MODEL)),
        ] + cast_specs,
        out_specs=[
            pl.BlockSpec((nb, tt, D_MODEL), lambda b, t: (b, t, 0)),
            st_spec,
        ] + cast_specs,
        out_shape=[
            jax.ShapeDtypeStruct(x.shape, F32),
            jax.ShapeDtypeStruct(state.shape, F32),
        ] + [jax.ShapeDtypeStruct(a.shape, BF16) for a in casts],
        scratch_shapes=[
            pltpu.VMEM((rows, D_MODEL), F32),
            pltpu.VMEM((rows, D_MODEL), F32),
            pltpu.VMEM((rows, D_MODEL), F32),
            pltpu.VMEM((rows, D_MODEL), BF16),
            pltpu.VMEM((rows, D_MODEL), F32),
            pltpu.VMEM((rows, D_MODEL), BF16),
        ],
        compiler_params=pltpu.CompilerParams(
            dimension_semantics=("arbitrary", "arbitrary"), vmem_limit_bytes=VMEM_LIMIT_BYTES),
        name="hgrn_mixer",
    )(x, state, npre, win, alb, gnorm, wout, npost, *casts)


def _rope_cols(x, cos_t, sin_t):
    lane = lax.broadcasted_iota(jnp.int32, (x.shape[0], LANES), 1)
    first_half = (lane % B_HEAD_DIM) < (B_HEAD_DIM // 2)
    cols = []
    for c0 in range(0, x.shape[1], LANES):
        xc = x[:, c0:c0 + LANES]
        partner = jnp.where(first_half,
                            pltpu.roll(xc, LANES - B_HEAD_DIM // 2, axis=1),
                            pltpu.roll(xc, B_HEAD_DIM // 2, axis=1))
        cols.append(xc * cos_t + partner * sin_t)
    return jnp.concatenate(cols, axis=1)


def _attn_cached_kernel(x_ref, kc_in_ref, vc_in_ref, cosq_ref, sinq_ref, cosk_ref, sinkt_ref,
                        sinks_ref, npre_ref, kvn_ref, wq_ref, wkv_ref, wo_ref, npost_ref,
                        y_ref, kc_ref, vc_ref, q_ref, kl_ref, kh_ref, vl_ref, ot_ref, *, nb, tt):
    rows = nb * tt
    ext = WINDOW + tt

    x = x_ref[...].reshape(rows, D_MODEL)
    xs = _rms_scale(x)
    hq = (xs * npre_ref[...]).astype(BF16)
    hk = (xs * kvn_ref[...]).astype(BF16)

    def per_stream(tab_ref):
        tab = tab_ref[...]
        return tab if nb == 1 else jnp.concatenate([tab] * nb, axis=0)

    q = _dot(hq, wq_ref[...])
    q_ref[...] = _rope_cols(q, per_stream(cosq_ref), per_stream(sinq_ref)).astype(BF16)
    kv = _dot(hk, wkv_ref[...])
    k_new = _rope_cols(kv[:, :KV_WIDTH], per_stream(cosk_ref), per_stream(sinkt_ref))
    v_new = kv[:, KV_WIDTH:]

    low = lax.broadcasted_iota(jnp.int32, (ext, LANES), 1) < B_HEAD_DIM
    for n in range(nb):
        k_ext = jnp.concatenate([kc_in_ref[n], k_new[n * tt:(n + 1) * tt]], axis=0)
        v_ext = jnp.concatenate([vc_in_ref[n], v_new[n * tt:(n + 1) * tt]], axis=0)
        kc_ref[n] = k_ext[ext - WINDOW:]
        vc_ref[n] = v_ext[ext - WINDOW:]
        for m in range(KV_WIDTH // LANES):
            ka = k_ext[:, m * LANES:(m + 1) * LANES]
            kr = pltpu.roll(ka, B_HEAD_DIM, axis=1)
            va = v_ext[:, m * LANES:(m + 1) * LANES]
            vr = pltpu.roll(va, B_HEAD_DIM, axis=1)
            kl_ref[2 * m, n] = jnp.where(low, ka, 0.0).astype(BF16)
            kh_ref[2 * m, n] = jnp.where(low, 0.0, kr).astype(BF16)
            kl_ref[2 * m + 1, n] = jnp.where(low, kr, 0.0).astype(BF16)
            kh_ref[2 * m + 1, n] = jnp.where(low, 0.0, ka).astype(BF16)
            vl_ref[2 * m, n] = jnp.where(low, va, 0.0).astype(BF16)
            vl_ref[2 * m + 1, n] = jnp.where(low, vr, 0.0).astype(BF16)

    first_head = lax.broadcasted_iota(jnp.int32, (1, 2 * tt), 1) < tt
    pairs = [(j, kx_ref, ha, hb) for j in range(B_KV_HEADS)
             for kx_ref, ha, hb in ((kl_ref, 4 * j, 4 * j + 2), (kh_ref, 4 * j + 1, 4 * j + 3))]

    def scores(n):
        rs = slice(n * tt, (n + 1) * tt)
        out = []
        for j, kx_ref, _, _ in pairs:
            qq = jnp.concatenate([q_ref[rs, 2 * j * LANES:(2 * j + 1) * LANES],
                                  q_ref[rs, (2 * j + 1) * LANES:(2 * j + 2) * LANES]], axis=0)
            out.append(_dot_nt(kx_ref[j, n], qq))
        return out

    def softmax(s_list):
        out = []
        for (_, _, ha, hb), s in zip(pairs, s_list):
            sink = jnp.where(first_head, sinks_ref[ha], sinks_ref[hb])
            mx = jnp.maximum(jnp.max(s, axis=0, keepdims=True), sink)
            p = jnp.exp(s - mx)
            den = jnp.sum(p, axis=0, keepdims=True) + jnp.exp(sink - mx)
            out.append((p.astype(BF16), den))
        return out

    def weighted_values(n, pd_list):
        cs = slice(n * tt, (n + 1) * tt)
        for (j, _, ha, hb), (p, den) in zip(pairs, pd_list):
            o = _dot_tn(vl_ref[j, n], p)[:B_HEAD_DIM]
            o = (o * (1.0 / den)).astype(BF16)
            ot_ref[ha * B_HEAD_DIM:(ha + 1) * B_HEAD_DIM, cs] = o[:, :tt]
            ot_ref[hb * B_HEAD_DIM:(hb + 1) * B_HEAD_DIM, cs] = o[:, tt:]

    s_next = scores(0)
    for n in range(nb):
        s_cur = s_next
        if n + 1 < nb:
            s_next = scores(n + 1)
        weighted_values(n, softmax(s_cur))

    mix = _dot_tn(ot_ref[...], wo_ref[...])
    y = x + _rms_scale(mix) * npost_ref[...]
    y_ref[...] = y.reshape(nb, tt, D_MODEL)


def _attn_cached_mixer(x, cache_k, cache_v, tables, sinks, npre, kvn, wq, wkv, wo, npost, *, nb):
    batch, tt, _ = x.shape
    assert batch % nb == 0 and tt % (2 * SUBLANES) == 0
    rows = nb * tt
    ext = WINDOW + tt
    kern = functools.partial(_attn_cached_kernel, nb=nb, tt=tt)
    x_spec = pl.BlockSpec((nb, tt, D_MODEL), lambda b: (b, 0, 0))
    c_spec = pl.BlockSpec((nb, WINDOW, KV_WIDTH), lambda b: (b, 0, 0))
    tab_spec = _const_spec((tt, LANES))
    cache_shape = jax.ShapeDtypeStruct((batch, WINDOW, KV_WIDTH), F32)
    return pl.pallas_call(
        kern,
        grid=(batch // nb,),
        in_specs=[x_spec, c_spec, c_spec] + [tab_spec] * 4 + [
            pl.BlockSpec(memory_space=pltpu.SMEM),
            _const_spec((1, D_MODEL)),
            _const_spec((1, D_MODEL)),
            _const_spec((D_MODEL, D_MODEL)),
            _const_spec((D_MODEL, 2 * KV_WIDTH)),
            _const_spec((D_MODEL, D_MODEL)),
            _const_spec((1, D_MODEL)),
        ],
        out_specs=[x_spec, c_spec, c_spec],
        out_shape=[jax.ShapeDtypeStruct(x.shape, F32), cache_shape, cache_shape],
        scratch_shapes=[
            pltpu.VMEM((rows, D_MODEL), BF16),
            pltpu.VMEM((B_KV_HEADS, nb, ext, LANES), BF16),
            pltpu.VMEM((B_KV_HEADS, nb, ext, LANES), BF16),
            pltpu.VMEM((B_KV_HEADS, nb, ext, LANES), BF16),
            pltpu.VMEM((D_MODEL, rows), BF16),
        ],
        compiler_params=pltpu.CompilerParams(
            dimension_semantics=("parallel",), vmem_limit_bytes=VMEM_LIMIT_BYTES),
        name="attn_cached_mixer",
    )(x, cache_k, cache_v, *tables, sinks, npre, kvn, wq, wkv, wo, npost)


def _ffn_steps(src_ref, dst_ref, npre_ref, npost_ref, win_ref, wout_ref):
    state = {}

    def chunk(c0):
        if not state:
            state["h"] = (_rms_scale(src_ref[...]) * npre_ref[...]).astype(BF16)
            state["acc"] = jnp.zeros(src_ref.shape, F32)
        h = state["h"]
        c1 = min(c0 + FFN_CHUNK, D_FF)
        a = _dot(h, win_ref[:, c0:c1])
        b = _dot(h, win_ref[:, D_FF + c0:D_FF + c1])
        g = (a * jax.nn.sigmoid(a) * b).astype(BF16)
        state["acc"] = state["acc"] + _dot(g, wout_ref[c0:c1, :])
        if c1 == D_FF:
            dst_ref[...] = (src_ref[...] + _rms_scale(state["acc"]) * npost_ref[...]).reshape(dst_ref.shape)

    return [functools.partial(chunk, c0) for c0 in range(0, D_FF, FFN_CHUNK)]


def _attn_pair_kernel(*refs, tt, fused):
    if fused:
        (x_ref, cosq_ref, sinq_ref, cosk_ref, sinkt_ref, sinks_ref,
         npre_ref, kvn_ref, wq_ref, wkv_ref, wo_ref, npost_ref,
         fpre_ref, fpost_ref, win_ref, wout_ref,
         y_ref, kc_ref, vc_ref, q_ref, kl_ref, kh_ref, vt_ref, ot_ref, mid_ref) = refs
    else:
        (x_ref, cosq_ref, sinq_ref, cosk_ref, sinkt_ref, sinks_ref,
         npre_ref, kvn_ref, wq_ref, wkv_ref, wo_ref, npost_ref,
         y_ref, kc_ref, vc_ref, q_ref, kl_ref, kh_ref, vt_ref, ot_ref) = refs
    t = pl.program_id(1)
    if not fused:
        _attention_tile(t, tt, x_ref, cosq_ref, sinq_ref, cosk_ref, sinkt_ref, sinks_ref,
                        npre_ref, kvn_ref, wq_ref, wkv_ref, wo_ref, npost_ref, kc_ref, vc_ref,
                        q_ref, kl_ref, kh_ref, vt_ref, ot_ref, [], y_ref.at[0])
        return

    last = pl.num_programs(1) - 1

    @pl.when((pl.program_id(0) == 0) & (t == 0))
    def _():
        mid_ref[...] = jnp.zeros(mid_ref.shape, F32)

    @pl.when(t < last)
    def _():
        jobs = _ffn_steps(mid_ref, y_ref, fpre_ref, fpost_ref, win_ref, wout_ref)
        _attention_tile(t, tt, x_ref, cosq_ref, sinq_ref, cosk_ref, sinkt_ref, sinks_ref,
                        npre_ref, kvn_ref, wq_ref, wkv_ref, wo_ref, npost_ref, kc_ref, vc_ref,
                        q_ref, kl_ref, kh_ref, vt_ref, ot_ref, jobs, mid_ref)

    @pl.when(t == last)
    def _():
        for job in _ffn_steps(mid_ref, y_ref, fpre_ref, fpost_ref, win_ref, wout_ref):
            job()


def _attention_tile(t, tt, x_ref, cosq_ref, sinq_ref, cosk_ref, sinkt_ref, sinks_ref,
                    npre_ref, kvn_ref, wq_ref, wkv_ref, wo_ref, npost_ref, kc_ref, vc_ref,
                    q_ref, kl_ref, kh_ref, vt_ref, ot_ref, jobs, out_ref):
    ext = WINDOW + tt
    pair = 2 * CHUNK
    nkeys = WINDOW + pair
    nvis = WINDOW + CHUNK
    jobs = list(jobs)
    njobs = len(jobs)

    def run_jobs(share):
        for _ in range(min(round(share * njobs), len(jobs))):
            jobs.pop(0)()

    @pl.when(t == 0)
    def _():
        kc_ref[...] = jnp.zeros(kc_ref.shape, F32)
        vc_ref[...] = jnp.zeros(vc_ref.shape, F32)

    x = x_ref[0]
    xs = _rms_scale(x)
    hq = (xs * npre_ref[...]).astype(BF16)
    hk = (xs * kvn_ref[...]).astype(BF16)

    run_jobs(JOB_SHARES[0])
    q = _dot(hq, wq_ref[...])
    q_ref[...] = _rope_cols(q, cosq_ref[...], sinq_ref[...]).astype(BF16)
    run_jobs(JOB_SHARES[1])
    kv = _dot(hk, wkv_ref[...])
    run_jobs(JOB_SHARES[2])
    k_ext = jnp.concatenate(
        [kc_ref[0], _rope_cols(kv[:, :KV_WIDTH], cosk_ref[...], sinkt_ref[...])], axis=0)
    v_ext = jnp.concatenate([vc_ref[0], kv[:, KV_WIDTH:]], axis=0)
    kc_ref[0] = k_ext[ext - WINDOW:]
    vc_ref[0] = v_ext[ext - WINDOW:]
    vt_ref[...] = v_ext.T.astype(BF16)

    low = lax.broadcasted_iota(jnp.int32, (ext, LANES), 1) < B_HEAD_DIM
    for m in range(KV_WIDTH // LANES):
        ka = k_ext[:, m * LANES:(m + 1) * LANES]
        kr = pltpu.roll(ka, B_HEAD_DIM, axis=1)
        kl_ref[2 * m] = jnp.where(low, ka, 0.0).astype(BF16)
        kh_ref[2 * m] = jnp.where(low, 0.0, kr).astype(BF16)
        kl_ref[2 * m + 1] = jnp.where(low, kr, 0.0).astype(BF16)
        kh_ref[2 * m + 1] = jnp.where(low, 0.0, ka).astype(BF16)

    vis_row = lax.broadcasted_iota(jnp.int32, (nvis, LANES), 0)
    first_bias = [jnp.where(vis_row + (t * tt - WINDOW + qc * CHUNK) >= 0, 0.0, -jnp.inf)
                  for qc in range(2)]
    first_head = lax.broadcasted_iota(jnp.int32, (1, LANES), 1) < CHUNK
    no_keys = jnp.zeros((CHUNK, LANES), BF16)

    units = [(j, kx_ref, ha, hb) for j in range(B_KV_HEADS)
             for kx_ref, ha, hb in ((kl_ref, 4 * j, 4 * j + 2), (kh_ref, 4 * j + 1, 4 * j + 3))]

    def scores(e):
        r0 = e * pair
        out = []
        for j, kx_ref, _, _ in units:
            ca = slice(2 * j * LANES, (2 * j + 1) * LANES)
            cb = slice((2 * j + 1) * LANES, (2 * j + 2) * LANES)
            qq = jnp.concatenate([q_ref[r0:r0 + CHUNK, ca], q_ref[r0:r0 + CHUNK, cb],
                                  q_ref[r0 + CHUNK:r0 + pair, ca], q_ref[r0 + CHUNK:r0 + pair, cb]], axis=0)
            out.append(_dot_nt(kx_ref[j, r0:r0 + nkeys, :], qq))
        return out

    def softmax(e, s_list):
        out = []
        for (_, _, ha, hb), s in zip(units, s_list):
            sink = jnp.where(first_head, sinks_ref[ha], sinks_ref[hb]) * LOG2_E
            ps, dens = [], []
            for qc in range(2):
                sq = s[qc * CHUNK:qc * CHUNK + nvis, qc * LANES:(qc + 1) * LANES]
                if e == 0:
                    sq = sq + first_bias[qc]
                mx = jnp.maximum(jnp.max(sq, axis=0, keepdims=True), sink)
                p = jnp.exp2(sq - mx)
                dens.append(jnp.sum(p, axis=0, keepdims=True) + jnp.exp2(sink - mx))
                ps.append(p.astype(BF16))
            pmat = jnp.concatenate([jnp.concatenate([ps[0], no_keys], axis=0),
                                    jnp.concatenate([no_keys, ps[1]], axis=0)], axis=1)
            out.append((pmat, jnp.concatenate(dens, axis=1)))
        return out

    def weighted_values(e, pd_list):
        r0 = e * pair
        for (j, _, ha, hb), (pmat, den) in zip(units, pd_list):
            o = _dot(vt_ref[j * B_HEAD_DIM:(j + 1) * B_HEAD_DIM, r0:r0 + nkeys], pmat)
            o = (o * (1.0 / den)).astype(BF16)
            for qc in range(2):
                c0 = r0 + qc * CHUNK
                ot_ref[ha * B_HEAD_DIM:(ha + 1) * B_HEAD_DIM, c0:c0 + CHUNK] = (
                    o[:, qc * LANES:qc * LANES + CHUNK])
                ot_ref[hb * B_HEAD_DIM:(hb + 1) * B_HEAD_DIM, c0:c0 + CHUNK] = (
                    o[:, qc * LANES + CHUNK:(qc + 1) * LANES])

    npairs = tt // pair
    s_next = scores(0)
    for e in range(npairs):
        s_cur = s_next
        if e + 1 < npairs:
            s_next = scores(e + 1)
        run_jobs(JOB_SHARES[3])
        weighted_values(e, softmax(e, s_cur))

    run_jobs(JOB_SHARES[4])
    mix = _dot_tn(ot_ref[...], wo_ref[...])
    run_jobs(1.0)
    out_ref[...] = x + _rms_scale(mix) * npost_ref[...]


def _attn_pair_mixer(x, tables, sinks, npre, kvn, wq, wkv, wo, npost, *, tt, ffn=None):
    batch, seq, _ = x.shape
    assert seq % tt == 0 and tt % (2 * CHUNK) == 0 and WINDOW == 2 * CHUNK
    ext = WINDOW + tt
    fused = ffn is not None
    nt = seq // tt
    kern = functools.partial(_attn_pair_kernel, tt=tt, fused=fused)
    if fused:
        in_spec = pl.BlockSpec((1, tt, D_MODEL), lambda b, t: (b, jnp.minimum(t, nt - 1), 0))
        out_spec = pl.BlockSpec((1, tt, D_MODEL), lambda b, t: (b, jnp.maximum(t - 1, 0), 0))
        tab_spec = pl.BlockSpec((tt, LANES), lambda b, t: (jnp.minimum(t, nt - 1), 0))
    else:
        in_spec = out_spec = pl.BlockSpec((1, tt, D_MODEL), lambda b, t: (b, t, 0))
        tab_spec = pl.BlockSpec((tt, LANES), lambda b, t: (t, 0))
    c_spec = pl.BlockSpec((1, WINDOW, KV_WIDTH), lambda b, t: (b, 0, 0))
    cache_shape = jax.ShapeDtypeStruct((batch, WINDOW, KV_WIDTH), F32)
    in_specs = [in_spec] + [tab_spec] * 4 + [
        pl.BlockSpec(memory_space=pltpu.SMEM),
        _const_spec((1, D_MODEL)),
        _const_spec((1, D_MODEL)),
        _const_spec((D_MODEL, D_MODEL)),
        _const_spec((D_MODEL, 2 * KV_WIDTH)),
        _const_spec((D_MODEL, D_MODEL)),
        _const_spec((1, D_MODEL)),
    ]
    args = [x, *tables, sinks, npre, kvn, wq, wkv, wo, npost]
    scratch = [
        pltpu.VMEM((tt, D_MODEL), BF16),
        pltpu.VMEM((B_KV_HEADS, ext, LANES), BF16),
        pltpu.VMEM((B_KV_HEADS, ext, LANES), BF16),
        pltpu.VMEM((KV_WIDTH, ext), BF16),
        pltpu.VMEM((D_MODEL, tt), BF16),
    ]
    if fused:
        fpre, fpost, win, wout, layer = ffn

        def layer_spec(shape):
            return pl.BlockSpec((None,) + shape, lambda b, t: (layer, 0, 0),
                                pipeline_mode=pl.Buffered(1))

        in_specs += [_const_spec((1, D_MODEL)), _const_spec((1, D_MODEL)),
                     layer_spec((D_MODEL, 2 * D_FF)), layer_spec((D_FF, D_MODEL))]
        args += [fpre, fpost, win, wout]
        scratch.append(pltpu.VMEM((tt, D_MODEL), F32))
    return pl.pallas_call(
        kern,
        grid=(batch, nt + 1 if fused else nt),
        in_specs=in_specs,
        out_specs=[out_spec, c_spec, c_spec],
        out_shape=[jax.ShapeDtypeStruct(x.shape, F32), cache_shape, cache_shape],
        scratch_shapes=scratch,
        compiler_params=pltpu.CompilerParams(
            dimension_semantics=("arbitrary", "arbitrary"), vmem_limit_bytes=VMEM_LIMIT_BYTES),
        name="attn_pair_mixer",
    )(*args)


def _rope_tables(pos, q_scale):
    half = B_HEAD_DIM // 2
    inv = ROPE_THETA ** (-jnp.arange(half, dtype=F32) / half)
    ang = pos.astype(F32)[:, None] * inv[None, :]
    cos = jnp.cos(ang)
    sin = jnp.sin(ang)
    reps = LANES // B_HEAD_DIM
    cos_t = jnp.tile(jnp.concatenate([cos, cos], axis=1), (1, reps))
    sin_t = jnp.tile(jnp.concatenate([-sin, sin], axis=1), (1, reps))
    return cos_t * q_scale, sin_t * q_scale, cos_t, sin_t


def _trunk(x, pos, state, cache_k, cache_v, w, *, nb, tt, hgrn_nb, hgrn_tt, hgrn_chunk, cq):
    batch, seq, _ = x.shape
    row = lambda a: a.reshape(1, D_MODEL)
    casts = ()
    if w["w_ffn_in"].dtype == F32:
        casts = (w["w_ffn_in"].reshape(-1, 2 * D_FF), w["w_ffn_out"].reshape(-1, D_MODEL))
    x, st, *cast_out = _hgrn_mixer(
        x, state, row(w["norm_mix_pre"][0]), w["w_a_in"], w["a_lower_bound"],
        row(w["a_out_norm"]), w["w_a_out"], row(w["norm_mix_post"][0]),
        nb=hgrn_nb, tt=hgrn_tt, chunk=hgrn_chunk, casts=casts)
    if cast_out:
        w = dict(w, w_ffn_in=cast_out[0].reshape(w["w_ffn_in"].shape),
                 w_ffn_out=cast_out[1].reshape(w["w_ffn_out"].shape))
    x = _ffn(x.reshape(batch * seq, D_MODEL), row(w["norm_ffn_pre"][0]), row(w["norm_ffn_post"][0]),
             w["w_ffn_in"], w["w_ffn_out"], 0).reshape(batch, seq, D_MODEL)
    attn_w = (w["b_sinks"], row(w["norm_mix_pre"][1]), row(w["kv_norm"]), w["w_b_q"], w["w_kv"],
              w["w_b_out"], row(w["norm_mix_post"][1]))
    ffn1 = (row(w["norm_ffn_pre"][1]), row(w["norm_ffn_post"][1]), w["w_ffn_in"], w["w_ffn_out"], 1)
    if cache_k is None:
        assert nb == 1 and cq == CHUNK
        x, kc, vc = _attn_pair_mixer(x, _rope_tables(pos, SOFTMAX_SCALE * LOG2_E), *attn_w, tt=tt,
                                     ffn=ffn1)
    else:
        assert tt == seq and cq == seq
        x, kc, vc = _attn_cached_mixer(x, cache_k, cache_v, _rope_tables(pos, SOFTMAX_SCALE),
                                       *attn_w, nb=nb)
        x = _ffn(x.reshape(batch * seq, D_MODEL), *ffn1).reshape(batch, seq, D_MODEL)
    return x, st, kc, vc, w


def kernel(x_prompt, x_sample, state_hgrn, cache_k, cache_v, norm_mix_pre, norm_mix_post, norm_ffn_pre, norm_ffn_post, w_ffn_in, w_ffn_out, w_a_in, a_lower_bound, a_out_norm, w_a_out, kv_norm, w_kv, w_b_q, b_sinks, w_b_out):
    w = dict(
        norm_mix_pre=norm_mix_pre, norm_mix_post=norm_mix_post,
        norm_ffn_pre=norm_ffn_pre, norm_ffn_post=norm_ffn_post,
        w_ffn_in=w_ffn_in, w_ffn_out=w_ffn_out,
        w_a_in=w_a_in[0].astype(BF16), a_lower_bound=a_lower_bound,
        a_out_norm=a_out_norm[0], w_a_out=w_a_out[0].astype(BF16),
        kv_norm=kv_norm, w_kv=w_kv.astype(BF16), w_b_q=w_b_q[0].astype(BF16),
        b_sinks=b_sinks[0], w_b_out=w_b_out[0].astype(BF16),
    )
    bp, tp, _ = x_prompt.shape
    bs, ts, _ = x_sample.shape

    zero_state = jnp.zeros((bp, A_HEADS, A_DK, A_DV), F32)
    y_p, st_p, kc_p, vc_p, w = _trunk(
        x_prompt, jnp.arange(tp), zero_state, None, None, w,
        nb=1, tt=ROW_TILE, hgrn_nb=HGRN_STREAMS, hgrn_tt=ROW_TILE // HGRN_STREAMS,
        hgrn_chunk=HGRN_CHUNK, cq=CHUNK)

    nb_s = ROW_TILE // (2 * ts)
    y_s, st_s, kc_s, vc_s, _ = _trunk(
        x_sample, PAST_LEN + jnp.arange(ts), state_hgrn[0],
        cache_k.reshape(bs, WINDOW, KV_WIDTH), cache_v.reshape(bs, WINDOW, KV_WIDTH), w,
        nb=nb_s, tt=ts, hgrn_nb=nb_s, hgrn_tt=ts, hgrn_chunk=ts, cq=ts)

    cache4 = lambda a: a.reshape(a.shape[0], WINDOW, B_KV_HEADS, B_HEAD_DIM)
    return (y_p, y_s, st_p[None], st_s[None],
            cache4(kc_p), cache4(vc_p), cache4(kc_s), cache4(vc_s))
```

```python
import functools

import jax
import jax.numpy as jnp
from jax import lax
from jax.experimental import pallas as pl
from jax.experimental.pallas import tpu as pltpu

F32 = jnp.float32
BF16 = jnp.bfloat16

D_MODEL = 1024
A_HEADS = 8
A_DK = 128
A_DV = 128
B_HEAD_DIM = 64
B_Q_HEADS = 16
B_KV_HEADS = 4
B_GROUP = B_Q_HEADS // B_KV_HEADS
KV_WIDTH = B_KV_HEADS * B_HEAD_DIM
WINDOW = 128
CHUNK = 64
PAST_LEN = 2048
D_FF = 2816
ROPE_THETA = 10000.0
NORM_EPS = 1e-6
SOFTMAX_SCALE = B_HEAD_DIM ** -0.5
LOG2_E = 1.4426950408889634

LANES = 128
SUBLANES = 8
VMEM_LIMIT_BYTES = 52 * 1024 * 1024

ROW_TILE = 512
FFN_ROW_TILE = 1024
HGRN_CHUNK = 128
FFN_CHUNK = 256
MAX_FACTOR_EXPONENT = 60.0
EXACT_BLOCK = 16
HGRN_STREAMS = 1
HEAD_GROUPS = 2
JOB_SHARES = (0.18, 0.09, 0.09, 0.09, 0.09)


def _rms_scale(x):
    ms = jnp.mean(x * x, axis=-1, keepdims=True)
    return x * lax.rsqrt(ms + NORM_EPS)


def _dot(a, b):
    return jnp.dot(a, b, preferred_element_type=F32)


def _dot_nt(a, b):
    return lax.dot_general(a, b, (((1,), (1,)), ((), ())), preferred_element_type=F32)


def _dot_tn(a, b):
    return lax.dot_general(a, b, (((0,), (0,)), ((), ())), preferred_element_type=F32)


def _const_spec(shape):
    nd = len(shape)
    return pl.BlockSpec(shape, lambda *_: (0,) * nd, pipeline_mode=pl.Buffered(1))


def _ffn_kernel(x_ref, npre_ref, npost_ref, win_ref, wout_ref, o_ref, *, sub_rows):
    starts = range(0, x_ref.shape[0], sub_rows)
    normed = [(_rms_scale(x_ref[r0:r0 + sub_rows, :]) * npre_ref[...]).astype(BF16) for r0 in starts]
    for r0, h in zip(starts, normed):
        x = x_ref[r0:r0 + sub_rows, :]
        acc = jnp.zeros(x.shape, F32)
        for c0 in range(0, D_FF, FFN_CHUNK):
            c1 = min(c0 + FFN_CHUNK, D_FF)
            a = _dot(h, win_ref[:, c0:c1])
            b = _dot(h, win_ref[:, D_FF + c0:D_FF + c1])
            g = (a * jax.nn.sigmoid(a) * b).astype(BF16)
            acc = acc + _dot(g, wout_ref[c0:c1, :])
        o_ref[r0:r0 + sub_rows, :] = x + _rms_scale(acc) * npost_ref[...]


def _ffn(x2d, npre, npost, win, wout, layer):
    rows = x2d.shape[0]
    tile = min(FFN_ROW_TILE, rows)
    sub_rows = min(ROW_TILE, tile)
    assert rows % tile == 0 and tile % sub_rows == 0

    def layer_spec(shape):
        return pl.BlockSpec((None,) + shape, lambda i: (layer, 0, 0), pipeline_mode=pl.Buffered(1))

    return pl.pallas_call(
        functools.partial(_ffn_kernel, sub_rows=sub_rows),
        grid=(rows // tile,),
        in_specs=[
            pl.BlockSpec((tile, D_MODEL), lambda i: (i, 0)),
            _const_spec((1, D_MODEL)),
            _const_spec((1, D_MODEL)),
            layer_spec((D_MODEL, 2 * D_FF)),
            layer_spec((D_FF, D_MODEL)),
        ],
        out_specs=pl.BlockSpec((tile, D_MODEL), lambda i: (i, 0)),
        out_shape=jax.ShapeDtypeStruct((rows, D_MODEL), F32),
        compiler_params=pltpu.CompilerParams(
            dimension_semantics=("parallel",), vmem_limit_bytes=VMEM_LIMIT_BYTES),
        name="ffn",
    )(x2d, npre, npost, win, wout)


def _cumsum_rows(x):
    c, w = x.shape
    groups = c // SUBLANES
    y = x.reshape(groups, SUBLANES, w)
    sub = lax.broadcasted_iota(jnp.int32, y.shape, 1)
    shift = 1
    while shift < SUBLANES:
        y = y + jnp.where(sub >= shift, pltpu.roll(y, shift, axis=1), 0.0)
        shift *= 2
    tot = jnp.broadcast_to(y[:, SUBLANES - 1:SUBLANES, :], y.shape)
    inc = tot
    shift = 1
    while shift < groups:
        inc = inc + jnp.concatenate(
            [jnp.zeros((shift, SUBLANES, w), F32), inc[:groups - shift]], axis=0)
        shift *= 2
    return (y + (inc - tot)).reshape(c, w)


def _hgrn_kernel(*refs, nb, tt, chunk, cast_blocks):
    ncast = len(cast_blocks)
    x_ref, st_in_ref, npre_ref, win_ref, alb_ref, gnorm_ref, wout_ref, npost_ref = refs[:8]
    cast_src = refs[8:8 + ncast]
    y_ref, st_ref = refs[8 + ncast:10 + ncast]
    cast_dst = refs[10 + ncast:10 + 2 * ncast]
    qs_ref, lf_ref, inp_ref, v_ref, sg_ref, on_ref = refs[10 + 2 * ncast:]
    t = pl.program_id(1)
    rows = nb * tt

    step = pl.program_id(0) * pl.num_programs(1) + t
    for src_ref, dst_ref, nblocks in zip(cast_src, cast_dst, cast_blocks):
        @pl.when(step < nblocks)
        def _(src_ref=src_ref, dst_ref=dst_ref):
            dst_ref[...] = src_ref[...].astype(BF16)

    @pl.when(t == 0)
    def _():
        for n in range(nb):
            for h in range(A_HEADS):
                st_ref[n, h] = st_in_ref[n, h].T

    x = x_ref[...].reshape(rows, D_MODEL)
    hn = (_rms_scale(x) * npre_ref[...]).astype(BF16)

    alb = alb_ref[...]
    e = jnp.exp(alb - jnp.max(alb, axis=0, keepdims=True))
    lb = e[0:1] / jnp.sum(e, axis=0, keepdims=True)

    f = _dot(hn, win_ref[:, D_MODEL:2 * D_MODEL])
    forget = lb + (1.0 - lb) * jax.nn.sigmoid(f)
    lf_ref[...] = jnp.log(forget)
    inp_ref[...] = 1.0 - forget

    def project_query(c0, width):
        q = _dot(hn, win_ref[:, c0:c0 + width])
        qs_ref[:, c0:c0 + width] = q * jax.nn.sigmoid(q)

    def project_value(c0, width):
        v_ref[:, c0:c0 + width] = _dot(
            hn, win_ref[:, 2 * D_MODEL + c0:2 * D_MODEL + c0 + width]).astype(BF16)

    def project_gate(c0, width):
        g = _dot(hn, win_ref[:, 3 * D_MODEL + c0:3 * D_MODEL + c0 + width])
        sg_ref[:, c0:c0 + width] = g * jax.nn.sigmoid(g)

    def write_rows(mix):
        y = x + _rms_scale(mix) * npost_ref[...]
        y_ref[...] = y.reshape(nb, tt, D_MODEL)

    half = chunk // 2
    ri = lax.broadcasted_iota(jnp.int32, (chunk, chunk), 0)
    ci = lax.broadcasted_iota(jnp.int32, (chunk, chunk), 1)
    causal = ri >= ci

    def prepare(blk, h):
        rs = slice(blk * chunk, (blk + 1) * chunk)
        sl = slice(h * A_DK, (h + 1) * A_DK)
        b = _cumsum_rows(lf_ref[rs, sl])
        bmid = b[half - 1:half]
        blast = b[chunk - 1:chunk]
        qsc = qs_ref[rs, sl] * jnp.exp(b - bmid)
        inp = inp_ref[rs, sl] * jnp.exp(bmid - b)
        return dict(
            rs=rs, sl=sl, qt=qsc.astype(BF16), kt=inp.astype(BF16),
            qi=(qsc * jnp.exp(bmid)).astype(BF16),
            kh=(inp * jnp.exp(blast - bmid)).astype(BF16),
            vh=v_ref[rs, sl], decay=jnp.exp(blast))

    def emit_output(u, sc, st):
        p = jnp.where(causal, sc, 0.0).astype(BF16)
        o = _dot(p, u["vh"]) + _dot_nt(u["qi"], st.astype(BF16))
        on = _rms_scale(o) * gnorm_ref[:, u["sl"]] * sg_ref[u["rs"], u["sl"]]
        on_ref[u["rs"], u["sl"]] = on.astype(BF16)

    def recurrence(heads, jobs):
        nstages = tt // chunk
        slots = 2 * nstages
        jobs = list(jobs)

        def run_jobs(slot):
            for _ in range(-(-len(jobs) // (slots - slot))):
                jobs.pop(0)()

        states = {}
        for stage in range(nstages):
            blocks = [n * nstages + stage for n in range(nb)]
            units = [(blk, h, prepare(blk, h)) for blk in blocks for h in heads]
            scores = [_dot_nt(u["qt"], u["kt"]) for _, _, u in units]
            run_jobs(2 * stage)
            updated = {}
            for blk, h, u in units:
                key = ((blk * chunk) // tt, h)
                if key not in states:
                    states[key] = st_ref[key[0], h]
                updated[key] = states[key] * u["decay"] + _dot_tn(u["vh"], u["kh"])
            run_jobs(2 * stage + 1)
            for (blk, h, u), sc in zip(units, scores):
                emit_output(u, sc, states[((blk * chunk) // tt, h)])
            states.update(updated)
        for (n, h), st in states.items():
            st_ref[n, h] = st

    def factorised_path():
        heads_per_group = A_HEADS // HEAD_GROUPS
        piece = 2 * LANES
        partial_mix = [[] for _ in range(HEAD_GROUPS)]

        def project_out(g, c0, width):
            partial_mix[g].append(
                _dot(on_ref[:, g * gw:(g + 1) * gw], wout_ref[g * gw:(g + 1) * gw, c0:c0 + width]))

        for g in range(HEAD_GROUPS):
            jobs = []
            if g + 1 < HEAD_GROUPS:
                jobs += [functools.partial(proj, (g + 1) * gw + c0, piece)
                         for proj in (project_query, project_value, project_gate)
                         for c0 in range(0, gw, piece)]
            if g > 0:
                jobs += [functools.partial(project_out, g - 1, c0, piece)
                         for c0 in range(0, D_MODEL, piece)]
            recurrence(range(g * heads_per_group, (g + 1) * heads_per_group), jobs)
        project_out(HEAD_GROUPS - 1, 0, D_MODEL)
        write_rows(functools.reduce(
            jnp.add, [jnp.concatenate(parts, axis=1) for parts in partial_mix]))

    def exact_step(idx, carry):
        n = (idx * EXACT_BLOCK) // tt
        rs = pl.ds(pl.multiple_of(idx * EXACT_BLOCK, EXACT_BLOCK), EXACT_BLOCK)
        row = lax.broadcasted_iota(jnp.int32, (EXACT_BLOCK, A_DK), 0)
        for h in range(A_HEADS):
            sl = slice(h * A_DK, (h + 1) * A_DK)
            b = _cumsum_rows(lf_ref[rs, sl])
            blast = b[EXACT_BLOCK - 1:EXACT_BLOCK]
            qs = qs_ref[rs, sl]
            inp = inp_ref[rs, sl]
            vh = v_ref[rs, sl]
            vf = vh.astype(F32)
            st = st_ref[n, h]
            o = _dot_nt((qs * jnp.exp(b)).astype(BF16), st.astype(BF16))
            for s in range(EXACT_BLOCK):
                decay = jnp.exp(jnp.where(row >= s, b - b[s:s + 1], -jnp.inf))
                score = jnp.sum(qs * decay * inp[s:s + 1], axis=-1, keepdims=True)
                o = o + score * vf[s:s + 1]
            kh = (inp * jnp.exp(blast - b)).astype(BF16)
            st_ref[n, h] = st * jnp.exp(blast) + _dot_tn(vh, kh)
            on = _rms_scale(o) * gnorm_ref[:, sl] * sg_ref[rs, sl]
            on_ref[rs, sl] = on.astype(BF16)
        return carry

    gw = D_MODEL // HEAD_GROUPS
    project_query(0, gw)
    project_value(0, gw)
    project_gate(0, gw)

    max_step = -jnp.min(lf_ref[...])
    bounded = max_step * half <= MAX_FACTOR_EXPONENT

    pl.when(bounded)(factorised_path)

    @pl.when(jnp.logical_not(bounded))
    def _():
        project_query(gw, D_MODEL - gw)
        project_value(gw, D_MODEL - gw)
        project_gate(gw, D_MODEL - gw)
        lax.fori_loop(0, rows // EXACT_BLOCK, exact_step, 0)
        write_rows(_dot(on_ref[...], wout_ref[...]))

    @pl.when(t == pl.num_programs(1) - 1)
    def _():
        for n in range(nb):
            for h in range(A_HEADS):
                st_ref[n, h] = st_ref[n, h].T


def _hgrn_mixer(x, state, npre, win, alb, gnorm, wout, npost, *, nb, tt, chunk, casts=()):
    batch, seq, _ = x.shape
    assert batch % nb == 0 and seq % tt == 0 and tt % chunk == 0 and chunk % (2 * SUBLANES) == 0
    rows = nb * tt
    nt = seq // tt
    nsteps = (batch // nb) * nt
    cast_specs, cast_blocks = [], []
    for a in casts:
        block_rows = next(r for r in range(2 * SUBLANES, a.shape[0] + 1, 2 * SUBLANES)
                          if a.shape[0] % r == 0 and a.shape[0] // r <= nsteps)
        nblocks = a.shape[0] // block_rows
        cast_blocks.append(nblocks)
        cast_specs.append(pl.BlockSpec(
            (block_rows, a.shape[1]),
            lambda b, t, nblocks=nblocks: (jnp.minimum(b * nt + t, nblocks - 1), 0)))
    kern = functools.partial(_hgrn_kernel, nb=nb, tt=tt, chunk=chunk, cast_blocks=tuple(cast_blocks))
    st_spec = pl.BlockSpec((nb, A_HEADS, A_DK, A_DV), lambda b, t: (b, 0, 0, 0))
    return pl.pallas_call(
        kern,
        grid=(batch // nb, nt),
        in_specs=[
            pl.BlockSpec((nb, tt, D_MODEL), lambda b, t: (b, t, 0)),
            st_spec,
            _const_spec((1, D_MODEL)),
            _const_spec((D_MODEL, 4 * D_MODEL)),
            _const_spec(alb.shape),
            _const_spec((1, D_MODEL)),
            _const_spec((D_MODEL, D_MODEL)),
            _const_spec((1, D_MODEL)),
        ] + cast_specs,
        out_specs=[
            pl.BlockSpec((nb, tt, D_MODEL), lambda b, t: (b, t, 0)),
            st_spec,
        ] + cast_specs,
        out_shape=[
            jax.ShapeDtypeStruct(x.shape, F32),
            jax.ShapeDtypeStruct(state.shape, F32),
        ] + [jax.ShapeDtypeStruct(a.shape, BF16) for a in casts],
        scratch_shapes=[
            pltpu.VMEM((rows, D_MODEL), F32),
            pltpu.VMEM((rows, D_MODEL), F32),
            pltpu.VMEM((rows, D_MODEL), F32),
            pltpu.VMEM((rows, D_MODEL), BF16),
            pltpu.VMEM((rows, D_MODEL), F32),
            pltpu.VMEM((rows, D_MODEL), BF16),
        ],
        compiler_params=pltpu.CompilerParams(
            dimension_semantics=("arbitrary", "arbitrary"), vmem_limit_bytes=VMEM_LIMIT_BYTES),
        name="hgrn_mixer",
    )(x, state, npre, win, alb, gnorm, wout, npost, *casts)


def _rope_cols(x, cos_t, sin_t):
    lane = lax.broadcasted_iota(jnp.int32, (x.shape[0], LANES), 1)
    first_half = (lane % B_HEAD_DIM) < (B_HEAD_DIM // 2)
    cols = []
    for c0 in range(0, x.shape[1], LANES):
        xc = x[:, c0:c0 + LANES]
        partner = jnp.where(first_half,
                            pltpu.roll(xc, LANES - B_HEAD_DIM // 2, axis=1),
                            pltpu.roll(xc, B_HEAD_DIM // 2, axis=1))
        cols.append(xc * cos_t + partner * sin_t)
    return jnp.concatenate(cols, axis=1)


def _attn_cached_kernel(x_ref, kc_in_ref, vc_in_ref, cosq_ref, sinq_ref, cosk_ref, sinkt_ref,
                        sinks_ref, npre_ref, kvn_ref, wq_ref, wkv_ref, wo_ref, npost_ref,
                        y_ref, kc_ref, vc_ref, q_ref, kl_ref, kh_ref, vl_ref, ot_ref, *, nb, tt):
    rows = nb * tt
    ext = WINDOW + tt

    x = x_ref[...].reshape(rows, D_MODEL)
    xs = _rms_scale(x)
    hq = (xs * npre_ref[...]).astype(BF16)
    hk = (xs * kvn_ref[...]).astype(BF16)

    def per_stream(tab_ref):
        tab = tab_ref[...]
        return tab if nb == 1 else jnp.concatenate([tab] * nb, axis=0)

    q = _dot(hq, wq_ref[...])
    q_ref[...] = _rope_cols(q, per_stream(cosq_ref), per_stream(sinq_ref)).astype(BF16)
    kv = _dot(hk, wkv_ref[...])
    k_new = _rope_cols(kv[:, :KV_WIDTH], per_stream(cosk_ref), per_stream(sinkt_ref))
    v_new = kv[:, KV_WIDTH:]

    low = lax.broadcasted_iota(jnp.int32, (ext, LANES), 1) < B_HEAD_DIM
    for n in range(nb):
        k_ext = jnp.concatenate([kc_in_ref[n], k_new[n * tt:(n + 1) * tt]], axis=0)
        v_ext = jnp.concatenate([vc_in_ref[n], v_new[n * tt:(n + 1) * tt]], axis=0)
        kc_ref[n] = k_ext[ext - WINDOW:]
        vc_ref[n] = v_ext[ext - WINDOW:]
        for m in range(KV_WIDTH // LANES):
            ka = k_ext[:, m * LANES:(m + 1) * LANES]
            kr = pltpu.roll(ka, B_HEAD_DIM, axis=1)
            va = v_ext[:, m * LANES:(m + 1) * LANES]
            vr = pltpu.roll(va, B_HEAD_DIM, axis=1)
            kl_ref[2 * m, n] = jnp.where(low, ka, 0.0).astype(BF16)
            kh_ref[2 * m, n] = jnp.where(low, 0.0, kr).astype(BF16)
            kl_ref[2 * m + 1, n] = jnp.where(low, kr, 0.0).astype(BF16)
            kh_ref[2 * m + 1, n] = jnp.where(low, 0.0, ka).astype(BF16)
            vl_ref[2 * m, n] = jnp.where(low, va, 0.0).astype(BF16)
            vl_ref[2 * m + 1, n] = jnp.where(low, vr, 0.0).astype(BF16)

    first_head = lax.broadcasted_iota(jnp.int32, (1, 2 * tt), 1) < tt
    pairs = [(j, kx_ref, ha, hb) for j in range(B_KV_HEADS)
             for kx_ref, ha, hb in ((kl_ref, 4 * j, 4 * j + 2), (kh_ref, 4 * j + 1, 4 * j + 3))]

    def scores(n):
        rs = slice(n * tt, (n + 1) * tt)
        out = []
        for j, kx_ref, _, _ in pairs:
            qq = jnp.concatenate([q_ref[rs, 2 * j * LANES:(2 * j + 1) * LANES],
                                  q_ref[rs, (2 * j + 1) * LANES:(2 * j + 2) * LANES]], axis=0)
            out.append(_dot_nt(kx_ref[j, n], qq))
        return out

    def softmax(s_list):
        out = []
        for (_, _, ha, hb), s in zip(pairs, s_list):
            sink = jnp.where(first_head, sinks_ref[ha], sinks_ref[hb])
            mx = jnp.maximum(jnp.max(s, axis=0, keepdims=True), sink)
            p = jnp.exp(s - mx)
            den = jnp.sum(p, axis=0, keepdims=True) + jnp.exp(sink - mx)
            out.append((p.astype(BF16), den))
        return out

    def weighted_values(n, pd_list):
        cs = slice(n * tt, (n + 1) * tt)
        for (j, _, ha, hb), (p, den) in zip(pairs, pd_list):
            o = _dot_tn(vl_ref[j, n], p)[:B_HEAD_DIM]
            o = (o * (1.0 / den)).astype(BF16)
            ot_ref[ha * B_HEAD_DIM:(ha + 1) * B_HEAD_DIM, cs] = o[:, :tt]
            ot_ref[hb * B_HEAD_DIM:(hb + 1) * B_HEAD_DIM, cs] = o[:, tt:]

    s_next = scores(0)
    for n in range(nb):
        s_cur = s_next
        if n + 1 < nb:
            s_next = scores(n + 1)
        weighted_values(n, softmax(s_cur))

    mix = _dot_tn(ot_ref[...], wo_ref[...])
    y = x + _rms_scale(mix) * npost_ref[...]
    y_ref[...] = y.reshape(nb, tt, D_MODEL)


def _attn_cached_mixer(x, cache_k, cache_v, tables, sinks, npre, kvn, wq, wkv, wo, npost, *, nb):
    batch, tt, _ = x.shape
    assert batch % nb == 0 and tt % (2 * SUBLANES) == 0
    rows = nb * tt
    ext = WINDOW + tt
    kern = functools.partial(_attn_cached_kernel, nb=nb, tt=tt)
    x_spec = pl.BlockSpec((nb, tt, D_MODEL), lambda b: (b, 0, 0))
    c_spec = pl.BlockSpec((nb, WINDOW, KV_WIDTH), lambda b: (b, 0, 0))
    tab_spec = _const_spec((tt, LANES))
    cache_shape = jax.ShapeDtypeStruct((batch, WINDOW, KV_WIDTH), F32)
    return pl.pallas_call(
        kern,
        grid=(batch // nb,),
        in_specs=[x_spec, c_spec, c_spec] + [tab_spec] * 4 + [
            pl.BlockSpec(memory_space=pltpu.SMEM),
            _const_spec((1, D_MODEL)),
            _const_spec((1, D_MODEL)),
            _const_spec((D_MODEL, D_MODEL)),
            _const_spec((D_MODEL, 2 * KV_WIDTH)),
            _const_spec((D_MODEL, D_MODEL)),
            _const_spec((1, D_MODEL)),
        ],
        out_specs=[x_spec, c_spec, c_spec],
        out_shape=[jax.ShapeDtypeStruct(x.shape, F32), cache_shape, cache_shape],
        scratch_shapes=[
            pltpu.VMEM((rows, D_MODEL), BF16),
            pltpu.VMEM((B_KV_HEADS, nb, ext, LANES), BF16),
            pltpu.VMEM((B_KV_HEADS, nb, ext, LANES), BF16),
            pltpu.VMEM((B_KV_HEADS, nb, ext, LANES), BF16),
            pltpu.VMEM((D_MODEL, rows), BF16),
        ],
        compiler_params=pltpu.CompilerParams(
            dimension_semantics=("parallel",), vmem_limit_bytes=VMEM_LIMIT_BYTES),
        name="attn_cached_mixer",
    )(x, cache_k, cache_v, *tables, sinks, npre, kvn, wq, wkv, wo, npost)


def _ffn_steps(src_ref, dst_ref, npre_ref, npost_ref, win_ref, wout_ref):
    state = {}

    def chunk(c0):
        if not state:
            state["h"] = (_rms_scale(src_ref[...]) * npre_ref[...]).astype(BF16)
            state["acc"] = jnp.zeros(src_ref.shape, F32)
        h = state["h"]
        c1 = min(c0 + FFN_CHUNK, D_FF)
        a = _dot(h, win_ref[:, c0:c1])
        b = _dot(h, win_ref[:, D_FF + c0:D_FF + c1])
        g = (a * jax.nn.sigmoid(a) * b).astype(BF16)
        state["acc"] = state["acc"] + _dot(g, wout_ref[c0:c1, :])
        if c1 == D_FF:
            dst_ref[...] = (src_ref[...] + _rms_scale(state["acc"]) * npost_ref[...]).reshape(dst_ref.shape)

    return [functools.partial(chunk, c0) for c0 in range(0, D_FF, FFN_CHUNK)]


def _attn_pair_kernel(*refs, tt, fused):
    if fused:
        (x_ref, cosq_ref, sinq_ref, cosk_ref, sinkt_ref, sinks_ref,
         npre_ref, kvn_ref, wq_ref, wkv_ref, wo_ref, npost_ref,
         fpre_ref, fpost_ref, win_ref, wout_ref,
         y_ref, kc_ref, vc_ref, q_ref, kl_ref, kh_ref, vt_ref, ot_ref, mid_ref) = refs
    else:
        (x_ref, cosq_ref, sinq_ref, cosk_ref, sinkt_ref, sinks_ref,
         npre_ref, kvn_ref, wq_ref, wkv_ref, wo_ref, npost_ref,
         y_ref, kc_ref, vc_ref, q_ref, kl_ref, kh_ref, vt_ref, ot_ref) = refs
    t = pl.program_id(1)
    if not fused:
        _attention_tile(t, tt, x_ref, cosq_ref, sinq_ref, cosk_ref, sinkt_ref, sinks_ref,
                        npre_ref, kvn_ref, wq_ref, wkv_ref, wo_ref, npost_ref, kc_ref, vc_ref,
                        q_ref, kl_ref, kh_ref, vt_ref, ot_ref, [], y_ref.at[0])
        return

    last = pl.num_programs(1) - 1

    @pl.when((pl.program_id(0) == 0) & (t == 0))
    def _():
        mid_ref[...] = jnp.zeros(mid_ref.shape, F32)

    @pl.when(t < last)
    def _():
        jobs = _ffn_steps(mid_ref, y_ref, fpre_ref, fpost_ref, win_ref, wout_ref)
        _attention_tile(t, tt, x_ref, cosq_ref, sinq_ref, cosk_ref, sinkt_ref, sinks_ref,
                        npre_ref, kvn_ref, wq_ref, wkv_ref, wo_ref, npost_ref, kc_ref, vc_ref,
                        q_ref, kl_ref, kh_ref, vt_ref, ot_ref, jobs, mid_ref)

    @pl.when(t == last)
    def _():
        for job in _ffn_steps(mid_ref, y_ref, fpre_ref, fpost_ref, win_ref, wout_ref):
            job()


def _attention_tile(t, tt, x_ref, cosq_ref, sinq_ref, cosk_ref, sinkt_ref, sinks_ref,
                    npre_ref, kvn_ref, wq_ref, wkv_ref, wo_ref, npost_ref, kc_ref, vc_ref,
                    q_ref, kl_ref, kh_ref, vt_ref, ot_ref, jobs, out_ref):
    ext = WINDOW + tt
    pair = 2 * CHUNK
    nkeys = WINDOW + pair
    nvis = WINDOW + CHUNK
    jobs = list(jobs)
    njobs = len(jobs)

    def run_jobs(share):
        for _ in range(min(round(share * njobs), len(jobs))):
            jobs.pop(0)()

    @pl.when(t == 0)
    def _():
        kc_ref[...] = jnp.zeros(kc_ref.shape, F32)
        vc_ref[...] = jnp.zeros(vc_ref.shape, F32)

    x = x_ref[0]
    xs = _rms_scale(x)
    hq = (xs * npre_ref[...]).astype(BF16)
    hk = (xs * kvn_ref[...]).astype(BF16)

    run_jobs(JOB_SHARES[0])
    q = _dot(hq, wq_ref[...])
    q_ref[...] = _rope_cols(q, cosq_ref[...], sinq_ref[...]).astype(BF16)
    run_jobs(JOB_SHARES[1])
    kv = _dot(hk, wkv_ref[...])
    run_jobs(JOB_SHARES[2])
    k_ext = jnp.concatenate(
        [kc_ref[0], _rope_cols(kv[:, :KV_WIDTH], cosk_ref[...], sinkt_ref[...])], axis=0)
    v_ext = jnp.concatenate([vc_ref[0], kv[:, KV_WIDTH:]], axis=0)
    kc_ref[0] = k_ext[ext - WINDOW:]
    vc_ref[0] = v_ext[ext - WINDOW:]
    vt_ref[...] = v_ext.T.astype(BF16)

    low = lax.broadcasted_iota(jnp.int32, (ext, LANES), 1) < B_HEAD_DIM
    for m in range(KV_WIDTH // LANES):
        ka = k_ext[:, m * LANES:(m + 1) * LANES]
        kr = pltpu.roll(ka, B_HEAD_DIM, axis=1)
        kl_ref[2 * m] = jnp.where(low, ka, 0.0).astype(BF16)
        kh_ref[2 * m] = jnp.where(low, 0.0, kr).astype(BF16)
        kl_ref[2 * m + 1] = jnp.where(low, kr, 0.0).astype(BF16)
        kh_ref[2 * m + 1] = jnp.where(low, 0.0, ka).astype(BF16)

    vis_row = lax.broadcasted_iota(jnp.int32, (nvis, LANES), 0)
    first_bias = [jnp.where(vis_row + (t * tt - WINDOW + qc * CHUNK) >= 0, 0.0, -jnp.inf)
                  for qc in range(2)]
    first_head = lax.broadcasted_iota(jnp.int32, (1, LANES), 1) < CHUNK
    no_keys = jnp.zeros((CHUNK, LANES), BF16)

    units = [(j, kx_ref, ha, hb) for j in range(B_KV_HEADS)
             for kx_ref, ha, hb in ((kl_ref, 4 * j, 4 * j + 2), (kh_ref, 4 * j + 1, 4 * j + 3))]

    def scores(e):
        r0 = e * pair
        out = []
        for j, kx_ref, _, _ in units:
            ca = slice(2 * j * LANES, (2 * j + 1) * LANES)
            cb = slice((2 * j + 1) * LANES, (2 * j + 2) * LANES)
            qq = jnp.concatenate([q_ref[r0:r0 + CHUNK, ca], q_ref[r0:r0 + CHUNK, cb],
                                  q_ref[r0 + CHUNK:r0 + pair, ca], q_ref[r0 + CHUNK:r0 + pair, cb]], axis=0)
            out.append(_dot_nt(kx_ref[j, r0:r0 + nkeys, :], qq))
        return out

    def softmax(e, s_list):
        out = []
        for (_, _, ha, hb), s in zip(units, s_list):
            sink = jnp.where(first_head, sinks_ref[ha], sinks_ref[hb]) * LOG2_E
            ps, dens = [], []
            for qc in range(2):
                sq = s[qc * CHUNK:qc * CHUNK + nvis, qc * LANES:(qc + 1) * LANES]
                if e == 0:
                    sq = sq + first_bias[qc]
                mx = jnp.maximum(jnp.max(sq, axis=0, keepdims=True), sink)
                p = jnp.exp2(sq - mx)
                dens.append(jnp.sum(p, axis=0, keepdims=True) + jnp.exp2(sink - mx))
                ps.append(p.astype(BF16))
            pmat = jnp.concatenate([jnp.concatenate([ps[0], no_keys], axis=0),
                                    jnp.concatenate([no_keys, ps[1]], axis=0)], axis=1)
            out.append((pmat, jnp.concatenate(dens, axis=1)))
        return out

    def weighted_values(e, pd_list):
        r0 = e * pair
        for (j, _, ha, hb), (pmat, den) in zip(units, pd_list):
            o = _dot(vt_ref[j * B_HEAD_DIM:(j + 1) * B_HEAD_DIM, r0:r0 + nkeys], pmat)
            o = (o * (1.0 / den)).astype(BF16)
            for qc in range(2):
                c0 = r0 + qc * CHUNK
                ot_ref[ha * B_HEAD_DIM:(ha + 1) * B_HEAD_DIM, c0:c0 + CHUNK] = (
                    o[:, qc * LANES:qc * LANES + CHUNK])
                ot_ref[hb * B_HEAD_DIM:(hb + 1) * B_HEAD_DIM, c0:c0 + CHUNK] = (
                    o[:, qc * LANES + CHUNK:(qc + 1) * LANES])

    npairs = tt // pair
    s_next = scores(0)
    for e in range(npairs):
        s_cur = s_next
        if e + 1 < npairs:
            s_next = scores(e + 1)
        run_jobs(JOB_SHARES[3])
        weighted_values(e, softmax(e, s_cur))

    run_jobs(JOB_SHARES[4])
    mix = _dot_tn(ot_ref[...], wo_ref[...])
    run_jobs(1.0)
    out_ref[...] = x + _rms_scale(mix) * npost_ref[...]


def _attn_pair_mixer(x, tables, sinks, npre, kvn, wq, wkv, wo, npost, *, tt, ffn=None):
    batch, seq, _ = x.shape
    assert seq % tt == 0 and tt % (2 * CHUNK) == 0 and WINDOW == 2 * CHUNK
    ext = WINDOW + tt
    fused = ffn is not None
    nt = seq // tt
    kern = functools.partial(_attn_pair_kernel, tt=tt, fused=fused)
    if fused:
        in_spec = pl.BlockSpec((1, tt, D_MODEL), lambda b, t: (b, jnp.minimum(t, nt - 1), 0))
        out_spec = pl.BlockSpec((1, tt, D_MODEL), lambda b, t: (b, jnp.maximum(t - 1, 0), 0))
        tab_spec = pl.BlockSpec((tt, LANES), lambda b, t: (jnp.minimum(t, nt - 1), 0))
    else:
        in_spec = out_spec = pl.BlockSpec((1, tt, D_MODEL), lambda b, t: (b, t, 0))
        tab_spec = pl.BlockSpec((tt, LANES), lambda b, t: (t, 0))
    c_spec = pl.BlockSpec((1, WINDOW, KV_WIDTH), lambda b, t: (b, 0, 0))
    cache_shape = jax.ShapeDtypeStruct((batch, WINDOW, KV_WIDTH), F32)
    in_specs = [in_spec] + [tab_spec] * 4 + [
        pl.BlockSpec(memory_space=pltpu.SMEM),
        _const_spec((1, D_MODEL)),
        _const_spec((1, D_MODEL)),
        _const_spec((D_MODEL, D_MODEL)),
        _const_spec((D_MODEL, 2 * KV_WIDTH)),
        _const_spec((D_MODEL, D_MODEL)),
        _const_spec((1, D_MODEL)),
    ]
    args = [x, *tables, sinks, npre, kvn, wq, wkv, wo, npost]
    scratch = [
        pltpu.VMEM((tt, D_MODEL), BF16),
        pltpu.VMEM((B_KV_HEADS, ext, LANES), BF16),
        pltpu.VMEM((B_KV_HEADS, ext, LANES), BF16),
        pltpu.VMEM((KV_WIDTH, ext), BF16),
        pltpu.VMEM((D_MODEL, tt), BF16),
    ]
    if fused:
        fpre, fpost, win, wout, layer = ffn

        def layer_spec(shape):
            return pl.BlockSpec((None,) + shape, lambda b, t: (layer, 0, 0),
                                pipeline_mode=pl.Buffered(1))

        in_specs += [_const_spec((1, D_MODEL)), _const_spec((1, D_MODEL)),
                     layer_spec((D_MODEL, 2 * D_FF)), layer_spec((D_FF, D_MODEL))]
        args += [fpre, fpost, win, wout]
        scratch.append(pltpu.VMEM((tt, D_MODEL), F32))
    return pl.pallas_call(
        kern,
        grid=(batch, nt + 1 if fused else nt),
        in_specs=in_specs,
        out_specs=[out_spec, c_spec, c_spec],
        out_shape=[jax.ShapeDtypeStruct(x.shape, F32), cache_shape, cache_shape],
        scratch_shapes=scratch,
        compiler_params=pltpu.CompilerParams(
            dimension_semantics=("arbitrary", "arbitrary"), vmem_limit_bytes=VMEM_LIMIT_BYTES),
        name="attn_pair_mixer",
    )(*args)


def _rope_tables(pos, q_scale):
    half = B_HEAD_DIM // 2
    inv = ROPE_THETA ** (-jnp.arange(half, dtype=F32) / half)
    ang = pos.astype(F32)[:, None] * inv[None, :]
    cos = jnp.cos(ang)[:, None, :]
    sin = jnp.sin(ang)[:, None, :]
    sign = jnp.tile(jnp.array([-1.0, 1.0], F32), LANES // B_HEAD_DIM)[None, :, None]
    shape = (pos.shape[0], LANES)

    def table(values, scale):
        return (jnp.broadcast_to(values, (pos.shape[0], LANES // half, half)) * scale).reshape(shape)

    return table(cos, q_scale), table(sin * sign, q_scale), table(cos, 1.0), table(sin * sign, 1.0)


def _trunk(x, pos, state, cache_k, cache_v, w, *, nb, tt, hgrn_nb, hgrn_tt, hgrn_chunk, cq):
    batch, seq, _ = x.shape
    row = lambda a: a.reshape(1, D_MODEL)
    casts = ()
    if w["w_ffn_in"].dtype == F32:
        casts = (w["w_ffn_in"].reshape(-1, 2 * D_FF), w["w_ffn_out"].reshape(-1, D_MODEL))
    x, st, *cast_out = _hgrn_mixer(
        x, state, row(w["norm_mix_pre"][0]), w["w_a_in"], w["a_lower_bound"],
        row(w["a_out_norm"]), w["w_a_out"], row(w["norm_mix_post"][0]),
        nb=hgrn_nb, tt=hgrn_tt, chunk=hgrn_chunk, casts=casts)
    if cast_out:
        w = dict(w, w_ffn_in=cast_out[0].reshape(w["w_ffn_in"].shape),
                 w_ffn_out=cast_out[1].reshape(w["w_ffn_out"].shape))
    x = _ffn(x.reshape(batch * seq, D_MODEL), row(w["norm_ffn_pre"][0]), row(w["norm_ffn_post"][0]),
             w["w_ffn_in"], w["w_ffn_out"], 0).reshape(batch, seq, D_MODEL)
    attn_w = (w["b_sinks"], row(w["norm_mix_pre"][1]), row(w["kv_norm"]), w["w_b_q"], w["w_kv"],
              w["w_b_out"], row(w["norm_mix_post"][1]))
    ffn1 = (row(w["norm_ffn_pre"][1]), row(w["norm_ffn_post"][1]), w["w_ffn_in"], w["w_ffn_out"], 1)
    if cache_k is None:
        assert nb == 1 and cq == CHUNK
        x, kc, vc = _attn_pair_mixer(x, _rope_tables(pos, SOFTMAX_SCALE * LOG2_E), *attn_w, tt=tt,
                                     ffn=ffn1)
    else:
        assert tt == seq and cq == seq
        x, kc, vc = _attn_cached_mixer(x, cache_k, cache_v, _rope_tables(pos, SOFTMAX_SCALE),
                                       *attn_w, nb=nb)
        x = _ffn(x.reshape(batch * seq, D_MODEL), *ffn1).reshape(batch, seq, D_MODEL)
    return x, st, kc, vc, w


def kernel(x_prompt, x_sample, state_hgrn, cache_k, cache_v, norm_mix_pre, norm_mix_post, norm_ffn_pre, norm_ffn_post, w_ffn_in, w_ffn_out, w_a_in, a_lower_bound, a_out_norm, w_a_out, kv_norm, w_kv, w_b_q, b_sinks, w_b_out):
    w = dict(
        norm_mix_pre=norm_mix_pre, norm_mix_post=norm_mix_post,
        norm_ffn_pre=norm_ffn_pre, norm_ffn_post=norm_ffn_post,
        w_ffn_in=w_ffn_in, w_ffn_out=w_ffn_out,
        w_a_in=w_a_in[0].astype(BF16), a_lower_bound=a_lower_bound,
        a_out_norm=a_out_norm[0], w_a_out=w_a_out[0].astype(BF16),
        kv_norm=kv_norm, w_kv=w_kv.astype(BF16), w_b_q=w_b_q[0].astype(BF16),
        b_sinks=b_sinks[0], w_b_out=w_b_out[0].astype(BF16),
    )
    bp, tp, _ = x_prompt.shape
    bs, ts, _ = x_sample.shape

    zero_state = jnp.zeros((bp, A_HEADS, A_DK, A_DV), F32)
    y_p, st_p, kc_p, vc_p, w = _trunk(
        x_prompt, jnp.arange(tp), zero_state, None, None, w,
        nb=1, tt=ROW_TILE, hgrn_nb=HGRN_STREAMS, hgrn_tt=ROW_TILE // HGRN_STREAMS,
        hgrn_chunk=HGRN_CHUNK, cq=CHUNK)

    nb_s = ROW_TILE // (2 * ts)
    y_s, st_s, kc_s, vc_s, _ = _trunk(
        x_sample, PAST_LEN + jnp.arange(ts), state_hgrn[0],
        cache_k.reshape(bs, WINDOW, KV_WIDTH), cache_v.reshape(bs, WINDOW, KV_WIDTH), w,
        nb=nb_s, tt=ts, hgrn_nb=nb_s, hgrn_tt=ts, hgrn_chunk=ts, cq=ts)

    cache4 = lambda a: a.reshape(a.shape[0], WINDOW, B_KV_HEADS, B_HEAD_DIM)
    return (y_p, y_s, st_p[None], st_s[None],
            cache4(kc_p), cache4(vc_p), cache4(kc_s), cache4(vc_s))
```

```python
import functools

import jax
import jax.numpy as jnp
from jax import lax
from jax.experimental import pallas as pl
from jax.experimental.pallas import tpu as pltpu

F32 = jnp.float32
BF16 = jnp.bfloat16

D_MODEL = 1024
A_HEADS = 8
A_DK = 128
A_DV = 128
B_HEAD_DIM = 64
B_Q_HEADS = 16
B_KV_HEADS = 4
KV_WIDTH = B_KV_HEADS * B_HEAD_DIM
WINDOW = 128
CHUNK = 64
PAST_LEN = 2048
D_FF = 2816
ROPE_THETA = 10000.0
NORM_EPS = 1e-6
SOFTMAX_SCALE = B_HEAD_DIM ** -0.5
LOG2_E = 1.4426950408889634

LANES = 128
SUBLANES = 8
VMEM_LIMIT_BYTES = 52 * 1024 * 1024

ROW_TILE = 512
FFN_ROW_TILE = 1024
HGRN_CHUNK = 128
FFN_CHUNK = 256
MAX_FACTOR_EXPONENT = 60.0
EXACT_BLOCK = 16
HEAD_GROUPS = 2
JOB_SHARES = (0.18, 0.09, 0.09, 0.09, 0.09)


def _rms_scale(x):
    ms = jnp.mean(x * x, axis=-1, keepdims=True)
    return x * lax.rsqrt(ms + NORM_EPS)


def _dot(a, b):
    return jnp.dot(a, b, preferred_element_type=F32)


def _dot_nt(a, b):
    return lax.dot_general(a, b, (((1,), (1,)), ((), ())), preferred_element_type=F32)


def _dot_tn(a, b):
    return lax.dot_general(a, b, (((0,), (0,)), ((), ())), preferred_element_type=F32)


def _const_spec(shape):
    nd = len(shape)
    return pl.BlockSpec(shape, lambda *_: (0,) * nd, pipeline_mode=pl.Buffered(1))


def _ffn_kernel(x_ref, npre_ref, npost_ref, win_ref, wout_ref, o_ref, *, sub_rows):
    starts = range(0, x_ref.shape[0], sub_rows)
    normed = [(_rms_scale(x_ref[r0:r0 + sub_rows, :]) * npre_ref[...]).astype(BF16) for r0 in starts]
    for r0, h in zip(starts, normed):
        x = x_ref[r0:r0 + sub_rows, :]
        acc = jnp.zeros(x.shape, F32)
        for c0 in range(0, D_FF, FFN_CHUNK):
            c1 = min(c0 + FFN_CHUNK, D_FF)
            a = _dot(h, win_ref[:, c0:c1])
            b = _dot(h, win_ref[:, D_FF + c0:D_FF + c1])
            g = (a * jax.nn.sigmoid(a) * b).astype(BF16)
            acc = acc + _dot(g, wout_ref[c0:c1, :])
        o_ref[r0:r0 + sub_rows, :] = x + _rms_scale(acc) * npost_ref[...]


def _ffn(x2d, npre, npost, win, wout, layer):
    rows = x2d.shape[0]
    tile = min(FFN_ROW_TILE, rows)
    sub_rows = min(ROW_TILE, tile)
    assert rows % tile == 0 and tile % sub_rows == 0

    def layer_spec(shape):
        return pl.BlockSpec((None,) + shape, lambda i: (layer, 0, 0), pipeline_mode=pl.Buffered(1))

    return pl.pallas_call(
        functools.partial(_ffn_kernel, sub_rows=sub_rows),
        grid=(rows // tile,),
        in_specs=[
            pl.BlockSpec((tile, D_MODEL), lambda i: (i, 0)),
            _const_spec((1, D_MODEL)),
            _const_spec((1, D_MODEL)),
            layer_spec((D_MODEL, 2 * D_FF)),
            layer_spec((D_FF, D_MODEL)),
        ],
        out_specs=pl.BlockSpec((tile, D_MODEL), lambda i: (i, 0)),
        out_shape=jax.ShapeDtypeStruct((rows, D_MODEL), F32),
        compiler_params=pltpu.CompilerParams(
            dimension_semantics=("parallel",), vmem_limit_bytes=VMEM_LIMIT_BYTES),
        name="ffn",
    )(x2d, npre, npost, win, wout)


def _cumsum_rows(x):
    c, w = x.shape
    groups = c // SUBLANES
    y = x.reshape(groups, SUBLANES, w)
    sub = lax.broadcasted_iota(jnp.int32, y.shape, 1)
    shift = 1
    while shift < SUBLANES:
        y = y + jnp.where(sub >= shift, pltpu.roll(y, shift, axis=1), 0.0)
        shift *= 2
    tot = jnp.broadcast_to(y[:, SUBLANES - 1:SUBLANES, :], y.shape)
    inc = tot
    shift = 1
    while shift < groups:
        inc = inc + jnp.concatenate(
            [jnp.zeros((shift, SUBLANES, w), F32), inc[:groups - shift]], axis=0)
        shift *= 2
    return (y + (inc - tot)).reshape(c, w)


def _hgrn_kernel(*refs, nb, tt, chunk, cast_blocks):
    ncast = len(cast_blocks)
    x_ref, st_in_ref, npre_ref, win_ref, alb_ref, gnorm_ref, wout_ref, npost_ref = refs[:8]
    cast_src = refs[8:8 + ncast]
    y_ref, st_ref = refs[8 + ncast:10 + ncast]
    cast_dst = refs[10 + ncast:10 + 2 * ncast]
    qs_ref, lf_ref, inp_ref, v_ref, sg_ref, on_ref = refs[10 + 2 * ncast:]
    t = pl.program_id(1)
    rows = nb * tt

    step = pl.program_id(0) * pl.num_programs(1) + t
    for src_ref, dst_ref, nblocks in zip(cast_src, cast_dst, cast_blocks):
        @pl.when(step < nblocks)
        def _(src_ref=src_ref, dst_ref=dst_ref):
            dst_ref[...] = src_ref[...].astype(BF16)

    @pl.when(t == 0)
    def _():
        for n in range(nb):
            for h in range(A_HEADS):
                st_ref[n, h] = st_in_ref[n, h].T

    x = x_ref[...].reshape(rows, D_MODEL)
    hn = (_rms_scale(x) * npre_ref[...]).astype(BF16)

    alb = alb_ref[...]
    e = jnp.exp(alb - jnp.max(alb, axis=0, keepdims=True))
    lb = e[0:1] / jnp.sum(e, axis=0, keepdims=True)

    f = _dot(hn, win_ref[:, D_MODEL:2 * D_MODEL])
    forget = lb + (1.0 - lb) * jax.nn.sigmoid(f)
    lf_ref[...] = jnp.log(forget)
    inp_ref[...] = 1.0 - forget

    def project_query(c0, width):
        q = _dot(hn, win_ref[:, c0:c0 + width])
        qs_ref[:, c0:c0 + width] = q * jax.nn.sigmoid(q)

    def project_value(c0, width):
        v_ref[:, c0:c0 + width] = _dot(
            hn, win_ref[:, 2 * D_MODEL + c0:2 * D_MODEL + c0 + width]).astype(BF16)

    def project_gate(c0, width):
        g = _dot(hn, win_ref[:, 3 * D_MODEL + c0:3 * D_MODEL + c0 + width])
        sg_ref[:, c0:c0 + width] = g * jax.nn.sigmoid(g)

    def write_rows(mix):
        y = x + _rms_scale(mix) * npost_ref[...]
        y_ref[...] = y.reshape(nb, tt, D_MODEL)

    half = chunk // 2
    ri = lax.broadcasted_iota(jnp.int32, (chunk, chunk), 0)
    ci = lax.broadcasted_iota(jnp.int32, (chunk, chunk), 1)
    causal = ri >= ci

    def prepare(blk, h):
        rs = slice(blk * chunk, (blk + 1) * chunk)
        sl = slice(h * A_DK, (h + 1) * A_DK)
        b = _cumsum_rows(lf_ref[rs, sl])
        bmid = b[half - 1:half]
        blast = b[chunk - 1:chunk]
        qsc = qs_ref[rs, sl] * jnp.exp(b - bmid)
        inp = inp_ref[rs, sl] * jnp.exp(bmid - b)
        return dict(
            rs=rs, sl=sl, qt=qsc.astype(BF16), kt=inp.astype(BF16),
            qi=(qsc * jnp.exp(bmid)).astype(BF16),
            kh=(inp * jnp.exp(blast - bmid)).astype(BF16),
            vh=v_ref[rs, sl], decay=jnp.exp(blast))

    def emit_output(u, sc, st):
        p = jnp.where(causal, sc, 0.0).astype(BF16)
        o = _dot(p, u["vh"]) + _dot_nt(u["qi"], st.astype(BF16))
        on = _rms_scale(o) * gnorm_ref[:, u["sl"]] * sg_ref[u["rs"], u["sl"]]
        on_ref[u["rs"], u["sl"]] = on.astype(BF16)

    def recurrence(heads, jobs):
        nstages = tt // chunk
        slots = 2 * nstages
        jobs = list(jobs)

        def run_jobs(slot):
            for _ in range(-(-len(jobs) // (slots - slot))):
                jobs.pop(0)()

        states = {}
        for stage in range(nstages):
            blocks = [n * nstages + stage for n in range(nb)]
            units = [(blk, h, prepare(blk, h)) for blk in blocks for h in heads]
            scores = [_dot_nt(u["qt"], u["kt"]) for _, _, u in units]
            run_jobs(2 * stage)
            updated = {}
            for blk, h, u in units:
                key = ((blk * chunk) // tt, h)
                if key not in states:
                    states[key] = st_ref[key[0], h]
                updated[key] = states[key] * u["decay"] + _dot_tn(u["vh"], u["kh"])
            run_jobs(2 * stage + 1)
            for (blk, h, u), sc in zip(units, scores):
                emit_output(u, sc, states[((blk * chunk) // tt, h)])
            states.update(updated)
        for (n, h), st in states.items():
            st_ref[n, h] = st

    def factorised_path():
        heads_per_group = A_HEADS // HEAD_GROUPS
        piece = 2 * LANES
        partial_mix = [[] for _ in range(HEAD_GROUPS)]

        def project_out(g, c0, width):
            partial_mix[g].append(
                _dot(on_ref[:, g * gw:(g + 1) * gw], wout_ref[g * gw:(g + 1) * gw, c0:c0 + width]))

        for g in range(HEAD_GROUPS):
            jobs = []
            if g + 1 < HEAD_GROUPS:
                jobs += [functools.partial(proj, (g + 1) * gw + c0, piece)
                         for proj in (project_query, project_value, project_gate)
                         for c0 in range(0, gw, piece)]
            if g > 0:
                jobs += [functools.partial(project_out, g - 1, c0, piece)
                         for c0 in range(0, D_MODEL, piece)]
            recurrence(range(g * heads_per_group, (g + 1) * heads_per_group), jobs)
        project_out(HEAD_GROUPS - 1, 0, D_MODEL)
        write_rows(functools.reduce(
            jnp.add, [jnp.concatenate(parts, axis=1) for parts in partial_mix]))

    def exact_step(idx, carry):
        n = (idx * EXACT_BLOCK) // tt
        rs = pl.ds(pl.multiple_of(idx * EXACT_BLOCK, EXACT_BLOCK), EXACT_BLOCK)
        row = lax.broadcasted_iota(jnp.int32, (EXACT_BLOCK, A_DK), 0)
        for h in range(A_HEADS):
            sl = slice(h * A_DK, (h + 1) * A_DK)
            b = _cumsum_rows(lf_ref[rs, sl])
            blast = b[EXACT_BLOCK - 1:EXACT_BLOCK]
            qs = qs_ref[rs, sl]
            inp = inp_ref[rs, sl]
            vh = v_ref[rs, sl]
            vf = vh.astype(F32)
            st = st_ref[n, h]
            o = _dot_nt((qs * jnp.exp(b)).astype(BF16), st.astype(BF16))
            for s in range(EXACT_BLOCK):
                decay = jnp.exp(jnp.where(row >= s, b - b[s:s + 1], -jnp.inf))
                score = jnp.sum(qs * decay * inp[s:s + 1], axis=-1, keepdims=True)
                o = o + score * vf[s:s + 1]
            kh = (inp * jnp.exp(blast - b)).astype(BF16)
            st_ref[n, h] = st * jnp.exp(blast) + _dot_tn(vh, kh)
            on = _rms_scale(o) * gnorm_ref[:, sl] * sg_ref[rs, sl]
            on_ref[rs, sl] = on.astype(BF16)
        return carry

    gw = D_MODEL // HEAD_GROUPS
    project_query(0, gw)
    project_value(0, gw)
    project_gate(0, gw)

    max_step = -jnp.min(lf_ref[...])
    bounded = max_step * half <= MAX_FACTOR_EXPONENT

    pl.when(bounded)(factorised_path)

    @pl.when(jnp.logical_not(bounded))
    def _():
        project_query(gw, D_MODEL - gw)
        project_value(gw, D_MODEL - gw)
        project_gate(gw, D_MODEL - gw)
        lax.fori_loop(0, rows // EXACT_BLOCK, exact_step, 0)
        write_rows(_dot(on_ref[...], wout_ref[...]))

    @pl.when(t == pl.num_programs(1) - 1)
    def _():
        for n in range(nb):
            for h in range(A_HEADS):
                st_ref[n, h] = st_ref[n, h].T


def _hgrn_mixer(x, state, npre, win, alb, gnorm, wout, npost, *, nb, tt, chunk, casts=()):
    batch, seq, _ = x.shape
    assert batch % nb == 0 and seq % tt == 0 and tt % chunk == 0 and chunk % (2 * SUBLANES) == 0
    rows = nb * tt
    nt = seq // tt
    nsteps = (batch // nb) * nt
    cast_specs, cast_blocks = [], []
    for a in casts:
        block_rows = next(r for r in range(2 * SUBLANES, a.shape[0] + 1, 2 * SUBLANES)
                          if a.shape[0] % r == 0 and a.shape[0] // r <= nsteps)
        nblocks = a.shape[0] // block_rows
        cast_blocks.append(nblocks)
        cast_specs.append(pl.BlockSpec(
            (block_rows, a.shape[1]),
            lambda b, t, nblocks=nblocks: (jnp.minimum(b * nt + t, nblocks - 1), 0)))
    kern = functools.partial(_hgrn_kernel, nb=nb, tt=tt, chunk=chunk, cast_blocks=tuple(cast_blocks))
    st_spec = pl.BlockSpec((nb, A_HEADS, A_DK, A_DV), lambda b, t: (b, 0, 0, 0))
    return pl.pallas_call(
        kern,
        grid=(batch // nb, nt),
        in_specs=[
            pl.BlockSpec((nb, tt, D_MODEL), lambda b, t: (b, t, 0)),
            st_spec,
            _const_spec((1, D_MODEL)),
            _const_spec((D_MODEL, 4 * D_MODEL)),
            _const_spec(alb.shape),
            _const_spec((1, D_MODEL)),
            _const_spec((D_MODEL, D_MODEL)),
            _const_spec((1, D_MODEL)),
        ] + cast_specs,
        out_specs=[
            pl.BlockSpec((nb, tt, D_MODEL), lambda b, t: (b, t, 0)),
            st_spec,
        ] + cast_specs,
        out_shape=[
            jax.ShapeDtypeStruct(x.shape, F32),
            jax.ShapeDtypeStruct(state.shape, F32),
        ] + [jax.ShapeDtypeStruct(a.shape, BF16) for a in casts],
        scratch_shapes=[
            pltpu.VMEM((rows, D_MODEL), F32),
            pltpu.VMEM((rows, D_MODEL), F32),
            pltpu.VMEM((rows, D_MODEL), F32),
            pltpu.VMEM((rows, D_MODEL), BF16),
            pltpu.VMEM((rows, D_MODEL), F32),
            pltpu.VMEM((rows, D_MODEL), BF16),
        ],
        compiler_params=pltpu.CompilerParams(
            dimension_semantics=("arbitrary", "arbitrary"), vmem_limit_bytes=VMEM_LIMIT_BYTES),
        name="hgrn_mixer",
    )(x, state, npre, win, alb, gnorm, wout, npost, *casts)


def _rope_cols(x, cos_t, sin_t):
    lane = lax.broadcasted_iota(jnp.int32, (x.shape[0], LANES), 1)
    first_half = (lane % B_HEAD_DIM) < (B_HEAD_DIM // 2)
    cols = []
    for c0 in range(0, x.shape[1], LANES):
        xc = x[:, c0:c0 + LANES]
        partner = jnp.where(first_half,
                            pltpu.roll(xc, LANES - B_HEAD_DIM // 2, axis=1),
                            pltpu.roll(xc, B_HEAD_DIM // 2, axis=1))
        cols.append(xc * cos_t + partner * sin_t)
    return jnp.concatenate(cols, axis=1)


def _attn_cached_kernel(x_ref, kc_in_ref, vc_in_ref, cosq_ref, sinq_ref, cosk_ref, sinkt_ref,
                        sinks_ref, npre_ref, kvn_ref, wq_ref, wkv_ref, wo_ref, npost_ref,
                        y_ref, kc_ref, vc_ref, q_ref, kl_ref, kh_ref, vl_ref, ot_ref, *, nb, tt):
    rows = nb * tt
    ext = WINDOW + tt

    x = x_ref[...].reshape(rows, D_MODEL)
    xs = _rms_scale(x)
    hq = (xs * npre_ref[...]).astype(BF16)
    hk = (xs * kvn_ref[...]).astype(BF16)

    def per_stream(tab_ref):
        tab = tab_ref[...]
        return tab if nb == 1 else jnp.concatenate([tab] * nb, axis=0)

    q = _dot(hq, wq_ref[...])
    q_ref[...] = _rope_cols(q, per_stream(cosq_ref), per_stream(sinq_ref)).astype(BF16)
    kv = _dot(hk, wkv_ref[...])
    k_new = _rope_cols(kv[:, :KV_WIDTH], per_stream(cosk_ref), per_stream(sinkt_ref))
    v_new = kv[:, KV_WIDTH:]

    low = lax.broadcasted_iota(jnp.int32, (ext, LANES), 1) < B_HEAD_DIM
    for n in range(nb):
        k_ext = jnp.concatenate([kc_in_ref[n], k_new[n * tt:(n + 1) * tt]], axis=0)
        v_ext = jnp.concatenate([vc_in_ref[n], v_new[n * tt:(n + 1) * tt]], axis=0)
        kc_ref[n] = k_ext[ext - WINDOW:]
        vc_ref[n] = v_ext[ext - WINDOW:]
        for m in range(KV_WIDTH // LANES):
            ka = k_ext[:, m * LANES:(m + 1) * LANES]
            kr = pltpu.roll(ka, B_HEAD_DIM, axis=1)
            va = v_ext[:, m * LANES:(m + 1) * LANES]
            vr = pltpu.roll(va, B_HEAD_DIM, axis=1)
            kl_ref[2 * m, n] = jnp.where(low, ka, 0.0).astype(BF16)
            kh_ref[2 * m, n] = jnp.where(low, 0.0, kr).astype(BF16)
            kl_ref[2 * m + 1, n] = jnp.where(low, kr, 0.0).astype(BF16)
            kh_ref[2 * m + 1, n] = jnp.where(low, 0.0, ka).astype(BF16)
            vl_ref[2 * m, n] = jnp.where(low, va, 0.0).astype(BF16)
            vl_ref[2 * m + 1, n] = jnp.where(low, vr, 0.0).astype(BF16)

    first_head = lax.broadcasted_iota(jnp.int32, (1, 2 * tt), 1) < tt
    pairs = [(j, kx_ref, ha, hb) for j in range(B_KV_HEADS)
             for kx_ref, ha, hb in ((kl_ref, 4 * j, 4 * j + 2), (kh_ref, 4 * j + 1, 4 * j + 3))]

    def scores(n):
        rs = slice(n * tt, (n + 1) * tt)
        out = []
        for j, kx_ref, _, _ in pairs:
            qq = jnp.concatenate([q_ref[rs, 2 * j * LANES:(2 * j + 1) * LANES],
                                  q_ref[rs, (2 * j + 1) * LANES:(2 * j + 2) * LANES]], axis=0)
            out.append(_dot_nt(kx_ref[j, n], qq))
        return out

    def softmax(s_list):
        out = []
        for (_, _, ha, hb), s in zip(pairs, s_list):
            sink = jnp.where(first_head, sinks_ref[ha], sinks_ref[hb])
            mx = jnp.maximum(jnp.max(s, axis=0, keepdims=True), sink)
            p = jnp.exp(s - mx)
            den = jnp.sum(p, axis=0, keepdims=True) + jnp.exp(sink - mx)
            out.append((p.astype(BF16), den))
        return out

    def weighted_values(n, pd_list):
        cs = slice(n * tt, (n + 1) * tt)
        for (j, _, ha, hb), (p, den) in zip(pairs, pd_list):
            o = _dot_tn(vl_ref[j, n], p)[:B_HEAD_DIM]
            o = (o * (1.0 / den)).astype(BF16)
            ot_ref[ha * B_HEAD_DIM:(ha + 1) * B_HEAD_DIM, cs] = o[:, :tt]
            ot_ref[hb * B_HEAD_DIM:(hb + 1) * B_HEAD_DIM, cs] = o[:, tt:]

    s_next = scores(0)
    for n in range(nb):
        s_cur = s_next
        if n + 1 < nb:
            s_next = scores(n + 1)
        weighted_values(n, softmax(s_cur))

    mix = _dot_tn(ot_ref[...], wo_ref[...])
    y = x + _rms_scale(mix) * npost_ref[...]
    y_ref[...] = y.reshape(nb, tt, D_MODEL)


def _attn_cached_mixer(x, cache_k, cache_v, tables, sinks, npre, kvn, wq, wkv, wo, npost, *, nb):
    batch, tt, _ = x.shape
    assert batch % nb == 0 and tt % (2 * SUBLANES) == 0
    rows = nb * tt
    ext = WINDOW + tt
    kern = functools.partial(_attn_cached_kernel, nb=nb, tt=tt)
    x_spec = pl.BlockSpec((nb, tt, D_MODEL), lambda b: (b, 0, 0))
    c_spec = pl.BlockSpec((nb, WINDOW, KV_WIDTH), lambda b: (b, 0, 0))
    tab_spec = _const_spec((tt, LANES))
    cache_shape = jax.ShapeDtypeStruct((batch, WINDOW, KV_WIDTH), F32)
    return pl.pallas_call(
        kern,
        grid=(batch // nb,),
        in_specs=[x_spec, c_spec, c_spec] + [tab_spec] * 4 + [
            pl.BlockSpec(memory_space=pltpu.SMEM),
            _const_spec((1, D_MODEL)),
            _const_spec((1, D_MODEL)),
            _const_spec((D_MODEL, D_MODEL)),
            _const_spec((D_MODEL, 2 * KV_WIDTH)),
            _const_spec((D_MODEL, D_MODEL)),
            _const_spec((1, D_MODEL)),
        ],
        out_specs=[x_spec, c_spec, c_spec],
        out_shape=[jax.ShapeDtypeStruct(x.shape, F32), cache_shape, cache_shape],
        scratch_shapes=[
            pltpu.VMEM((rows, D_MODEL), BF16),
            pltpu.VMEM((B_KV_HEADS, nb, ext, LANES), BF16),
            pltpu.VMEM((B_KV_HEADS, nb, ext, LANES), BF16),
            pltpu.VMEM((B_KV_HEADS, nb, ext, LANES), BF16),
            pltpu.VMEM((D_MODEL, rows), BF16),
        ],
        compiler_params=pltpu.CompilerParams(
            dimension_semantics=("parallel",), vmem_limit_bytes=VMEM_LIMIT_BYTES),
        name="attn_cached_mixer",
    )(x, cache_k, cache_v, *tables, sinks, npre, kvn, wq, wkv, wo, npost)


def _ffn_steps(src_ref, dst_ref, npre_ref, npost_ref, win_ref, wout_ref):
    state = {}

    def chunk(c0):
        if not state:
            state["h"] = (_rms_scale(src_ref[...]) * npre_ref[...]).astype(BF16)
            state["acc"] = jnp.zeros(src_ref.shape, F32)
        h = state["h"]
        c1 = min(c0 + FFN_CHUNK, D_FF)
        a = _dot(h, win_ref[:, c0:c1])
        b = _dot(h, win_ref[:, D_FF + c0:D_FF + c1])
        g = (a * jax.nn.sigmoid(a) * b).astype(BF16)
        state["acc"] = state["acc"] + _dot(g, wout_ref[c0:c1, :])
        if c1 == D_FF:
            dst_ref[...] = (src_ref[...] + _rms_scale(state["acc"]) * npost_ref[...]).reshape(dst_ref.shape)

    return [functools.partial(chunk, c0) for c0 in range(0, D_FF, FFN_CHUNK)]


def _attn_pair_kernel(x_ref, cosq_ref, sinq_ref, cosk_ref, sinkt_ref, sinks_ref,
                      npre_ref, kvn_ref, wq_ref, wkv_ref, wo_ref, npost_ref,
                      fpre_ref, fpost_ref, win_ref, wout_ref,
                      y_ref, kc_ref, vc_ref, q_ref, kl_ref, kh_ref, vt_ref, ot_ref, mid_ref, *, tt):
    t = pl.program_id(1)
    last = pl.num_programs(1) - 1

    @pl.when((pl.program_id(0) == 0) & (t == 0))
    def _():
        mid_ref[...] = jnp.zeros(mid_ref.shape, F32)

    @pl.when(t < last)
    def _():
        jobs = _ffn_steps(mid_ref, y_ref, fpre_ref, fpost_ref, win_ref, wout_ref)
        _attention_tile(t, tt, x_ref, cosq_ref, sinq_ref, cosk_ref, sinkt_ref, sinks_ref,
                        npre_ref, kvn_ref, wq_ref, wkv_ref, wo_ref, npost_ref, kc_ref, vc_ref,
                        q_ref, kl_ref, kh_ref, vt_ref, ot_ref, jobs, mid_ref)

    @pl.when(t == last)
    def _():
        for job in _ffn_steps(mid_ref, y_ref, fpre_ref, fpost_ref, win_ref, wout_ref):
            job()


def _attention_tile(t, tt, x_ref, cosq_ref, sinq_ref, cosk_ref, sinkt_ref, sinks_ref,
                    npre_ref, kvn_ref, wq_ref, wkv_ref, wo_ref, npost_ref, kc_ref, vc_ref,
                    q_ref, kl_ref, kh_ref, vt_ref, ot_ref, jobs, out_ref):
    ext = WINDOW + tt
    pair = 2 * CHUNK
    nkeys = WINDOW + pair
    nvis = WINDOW + CHUNK
    jobs = list(jobs)
    njobs = len(jobs)

    def run_jobs(share):
        for _ in range(min(round(share * njobs), len(jobs))):
            jobs.pop(0)()

    @pl.when(t == 0)
    def _():
        kc_ref[...] = jnp.zeros(kc_ref.shape, F32)
        vc_ref[...] = jnp.zeros(vc_ref.shape, F32)

    x = x_ref[0]
    xs = _rms_scale(x)
    hq = (xs * npre_ref[...]).astype(BF16)
    hk = (xs * kvn_ref[...]).astype(BF16)

    run_jobs(JOB_SHARES[0])
    q = _dot(hq, wq_ref[...])
    q_ref[...] = _rope_cols(q, cosq_ref[...], sinq_ref[...]).astype(BF16)
    run_jobs(JOB_SHARES[1])
    kv = _dot(hk, wkv_ref[...])
    run_jobs(JOB_SHARES[2])
    k_ext = jnp.concatenate(
        [kc_ref[0], _rope_cols(kv[:, :KV_WIDTH], cosk_ref[...], sinkt_ref[...])], axis=0)
    v_ext = jnp.concatenate([vc_ref[0], kv[:, KV_WIDTH:]], axis=0)
    kc_ref[0] = k_ext[ext - WINDOW:]
    vc_ref[0] = v_ext[ext - WINDOW:]
    vt_ref[...] = v_ext.T.astype(BF16)

    low = lax.broadcasted_iota(jnp.int32, (ext, LANES), 1) < B_HEAD_DIM
    for m in range(KV_WIDTH // LANES):
        ka = k_ext[:, m * LANES:(m + 1) * LANES]
        kr = pltpu.roll(ka, B_HEAD_DIM, axis=1)
        kl_ref[2 * m] = jnp.where(low, ka, 0.0).astype(BF16)
        kh_ref[2 * m] = jnp.where(low, 0.0, kr).astype(BF16)
        kl_ref[2 * m + 1] = jnp.where(low, kr, 0.0).astype(BF16)
        kh_ref[2 * m + 1] = jnp.where(low, 0.0, ka).astype(BF16)

    vis_row = lax.broadcasted_iota(jnp.int32, (nvis, LANES), 0)
    first_bias = [jnp.where(vis_row + (t * tt - WINDOW + qc * CHUNK) >= 0, 0.0, -jnp.inf)
                  for qc in range(2)]
    first_head = lax.broadcasted_iota(jnp.int32, (1, LANES), 1) < CHUNK
    no_keys = jnp.zeros((CHUNK, LANES), BF16)

    units = [(j, kx_ref, ha, hb) for j in range(B_KV_HEADS)
             for kx_ref, ha, hb in ((kl_ref, 4 * j, 4 * j + 2), (kh_ref, 4 * j + 1, 4 * j + 3))]

    def scores(e):
        r0 = e * pair
        out = []
        for j, kx_ref, _, _ in units:
            ca = slice(2 * j * LANES, (2 * j + 1) * LANES)
            cb = slice((2 * j + 1) * LANES, (2 * j + 2) * LANES)
            qq = jnp.concatenate([q_ref[r0:r0 + CHUNK, ca], q_ref[r0:r0 + CHUNK, cb],
                                  q_ref[r0 + CHUNK:r0 + pair, ca], q_ref[r0 + CHUNK:r0 + pair, cb]], axis=0)
            out.append(_dot_nt(kx_ref[j, r0:r0 + nkeys, :], qq))
        return out

    def softmax(e, s_list):
        out = []
        for (_, _, ha, hb), s in zip(units, s_list):
            sink = jnp.where(first_head, sinks_ref[ha], sinks_ref[hb]) * LOG2_E
            ps, dens = [], []
            for qc in range(2):
                sq = s[qc * CHUNK:qc * CHUNK + nvis, qc * LANES:(qc + 1) * LANES]
                if e == 0:
                    sq = sq + first_bias[qc]
                mx = jnp.maximum(jnp.max(sq, axis=0, keepdims=True), sink)
                p = jnp.exp2(sq - mx)
                dens.append(jnp.sum(p, axis=0, keepdims=True) + jnp.exp2(sink - mx))
                ps.append(p.astype(BF16))
            pmat = jnp.concatenate([jnp.concatenate([ps[0], no_keys], axis=0),
                                    jnp.concatenate([no_keys, ps[1]], axis=0)], axis=1)
            out.append((pmat, jnp.concatenate(dens, axis=1)))
        return out

    def weighted_values(e, pd_list):
        r0 = e * pair
        for (j, _, ha, hb), (pmat, den) in zip(units, pd_list):
            o = _dot(vt_ref[j * B_HEAD_DIM:(j + 1) * B_HEAD_DIM, r0:r0 + nkeys], pmat)
            o = (o * (1.0 / den)).astype(BF16)
            for qc in range(2):
                c0 = r0 + qc * CHUNK
                ot_ref[ha * B_HEAD_DIM:(ha + 1) * B_HEAD_DIM, c0:c0 + CHUNK] = (
                    o[:, qc * LANES:qc * LANES + CHUNK])
                ot_ref[hb * B_HEAD_DIM:(hb + 1) * B_HEAD_DIM, c0:c0 + CHUNK] = (
                    o[:, qc * LANES + CHUNK:(qc + 1) * LANES])

    npairs = tt // pair
    s_next = scores(0)
    for e in range(npairs):
        s_cur = s_next
        if e + 1 < npairs:
            s_next = scores(e + 1)
        run_jobs(JOB_SHARES[3])
        weighted_values(e, softmax(e, s_cur))

    run_jobs(JOB_SHARES[4])
    mix = _dot_tn(ot_ref[...], wo_ref[...])
    run_jobs(1.0)
    out_ref[...] = x + _rms_scale(mix) * npost_ref[...]


def _attn_pair_mixer(x, tables, sinks, npre, kvn, wq, wkv, wo, npost, *, tt, ffn):
    batch, seq, _ = x.shape
    assert seq % tt == 0 and tt % (2 * CHUNK) == 0 and WINDOW == 2 * CHUNK
    ext = WINDOW + tt
    nt = seq // tt
    kern = functools.partial(_attn_pair_kernel, tt=tt)
    in_spec = pl.BlockSpec((1, tt, D_MODEL), lambda b, t: (b, jnp.minimum(t, nt - 1), 0))
    out_spec = pl.BlockSpec((1, tt, D_MODEL), lambda b, t: (b, jnp.maximum(t - 1, 0), 0))
    tab_spec = pl.BlockSpec((tt, LANES), lambda b, t: (jnp.minimum(t, nt - 1), 0))
    c_spec = pl.BlockSpec((1, WINDOW, KV_WIDTH), lambda b, t: (b, 0, 0))
    cache_shape = jax.ShapeDtypeStruct((batch, WINDOW, KV_WIDTH), F32)
    in_specs = [in_spec] + [tab_spec] * 4 + [
        pl.BlockSpec(memory_space=pltpu.SMEM),
        _const_spec((1, D_MODEL)),
        _const_spec((1, D_MODEL)),
        _const_spec((D_MODEL, D_MODEL)),
        _const_spec((D_MODEL, 2 * KV_WIDTH)),
        _const_spec((D_MODEL, D_MODEL)),
        _const_spec((1, D_MODEL)),
    ]
    args = [x, *tables, sinks, npre, kvn, wq, wkv, wo, npost]
    scratch = [
        pltpu.VMEM((tt, D_MODEL), BF16),
        pltpu.VMEM((B_KV_HEADS, ext, LANES), BF16),
        pltpu.VMEM((B_KV_HEADS, ext, LANES), BF16),
        pltpu.VMEM((KV_WIDTH, ext), BF16),
        pltpu.VMEM((D_MODEL, tt), BF16),
    ]
    fpre, fpost, win, wout, layer = ffn

    def layer_spec(shape):
        return pl.BlockSpec((None,) + shape, lambda b, t: (layer, 0, 0),
                            pipeline_mode=pl.Buffered(1))

    in_specs += [_const_spec((1, D_MODEL)), _const_spec((1, D_MODEL)),
                 layer_spec((D_MODEL, 2 * D_FF)), layer_spec((D_FF, D_MODEL))]
    args += [fpre, fpost, win, wout]
    scratch.append(pltpu.VMEM((tt, D_MODEL), F32))
    return pl.pallas_call(
        kern,
        grid=(batch, nt + 1),
        in_specs=in_specs,
        out_specs=[out_spec, c_spec, c_spec],
        out_shape=[jax.ShapeDtypeStruct(x.shape, F32), cache_shape, cache_shape],
        scratch_shapes=scratch,
        compiler_params=pltpu.CompilerParams(
            dimension_semantics=("arbitrary", "arbitrary"), vmem_limit_bytes=VMEM_LIMIT_BYTES),
        name="attn_pair_mixer",
    )(*args)


def _rope_tables(pos, q_scale):
    half = B_HEAD_DIM // 2
    inv = ROPE_THETA ** (-jnp.arange(half, dtype=F32) / half)
    ang = pos.astype(F32)[:, None] * inv[None, :]
    cos = jnp.cos(ang)
    sin = jnp.sin(ang)
    reps = LANES // B_HEAD_DIM
    cos_t = jnp.tile(jnp.concatenate([cos, cos], axis=1), (1, reps))
    sin_t = jnp.tile(jnp.concatenate([-sin, sin], axis=1), (1, reps))
    return cos_t * q_scale, sin_t * q_scale, cos_t, sin_t


def _trunk(x, pos, state, cache_k, cache_v, w, *, nb, tt, hgrn_chunk, cq):
    batch, seq, _ = x.shape
    row = lambda a: a.reshape(1, D_MODEL)
    casts = ()
    if w["w_ffn_in"].dtype == F32:
        casts = (w["w_ffn_in"].reshape(-1, 2 * D_FF), w["w_ffn_out"].reshape(-1, D_MODEL))
    x, st, *cast_out = _hgrn_mixer(
        x, state, row(w["norm_mix_pre"][0]), w["w_a_in"], w["a_lower_bound"],
        row(w["a_out_norm"]), w["w_a_out"], row(w["norm_mix_post"][0]),
        nb=nb, tt=tt, chunk=hgrn_chunk, casts=casts)
    if cast_out:
        w = dict(w, w_ffn_in=cast_out[0].reshape(w["w_ffn_in"].shape),
                 w_ffn_out=cast_out[1].reshape(w["w_ffn_out"].shape))
    x = _ffn(x.reshape(batch * seq, D_MODEL), row(w["norm_ffn_pre"][0]), row(w["norm_ffn_post"][0]),
             w["w_ffn_in"], w["w_ffn_out"], 0).reshape(batch, seq, D_MODEL)
    attn_w = (w["b_sinks"], row(w["norm_mix_pre"][1]), row(w["kv_norm"]), w["w_b_q"], w["w_kv"],
              w["w_b_out"], row(w["norm_mix_post"][1]))
    ffn1 = (row(w["norm_ffn_pre"][1]), row(w["norm_ffn_post"][1]), w["w_ffn_in"], w["w_ffn_out"], 1)
    if cache_k is None:
        assert nb == 1 and cq == CHUNK
        x, kc, vc = _attn_pair_mixer(x, _rope_tables(pos, SOFTMAX_SCALE * LOG2_E), *attn_w, tt=tt,
                                     ffn=ffn1)
    else:
        assert tt == seq and cq == seq
        x, kc, vc = _attn_cached_mixer(x, cache_k, cache_v, _rope_tables(pos, SOFTMAX_SCALE),
                                       *attn_w, nb=nb)
        x = _ffn(x.reshape(batch * seq, D_MODEL), *ffn1).reshape(batch, seq, D_MODEL)
    return x, st, kc, vc, w


def kernel(x_prompt, x_sample, state_hgrn, cache_k, cache_v, norm_mix_pre, norm_mix_post, norm_ffn_pre, norm_ffn_post, w_ffn_in, w_ffn_out, w_a_in, a_lower_bound, a_out_norm, w_a_out, kv_norm, w_kv, w_b_q, b_sinks, w_b_out):
    w = dict(
        norm_mix_pre=norm_mix_pre, norm_mix_post=norm_mix_post,
        norm_ffn_pre=norm_ffn_pre, norm_ffn_post=norm_ffn_post,
        w_ffn_in=w_ffn_in, w_ffn_out=w_ffn_out,
        w_a_in=w_a_in[0].astype(BF16), a_lower_bound=a_lower_bound,
        a_out_norm=a_out_norm[0], w_a_out=w_a_out[0].astype(BF16),
        kv_norm=kv_norm, w_kv=w_kv.astype(BF16), w_b_q=w_b_q[0].astype(BF16),
        b_sinks=b_sinks[0], w_b_out=w_b_out[0].astype(BF16),
    )
    bp, tp, _ = x_prompt.shape
    bs, ts, _ = x_sample.shape

    zero_state = jnp.zeros((bp, A_HEADS, A_DK, A_DV), F32)
    y_p, st_p, kc_p, vc_p, w = _trunk(
        x_prompt, jnp.arange(tp), zero_state, None, None, w,
        nb=1, tt=ROW_TILE, hgrn_chunk=HGRN_CHUNK, cq=CHUNK)

    nb_s = ROW_TILE // (2 * ts)
    y_s, st_s, kc_s, vc_s, _ = _trunk(
        x_sample, PAST_LEN + jnp.arange(ts), state_hgrn[0],
        cache_k.reshape(bs, WINDOW, KV_WIDTH), cache_v.reshape(bs, WINDOW, KV_WIDTH), w,
        nb=nb_s, tt=ts, hgrn_chunk=ts, cq=ts)

    cache4 = lambda a: a.reshape(a.shape[0], WINDOW, B_KV_HEADS, B_HEAD_DIM)
    return (y_p, y_s, st_p[None], st_s[None],
            cache4(kc_p), cache4(vc_p), cache4(kc_s), cache4(vc_s))
```

```python
import functools

import jax
import jax.numpy as jnp
from jax import lax
from jax.experimental import pallas as pl
from jax.experimental.pallas import tpu as pltpu

F32 = jnp.float32
BF16 = jnp.bfloat16

D_MODEL = 1024
A_HEADS = 8
A_DK = 128
A_DV = 128
B_HEAD_DIM = 64
B_Q_HEADS = 16
B_KV_HEADS = 4
KV_WIDTH = B_KV_HEADS * B_HEAD_DIM
WINDOW = 128
CHUNK = 64
PAST_LEN = 2048
D_FF = 2816
ROPE_THETA = 10000.0
NORM_EPS = 1e-6
SOFTMAX_SCALE = B_HEAD_DIM ** -0.5
LOG2_E = 1.4426950408889634

LANES = 128
SUBLANES = 8
VMEM_LIMIT_BYTES = 52 * 1024 * 1024

ROW_TILE = 512
FFN_ROW_TILE = 1024
HGRN_CHUNK = 128
FFN_CHUNK = 256
MAX_FACTOR_EXPONENT = 60.0
EXACT_BLOCK = 16
HEAD_GROUPS = 2
LATER_WEIGHTS = ("w_ffn_in", "w_ffn_out", "w_b_q", "w_kv", "w_b_out")
JOB_SHARES = (0.18, 0.09, 0.09, 0.09, 0.09)


def _rms_scale(x):
    ms = jnp.mean(x * x, axis=-1, keepdims=True)
    return x * lax.rsqrt(ms + NORM_EPS)


def _dot(a, b):
    return jnp.dot(a, b, preferred_element_type=F32)


def _dot_nt(a, b):
    return lax.dot_general(a, b, (((1,), (1,)), ((), ())), preferred_element_type=F32)


def _dot_tn(a, b):
    return lax.dot_general(a, b, (((0,), (0,)), ((), ())), preferred_element_type=F32)


def _const_spec(shape):
    nd = len(shape)
    return pl.BlockSpec(shape, lambda *_: (0,) * nd, pipeline_mode=pl.Buffered(1))


def _ffn_kernel(x_ref, npre_ref, npost_ref, win_ref, wout_ref, o_ref, *, sub_rows):
    starts = range(0, x_ref.shape[0], sub_rows)
    normed = [(_rms_scale(x_ref[r0:r0 + sub_rows, :]) * npre_ref[...]).astype(BF16) for r0 in starts]
    for r0, h in zip(starts, normed):
        x = x_ref[r0:r0 + sub_rows, :]
        acc = jnp.zeros(x.shape, F32)
        for c0 in range(0, D_FF, FFN_CHUNK):
            c1 = min(c0 + FFN_CHUNK, D_FF)
            a = _dot(h, win_ref[:, c0:c1])
            b = _dot(h, win_ref[:, D_FF + c0:D_FF + c1])
            g = (a * jax.nn.sigmoid(a) * b).astype(BF16)
            acc = acc + _dot(g, wout_ref[c0:c1, :])
        o_ref[r0:r0 + sub_rows, :] = x + _rms_scale(acc) * npost_ref[...]


def _ffn(x2d, npre, npost, win, wout, layer):
    rows = x2d.shape[0]
    tile = min(FFN_ROW_TILE, rows)
    sub_rows = min(ROW_TILE, tile)
    assert rows % tile == 0 and tile % sub_rows == 0

    def layer_spec(shape):
        return pl.BlockSpec((None,) + shape, lambda i: (layer, 0, 0), pipeline_mode=pl.Buffered(1))

    return pl.pallas_call(
        functools.partial(_ffn_kernel, sub_rows=sub_rows),
        grid=(rows // tile,),
        in_specs=[
            pl.BlockSpec((tile, D_MODEL), lambda i: (i, 0)),
            _const_spec((1, D_MODEL)),
            _const_spec((1, D_MODEL)),
            layer_spec((D_MODEL, 2 * D_FF)),
            layer_spec((D_FF, D_MODEL)),
        ],
        out_specs=pl.BlockSpec((tile, D_MODEL), lambda i: (i, 0)),
        out_shape=jax.ShapeDtypeStruct((rows, D_MODEL), F32),
        compiler_params=pltpu.CompilerParams(
            dimension_semantics=("parallel",), vmem_limit_bytes=VMEM_LIMIT_BYTES),
        name="ffn",
    )(x2d, npre, npost, win, wout)


def _cumsum_rows(x):
    c, w = x.shape
    groups = c // SUBLANES
    y = x.reshape(groups, SUBLANES, w)
    sub = lax.broadcasted_iota(jnp.int32, y.shape, 1)
    shift = 1
    while shift < SUBLANES:
        y = y + jnp.where(sub >= shift, pltpu.roll(y, shift, axis=1), 0.0)
        shift *= 2
    tot = jnp.broadcast_to(y[:, SUBLANES - 1:SUBLANES, :], y.shape)
    inc = tot
    shift = 1
    while shift < groups:
        inc = inc + jnp.concatenate(
            [jnp.zeros((shift, SUBLANES, w), F32), inc[:groups - shift]], axis=0)
        shift *= 2
    return (y + (inc - tot)).reshape(c, w)


def _hgrn_kernel(*refs, nb, tt, chunk, cast_blocks):
    ncast = len(cast_blocks)
    x_ref, st_in_ref, npre_ref, win_ref, alb_ref, gnorm_ref, wout_ref, npost_ref = refs[:8]
    cast_src = refs[8:8 + ncast]
    y_ref, st_ref = refs[8 + ncast:10 + ncast]
    cast_dst = refs[10 + ncast:10 + 2 * ncast]
    qs_ref, lf_ref, inp_ref, v_ref, sg_ref, on_ref = refs[10 + 2 * ncast:]
    t = pl.program_id(1)
    rows = nb * tt

    step = pl.program_id(0) * pl.num_programs(1) + t
    for src_ref, dst_ref, nblocks in zip(cast_src, cast_dst, cast_blocks):
        if nblocks is None:
            dst_ref[...] = src_ref[...].astype(BF16)
        else:
            @pl.when(step < nblocks)
            def _(src_ref=src_ref, dst_ref=dst_ref):
                dst_ref[...] = src_ref[...].astype(BF16)

    @pl.when(t == 0)
    def _():
        for n in range(nb):
            for h in range(A_HEADS):
                st_ref[n, h] = st_in_ref[n, h].T

    x = x_ref[...].reshape(rows, D_MODEL)
    hn = (_rms_scale(x) * npre_ref[...]).astype(BF16)

    alb = alb_ref[...]
    e = jnp.exp(alb - jnp.max(alb, axis=0, keepdims=True))
    lb = e[0:1] / jnp.sum(e, axis=0, keepdims=True)

    f = _dot(hn, win_ref[:, D_MODEL:2 * D_MODEL])
    forget = lb + (1.0 - lb) * jax.nn.sigmoid(f)
    lf_ref[...] = jnp.log(forget)
    inp_ref[...] = 1.0 - forget

    def project_query(c0, width):
        q = _dot(hn, win_ref[:, c0:c0 + width])
        qs_ref[:, c0:c0 + width] = q * jax.nn.sigmoid(q)

    def project_value(c0, width):
        v_ref[:, c0:c0 + width] = _dot(
            hn, win_ref[:, 2 * D_MODEL + c0:2 * D_MODEL + c0 + width]).astype(BF16)

    def project_gate(c0, width):
        g = _dot(hn, win_ref[:, 3 * D_MODEL + c0:3 * D_MODEL + c0 + width])
        sg_ref[:, c0:c0 + width] = g * jax.nn.sigmoid(g)

    def write_rows(mix):
        y = x + _rms_scale(mix) * npost_ref[...]
        y_ref[...] = y.reshape(nb, tt, D_MODEL)

    half = chunk // 2
    ri = lax.broadcasted_iota(jnp.int32, (chunk, chunk), 0)
    ci = lax.broadcasted_iota(jnp.int32, (chunk, chunk), 1)
    causal = ri >= ci

    def prepare(blk, h):
        rs = slice(blk * chunk, (blk + 1) * chunk)
        sl = slice(h * A_DK, (h + 1) * A_DK)
        b = _cumsum_rows(lf_ref[rs, sl])
        bmid = b[half - 1:half]
        blast = b[chunk - 1:chunk]
        qsc = qs_ref[rs, sl] * jnp.exp(b - bmid)
        inp = inp_ref[rs, sl] * jnp.exp(bmid - b)
        return dict(
            rs=rs, sl=sl, qt=qsc.astype(BF16), kt=inp.astype(BF16),
            qi=(qsc * jnp.exp(bmid)).astype(BF16),
            kh=(inp * jnp.exp(blast - bmid)).astype(BF16),
            vh=v_ref[rs, sl], decay=jnp.exp(blast))

    def emit_output(u, sc, st):
        p = jnp.where(causal, sc, 0.0).astype(BF16)
        o = _dot(p, u["vh"]) + _dot_nt(u["qi"], st.astype(BF16))
        on = _rms_scale(o) * gnorm_ref[:, u["sl"]] * sg_ref[u["rs"], u["sl"]]
        on_ref[u["rs"], u["sl"]] = on.astype(BF16)

    def recurrence(heads, jobs):
        nstages = tt // chunk
        slots = 2 * nstages
        jobs = list(jobs)

        def run_jobs(slot):
            for _ in range(-(-len(jobs) // (slots - slot))):
                jobs.pop(0)()

        states = {}
        for stage in range(nstages):
            blocks = [n * nstages + stage for n in range(nb)]
            units = [(blk, h, prepare(blk, h)) for blk in blocks for h in heads]
            scores = [_dot_nt(u["qt"], u["kt"]) for _, _, u in units]
            run_jobs(2 * stage)
            updated = {}
            for blk, h, u in units:
                key = ((blk * chunk) // tt, h)
                if key not in states:
                    states[key] = st_ref[key[0], h]
                updated[key] = states[key] * u["decay"] + _dot_tn(u["vh"], u["kh"])
            run_jobs(2 * stage + 1)
            for (blk, h, u), sc in zip(units, scores):
                emit_output(u, sc, states[((blk * chunk) // tt, h)])
            states.update(updated)
        for (n, h), st in states.items():
            st_ref[n, h] = st

    def factorised_path():
        heads_per_group = A_HEADS // HEAD_GROUPS
        piece = 2 * LANES
        partial_mix = [[] for _ in range(HEAD_GROUPS)]

        def project_out(g, c0, width):
            partial_mix[g].append(
                _dot(on_ref[:, g * gw:(g + 1) * gw], wout_ref[g * gw:(g + 1) * gw, c0:c0 + width]))

        for g in range(HEAD_GROUPS):
            jobs = []
            if g + 1 < HEAD_GROUPS:
                jobs += [functools.partial(proj, (g + 1) * gw + c0, piece)
                         for proj in (project_query, project_value, project_gate)
                         for c0 in range(0, gw, piece)]
            if g > 0:
                jobs += [functools.partial(project_out, g - 1, c0, piece)
                         for c0 in range(0, D_MODEL, piece)]
            recurrence(range(g * heads_per_group, (g + 1) * heads_per_group), jobs)
        project_out(HEAD_GROUPS - 1, 0, D_MODEL)
        write_rows(functools.reduce(
            jnp.add, [jnp.concatenate(parts, axis=1) for parts in partial_mix]))

    def exact_step(idx, carry):
        n = (idx * EXACT_BLOCK) // tt
        rs = pl.ds(pl.multiple_of(idx * EXACT_BLOCK, EXACT_BLOCK), EXACT_BLOCK)
        row = lax.broadcasted_iota(jnp.int32, (EXACT_BLOCK, A_DK), 0)
        for h in range(A_HEADS):
            sl = slice(h * A_DK, (h + 1) * A_DK)
            b = _cumsum_rows(lf_ref[rs, sl])
            blast = b[EXACT_BLOCK - 1:EXACT_BLOCK]
            qs = qs_ref[rs, sl]
            inp = inp_ref[rs, sl]
            vh = v_ref[rs, sl]
            vf = vh.astype(F32)
            st = st_ref[n, h]
            o = _dot_nt((qs * jnp.exp(b)).astype(BF16), st.astype(BF16))
            for s in range(EXACT_BLOCK):
                decay = jnp.exp(jnp.where(row >= s, b - b[s:s + 1], -jnp.inf))
                score = jnp.sum(qs * decay * inp[s:s + 1], axis=-1, keepdims=True)
                o = o + score * vf[s:s + 1]
            kh = (inp * jnp.exp(blast - b)).astype(BF16)
            st_ref[n, h] = st * jnp.exp(blast) + _dot_tn(vh, kh)
            on = _rms_scale(o) * gnorm_ref[:, sl] * sg_ref[rs, sl]
            on_ref[rs, sl] = on.astype(BF16)
        return carry

    gw = D_MODEL // HEAD_GROUPS
    project_query(0, gw)
    project_value(0, gw)
    project_gate(0, gw)

    max_step = -jnp.min(lf_ref[...])
    bounded = max_step * half <= MAX_FACTOR_EXPONENT

    pl.when(bounded)(factorised_path)

    @pl.when(jnp.logical_not(bounded))
    def _():
        project_query(gw, D_MODEL - gw)
        project_value(gw, D_MODEL - gw)
        project_gate(gw, D_MODEL - gw)
        lax.fori_loop(0, rows // EXACT_BLOCK, exact_step, 0)
        write_rows(_dot(on_ref[...], wout_ref[...]))

    @pl.when(t == pl.num_programs(1) - 1)
    def _():
        for n in range(nb):
            for h in range(A_HEADS):
                st_ref[n, h] = st_ref[n, h].T


def _hgrn_mixer(x, state, npre, win, alb, gnorm, wout, npost, *, nb, tt, chunk, casts=()):
    batch, seq, _ = x.shape
    assert batch % nb == 0 and seq % tt == 0 and tt % chunk == 0 and chunk % (2 * SUBLANES) == 0
    rows = nb * tt
    nt = seq // tt
    nsteps = (batch // nb) * nt
    cast_specs, cast_blocks, cast_views = [], [], []
    for a in casts:
        width = next((wd for wd in (a.shape[-1], a.shape[-1] // 2, a.shape[-1] // 4)
                      if wd % LANES == 0 and (a.size // wd) % (2 * SUBLANES * nsteps) == 0),
                     a.shape[-1])
        view = a.reshape(-1, width)
        block_rows = next(r for r in range(2 * SUBLANES, view.shape[0] + 1, 2 * SUBLANES)
                          if view.shape[0] % r == 0 and view.shape[0] // r <= nsteps)
        nblocks = view.shape[0] // block_rows
        cast_views.append(view)
        cast_blocks.append(None if nblocks == nsteps else nblocks)
        cast_specs.append(pl.BlockSpec(
            (block_rows, width),
            lambda b, t, nblocks=nblocks: (jnp.minimum(b * nt + t, nblocks - 1), 0)))
    kern = functools.partial(_hgrn_kernel, nb=nb, tt=tt, chunk=chunk, cast_blocks=tuple(cast_blocks))
    st_spec = pl.BlockSpec((nb, A_HEADS, A_DK, A_DV), lambda b, t: (b, 0, 0, 0))
    return pl.pallas_call(
        kern,
        grid=(batch // nb, nt),
        in_specs=[
            pl.BlockSpec((nb, tt, D_MODEL), lambda b, t: (b, t, 0)),
            st_spec,
            _const_spec((1, D_MODEL)),
            _const_spec((D_MODEL, 4 * D_MODEL)),
            _const_spec(alb.shape),
            _const_spec((1, D_MODEL)),
            _const_spec((D_MODEL, D_MODEL)),
            _const_spec((1, D_MODEL)),
        ] + cast_specs,
        out_specs=[
            pl.BlockSpec((nb, tt, D_MODEL), lambda b, t: (b, t, 0)),
            st_spec,
        ] + cast_specs,
        out_shape=[
            jax.ShapeDtypeStruct(x.shape, F32),
            jax.ShapeDtypeStruct(state.shape, F32),
        ] + [jax.ShapeDtypeStruct(v.shape, BF16) for v in cast_views],
        scratch_shapes=[
            pltpu.VMEM((rows, D_MODEL), F32),
            pltpu.VMEM((rows, D_MODEL), F32),
            pltpu.VMEM((rows, D_MODEL), F32),
            pltpu.VMEM((rows, D_MODEL), BF16),
            pltpu.VMEM((rows, D_MODEL), F32),
            pltpu.VMEM((rows, D_MODEL), BF16),
        ],
        compiler_params=pltpu.CompilerParams(
            dimension_semantics=("arbitrary", "arbitrary"), vmem_limit_bytes=VMEM_LIMIT_BYTES),
        name="hgrn_mixer",
    )(x, state, npre, win, alb, gnorm, wout, npost, *cast_views)


def _rope_cols(x, cos_t, sin_t):
    lane = lax.broadcasted_iota(jnp.int32, (x.shape[0], LANES), 1)
    first_half = (lane % B_HEAD_DIM) < (B_HEAD_DIM // 2)
    cols = []
    for c0 in range(0, x.shape[1], LANES):
        xc = x[:, c0:c0 + LANES]
        partner = jnp.where(first_half,
                            pltpu.roll(xc, LANES - B_HEAD_DIM // 2, axis=1),
                            pltpu.roll(xc, B_HEAD_DIM // 2, axis=1))
        cols.append(xc * cos_t + partner * sin_t)
    return jnp.concatenate(cols, axis=1)


def _attn_cached_kernel(x_ref, kc_in_ref, vc_in_ref, cosq_ref, sinq_ref, cosk_ref, sinkt_ref,
                        sinks_ref, npre_ref, kvn_ref, wq_ref, wkv_ref, wo_ref, npost_ref,
                        y_ref, kc_ref, vc_ref, q_ref, kl_ref, kh_ref, vl_ref, ot_ref, *, nb, tt):
    rows = nb * tt
    ext = WINDOW + tt

    x = x_ref[...].reshape(rows, D_MODEL)
    xs = _rms_scale(x)
    hq = (xs * npre_ref[...]).astype(BF16)
    hk = (xs * kvn_ref[...]).astype(BF16)

    def per_stream(tab_ref):
        tab = tab_ref[...]
        return tab if nb == 1 else jnp.concatenate([tab] * nb, axis=0)

    q = _dot(hq, wq_ref[...])
    q_ref[...] = _rope_cols(q, per_stream(cosq_ref), per_stream(sinq_ref)).astype(BF16)
    kv = _dot(hk, wkv_ref[...])
    k_new = _rope_cols(kv[:, :KV_WIDTH], per_stream(cosk_ref), per_stream(sinkt_ref))
    v_new = kv[:, KV_WIDTH:]

    low = lax.broadcasted_iota(jnp.int32, (ext, LANES), 1) < B_HEAD_DIM
    for n in range(nb):
        k_ext = jnp.concatenate([kc_in_ref[n], k_new[n * tt:(n + 1) * tt]], axis=0)
        v_ext = jnp.concatenate([vc_in_ref[n], v_new[n * tt:(n + 1) * tt]], axis=0)
        kc_ref[n] = k_ext[ext - WINDOW:]
        vc_ref[n] = v_ext[ext - WINDOW:]
        for m in range(KV_WIDTH // LANES):
            ka = k_ext[:, m * LANES:(m + 1) * LANES]
            kr = pltpu.roll(ka, B_HEAD_DIM, axis=1)
            va = v_ext[:, m * LANES:(m + 1) * LANES]
            vr = pltpu.roll(va, B_HEAD_DIM, axis=1)
            kl_ref[2 * m, n] = jnp.where(low, ka, 0.0).astype(BF16)
            kh_ref[2 * m, n] = jnp.where(low, 0.0, kr).astype(BF16)
            kl_ref[2 * m + 1, n] = jnp.where(low, kr, 0.0).astype(BF16)
            kh_ref[2 * m + 1, n] = jnp.where(low, 0.0, ka).astype(BF16)
            vl_ref[2 * m, n] = jnp.where(low, va, 0.0).astype(BF16)
            vl_ref[2 * m + 1, n] = jnp.where(low, vr, 0.0).astype(BF16)

    first_head = lax.broadcasted_iota(jnp.int32, (1, 2 * tt), 1) < tt
    pairs = [(j, kx_ref, ha, hb) for j in range(B_KV_HEADS)
             for kx_ref, ha, hb in ((kl_ref, 4 * j, 4 * j + 2), (kh_ref, 4 * j + 1, 4 * j + 3))]

    def scores(n):
        rs = slice(n * tt, (n + 1) * tt)
        out = []
        for j, kx_ref, _, _ in pairs:
            qq = jnp.concatenate([q_ref[rs, 2 * j * LANES:(2 * j + 1) * LANES],
                                  q_ref[rs, (2 * j + 1) * LANES:(2 * j + 2) * LANES]], axis=0)
            out.append(_dot_nt(kx_ref[j, n], qq))
        return out

    def softmax(s_list):
        out = []
        for (_, _, ha, hb), s in zip(pairs, s_list):
            sink = jnp.where(first_head, sinks_ref[ha], sinks_ref[hb])
            mx = jnp.maximum(jnp.max(s, axis=0, keepdims=True), sink)
            p = jnp.exp(s - mx)
            den = jnp.sum(p, axis=0, keepdims=True) + jnp.exp(sink - mx)
            out.append((p.astype(BF16), den))
        return out

    def weighted_values(n, pd_list):
        cs = slice(n * tt, (n + 1) * tt)
        for (j, _, ha, hb), (p, den) in zip(pairs, pd_list):
            o = _dot_tn(vl_ref[j, n], p)[:B_HEAD_DIM]
            o = (o * (1.0 / den)).astype(BF16)
            ot_ref[ha * B_HEAD_DIM:(ha + 1) * B_HEAD_DIM, cs] = o[:, :tt]
            ot_ref[hb * B_HEAD_DIM:(hb + 1) * B_HEAD_DIM, cs] = o[:, tt:]

    s_next = scores(0)
    for n in range(nb):
        s_cur = s_next
        if n + 1 < nb:
            s_next = scores(n + 1)
        weighted_values(n, softmax(s_cur))

    mix = _dot_tn(ot_ref[...], wo_ref[...])
    y = x + _rms_scale(mix) * npost_ref[...]
    y_ref[...] = y.reshape(nb, tt, D_MODEL)


def _attn_cached_mixer(x, cache_k, cache_v, tables, sinks, npre, kvn, wq, wkv, wo, npost, *, nb):
    batch, tt, _ = x.shape
    assert batch % nb == 0 and tt % (2 * SUBLANES) == 0
    rows = nb * tt
    ext = WINDOW + tt
    kern = functools.partial(_attn_cached_kernel, nb=nb, tt=tt)
    x_spec = pl.BlockSpec((nb, tt, D_MODEL), lambda b: (b, 0, 0))
    c_spec = pl.BlockSpec((nb, WINDOW, KV_WIDTH), lambda b: (b, 0, 0))
    tab_spec = _const_spec((tt, LANES))
    cache_shape = jax.ShapeDtypeStruct((batch, WINDOW, KV_WIDTH), F32)
    return pl.pallas_call(
        kern,
        grid=(batch // nb,),
        in_specs=[x_spec, c_spec, c_spec] + [tab_spec] * 4 + [
            pl.BlockSpec(memory_space=pltpu.SMEM),
            _const_spec((1, D_MODEL)),
            _const_spec((1, D_MODEL)),
            _const_spec((D_MODEL, D_MODEL)),
            _const_spec((D_MODEL, 2 * KV_WIDTH)),
            _const_spec((D_MODEL, D_MODEL)),
            _const_spec((1, D_MODEL)),
        ],
        out_specs=[x_spec, c_spec, c_spec],
        out_shape=[jax.ShapeDtypeStruct(x.shape, F32), cache_shape, cache_shape],
        scratch_shapes=[
            pltpu.VMEM((rows, D_MODEL), BF16),
            pltpu.VMEM((B_KV_HEADS, nb, ext, LANES), BF16),
            pltpu.VMEM((B_KV_HEADS, nb, ext, LANES), BF16),
            pltpu.VMEM((B_KV_HEADS, nb, ext, LANES), BF16),
            pltpu.VMEM((D_MODEL, rows), BF16),
        ],
        compiler_params=pltpu.CompilerParams(
            dimension_semantics=("parallel",), vmem_limit_bytes=VMEM_LIMIT_BYTES),
        name="attn_cached_mixer",
    )(x, cache_k, cache_v, *tables, sinks, npre, kvn, wq, wkv, wo, npost)


def _ffn_steps(src_ref, dst_ref, npre_ref, npost_ref, win_ref, wout_ref):
    state = {}

    def chunk(c0):
        if not state:
            state["h"] = (_rms_scale(src_ref[...]) * npre_ref[...]).astype(BF16)
            state["acc"] = jnp.zeros(src_ref.shape, F32)
        h = state["h"]
        c1 = min(c0 + FFN_CHUNK, D_FF)
        a = _dot(h, win_ref[:, c0:c1])
        b = _dot(h, win_ref[:, D_FF + c0:D_FF + c1])
        g = (a * jax.nn.sigmoid(a) * b).astype(BF16)
        state["acc"] = state["acc"] + _dot(g, wout_ref[c0:c1, :])
        if c1 == D_FF:
            dst_ref[...] = (src_ref[...] + _rms_scale(state["acc"]) * npost_ref[...]).reshape(dst_ref.shape)

    return [functools.partial(chunk, c0) for c0 in range(0, D_FF, FFN_CHUNK)]


def _attn_pair_kernel(x_ref, cosq_ref, sinq_ref, cosk_ref, sinkt_ref, sinks_ref,
                      npre_ref, kvn_ref, wq_ref, wkv_ref, wo_ref, npost_ref,
                      fpre_ref, fpost_ref, win_ref, wout_ref,
                      y_ref, kc_ref, vc_ref, q_ref, kl_ref, kh_ref, vt_ref, ot_ref, mid_ref, *, tt):
    t = pl.program_id(1)
    last = pl.num_programs(1) - 1

    @pl.when((pl.program_id(0) == 0) & (t == 0))
    def _():
        mid_ref[...] = jnp.zeros(mid_ref.shape, F32)

    @pl.when(t < last)
    def _():
        jobs = _ffn_steps(mid_ref, y_ref, fpre_ref, fpost_ref, win_ref, wout_ref)
        _attention_tile(t, tt, x_ref, cosq_ref, sinq_ref, cosk_ref, sinkt_ref, sinks_ref,
                        npre_ref, kvn_ref, wq_ref, wkv_ref, wo_ref, npost_ref, kc_ref, vc_ref,
                        q_ref, kl_ref, kh_ref, vt_ref, ot_ref, jobs, mid_ref)

    @pl.when(t == last)
    def _():
        for job in _ffn_steps(mid_ref, y_ref, fpre_ref, fpost_ref, win_ref, wout_ref):
            job()


def _attention_tile(t, tt, x_ref, cosq_ref, sinq_ref, cosk_ref, sinkt_ref, sinks_ref,
                    npre_ref, kvn_ref, wq_ref, wkv_ref, wo_ref, npost_ref, kc_ref, vc_ref,
                    q_ref, kl_ref, kh_ref, vt_ref, ot_ref, jobs, out_ref):
    ext = WINDOW + tt
    pair = 2 * CHUNK
    nkeys = WINDOW + pair
    nvis = WINDOW + CHUNK
    jobs = list(jobs)
    njobs = len(jobs)

    def run_jobs(share):
        for _ in range(min(round(share * njobs), len(jobs))):
            jobs.pop(0)()

    @pl.when(t == 0)
    def _():
        kc_ref[...] = jnp.zeros(kc_ref.shape, F32)
        vc_ref[...] = jnp.zeros(vc_ref.shape, F32)

    x = x_ref[0]
    xs = _rms_scale(x)
    hq = (xs * npre_ref[...]).astype(BF16)
    hk = (xs * kvn_ref[...]).astype(BF16)

    run_jobs(JOB_SHARES[0])
    q = _dot(hq, wq_ref[...])
    q_ref[...] = _rope_cols(q, cosq_ref[...], sinq_ref[...]).astype(BF16)
    run_jobs(JOB_SHARES[1])
    kv = _dot(hk, wkv_ref[...])
    run_jobs(JOB_SHARES[2])
    k_ext = jnp.concatenate(
        [kc_ref[0], _rope_cols(kv[:, :KV_WIDTH], cosk_ref[...], sinkt_ref[...])], axis=0)
    v_ext = jnp.concatenate([vc_ref[0], kv[:, KV_WIDTH:]], axis=0)
    kc_ref[0] = k_ext[ext - WINDOW:]
    vc_ref[0] = v_ext[ext - WINDOW:]
    vt_ref[...] = v_ext.T.astype(BF16)

    low = lax.broadcasted_iota(jnp.int32, (ext, LANES), 1) < B_HEAD_DIM
    for m in range(KV_WIDTH // LANES):
        ka = k_ext[:, m * LANES:(m + 1) * LANES]
        kr = pltpu.roll(ka, B_HEAD_DIM, axis=1)
        kl_ref[2 * m] = jnp.where(low, ka, 0.0).astype(BF16)
        kh_ref[2 * m] = jnp.where(low, 0.0, kr).astype(BF16)
        kl_ref[2 * m + 1] = jnp.where(low, kr, 0.0).astype(BF16)
        kh_ref[2 * m + 1] = jnp.where(low, 0.0, ka).astype(BF16)

    vis_row = lax.broadcasted_iota(jnp.int32, (nvis, LANES), 0)
    first_bias = [jnp.where(vis_row + (t * tt - WINDOW + qc * CHUNK) >= 0, 0.0, -jnp.inf)
                  for qc in range(2)]
    first_head = lax.broadcasted_iota(jnp.int32, (1, LANES), 1) < CHUNK
    no_keys = jnp.zeros((CHUNK, LANES), BF16)

    units = [(j, kx_ref, ha, hb) for j in range(B_KV_HEADS)
             for kx_ref, ha, hb in ((kl_ref, 4 * j, 4 * j + 2), (kh_ref, 4 * j + 1, 4 * j + 3))]

    def scores(e):
        r0 = e * pair
        out = []
        for j, kx_ref, _, _ in units:
            ca = slice(2 * j * LANES, (2 * j + 1) * LANES)
            cb = slice((2 * j + 1) * LANES, (2 * j + 2) * LANES)
            qq = jnp.concatenate([q_ref[r0:r0 + CHUNK, ca], q_ref[r0:r0 + CHUNK, cb],
                                  q_ref[r0 + CHUNK:r0 + pair, ca], q_ref[r0 + CHUNK:r0 + pair, cb]], axis=0)
            out.append(_dot_nt(kx_ref[j, r0:r0 + nkeys, :], qq))
        return out

    def softmax(e, s_list):
        out = []
        for (_, _, ha, hb), s in zip(units, s_list):
            sink = jnp.where(first_head, sinks_ref[ha], sinks_ref[hb]) * LOG2_E
            ps, dens = [], []
            for qc in range(2):
                sq = s[qc * CHUNK:qc * CHUNK + nvis, qc * LANES:(qc + 1) * LANES]
                if e == 0:
                    sq = sq + first_bias[qc]
                mx = jnp.maximum(jnp.max(sq, axis=0, keepdims=True), sink)
                p = jnp.exp2(sq - mx)
                dens.append(jnp.sum(p, axis=0, keepdims=True) + jnp.exp2(sink - mx))
                ps.append(p.astype(BF16))
            pmat = jnp.concatenate([jnp.concatenate([ps[0], no_keys], axis=0),
                                    jnp.concatenate([no_keys, ps[1]], axis=0)], axis=1)
            out.append((pmat, jnp.concatenate(dens, axis=1)))
        return out

    def weighted_values(e, pd_list):
        r0 = e * pair
        for (j, _, ha, hb), (pmat, den) in zip(units, pd_list):
            o = _dot(vt_ref[j * B_HEAD_DIM:(j + 1) * B_HEAD_DIM, r0:r0 + nkeys], pmat)
            o = (o * (1.0 / den)).astype(BF16)
            for qc in range(2):
                c0 = r0 + qc * CHUNK
                ot_ref[ha * B_HEAD_DIM:(ha + 1) * B_HEAD_DIM, c0:c0 + CHUNK] = (
                    o[:, qc * LANES:qc * LANES + CHUNK])
                ot_ref[hb * B_HEAD_DIM:(hb + 1) * B_HEAD_DIM, c0:c0 + CHUNK] = (
                    o[:, qc * LANES + CHUNK:(qc + 1) * LANES])

    npairs = tt // pair
    s_next = scores(0)
    for e in range(npairs):
        s_cur = s_next
        if e + 1 < npairs:
            s_next = scores(e + 1)
        run_jobs(JOB_SHARES[3])
        weighted_values(e, softmax(e, s_cur))

    run_jobs(JOB_SHARES[4])
    mix = _dot_tn(ot_ref[...], wo_ref[...])
    run_jobs(1.0)
    out_ref[...] = x + _rms_scale(mix) * npost_ref[...]


def _attn_pair_mixer(x, tables, sinks, npre, kvn, wq, wkv, wo, npost, *, tt, ffn):
    batch, seq, _ = x.shape
    assert seq % tt == 0 and tt % (2 * CHUNK) == 0 and WINDOW == 2 * CHUNK
    ext = WINDOW + tt
    nt = seq // tt
    kern = functools.partial(_attn_pair_kernel, tt=tt)
    in_spec = pl.BlockSpec((1, tt, D_MODEL), lambda b, t: (b, jnp.minimum(t, nt - 1), 0))
    out_spec = pl.BlockSpec((1, tt, D_MODEL), lambda b, t: (b, jnp.maximum(t - 1, 0), 0))
    tab_spec = pl.BlockSpec((tt, LANES), lambda b, t: (jnp.minimum(t, nt - 1), 0))
    c_spec = pl.BlockSpec((1, WINDOW, KV_WIDTH), lambda b, t: (b, 0, 0))
    cache_shape = jax.ShapeDtypeStruct((batch, WINDOW, KV_WIDTH), F32)
    in_specs = [in_spec] + [tab_spec] * 4 + [
        pl.BlockSpec(memory_space=pltpu.SMEM),
        _const_spec((1, D_MODEL)),
        _const_spec((1, D_MODEL)),
        _const_spec((D_MODEL, D_MODEL)),
        _const_spec((D_MODEL, 2 * KV_WIDTH)),
        _const_spec((D_MODEL, D_MODEL)),
        _const_spec((1, D_MODEL)),
    ]
    args = [x, *tables, sinks, npre, kvn, wq, wkv, wo, npost]
    scratch = [
        pltpu.VMEM((tt, D_MODEL), BF16),
        pltpu.VMEM((B_KV_HEADS, ext, LANES), BF16),
        pltpu.VMEM((B_KV_HEADS, ext, LANES), BF16),
        pltpu.VMEM((KV_WIDTH, ext), BF16),
        pltpu.VMEM((D_MODEL, tt), BF16),
    ]
    fpre, fpost, win, wout, layer = ffn

    def layer_spec(shape):
        return pl.BlockSpec((None,) + shape, lambda b, t: (layer, 0, 0),
                            pipeline_mode=pl.Buffered(1))

    in_specs += [_const_spec((1, D_MODEL)), _const_spec((1, D_MODEL)),
                 layer_spec((D_MODEL, 2 * D_FF)), layer_spec((D_FF, D_MODEL))]
    args += [fpre, fpost, win, wout]
    scratch.append(pltpu.VMEM((tt, D_MODEL), F32))
    return pl.pallas_call(
        kern,
        grid=(batch, nt + 1),
        in_specs=in_specs,
        out_specs=[out_spec, c_spec, c_spec],
        out_shape=[jax.ShapeDtypeStruct(x.shape, F32), cache_shape, cache_shape],
        scratch_shapes=scratch,
        compiler_params=pltpu.CompilerParams(
            dimension_semantics=("arbitrary", "arbitrary"), vmem_limit_bytes=VMEM_LIMIT_BYTES),
        name="attn_pair_mixer",
    )(*args)


def _rope_tables(pos, q_scale):
    half = B_HEAD_DIM // 2
    inv = ROPE_THETA ** (-jnp.arange(half, dtype=F32) / half)
    ang = pos.astype(F32)[:, None] * inv[None, :]
    cos = jnp.cos(ang)
    sin = jnp.sin(ang)
    reps = LANES // B_HEAD_DIM
    cos_t = jnp.tile(jnp.concatenate([cos, cos], axis=1), (1, reps))
    sin_t = jnp.tile(jnp.concatenate([-sin, sin], axis=1), (1, reps))
    return cos_t * q_scale, sin_t * q_scale, cos_t, sin_t


def _trunk(x, pos, state, cache_k, cache_v, w, *, nb, tt, hgrn_chunk, cq):
    batch, seq, _ = x.shape
    row = lambda a: a.reshape(1, D_MODEL)
    pending = [k for k in LATER_WEIGHTS if w[k].dtype == F32]
    x, st, *cast_out = _hgrn_mixer(
        x, state, row(w["norm_mix_pre"][0]), w["w_a_in"], w["a_lower_bound"],
        row(w["a_out_norm"]), w["w_a_out"], row(w["norm_mix_post"][0]),
        nb=nb, tt=tt, chunk=hgrn_chunk, casts=[w[k] for k in pending])
    w = dict(w, **{k: c.reshape(w[k].shape) for k, c in zip(pending, cast_out)})
    x = _ffn(x.reshape(batch * seq, D_MODEL), row(w["norm_ffn_pre"][0]), row(w["norm_ffn_post"][0]),
             w["w_ffn_in"], w["w_ffn_out"], 0).reshape(batch, seq, D_MODEL)
    attn_w = (w["b_sinks"], row(w["norm_mix_pre"][1]), row(w["kv_norm"]), w["w_b_q"], w["w_kv"],
              w["w_b_out"], row(w["norm_mix_post"][1]))
    ffn1 = (row(w["norm_ffn_pre"][1]), row(w["norm_ffn_post"][1]), w["w_ffn_in"], w["w_ffn_out"], 1)
    if cache_k is None:
        assert nb == 1 and cq == CHUNK
        x, kc, vc = _attn_pair_mixer(x, _rope_tables(pos, SOFTMAX_SCALE * LOG2_E), *attn_w, tt=tt,
                                     ffn=ffn1)
    else:
        assert tt == seq and cq == seq
        x, kc, vc = _attn_cached_mixer(x, cache_k, cache_v, _rope_tables(pos, SOFTMAX_SCALE),
                                       *attn_w, nb=nb)
        x = _ffn(x.reshape(batch * seq, D_MODEL), *ffn1).reshape(batch, seq, D_MODEL)
    return x, st, kc, vc, w


def kernel(x_prompt, x_sample, state_hgrn, cache_k, cache_v, norm_mix_pre, norm_mix_post, norm_ffn_pre, norm_ffn_post, w_ffn_in, w_ffn_out, w_a_in, a_lower_bound, a_out_norm, w_a_out, kv_norm, w_kv, w_b_q, b_sinks, w_b_out):
    w = dict(
        norm_mix_pre=norm_mix_pre, norm_mix_post=norm_mix_post,
        norm_ffn_pre=norm_ffn_pre, norm_ffn_post=norm_ffn_post,
        w_ffn_in=w_ffn_in, w_ffn_out=w_ffn_out,
        w_a_in=w_a_in[0].astype(BF16), a_lower_bound=a_lower_bound,
        a_out_norm=a_out_norm[0], w_a_out=w_a_out[0].astype(BF16),
        kv_norm=kv_norm, w_kv=w_kv, w_b_q=w_b_q[0], b_sinks=b_sinks[0], w_b_out=w_b_out[0],
    )
    bp, tp, _ = x_prompt.shape
    bs, ts, _ = x_sample.shape

    zero_state = jnp.zeros((bp, A_HEADS, A_DK, A_DV), F32)
    y_p, st_p, kc_p, vc_p, w = _trunk(
        x_prompt, jnp.arange(tp), zero_state, None, None, w,
        nb=1, tt=ROW_TILE, hgrn_chunk=HGRN_CHUNK, cq=CHUNK)

    nb_s = ROW_TILE // (2 * ts)
    y_s, st_s, kc_s, vc_s, _ = _trunk(
        x_sample, PAST_LEN + jnp.arange(ts), state_hgrn[0],
        cache_k.reshape(bs, WINDOW, KV_WIDTH), cache_v.reshape(bs, WINDOW, KV_WIDTH), w,
        nb=nb_s, tt=ts, hgrn_chunk=ts, cq=ts)

    cache4 = lambda a: a.reshape(a.shape[0], WINDOW, B_KV_HEADS, B_HEAD_DIM)
    return (y_p, y_s, st_p[None], st_s[None],
            cache4(kc_p), cache4(vc_p), cache4(kc_s), cache4(vc_s))
```

```python
import functools

import jax
import jax.numpy as jnp
from jax import lax
from jax.experimental import pallas as pl
from jax.experimental.pallas import tpu as pltpu

F32 = jnp.float32
BF16 = jnp.bfloat16

D_MODEL = 1024
A_HEADS = 8
A_DK = 128
A_DV = 128
B_HEAD_DIM = 64
B_Q_HEADS = 16
B_KV_HEADS = 4
KV_WIDTH = B_KV_HEADS * B_HEAD_DIM
WINDOW = 128
CHUNK = 64
PAST_LEN = 2048
D_FF = 2816
ROPE_THETA = 10000.0
NORM_EPS = 1e-6
SOFTMAX_SCALE = B_HEAD_DIM ** -0.5
LOG2_E = 1.4426950408889634

LANES = 128
SUBLANES = 8
VMEM_LIMIT_BYTES = 52 * 1024 * 1024

ROW_TILE = 512
FFN_ROW_TILE = 1024
HGRN_CHUNK = 128
FFN_CHUNK = 256
MAX_FACTOR_EXPONENT = 60.0
EXACT_BLOCK = 16
HEAD_GROUPS = 2
LATER_WEIGHTS = ("w_ffn_in", "w_ffn_out", "w_b_q", "w_kv", "w_b_out")
JOB_SHARES = (0.18, 0.09, 0.09, 0.09, 0.09)


def _rms_scale(x):
    ms = jnp.mean(x * x, axis=-1, keepdims=True)
    return x * lax.rsqrt(ms + NORM_EPS)


def _dot(a, b):
    return jnp.dot(a, b, preferred_element_type=F32)


def _dot_nt(a, b):
    return lax.dot_general(a, b, (((1,), (1,)), ((), ())), preferred_element_type=F32)


def _dot_tn(a, b):
    return lax.dot_general(a, b, (((0,), (0,)), ((), ())), preferred_element_type=F32)


def _const_spec(shape):
    nd = len(shape)
    return pl.BlockSpec(shape, lambda *_: (0,) * nd, pipeline_mode=pl.Buffered(1))


def _ffn_kernel(x_ref, npre_ref, npost_ref, win_ref, wout_ref, o_ref, *, sub_rows):
    starts = range(0, x_ref.shape[0], sub_rows)
    normed = [(_rms_scale(x_ref[r0:r0 + sub_rows, :]) * npre_ref[...]).astype(BF16) for r0 in starts]
    for r0, h in zip(starts, normed):
        x = x_ref[r0:r0 + sub_rows, :]
        acc = jnp.zeros(x.shape, F32)
        for c0 in range(0, D_FF, FFN_CHUNK):
            c1 = min(c0 + FFN_CHUNK, D_FF)
            a = _dot(h, win_ref[:, c0:c1])
            b = _dot(h, win_ref[:, D_FF + c0:D_FF + c1])
            g = (a * jax.nn.sigmoid(a) * b).astype(BF16)
            acc = acc + _dot(g, wout_ref[c0:c1, :])
        o_ref[r0:r0 + sub_rows, :] = x + _rms_scale(acc) * npost_ref[...]


def _ffn(x2d, npre, npost, win, wout, layer):
    rows = x2d.shape[0]
    tile = min(FFN_ROW_TILE, rows)
    sub_rows = min(ROW_TILE, tile)
    assert rows % tile == 0 and tile % sub_rows == 0

    def layer_spec(shape):
        return pl.BlockSpec((None,) + shape, lambda i: (layer, 0, 0), pipeline_mode=pl.Buffered(1))

    return pl.pallas_call(
        functools.partial(_ffn_kernel, sub_rows=sub_rows),
        grid=(rows // tile,),
        in_specs=[
            pl.BlockSpec((tile, D_MODEL), lambda i: (i, 0)),
            _const_spec((1, D_MODEL)),
            _const_spec((1, D_MODEL)),
            layer_spec((D_MODEL, 2 * D_FF)),
            layer_spec((D_FF, D_MODEL)),
        ],
        out_specs=pl.BlockSpec((tile, D_MODEL), lambda i: (i, 0)),
        out_shape=jax.ShapeDtypeStruct((rows, D_MODEL), F32),
        compiler_params=pltpu.CompilerParams(
            dimension_semantics=("parallel",), vmem_limit_bytes=VMEM_LIMIT_BYTES),
        name="ffn",
    )(x2d, npre, npost, win, wout)


def _cumsum_rows(x):
    c, w = x.shape
    groups = c // SUBLANES
    y = x.reshape(groups, SUBLANES, w)
    sub = lax.broadcasted_iota(jnp.int32, y.shape, 1)
    shift = 1
    while shift < SUBLANES:
        y = y + jnp.where(sub >= shift, pltpu.roll(y, shift, axis=1), 0.0)
        shift *= 2
    tot = jnp.broadcast_to(y[:, SUBLANES - 1:SUBLANES, :], y.shape)
    inc = tot
    shift = 1
    while shift < groups:
        inc = inc + jnp.concatenate(
            [jnp.zeros((shift, SUBLANES, w), F32), inc[:groups - shift]], axis=0)
        shift *= 2
    return (y + (inc - tot)).reshape(c, w)


def _hgrn_kernel(*refs, nb, tt, chunk, cast_blocks):
    ncast = len(cast_blocks)
    x_ref, st_in_ref, npre_ref, win_ref, alb_ref, gnorm_ref, wout_ref, npost_ref = refs[:8]
    cast_src = refs[8:8 + ncast]
    y_ref, st_ref = refs[8 + ncast:10 + ncast]
    cast_dst = refs[10 + ncast:10 + 2 * ncast]
    qs_ref, lf_ref, inp_ref, v_ref, sg_ref, on_ref = refs[10 + 2 * ncast:]
    t = pl.program_id(1)
    rows = nb * tt

    step = pl.program_id(0) * pl.num_programs(1) + t
    for src_ref, dst_ref, nblocks in zip(cast_src, cast_dst, cast_blocks):
        if nblocks is None:
            dst_ref[...] = src_ref[...].astype(BF16)
        else:
            @pl.when(step < nblocks)
            def _(src_ref=src_ref, dst_ref=dst_ref):
                dst_ref[...] = src_ref[...].astype(BF16)

    @pl.when(t == 0)
    def _():
        for n in range(nb):
            for h in range(A_HEADS):
                st_ref[n, h] = st_in_ref[n, h].T

    x = x_ref[...].reshape(rows, D_MODEL)
    hn = (_rms_scale(x) * npre_ref[...]).astype(BF16)

    alb = alb_ref[...]
    e = jnp.exp(alb - jnp.max(alb, axis=0, keepdims=True))
    lb = e[0:1] / jnp.sum(e, axis=0, keepdims=True)

    f = _dot(hn, win_ref[:, D_MODEL:2 * D_MODEL])
    forget = lb + (1.0 - lb) * jax.nn.sigmoid(f)
    lf_ref[...] = jnp.log(forget)
    inp_ref[...] = 1.0 - forget

    def project_query(c0, width):
        q = _dot(hn, win_ref[:, c0:c0 + width])
        qs_ref[:, c0:c0 + width] = q * jax.nn.sigmoid(q)

    def project_value(c0, width):
        v_ref[:, c0:c0 + width] = _dot(
            hn, win_ref[:, 2 * D_MODEL + c0:2 * D_MODEL + c0 + width]).astype(BF16)

    def project_gate(c0, width):
        g = _dot(hn, win_ref[:, 3 * D_MODEL + c0:3 * D_MODEL + c0 + width])
        sg_ref[:, c0:c0 + width] = g * jax.nn.sigmoid(g)

    def write_rows(mix):
        y = x + _rms_scale(mix) * npost_ref[...]
        y_ref[...] = y.reshape(nb, tt, D_MODEL)

    half = chunk // 2
    ri = lax.broadcasted_iota(jnp.int32, (chunk, chunk), 0)
    ci = lax.broadcasted_iota(jnp.int32, (chunk, chunk), 1)
    causal = ri >= ci

    def prepare(blk, h):
        rs = slice(blk * chunk, (blk + 1) * chunk)
        sl = slice(h * A_DK, (h + 1) * A_DK)
        b = _cumsum_rows(lf_ref[rs, sl])
        bmid = b[half - 1:half]
        blast = b[chunk - 1:chunk]
        qsc = qs_ref[rs, sl] * jnp.exp(b - bmid)
        inp = inp_ref[rs, sl] * jnp.exp(bmid - b)
        return dict(
            rs=rs, sl=sl, qt=qsc.astype(BF16), kt=inp.astype(BF16),
            qi=(qsc * jnp.exp(bmid)).astype(BF16),
            kh=(inp * jnp.exp(blast - bmid)).astype(BF16),
            vh=v_ref[rs, sl], decay=jnp.exp(blast))

    def emit_output(u, sc, st):
        p = jnp.where(causal, sc, 0.0).astype(BF16)
        o = _dot(p, u["vh"]) + _dot_nt(u["qi"], st.astype(BF16))
        on = _rms_scale(o) * gnorm_ref[:, u["sl"]] * sg_ref[u["rs"], u["sl"]]
        on_ref[u["rs"], u["sl"]] = on.astype(BF16)

    def recurrence(heads, jobs):
        nstages = tt // chunk
        slots = 2 * nstages
        jobs = list(jobs)

        def run_jobs(slot):
            for _ in range(-(-len(jobs) // (slots - slot))):
                jobs.pop(0)()

        states = {}
        for stage in range(nstages):
            blocks = [n * nstages + stage for n in range(nb)]
            units = [(blk, h, prepare(blk, h)) for blk in blocks for h in heads]
            scores = [_dot_nt(u["qt"], u["kt"]) for _, _, u in units]
            run_jobs(2 * stage)
            updated = {}
            for blk, h, u in units:
                key = ((blk * chunk) // tt, h)
                if key not in states:
                    states[key] = st_ref[key[0], h]
                updated[key] = states[key] * u["decay"] + _dot_tn(u["vh"], u["kh"])
            run_jobs(2 * stage + 1)
            for (blk, h, u), sc in zip(units, scores):
                emit_output(u, sc, states[((blk * chunk) // tt, h)])
            states.update(updated)
        for (n, h), st in states.items():
            st_ref[n, h] = st

    def factorised_path():
        heads_per_group = A_HEADS // HEAD_GROUPS
        piece = 2 * LANES
        partial_mix = [[] for _ in range(HEAD_GROUPS)]

        def project_out(g, c0, width):
            partial_mix[g].append(
                _dot(on_ref[:, g * gw:(g + 1) * gw], wout_ref[g * gw:(g + 1) * gw, c0:c0 + width]))

        for g in range(HEAD_GROUPS):
            jobs = []
            if g + 1 < HEAD_GROUPS:
                jobs += [functools.partial(proj, (g + 1) * gw + c0, piece)
                         for proj in (project_query, project_value, project_gate)
                         for c0 in range(0, gw, piece)]
            if g > 0:
                jobs += [functools.partial(project_out, g - 1, c0, piece)
                         for c0 in range(0, D_MODEL, piece)]
            recurrence(range(g * heads_per_group, (g + 1) * heads_per_group), jobs)
        project_out(HEAD_GROUPS - 1, 0, D_MODEL)
        write_rows(functools.reduce(
            jnp.add, [jnp.concatenate(parts, axis=1) for parts in partial_mix]))

    def exact_step(idx, carry):
        n = (idx * EXACT_BLOCK) // tt
        rs = pl.ds(pl.multiple_of(idx * EXACT_BLOCK, EXACT_BLOCK), EXACT_BLOCK)
        row = lax.broadcasted_iota(jnp.int32, (EXACT_BLOCK, A_DK), 0)
        for h in range(A_HEADS):
            sl = slice(h * A_DK, (h + 1) * A_DK)
            b = _cumsum_rows(lf_ref[rs, sl])
            blast = b[EXACT_BLOCK - 1:EXACT_BLOCK]
            qs = qs_ref[rs, sl]
            inp = inp_ref[rs, sl]
            vh = v_ref[rs, sl]
            vf = vh.astype(F32)
            st = st_ref[n, h]
            o = _dot_nt((qs * jnp.exp(b)).astype(BF16), st.astype(BF16))
            for s in range(EXACT_BLOCK):
                decay = jnp.exp(jnp.where(row >= s, b - b[s:s + 1], -jnp.inf))
                score = jnp.sum(qs * decay * inp[s:s + 1], axis=-1, keepdims=True)
                o = o + score * vf[s:s + 1]
            kh = (inp * jnp.exp(blast - b)).astype(BF16)
            st_ref[n, h] = st * jnp.exp(blast) + _dot_tn(vh, kh)
            on = _rms_scale(o) * gnorm_ref[:, sl] * sg_ref[rs, sl]
            on_ref[rs, sl] = on.astype(BF16)
        return carry

    gw = D_MODEL // HEAD_GROUPS
    project_query(0, gw)
    project_value(0, gw)
    project_gate(0, gw)

    max_step = -jnp.min(lf_ref[...])
    bounded = max_step * half <= MAX_FACTOR_EXPONENT

    pl.when(bounded)(factorised_path)

    @pl.when(jnp.logical_not(bounded))
    def _():
        project_query(gw, D_MODEL - gw)
        project_value(gw, D_MODEL - gw)
        project_gate(gw, D_MODEL - gw)
        lax.fori_loop(0, rows // EXACT_BLOCK, exact_step, 0)
        write_rows(_dot(on_ref[...], wout_ref[...]))

    @pl.when(t == pl.num_programs(1) - 1)
    def _():
        for n in range(nb):
            for h in range(A_HEADS):
                st_ref[n, h] = st_ref[n, h].T


def _hgrn_mixer(x, state, npre, win, alb, gnorm, wout, npost, *, nb, tt, chunk, casts=()):
    batch, seq, _ = x.shape
    assert batch % nb == 0 and seq % tt == 0 and tt % chunk == 0 and chunk % (2 * SUBLANES) == 0
    rows = nb * tt
    nt = seq // tt
    nsteps = (batch // nb) * nt
    cast_specs, cast_blocks, cast_views = [], [], []
    for a in casts:
        width = a.shape[-1]
        view = a.reshape(-1, width)
        block_rows = next(r for r in range(2 * SUBLANES, view.shape[0] + 1, 2 * SUBLANES)
                          if view.shape[0] % r == 0 and view.shape[0] // r <= nsteps)
        nblocks = view.shape[0] // block_rows
        cast_views.append(view)
        cast_blocks.append(None if nblocks == nsteps else nblocks)
        cast_specs.append(pl.BlockSpec(
            (block_rows, width),
            lambda b, t, nblocks=nblocks: (jnp.minimum(b * nt + t, nblocks - 1), 0)))
    kern = functools.partial(_hgrn_kernel, nb=nb, tt=tt, chunk=chunk, cast_blocks=tuple(cast_blocks))
    st_spec = pl.BlockSpec((nb, A_HEADS, A_DK, A_DV), lambda b, t: (b, 0, 0, 0))
    return pl.pallas_call(
        kern,
        grid=(batch // nb, nt),
        in_specs=[
            pl.BlockSpec((nb, tt, D_MODEL), lambda b, t: (b, t, 0)),
            st_spec,
            _const_spec((1, D_MODEL)),
            _const_spec((D_MODEL, 4 * D_MODEL)),
            _const_spec(alb.shape),
            _const_spec((1, D_MODEL)),
            _const_spec((D_MODEL, D_MODEL)),
            _const_spec((1, D_MODEL)),
        ] + cast_specs,
        out_specs=[
            pl.BlockSpec((nb, tt, D_MODEL), lambda b, t: (b, t, 0)),
            st_spec,
        ] + cast_specs,
        out_shape=[
            jax.ShapeDtypeStruct(x.shape, F32),
            jax.ShapeDtypeStruct(state.shape, F32),
        ] + [jax.ShapeDtypeStruct(v.shape, BF16) for v in cast_views],
        scratch_shapes=[
            pltpu.VMEM((rows, D_MODEL), F32),
            pltpu.VMEM((rows, D_MODEL), F32),
            pltpu.VMEM((rows, D_MODEL), F32),
            pltpu.VMEM((rows, D_MODEL), BF16),
            pltpu.VMEM((rows, D_MODEL), F32),
            pltpu.VMEM((rows, D_MODEL), BF16),
        ],
        compiler_params=pltpu.CompilerParams(
            dimension_semantics=("arbitrary", "arbitrary"), vmem_limit_bytes=VMEM_LIMIT_BYTES),
        name="hgrn_mixer",
    )(x, state, npre, win, alb, gnorm, wout, npost, *cast_views)


def _rope_cols(x, cos_t, sin_t):
    lane = lax.broadcasted_iota(jnp.int32, (x.shape[0], LANES), 1)
    first_half = (lane % B_HEAD_DIM) < (B_HEAD_DIM // 2)
    cols = []
    for c0 in range(0, x.shape[1], LANES):
        xc = x[:, c0:c0 + LANES]
        partner = jnp.where(first_half,
                            pltpu.roll(xc, LANES - B_HEAD_DIM // 2, axis=1),
                            pltpu.roll(xc, B_HEAD_DIM // 2, axis=1))
        cols.append(xc * cos_t + partner * sin_t)
    return jnp.concatenate(cols, axis=1)


def _attn_cached_kernel(x_ref, kc_in_ref, vc_in_ref, cosq_ref, sinq_ref, cosk_ref, sinkt_ref,
                        sinks_ref, npre_ref, kvn_ref, wq_ref, wkv_ref, wo_ref, npost_ref,
                        y_ref, kc_ref, vc_ref, q_ref, kl_ref, kh_ref, vl_ref, ot_ref, *, nb, tt):
    rows = nb * tt
    ext = WINDOW + tt

    x = x_ref[...].reshape(rows, D_MODEL)
    xs = _rms_scale(x)
    hq = (xs * npre_ref[...]).astype(BF16)
    hk = (xs * kvn_ref[...]).astype(BF16)

    def per_stream(tab_ref):
        tab = tab_ref[...]
        return tab if nb == 1 else jnp.concatenate([tab] * nb, axis=0)

    q = _dot(hq, wq_ref[...])
    q_ref[...] = _rope_cols(q, per_stream(cosq_ref), per_stream(sinq_ref)).astype(BF16)
    kv = _dot(hk, wkv_ref[...])
    k_new = _rope_cols(kv[:, :KV_WIDTH], per_stream(cosk_ref), per_stream(sinkt_ref))
    v_new = kv[:, KV_WIDTH:]

    low = lax.broadcasted_iota(jnp.int32, (ext, LANES), 1) < B_HEAD_DIM
    for n in range(nb):
        k_ext = jnp.concatenate([kc_in_ref[n], k_new[n * tt:(n + 1) * tt]], axis=0)
        v_ext = jnp.concatenate([vc_in_ref[n], v_new[n * tt:(n + 1) * tt]], axis=0)
        kc_ref[n] = k_ext[ext - WINDOW:]
        vc_ref[n] = v_ext[ext - WINDOW:]
        for m in range(KV_WIDTH // LANES):
            ka = k_ext[:, m * LANES:(m + 1) * LANES]
            kr = pltpu.roll(ka, B_HEAD_DIM, axis=1)
            va = v_ext[:, m * LANES:(m + 1) * LANES]
            vr = pltpu.roll(va, B_HEAD_DIM, axis=1)
            kl_ref[2 * m, n] = jnp.where(low, ka, 0.0).astype(BF16)
            kh_ref[2 * m, n] = jnp.where(low, 0.0, kr).astype(BF16)
            kl_ref[2 * m + 1, n] = jnp.where(low, kr, 0.0).astype(BF16)
            kh_ref[2 * m + 1, n] = jnp.where(low, 0.0, ka).astype(BF16)
            vl_ref[2 * m, n] = jnp.where(low, va, 0.0).astype(BF16)
            vl_ref[2 * m + 1, n] = jnp.where(low, vr, 0.0).astype(BF16)

    first_head = lax.broadcasted_iota(jnp.int32, (1, 2 * tt), 1) < tt
    pairs = [(j, kx_ref, ha, hb) for j in range(B_KV_HEADS)
             for kx_ref, ha, hb in ((kl_ref, 4 * j, 4 * j + 2), (kh_ref, 4 * j + 1, 4 * j + 3))]

    def scores(n):
        rs = slice(n * tt, (n + 1) * tt)
        out = []
        for j, kx_ref, _, _ in pairs:
            qq = jnp.concatenate([q_ref[rs, 2 * j * LANES:(2 * j + 1) * LANES],
                                  q_ref[rs, (2 * j + 1) * LANES:(2 * j + 2) * LANES]], axis=0)
            out.append(_dot_nt(kx_ref[j, n], qq))
        return out

    def softmax(s_list):
        out = []
        for (_, _, ha, hb), s in zip(pairs, s_list):
            sink = jnp.where(first_head, sinks_ref[ha], sinks_ref[hb])
            mx = jnp.maximum(jnp.max(s, axis=0, keepdims=True), sink)
            p = jnp.exp(s - mx)
            den = jnp.sum(p, axis=0, keepdims=True) + jnp.exp(sink - mx)
            out.append((p.astype(BF16), den))
        return out

    def weighted_values(n, pd_list):
        cs = slice(n * tt, (n + 1) * tt)
        for (j, _, ha, hb), (p, den) in zip(pairs, pd_list):
            o = _dot_tn(vl_ref[j, n], p)[:B_HEAD_DIM]
            o = (o * (1.0 / den)).astype(BF16)
            ot_ref[ha * B_HEAD_DIM:(ha + 1) * B_HEAD_DIM, cs] = o[:, :tt]
            ot_ref[hb * B_HEAD_DIM:(hb + 1) * B_HEAD_DIM, cs] = o[:, tt:]

    s_next = scores(0)
    for n in range(nb):
        s_cur = s_next
        if n + 1 < nb:
            s_next = scores(n + 1)
        weighted_values(n, softmax(s_cur))

    mix = _dot_tn(ot_ref[...], wo_ref[...])
    y = x + _rms_scale(mix) * npost_ref[...]
    y_ref[...] = y.reshape(nb, tt, D_MODEL)


def _attn_cached_mixer(x, cache_k, cache_v, tables, sinks, npre, kvn, wq, wkv, wo, npost, *, nb):
    batch, tt, _ = x.shape
    assert batch % nb == 0 and tt % (2 * SUBLANES) == 0
    rows = nb * tt
    ext = WINDOW + tt
    kern = functools.partial(_attn_cached_kernel, nb=nb, tt=tt)
    x_spec = pl.BlockSpec((nb, tt, D_MODEL), lambda b: (b, 0, 0))
    c_spec = pl.BlockSpec((nb, WINDOW, KV_WIDTH), lambda b: (b, 0, 0))
    tab_spec = _const_spec((tt, LANES))
    cache_shape = jax.ShapeDtypeStruct((batch, WINDOW, KV_WIDTH), F32)
    return pl.pallas_call(
        kern,
        grid=(batch // nb,),
        in_specs=[x_spec, c_spec, c_spec] + [tab_spec] * 4 + [
            pl.BlockSpec(memory_space=pltpu.SMEM),
            _const_spec((1, D_MODEL)),
            _const_spec((1, D_MODEL)),
            _const_spec((D_MODEL, D_MODEL)),
            _const_spec((D_MODEL, 2 * KV_WIDTH)),
            _const_spec((D_MODEL, D_MODEL)),
            _const_spec((1, D_MODEL)),
        ],
        out_specs=[x_spec, c_spec, c_spec],
        out_shape=[jax.ShapeDtypeStruct(x.shape, F32), cache_shape, cache_shape],
        scratch_shapes=[
            pltpu.VMEM((rows, D_MODEL), BF16),
            pltpu.VMEM((B_KV_HEADS, nb, ext, LANES), BF16),
            pltpu.VMEM((B_KV_HEADS, nb, ext, LANES), BF16),
            pltpu.VMEM((B_KV_HEADS, nb, ext, LANES), BF16),
            pltpu.VMEM((D_MODEL, rows), BF16),
        ],
        compiler_params=pltpu.CompilerParams(
            dimension_semantics=("parallel",), vmem_limit_bytes=VMEM_LIMIT_BYTES),
        name="attn_cached_mixer",
    )(x, cache_k, cache_v, *tables, sinks, npre, kvn, wq, wkv, wo, npost)


def _ffn_steps(src_ref, dst_ref, npre_ref, npost_ref, win_ref, wout_ref):
    state = {}

    def chunk(c0):
        if not state:
            state["h"] = (_rms_scale(src_ref[...]) * npre_ref[...]).astype(BF16)
            state["acc"] = jnp.zeros(src_ref.shape, F32)
        h = state["h"]
        c1 = min(c0 + FFN_CHUNK, D_FF)
        a = _dot(h, win_ref[:, c0:c1])
        b = _dot(h, win_ref[:, D_FF + c0:D_FF + c1])
        g = (a * jax.nn.sigmoid(a) * b).astype(BF16)
        state["acc"] = state["acc"] + _dot(g, wout_ref[c0:c1, :])
        if c1 == D_FF:
            dst_ref[...] = (src_ref[...] + _rms_scale(state["acc"]) * npost_ref[...]).reshape(dst_ref.shape)

    return [functools.partial(chunk, c0) for c0 in range(0, D_FF, FFN_CHUNK)]


def _attn_pair_kernel(x_ref, cosq_ref, sinq_ref, cosk_ref, sinkt_ref, sinks_ref,
                      npre_ref, kvn_ref, wq_ref, wkv_ref, wo_ref, npost_ref,
                      fpre_ref, fpost_ref, win_ref, wout_ref,
                      y_ref, kc_ref, vc_ref, q_ref, kl_ref, kh_ref, vt_ref, ot_ref, mid_ref, *, tt):
    t = pl.program_id(1)
    last = pl.num_programs(1) - 1

    @pl.when((pl.program_id(0) == 0) & (t == 0))
    def _():
        mid_ref[...] = jnp.zeros(mid_ref.shape, F32)

    @pl.when(t < last)
    def _():
        jobs = _ffn_steps(mid_ref, y_ref, fpre_ref, fpost_ref, win_ref, wout_ref)
        _attention_tile(t, tt, x_ref, cosq_ref, sinq_ref, cosk_ref, sinkt_ref, sinks_ref,
                        npre_ref, kvn_ref, wq_ref, wkv_ref, wo_ref, npost_ref, kc_ref, vc_ref,
                        q_ref, kl_ref, kh_ref, vt_ref, ot_ref, jobs, mid_ref)

    @pl.when(t == last)
    def _():
        for job in _ffn_steps(mid_ref, y_ref, fpre_ref, fpost_ref, win_ref, wout_ref):
            job()


def _attention_tile(t, tt, x_ref, cosq_ref, sinq_ref, cosk_ref, sinkt_ref, sinks_ref,
                    npre_ref, kvn_ref, wq_ref, wkv_ref, wo_ref, npost_ref, kc_ref, vc_ref,
                    q_ref, kl_ref, kh_ref, vt_ref, ot_ref, jobs, out_ref):
    ext = WINDOW + tt
    pair = 2 * CHUNK
    nkeys = WINDOW + pair
    nvis = WINDOW + CHUNK
    jobs = list(jobs)
    njobs = len(jobs)

    def run_jobs(share):
        for _ in range(min(round(share * njobs), len(jobs))):
            jobs.pop(0)()

    @pl.when(t == 0)
    def _():
        kc_ref[...] = jnp.zeros(kc_ref.shape, F32)
        vc_ref[...] = jnp.zeros(vc_ref.shape, F32)

    x = x_ref[0]
    xs = _rms_scale(x)
    hq = (xs * npre_ref[...]).astype(BF16)
    hk = (xs * kvn_ref[...]).astype(BF16)

    run_jobs(JOB_SHARES[0])
    q = _dot(hq, wq_ref[...])
    q_ref[...] = _rope_cols(q, cosq_ref[...], sinq_ref[...]).astype(BF16)
    run_jobs(JOB_SHARES[1])
    kv = _dot(hk, wkv_ref[...])
    run_jobs(JOB_SHARES[2])
    k_ext = jnp.concatenate(
        [kc_ref[0], _rope_cols(kv[:, :KV_WIDTH], cosk_ref[...], sinkt_ref[...])], axis=0)
    v_ext = jnp.concatenate([vc_ref[0], kv[:, KV_WIDTH:]], axis=0)
    kc_ref[0] = k_ext[ext - WINDOW:]
    vc_ref[0] = v_ext[ext - WINDOW:]
    vt_ref[...] = v_ext.T.astype(BF16)

    low = lax.broadcasted_iota(jnp.int32, (ext, LANES), 1) < B_HEAD_DIM
    for m in range(KV_WIDTH // LANES):
        ka = k_ext[:, m * LANES:(m + 1) * LANES]
        kr = pltpu.roll(ka, B_HEAD_DIM, axis=1)
        kl_ref[2 * m] = jnp.where(low, ka, 0.0).astype(BF16)
        kh_ref[2 * m] = jnp.where(low, 0.0, kr).astype(BF16)
        kl_ref[2 * m + 1] = jnp.where(low, kr, 0.0).astype(BF16)
        kh_ref[2 * m + 1] = jnp.where(low, 0.0, ka).astype(BF16)

    vis_row = lax.broadcasted_iota(jnp.int32, (nvis, LANES), 0)
    first_bias = [jnp.where(vis_row + (t * tt - WINDOW + qc * CHUNK) >= 0, 0.0, -jnp.inf)
                  for qc in range(2)]
    first_head = lax.broadcasted_iota(jnp.int32, (1, LANES), 1) < CHUNK
    no_keys = jnp.zeros((CHUNK, LANES), BF16)

    units = [(j, kx_ref, ha, hb) for j in range(B_KV_HEADS)
             for kx_ref, ha, hb in ((kl_ref, 4 * j, 4 * j + 2), (kh_ref, 4 * j + 1, 4 * j + 3))]

    def scores(e):
        r0 = e * pair
        out = []
        for j, kx_ref, _, _ in units:
            ca = slice(2 * j * LANES, (2 * j + 1) * LANES)
            cb = slice((2 * j + 1) * LANES, (2 * j + 2) * LANES)
            qq = jnp.concatenate([q_ref[r0:r0 + CHUNK, ca], q_ref[r0:r0 + CHUNK, cb],
                                  q_ref[r0 + CHUNK:r0 + pair, ca], q_ref[r0 + CHUNK:r0 + pair, cb]], axis=0)
            out.append(_dot_nt(kx_ref[j, r0:r0 + nkeys, :], qq))
        return out

    def softmax(e, s_list):
        out = []
        for (_, _, ha, hb), s in zip(units, s_list):
            sink = jnp.where(first_head, sinks_ref[ha], sinks_ref[hb]) * LOG2_E
            ps, dens = [], []
            for qc in range(2):
                sq = s[qc * CHUNK:qc * CHUNK + nvis, qc * LANES:(qc + 1) * LANES]
                if e == 0:
                    sq = sq + first_bias[qc]
                mx = jnp.maximum(jnp.max(sq, axis=0, keepdims=True), sink)
                p = jnp.exp2(sq - mx)
                dens.append(jnp.sum(p, axis=0, keepdims=True) + jnp.exp2(sink - mx))
                ps.append(p.astype(BF16))
            pmat = jnp.concatenate([jnp.concatenate([ps[0], no_keys], axis=0),
                                    jnp.concatenate([no_keys, ps[1]], axis=0)], axis=1)
            out.append((pmat, jnp.concatenate(dens, axis=1)))
        return out

    def weighted_values(e, pd_list):
        r0 = e * pair
        for (j, _, ha, hb), (pmat, den) in zip(units, pd_list):
            o = _dot(vt_ref[j * B_HEAD_DIM:(j + 1) * B_HEAD_DIM, r0:r0 + nkeys], pmat)
            o = (o * (1.0 / den)).astype(BF16)
            for qc in range(2):
                c0 = r0 + qc * CHUNK
                ot_ref[ha * B_HEAD_DIM:(ha + 1) * B_HEAD_DIM, c0:c0 + CHUNK] = (
                    o[:, qc * LANES:qc * LANES + CHUNK])
                ot_ref[hb * B_HEAD_DIM:(hb + 1) * B_HEAD_DIM, c0:c0 + CHUNK] = (
                    o[:, qc * LANES + CHUNK:(qc + 1) * LANES])

    npairs = tt // pair
    s_next = scores(0)
    for e in range(npairs):
        s_cur = s_next
        if e + 1 < npairs:
            s_next = scores(e + 1)
        run_jobs(JOB_SHARES[3])
        weighted_values(e, softmax(e, s_cur))

    run_jobs(JOB_SHARES[4])
    mix = _dot_tn(ot_ref[...], wo_ref[...])
    run_jobs(1.0)
    out_ref[...] = x + _rms_scale(mix) * npost_ref[...]


def _attn_pair_mixer(x, tables, sinks, npre, kvn, wq, wkv, wo, npost, *, tt, ffn):
    batch, seq, _ = x.shape
    assert seq % tt == 0 and tt % (2 * CHUNK) == 0 and WINDOW == 2 * CHUNK
    ext = WINDOW + tt
    nt = seq // tt
    kern = functools.partial(_attn_pair_kernel, tt=tt)
    in_spec = pl.BlockSpec((1, tt, D_MODEL), lambda b, t: (b, jnp.minimum(t, nt - 1), 0))
    out_spec = pl.BlockSpec((1, tt, D_MODEL), lambda b, t: (b, jnp.maximum(t - 1, 0), 0))
    tab_spec = pl.BlockSpec((tt, LANES), lambda b, t: (jnp.minimum(t, nt - 1), 0))
    c_spec = pl.BlockSpec((1, WINDOW, KV_WIDTH), lambda b, t: (b, 0, 0))
    cache_shape = jax.ShapeDtypeStruct((batch, WINDOW, KV_WIDTH), F32)
    in_specs = [in_spec] + [tab_spec] * 4 + [
        pl.BlockSpec(memory_space=pltpu.SMEM),
        _const_spec((1, D_MODEL)),
        _const_spec((1, D_MODEL)),
        _const_spec((D_MODEL, D_MODEL)),
        _const_spec((D_MODEL, 2 * KV_WIDTH)),
        _const_spec((D_MODEL, D_MODEL)),
        _const_spec((1, D_MODEL)),
    ]
    args = [x, *tables, sinks, npre, kvn, wq, wkv, wo, npost]
    scratch = [
        pltpu.VMEM((tt, D_MODEL), BF16),
        pltpu.VMEM((B_KV_HEADS, ext, LANES), BF16),
        pltpu.VMEM((B_KV_HEADS, ext, LANES), BF16),
        pltpu.VMEM((KV_WIDTH, ext), BF16),
        pltpu.VMEM((D_MODEL, tt), BF16),
    ]
    fpre, fpost, win, wout, layer = ffn

    def layer_spec(shape):
        return pl.BlockSpec((None,) + shape, lambda b, t: (layer, 0, 0),
                            pipeline_mode=pl.Buffered(1))

    in_specs += [_const_spec((1, D_MODEL)), _const_spec((1, D_MODEL)),
                 layer_spec((D_MODEL, 2 * D_FF)), layer_spec((D_FF, D_MODEL))]
    args += [fpre, fpost, win, wout]
    scratch.append(pltpu.VMEM((tt, D_MODEL), F32))
    return pl.pallas_call(
        kern,
        grid=(batch, nt + 1),
        in_specs=in_specs,
        out_specs=[out_spec, c_spec, c_spec],
        out_shape=[jax.ShapeDtypeStruct(x.shape, F32), cache_shape, cache_shape],
        scratch_shapes=scratch,
        compiler_params=pltpu.CompilerParams(
            dimension_semantics=("arbitrary", "arbitrary"), vmem_limit_bytes=VMEM_LIMIT_BYTES),
        name="attn_pair_mixer",
    )(*args)


def _rope_tables(pos, q_scale):
    half = B_HEAD_DIM // 2
    inv = ROPE_THETA ** (-jnp.arange(half, dtype=F32) / half)
    ang = pos.astype(F32)[:, None] * inv[None, :]
    cos = jnp.cos(ang)
    sin = jnp.sin(ang)
    reps = LANES // B_HEAD_DIM
    cos_t = jnp.tile(jnp.concatenate([cos, cos], axis=1), (1, reps))
    sin_t = jnp.tile(jnp.concatenate([-sin, sin], axis=1), (1, reps))
    return cos_t * q_scale, sin_t * q_scale, cos_t, sin_t


def _trunk(x, pos, state, cache_k, cache_v, w, *, nb, tt, hgrn_chunk, cq):
    batch, seq, _ = x.shape
    row = lambda a: a.reshape(1, D_MODEL)
    pending = [k for k in LATER_WEIGHTS if w[k].dtype == F32]
    x, st, *cast_out = _hgrn_mixer(
        x, state, row(w["norm_mix_pre"][0]), w["w_a_in"], w["a_lower_bound"],
        row(w["a_out_norm"]), w["w_a_out"], row(w["norm_mix_post"][0]),
        nb=nb, tt=tt, chunk=hgrn_chunk, casts=[w[k] for k in pending])
    w = dict(w, **{k: c.reshape(w[k].shape) for k, c in zip(pending, cast_out)})
    x = _ffn(x.reshape(batch * seq, D_MODEL), row(w["norm_ffn_pre"][0]), row(w["norm_ffn_post"][0]),
             w["w_ffn_in"], w["w_ffn_out"], 0).reshape(batch, seq, D_MODEL)
    attn_w = (w["b_sinks"], row(w["norm_mix_pre"][1]), row(w["kv_norm"]), w["w_b_q"], w["w_kv"],
              w["w_b_out"], row(w["norm_mix_post"][1]))
    ffn1 = (row(w["norm_ffn_pre"][1]), row(w["norm_ffn_post"][1]), w["w_ffn_in"], w["w_ffn_out"], 1)
    if cache_k is None:
        assert nb == 1 and cq == CHUNK
        x, kc, vc = _attn_pair_mixer(x, _rope_tables(pos, SOFTMAX_SCALE * LOG2_E), *attn_w, tt=tt,
                                     ffn=ffn1)
    else:
        assert tt == seq and cq == seq
        x, kc, vc = _attn_cached_mixer(x, cache_k, cache_v, _rope_tables(pos, SOFTMAX_SCALE),
                                       *attn_w, nb=nb)
        x = _ffn(x.reshape(batch * seq, D_MODEL), *ffn1).reshape(batch, seq, D_MODEL)
    return x, st, kc, vc, w


def kernel(x_prompt, x_sample, state_hgrn, cache_k, cache_v, norm_mix_pre, norm_mix_post, norm_ffn_pre, norm_ffn_post, w_ffn_in, w_ffn_out, w_a_in, a_lower_bound, a_out_norm, w_a_out, kv_norm, w_kv, w_b_q, b_sinks, w_b_out):
    w = dict(
        norm_mix_pre=norm_mix_pre, norm_mix_post=norm_mix_post,
        norm_ffn_pre=norm_ffn_pre, norm_ffn_post=norm_ffn_post,
        w_ffn_in=w_ffn_in, w_ffn_out=w_ffn_out,
        w_a_in=w_a_in[0].astype(BF16), a_lower_bound=a_lower_bound,
        a_out_norm=a_out_norm[0], w_a_out=w_a_out[0].astype(BF16),
        kv_norm=kv_norm, w_kv=w_kv, w_b_q=w_b_q[0], b_sinks=b_sinks[0], w_b_out=w_b_out[0],
    )
    bp, tp, _ = x_prompt.shape
    bs, ts, _ = x_sample.shape

    zero_state = jnp.zeros((bp, A_HEADS, A_DK, A_DV), F32)
    y_p, st_p, kc_p, vc_p, w = _trunk(
        x_prompt, jnp.arange(tp), zero_state, None, None, w,
        nb=1, tt=ROW_TILE, hgrn_chunk=HGRN_CHUNK, cq=CHUNK)

    nb_s = ROW_TILE // (2 * ts)
    y_s, st_s, kc_s, vc_s, _ = _trunk(
        x_sample, PAST_LEN + jnp.arange(ts), state_hgrn[0],
        cache_k.reshape(bs, WINDOW, KV_WIDTH), cache_v.reshape(bs, WINDOW, KV_WIDTH), w,
        nb=nb_s, tt=ts, hgrn_chunk=ts, cq=ts)

    cache4 = lambda a: a.reshape(a.shape[0], WINDOW, B_KV_HEADS, B_HEAD_DIM)
    return (y_p, y_s, st_p[None], st_s[None],
            cache4(kc_p), cache4(vc_p), cache4(kc_s), cache4(vc_s))
```

```python
import functools

import jax
import jax.numpy as jnp
from jax import lax
from jax.experimental import pallas as pl
from jax.experimental.pallas import tpu as pltpu

F32 = jnp.float32
BF16 = jnp.bfloat16

D_MODEL = 1024
A_HEADS = 8
A_DK = 128
A_DV = 128
B_HEAD_DIM = 64
B_Q_HEADS = 16
B_KV_HEADS = 4
KV_WIDTH = B_KV_HEADS * B_HEAD_DIM
WINDOW = 128
CHUNK = 64
PAST_LEN = 2048
D_FF = 2816
ROPE_THETA = 10000.0
NORM_EPS = 1e-6
SOFTMAX_SCALE = B_HEAD_DIM ** -0.5
LOG2_E = 1.4426950408889634

LANES = 128
SUBLANES = 8
VMEM_LIMIT_BYTES = 52 * 1024 * 1024

ROW_TILE = 512
FFN_ROW_TILE = 1024
HGRN_CHUNK = 128
FFN_CHUNK = 256
MAX_FACTOR_EXPONENT = 60.0
EXACT_BLOCK = 16
HEAD_GROUPS = 2
LATER_WEIGHTS = ("w_ffn_in", "w_ffn_out", "w_b_q", "w_kv", "w_b_out")
JOB_SHARES = (0.18, 0.09, 0.09, 0.09, 0.09)


def _rms_scale(x):
    ms = jnp.mean(x * x, axis=-1, keepdims=True)
    return x * lax.rsqrt(ms + NORM_EPS)


def _dot(a, b):
    return jnp.dot(a, b, preferred_element_type=F32)


def _dot_nt(a, b):
    return lax.dot_general(a, b, (((1,), (1,)), ((), ())), preferred_element_type=F32)


def _dot_tn(a, b):
    return lax.dot_general(a, b, (((0,), (0,)), ((), ())), preferred_element_type=F32)


def _const_spec(shape):
    nd = len(shape)
    return pl.BlockSpec(shape, lambda *_: (0,) * nd, pipeline_mode=pl.Buffered(1))


def _ffn_kernel(x_ref, npre_ref, npost_ref, win_ref, wout_ref, o_ref, *, sub_rows):
    starts = range(0, x_ref.shape[0], sub_rows)
    normed = [(_rms_scale(x_ref[r0:r0 + sub_rows, :]) * npre_ref[...]).astype(BF16) for r0 in starts]
    for r0, h in zip(starts, normed):
        x = x_ref[r0:r0 + sub_rows, :]
        acc = jnp.zeros(x.shape, F32)
        for c0 in range(0, D_FF, FFN_CHUNK):
            c1 = min(c0 + FFN_CHUNK, D_FF)
            a = _dot(h, win_ref[:, c0:c1])
            b = _dot(h, win_ref[:, D_FF + c0:D_FF + c1])
            g = (a * jax.nn.sigmoid(a) * b).astype(BF16)
            acc = acc + _dot(g, wout_ref[c0:c1, :])
        o_ref[r0:r0 + sub_rows, :] = x + _rms_scale(acc) * npost_ref[...]


def _ffn(x2d, npre, npost, win, wout, layer):
    rows = x2d.shape[0]
    tile = min(FFN_ROW_TILE, rows)
    sub_rows = min(ROW_TILE, tile)
    assert rows % tile == 0 and tile % sub_rows == 0

    def layer_spec(shape):
        return pl.BlockSpec((None,) + shape, lambda i: (layer, 0, 0), pipeline_mode=pl.Buffered(1))

    return pl.pallas_call(
        functools.partial(_ffn_kernel, sub_rows=sub_rows),
        grid=(rows // tile,),
        in_specs=[
            pl.BlockSpec((tile, D_MODEL), lambda i: (i, 0)),
            _const_spec((1, D_MODEL)),
            _const_spec((1, D_MODEL)),
            layer_spec((D_MODEL, 2 * D_FF)),
            layer_spec((D_FF, D_MODEL)),
        ],
        out_specs=pl.BlockSpec((tile, D_MODEL), lambda i: (i, 0)),
        out_shape=jax.ShapeDtypeStruct((rows, D_MODEL), F32),
        compiler_params=pltpu.CompilerParams(
            dimension_semantics=("parallel",), vmem_limit_bytes=VMEM_LIMIT_BYTES),
        name="ffn",
    )(x2d, npre, npost, win, wout)


def _cumsum_rows(x):
    c, w = x.shape
    groups = c // SUBLANES
    y = x.reshape(groups, SUBLANES, w)
    sub = lax.broadcasted_iota(jnp.int32, y.shape, 1)
    shift = 1
    while shift < SUBLANES:
        y = y + jnp.where(sub >= shift, pltpu.roll(y, shift, axis=1), 0.0)
        shift *= 2
    tot = jnp.broadcast_to(y[:, SUBLANES - 1:SUBLANES, :], y.shape)
    inc = tot
    shift = 1
    while shift < groups:
        inc = inc + jnp.concatenate(
            [jnp.zeros((shift, SUBLANES, w), F32), inc[:groups - shift]], axis=0)
        shift *= 2
    return (y + (inc - tot)).reshape(c, w)


def _hgrn_kernel(*refs, nb, tt, chunk, cast_blocks):
    ncast = len(cast_blocks)
    x_ref, st_in_ref, npre_ref, win_ref, alb_ref, gnorm_ref, wout_ref, npost_ref = refs[:8]
    cast_src = refs[8:8 + ncast]
    y_ref, st_ref = refs[8 + ncast:10 + ncast]
    cast_dst = refs[10 + ncast:10 + 2 * ncast]
    qs_ref, lf_ref, inp_ref, v_ref, sg_ref, on_ref = refs[10 + 2 * ncast:]
    t = pl.program_id(1)
    rows = nb * tt

    step = pl.program_id(0) * pl.num_programs(1) + t
    for src_ref, dst_ref, nblocks in zip(cast_src, cast_dst, cast_blocks):
        if nblocks is None:
            dst_ref[...] = src_ref[...].astype(BF16)
        else:
            @pl.when(step < nblocks)
            def _(src_ref=src_ref, dst_ref=dst_ref):
                dst_ref[...] = src_ref[...].astype(BF16)

    @pl.when(t == 0)
    def _():
        for n in range(nb):
            for h in range(A_HEADS):
                st_ref[n, h] = st_in_ref[n, h].T

    x = x_ref[...].reshape(rows, D_MODEL)
    hn = (_rms_scale(x) * npre_ref[...]).astype(BF16)

    alb = alb_ref[...]
    e = jnp.exp(alb - jnp.max(alb, axis=0, keepdims=True))
    lb = e[0:1] / jnp.sum(e, axis=0, keepdims=True)

    f = _dot(hn, win_ref[:, D_MODEL:2 * D_MODEL])
    forget = lb + (1.0 - lb) * jax.nn.sigmoid(f)
    lf_ref[...] = jnp.log(forget)
    inp_ref[...] = 1.0 - forget

    def project_query(c0, width):
        q = _dot(hn, win_ref[:, c0:c0 + width])
        qs_ref[:, c0:c0 + width] = q * jax.nn.sigmoid(q)

    def project_value(c0, width):
        v_ref[:, c0:c0 + width] = _dot(
            hn, win_ref[:, 2 * D_MODEL + c0:2 * D_MODEL + c0 + width]).astype(BF16)

    def project_gate(c0, width):
        g = _dot(hn, win_ref[:, 3 * D_MODEL + c0:3 * D_MODEL + c0 + width])
        sg_ref[:, c0:c0 + width] = g * jax.nn.sigmoid(g)

    def write_rows(mix):
        y = x + _rms_scale(mix) * npost_ref[...]
        y_ref[...] = y.reshape(nb, tt, D_MODEL)

    half = chunk // 2
    ri = lax.broadcasted_iota(jnp.int32, (chunk, chunk), 0)
    ci = lax.broadcasted_iota(jnp.int32, (chunk, chunk), 1)
    causal = ri >= ci

    def prepare(blk, h):
        rs = slice(blk * chunk, (blk + 1) * chunk)
        sl = slice(h * A_DK, (h + 1) * A_DK)
        b = _cumsum_rows(lf_ref[rs, sl])
        bmid = b[half - 1:half]
        blast = b[chunk - 1:chunk]
        qsc = qs_ref[rs, sl] * jnp.exp(b - bmid)
        inp = inp_ref[rs, sl] * jnp.exp(bmid - b)
        return dict(
            rs=rs, sl=sl, qt=qsc.astype(BF16), kt=inp.astype(BF16),
            qi=(qsc * jnp.exp(bmid)).astype(BF16),
            kh=(inp * jnp.exp(blast - bmid)).astype(BF16),
            vh=v_ref[rs, sl], decay=jnp.exp(blast))

    def emit_output(u, sc, st):
        p = jnp.where(causal, sc, 0.0).astype(BF16)
        o = _dot(p, u["vh"]) + _dot_nt(u["qi"], st.astype(BF16))
        on = _rms_scale(o) * gnorm_ref[:, u["sl"]] * sg_ref[u["rs"], u["sl"]]
        on_ref[u["rs"], u["sl"]] = on.astype(BF16)

    def recurrence(heads, jobs):
        nstages = tt // chunk
        slots = 2 * nstages
        jobs = list(jobs)

        def run_jobs(slot):
            for _ in range(-(-len(jobs) // (slots - slot))):
                jobs.pop(0)()

        states = {}
        for stage in range(nstages):
            blocks = [n * nstages + stage for n in range(nb)]
            units = [(blk, h, prepare(blk, h)) for blk in blocks for h in heads]
            scores = [_dot_nt(u["qt"], u["kt"]) for _, _, u in units]
            run_jobs(2 * stage)
            updated = {}
            for blk, h, u in units:
                key = ((blk * chunk) // tt, h)
                if key not in states:
                    states[key] = st_ref[key[0], h]
                updated[key] = states[key] * u["decay"] + _dot_tn(u["vh"], u["kh"])
            run_jobs(2 * stage + 1)
            for (blk, h, u), sc in zip(units, scores):
                emit_output(u, sc, states[((blk * chunk) // tt, h)])
            states.update(updated)
        for (n, h), st in states.items():
            st_ref[n, h] = st

    def factorised_path():
        heads_per_group = A_HEADS // HEAD_GROUPS
        piece = 2 * LANES
        partial_mix = [[] for _ in range(HEAD_GROUPS)]

        def project_out(g, c0, width):
            partial_mix[g].append(
                _dot(on_ref[:, g * gw:(g + 1) * gw], wout_ref[g * gw:(g + 1) * gw, c0:c0 + width]))

        for g in range(HEAD_GROUPS):
            jobs = []
            if g + 1 < HEAD_GROUPS:
                jobs += [functools.partial(proj, (g + 1) * gw + c0, piece)
                         for proj in (project_query, project_value, project_gate)
                         for c0 in range(0, gw, piece)]
            if g > 0:
                jobs += [functools.partial(project_out, g - 1, c0, piece)
                         for c0 in range(0, D_MODEL, piece)]
            recurrence(range(g * heads_per_group, (g + 1) * heads_per_group), jobs)
        project_out(HEAD_GROUPS - 1, 0, D_MODEL)
        write_rows(functools.reduce(
            jnp.add, [jnp.concatenate(parts, axis=1) for parts in partial_mix]))

    def exact_step(idx, carry):
        n = (idx * EXACT_BLOCK) // tt
        rs = pl.ds(pl.multiple_of(idx * EXACT_BLOCK, EXACT_BLOCK), EXACT_BLOCK)
        row = lax.broadcasted_iota(jnp.int32, (EXACT_BLOCK, A_DK), 0)
        for h in range(A_HEADS):
            sl = slice(h * A_DK, (h + 1) * A_DK)
            b = _cumsum_rows(lf_ref[rs, sl])
            blast = b[EXACT_BLOCK - 1:EXACT_BLOCK]
            qs = qs_ref[rs, sl]
            inp = inp_ref[rs, sl]
            vh = v_ref[rs, sl]
            vf = vh.astype(F32)
            st = st_ref[n, h]
            o = _dot_nt((qs * jnp.exp(b)).astype(BF16), st.astype(BF16))
            for s in range(EXACT_BLOCK):
                decay = jnp.exp(jnp.where(row >= s, b - b[s:s + 1], -jnp.inf))
                score = jnp.sum(qs * decay * inp[s:s + 1], axis=-1, keepdims=True)
                o = o + score * vf[s:s + 1]
            kh = (inp * jnp.exp(blast - b)).astype(BF16)
            st_ref[n, h] = st * jnp.exp(blast) + _dot_tn(vh, kh)
            on = _rms_scale(o) * gnorm_ref[:, sl] * sg_ref[rs, sl]
            on_ref[rs, sl] = on.astype(BF16)
        return carry

    gw = D_MODEL // HEAD_GROUPS
    project_query(0, gw)
    project_value(0, gw)
    project_gate(0, gw)

    max_step = -jnp.min(lf_ref[...])
    bounded = max_step * half <= MAX_FACTOR_EXPONENT

    pl.when(bounded)(factorised_path)

    @pl.when(jnp.logical_not(bounded))
    def _():
        project_query(gw, D_MODEL - gw)
        project_value(gw, D_MODEL - gw)
        project_gate(gw, D_MODEL - gw)
        lax.fori_loop(0, rows // EXACT_BLOCK, exact_step, 0)
        write_rows(_dot(on_ref[...], wout_ref[...]))

    @pl.when(t == pl.num_programs(1) - 1)
    def _():
        for n in range(nb):
            for h in range(A_HEADS):
                st_ref[n, h] = st_ref[n, h].T


def _hgrn_mixer(x, state, npre, win, alb, gnorm, wout, npost, *, nb, tt, chunk, casts=()):
    batch, seq, _ = x.shape
    assert batch % nb == 0 and seq % tt == 0 and tt % chunk == 0 and chunk % (2 * SUBLANES) == 0
    rows = nb * tt
    nt = seq // tt
    nsteps = (batch // nb) * nt
    cast_specs, cast_blocks, cast_views = [], [], []
    for a in casts:
        width = a.shape[-1]
        view = a.reshape(-1, width)
        block_rows = next(r for r in range(2 * SUBLANES, view.shape[0] + 1, 2 * SUBLANES)
                          if view.shape[0] % r == 0 and view.shape[0] // r <= nsteps)
        nblocks = view.shape[0] // block_rows
        cast_views.append(view)
        cast_blocks.append(None if nblocks == nsteps else nblocks)
        cast_specs.append(pl.BlockSpec(
            (block_rows, width),
            lambda b, t, nblocks=nblocks: (jnp.minimum(b * nt + t, nblocks - 1), 0)))
    kern = functools.partial(_hgrn_kernel, nb=nb, tt=tt, chunk=chunk, cast_blocks=tuple(cast_blocks))
    st_spec = pl.BlockSpec((nb, A_HEADS, A_DK, A_DV), lambda b, t: (b, 0, 0, 0))
    return pl.pallas_call(
        kern,
        grid=(batch // nb, nt),
        in_specs=[
            pl.BlockSpec((nb, tt, D_MODEL), lambda b, t: (b, t, 0)),
            st_spec,
            _const_spec((1, D_MODEL)),
            _const_spec((D_MODEL, 4 * D_MODEL)),
            _const_spec(alb.shape),
            _const_spec((1, D_MODEL)),
            _const_spec((D_MODEL, D_MODEL)),
            _const_spec((1, D_MODEL)),
        ] + cast_specs,
        out_specs=[
            pl.BlockSpec((nb, tt, D_MODEL), lambda b, t: (b, t, 0)),
            st_spec,
        ] + cast_specs,
        out_shape=[
            jax.ShapeDtypeStruct(x.shape, F32),
            jax.ShapeDtypeStruct(state.shape, F32),
        ] + [jax.ShapeDtypeStruct(v.shape, BF16) for v in cast_views],
        scratch_shapes=[
            pltpu.VMEM((rows, D_MODEL), F32),
            pltpu.VMEM((rows, D_MODEL), F32),
            pltpu.VMEM((rows, D_MODEL), F32),
            pltpu.VMEM((rows, D_MODEL), BF16),
            pltpu.VMEM((rows, D_MODEL), F32),
            pltpu.VMEM((rows, D_MODEL), BF16),
        ],
        compiler_params=pltpu.CompilerParams(
            dimension_semantics=("arbitrary", "arbitrary"), vmem_limit_bytes=VMEM_LIMIT_BYTES),
        name="hgrn_mixer",
    )(x, state, npre, win, alb, gnorm, wout, npost, *cast_views)


def _rope_cols(x, cos_t, sin_t):
    lane = lax.broadcasted_iota(jnp.int32, (x.shape[0], LANES), 1)
    first_half = (lane % B_HEAD_DIM) < (B_HEAD_DIM // 2)
    cols = []
    for c0 in range(0, x.shape[1], LANES):
        xc = x[:, c0:c0 + LANES]
        partner = jnp.where(first_half,
                            pltpu.roll(xc, LANES - B_HEAD_DIM // 2, axis=1),
                            pltpu.roll(xc, B_HEAD_DIM // 2, axis=1))
        cols.append(xc * cos_t + partner * sin_t)
    return jnp.concatenate(cols, axis=1)


def _rope_cols_mxu(x, cos_t, sin_t):
    src = lax.broadcasted_iota(jnp.int32, (LANES, LANES), 0)
    dst = lax.broadcasted_iota(jnp.int32, (LANES, LANES), 1)
    half = B_HEAD_DIM // 2
    partner_of = jnp.where((dst % B_HEAD_DIM) < half, dst + half, dst - half)
    perm = jnp.where(src == partner_of, 1.0, 0.0).astype(BF16)
    cols = []
    for c0 in range(0, x.shape[1], LANES):
        xc = x[:, c0:c0 + LANES]
        cols.append(xc * cos_t + _dot(xc.astype(BF16), perm) * sin_t)
    return jnp.concatenate(cols, axis=1)


def _attn_cached_kernel(x_ref, kc_in_ref, vc_in_ref, cosq_ref, sinq_ref, cosk_ref, sinkt_ref,
                        sinks_ref, npre_ref, kvn_ref, wq_ref, wkv_ref, wo_ref, npost_ref,
                        y_ref, kc_ref, vc_ref, q_ref, kl_ref, kh_ref, vl_ref, ot_ref, *, nb, tt):
    rows = nb * tt
    ext = WINDOW + tt

    x = x_ref[...].reshape(rows, D_MODEL)
    xs = _rms_scale(x)
    hq = (xs * npre_ref[...]).astype(BF16)
    hk = (xs * kvn_ref[...]).astype(BF16)

    def per_stream(tab_ref):
        tab = tab_ref[...]
        return tab if nb == 1 else jnp.concatenate([tab] * nb, axis=0)

    q = _dot(hq, wq_ref[...])
    q_ref[...] = _rope_cols(q, per_stream(cosq_ref), per_stream(sinq_ref)).astype(BF16)
    kv = _dot(hk, wkv_ref[...])
    k_new = _rope_cols(kv[:, :KV_WIDTH], per_stream(cosk_ref), per_stream(sinkt_ref))
    v_new = kv[:, KV_WIDTH:]

    low = lax.broadcasted_iota(jnp.int32, (ext, LANES), 1) < B_HEAD_DIM
    for n in range(nb):
        k_ext = jnp.concatenate([kc_in_ref[n], k_new[n * tt:(n + 1) * tt]], axis=0)
        v_ext = jnp.concatenate([vc_in_ref[n], v_new[n * tt:(n + 1) * tt]], axis=0)
        kc_ref[n] = k_ext[ext - WINDOW:]
        vc_ref[n] = v_ext[ext - WINDOW:]
        for m in range(KV_WIDTH // LANES):
            ka = k_ext[:, m * LANES:(m + 1) * LANES]
            kr = pltpu.roll(ka, B_HEAD_DIM, axis=1)
            va = v_ext[:, m * LANES:(m + 1) * LANES]
            vr = pltpu.roll(va, B_HEAD_DIM, axis=1)
            kl_ref[2 * m, n] = jnp.where(low, ka, 0.0).astype(BF16)
            kh_ref[2 * m, n] = jnp.where(low, 0.0, kr).astype(BF16)
            kl_ref[2 * m + 1, n] = jnp.where(low, kr, 0.0).astype(BF16)
            kh_ref[2 * m + 1, n] = jnp.where(low, 0.0, ka).astype(BF16)
            vl_ref[2 * m, n] = jnp.where(low, va, 0.0).astype(BF16)
            vl_ref[2 * m + 1, n] = jnp.where(low, vr, 0.0).astype(BF16)

    first_head = lax.broadcasted_iota(jnp.int32, (1, 2 * tt), 1) < tt
    pairs = [(j, kx_ref, ha, hb) for j in range(B_KV_HEADS)
             for kx_ref, ha, hb in ((kl_ref, 4 * j, 4 * j + 2), (kh_ref, 4 * j + 1, 4 * j + 3))]

    def scores(n):
        rs = slice(n * tt, (n + 1) * tt)
        out = []
        for j, kx_ref, _, _ in pairs:
            qq = jnp.concatenate([q_ref[rs, 2 * j * LANES:(2 * j + 1) * LANES],
                                  q_ref[rs, (2 * j + 1) * LANES:(2 * j + 2) * LANES]], axis=0)
            out.append(_dot_nt(kx_ref[j, n], qq))
        return out

    def softmax(s_list):
        out = []
        for (_, _, ha, hb), s in zip(pairs, s_list):
            sink = jnp.where(first_head, sinks_ref[ha], sinks_ref[hb])
            mx = jnp.maximum(jnp.max(s, axis=0, keepdims=True), sink)
            p = jnp.exp(s - mx)
            den = jnp.sum(p, axis=0, keepdims=True) + jnp.exp(sink - mx)
            out.append((p.astype(BF16), den))
        return out

    def weighted_values(n, pd_list):
        cs = slice(n * tt, (n + 1) * tt)
        for (j, _, ha, hb), (p, den) in zip(pairs, pd_list):
            o = _dot_tn(vl_ref[j, n], p)[:B_HEAD_DIM]
            o = (o * (1.0 / den)).astype(BF16)
            ot_ref[ha * B_HEAD_DIM:(ha + 1) * B_HEAD_DIM, cs] = o[:, :tt]
            ot_ref[hb * B_HEAD_DIM:(hb + 1) * B_HEAD_DIM, cs] = o[:, tt:]

    s_next = scores(0)
    for n in range(nb):
        s_cur = s_next
        if n + 1 < nb:
            s_next = scores(n + 1)
        weighted_values(n, softmax(s_cur))

    mix = _dot_tn(ot_ref[...], wo_ref[...])
    y = x + _rms_scale(mix) * npost_ref[...]
    y_ref[...] = y.reshape(nb, tt, D_MODEL)


def _attn_cached_mixer(x, cache_k, cache_v, tables, sinks, npre, kvn, wq, wkv, wo, npost, *, nb):
    batch, tt, _ = x.shape
    assert batch % nb == 0 and tt % (2 * SUBLANES) == 0
    rows = nb * tt
    ext = WINDOW + tt
    kern = functools.partial(_attn_cached_kernel, nb=nb, tt=tt)
    x_spec = pl.BlockSpec((nb, tt, D_MODEL), lambda b: (b, 0, 0))
    c_spec = pl.BlockSpec((nb, WINDOW, KV_WIDTH), lambda b: (b, 0, 0))
    tab_spec = _const_spec((tt, LANES))
    cache_shape = jax.ShapeDtypeStruct((batch, WINDOW, KV_WIDTH), F32)
    return pl.pallas_call(
        kern,
        grid=(batch // nb,),
        in_specs=[x_spec, c_spec, c_spec] + [tab_spec] * 4 + [
            pl.BlockSpec(memory_space=pltpu.SMEM),
            _const_spec((1, D_MODEL)),
            _const_spec((1, D_MODEL)),
            _const_spec((D_MODEL, D_MODEL)),
            _const_spec((D_MODEL, 2 * KV_WIDTH)),
            _const_spec((D_MODEL, D_MODEL)),
            _const_spec((1, D_MODEL)),
        ],
        out_specs=[x_spec, c_spec, c_spec],
        out_shape=[jax.ShapeDtypeStruct(x.shape, F32), cache_shape, cache_shape],
        scratch_shapes=[
            pltpu.VMEM((rows, D_MODEL), BF16),
            pltpu.VMEM((B_KV_HEADS, nb, ext, LANES), BF16),
            pltpu.VMEM((B_KV_HEADS, nb, ext, LANES), BF16),
            pltpu.VMEM((B_KV_HEADS, nb, ext, LANES), BF16),
            pltpu.VMEM((D_MODEL, rows), BF16),
        ],
        compiler_params=pltpu.CompilerParams(
            dimension_semantics=("parallel",), vmem_limit_bytes=VMEM_LIMIT_BYTES),
        name="attn_cached_mixer",
    )(x, cache_k, cache_v, *tables, sinks, npre, kvn, wq, wkv, wo, npost)


def _ffn_steps(src_ref, dst_ref, npre_ref, npost_ref, win_ref, wout_ref):
    state = {}

    def chunk(c0):
        if not state:
            state["h"] = (_rms_scale(src_ref[...]) * npre_ref[...]).astype(BF16)
            state["acc"] = jnp.zeros(src_ref.shape, F32)
        h = state["h"]
        c1 = min(c0 + FFN_CHUNK, D_FF)
        a = _dot(h, win_ref[:, c0:c1])
        b = _dot(h, win_ref[:, D_FF + c0:D_FF + c1])
        g = (a * jax.nn.sigmoid(a) * b).astype(BF16)
        state["acc"] = state["acc"] + _dot(g, wout_ref[c0:c1, :])
        if c1 == D_FF:
            dst_ref[...] = (src_ref[...] + _rms_scale(state["acc"]) * npost_ref[...]).reshape(dst_ref.shape)

    return [functools.partial(chunk, c0) for c0 in range(0, D_FF, FFN_CHUNK)]


def _attn_pair_kernel(x_ref, cosq_ref, sinq_ref, cosk_ref, sinkt_ref, sinks_ref,
                      npre_ref, kvn_ref, wq_ref, wkv_ref, wo_ref, npost_ref,
                      fpre_ref, fpost_ref, win_ref, wout_ref,
                      y_ref, kc_ref, vc_ref, q_ref, kl_ref, kh_ref, vt_ref, ot_ref, mid_ref, *, tt):
    t = pl.program_id(1)
    last = pl.num_programs(1) - 1

    @pl.when((pl.program_id(0) == 0) & (t == 0))
    def _():
        mid_ref[...] = jnp.zeros(mid_ref.shape, F32)

    @pl.when(t < last)
    def _():
        jobs = _ffn_steps(mid_ref, y_ref, fpre_ref, fpost_ref, win_ref, wout_ref)
        _attention_tile(t, tt, x_ref, cosq_ref, sinq_ref, cosk_ref, sinkt_ref, sinks_ref,
                        npre_ref, kvn_ref, wq_ref, wkv_ref, wo_ref, npost_ref, kc_ref, vc_ref,
                        q_ref, kl_ref, kh_ref, vt_ref, ot_ref, jobs, mid_ref)

    @pl.when(t == last)
    def _():
        for job in _ffn_steps(mid_ref, y_ref, fpre_ref, fpost_ref, win_ref, wout_ref):
            job()


def _attention_tile(t, tt, x_ref, cosq_ref, sinq_ref, cosk_ref, sinkt_ref, sinks_ref,
                    npre_ref, kvn_ref, wq_ref, wkv_ref, wo_ref, npost_ref, kc_ref, vc_ref,
                    q_ref, kl_ref, kh_ref, vt_ref, ot_ref, jobs, out_ref):
    ext = WINDOW + tt
    pair = 2 * CHUNK
    nkeys = WINDOW + pair
    nvis = WINDOW + CHUNK
    jobs = list(jobs)
    njobs = len(jobs)

    def run_jobs(share):
        for _ in range(min(round(share * njobs), len(jobs))):
            jobs.pop(0)()

    @pl.when(t == 0)
    def _():
        kc_ref[...] = jnp.zeros(kc_ref.shape, F32)
        vc_ref[...] = jnp.zeros(vc_ref.shape, F32)

    x = x_ref[0]
    xs = _rms_scale(x)
    hq = (xs * npre_ref[...]).astype(BF16)
    hk = (xs * kvn_ref[...]).astype(BF16)

    run_jobs(JOB_SHARES[0])
    q = _dot(hq, wq_ref[...])
    q_ref[...] = _rope_cols_mxu(q, cosq_ref[...], sinq_ref[...]).astype(BF16)
    run_jobs(JOB_SHARES[1])
    kv = _dot(hk, wkv_ref[...])
    run_jobs(JOB_SHARES[2])
    k_ext = jnp.concatenate(
        [kc_ref[0], _rope_cols(kv[:, :KV_WIDTH], cosk_ref[...], sinkt_ref[...])], axis=0)
    v_ext = jnp.concatenate([vc_ref[0], kv[:, KV_WIDTH:]], axis=0)
    kc_ref[0] = k_ext[ext - WINDOW:]
    vc_ref[0] = v_ext[ext - WINDOW:]
    vt_ref[...] = v_ext.T.astype(BF16)

    low = lax.broadcasted_iota(jnp.int32, (ext, LANES), 1) < B_HEAD_DIM
    for m in range(KV_WIDTH // LANES):
        ka = k_ext[:, m * LANES:(m + 1) * LANES]
        kr = pltpu.roll(ka, B_HEAD_DIM, axis=1)
        kl_ref[2 * m] = jnp.where(low, ka, 0.0).astype(BF16)
        kh_ref[2 * m] = jnp.where(low, 0.0, kr).astype(BF16)
        kl_ref[2 * m + 1] = jnp.where(low, kr, 0.0).astype(BF16)
        kh_ref[2 * m + 1] = jnp.where(low, 0.0, ka).astype(BF16)

    vis_row = lax.broadcasted_iota(jnp.int32, (nvis, LANES), 0)
    first_bias = [jnp.where(vis_row + (t * tt - WINDOW + qc * CHUNK) >= 0, 0.0, -jnp.inf)
                  for qc in range(2)]
    first_head = lax.broadcasted_iota(jnp.int32, (1, LANES), 1) < CHUNK
    no_keys = jnp.zeros((CHUNK, LANES), BF16)

    units = [(j, kx_ref, ha, hb) for j in range(B_KV_HEADS)
             for kx_ref, ha, hb in ((kl_ref, 4 * j, 4 * j + 2), (kh_ref, 4 * j + 1, 4 * j + 3))]

    def scores(e):
        r0 = e * pair
        out = []
        for j, kx_ref, _, _ in units:
            ca = slice(2 * j * LANES, (2 * j + 1) * LANES)
            cb = slice((2 * j + 1) * LANES, (2 * j + 2) * LANES)
            qq = jnp.concatenate([q_ref[r0:r0 + CHUNK, ca], q_ref[r0:r0 + CHUNK, cb],
                                  q_ref[r0 + CHUNK:r0 + pair, ca], q_ref[r0 + CHUNK:r0 + pair, cb]], axis=0)
            out.append(_dot_nt(kx_ref[j, r0:r0 + nkeys, :], qq))
        return out

    def softmax(e, s_list):
        out = []
        for (_, _, ha, hb), s in zip(units, s_list):
            sink = jnp.where(first_head, sinks_ref[ha], sinks_ref[hb]) * LOG2_E
            ps, dens = [], []
            for qc in range(2):
                sq = s[qc * CHUNK:qc * CHUNK + nvis, qc * LANES:(qc + 1) * LANES]
                if e == 0:
                    sq = sq + first_bias[qc]
                mx = jnp.maximum(jnp.max(sq, axis=0, keepdims=True), sink)
                p = jnp.exp2(sq - mx)
                dens.append(jnp.sum(p, axis=0, keepdims=True) + jnp.exp2(sink - mx))
                ps.append(p.astype(BF16))
            pmat = jnp.concatenate([jnp.concatenate([ps[0], no_keys], axis=0),
                                    jnp.concatenate([no_keys, ps[1]], axis=0)], axis=1)
            out.append((pmat, jnp.concatenate(dens, axis=1)))
        return out

    def weighted_values(e, pd_list):
        r0 = e * pair
        for (j, _, ha, hb), (pmat, den) in zip(units, pd_list):
            o = _dot(vt_ref[j * B_HEAD_DIM:(j + 1) * B_HEAD_DIM, r0:r0 + nkeys], pmat)
            o = (o * (1.0 / den)).astype(BF16)
            for qc in range(2):
                c0 = r0 + qc * CHUNK
                ot_ref[ha * B_HEAD_DIM:(ha + 1) * B_HEAD_DIM, c0:c0 + CHUNK] = (
                    o[:, qc * LANES:qc * LANES + CHUNK])
                ot_ref[hb * B_HEAD_DIM:(hb + 1) * B_HEAD_DIM, c0:c0 + CHUNK] = (
                    o[:, qc * LANES + CHUNK:(qc + 1) * LANES])

    npairs = tt // pair
    s_next = scores(0)
    for e in range(npairs):
        s_cur = s_next
        if e + 1 < npairs:
            s_next = scores(e + 1)
        run_jobs(JOB_SHARES[3])
        weighted_values(e, softmax(e, s_cur))

    run_jobs(JOB_SHARES[4])
    mix = _dot_tn(ot_ref[...], wo_ref[...])
    run_jobs(1.0)
    out_ref[...] = x + _rms_scale(mix) * npost_ref[...]


def _attn_pair_mixer(x, tables, sinks, npre, kvn, wq, wkv, wo, npost, *, tt, ffn):
    batch, seq, _ = x.shape
    assert seq % tt == 0 and tt % (2 * CHUNK) == 0 and WINDOW == 2 * CHUNK
    ext = WINDOW + tt
    nt = seq // tt
    kern = functools.partial(_attn_pair_kernel, tt=tt)
    in_spec = pl.BlockSpec((1, tt, D_MODEL), lambda b, t: (b, jnp.minimum(t, nt - 1), 0))
    out_spec = pl.BlockSpec((1, tt, D_MODEL), lambda b, t: (b, jnp.maximum(t - 1, 0), 0))
    tab_spec = pl.BlockSpec((tt, LANES), lambda b, t: (jnp.minimum(t, nt - 1), 0))
    c_spec = pl.BlockSpec((1, WINDOW, KV_WIDTH), lambda b, t: (b, 0, 0))
    cache_shape = jax.ShapeDtypeStruct((batch, WINDOW, KV_WIDTH), F32)
    in_specs = [in_spec] + [tab_spec] * 4 + [
        pl.BlockSpec(memory_space=pltpu.SMEM),
        _const_spec((1, D_MODEL)),
        _const_spec((1, D_MODEL)),
        _const_spec((D_MODEL, D_MODEL)),
        _const_spec((D_MODEL, 2 * KV_WIDTH)),
        _const_spec((D_MODEL, D_MODEL)),
        _const_spec((1, D_MODEL)),
    ]
    args = [x, *tables, sinks, npre, kvn, wq, wkv, wo, npost]
    scratch = [
        pltpu.VMEM((tt, D_MODEL), BF16),
        pltpu.VMEM((B_KV_HEADS, ext, LANES), BF16),
        pltpu.VMEM((B_KV_HEADS, ext, LANES), BF16),
        pltpu.VMEM((KV_WIDTH, ext), BF16),
        pltpu.VMEM((D_MODEL, tt), BF16),
    ]
    fpre, fpost, win, wout, layer = ffn

    def layer_spec(shape):
        return pl.BlockSpec((None,) + shape, lambda b, t: (layer, 0, 0),
                            pipeline_mode=pl.Buffered(1))

    in_specs += [_const_spec((1, D_MODEL)), _const_spec((1, D_MODEL)),
                 layer_spec((D_MODEL, 2 * D_FF)), layer_spec((D_FF, D_MODEL))]
    args += [fpre, fpost, win, wout]
    scratch.append(pltpu.VMEM((tt, D_MODEL), F32))
    return pl.pallas_call(
        kern,
        grid=(batch, nt + 1),
        in_specs=in_specs,
        out_specs=[out_spec, c_spec, c_spec],
        out_shape=[jax.ShapeDtypeStruct(x.shape, F32), cache_shape, cache_shape],
        scratch_shapes=scratch,
        compiler_params=pltpu.CompilerParams(
            dimension_semantics=("arbitrary", "arbitrary"), vmem_limit_bytes=VMEM_LIMIT_BYTES),
        name="attn_pair_mixer",
    )(*args)


def _rope_tables(pos, q_scale):
    half = B_HEAD_DIM // 2
    inv = ROPE_THETA ** (-jnp.arange(half, dtype=F32) / half)
    ang = pos.astype(F32)[:, None] * inv[None, :]
    cos = jnp.cos(ang)
    sin = jnp.sin(ang)
    reps = LANES // B_HEAD_DIM
    cos_t = jnp.tile(jnp.concatenate([cos, cos], axis=1), (1, reps))
    sin_t = jnp.tile(jnp.concatenate([-sin, sin], axis=1), (1, reps))
    return cos_t * q_scale, sin_t * q_scale, cos_t, sin_t


def _trunk(x, pos, state, cache_k, cache_v, w, *, nb, tt, hgrn_chunk, cq):
    batch, seq, _ = x.shape
    row = lambda a: a.reshape(1, D_MODEL)
    pending = [k for k in LATER_WEIGHTS if w[k].dtype == F32]
    x, st, *cast_out = _hgrn_mixer(
        x, state, row(w["norm_mix_pre"][0]), w["w_a_in"], w["a_lower_bound"],
        row(w["a_out_norm"]), w["w_a_out"], row(w["norm_mix_post"][0]),
        nb=nb, tt=tt, chunk=hgrn_chunk, casts=[w[k] for k in pending])
    w = dict(w, **{k: c.reshape(w[k].shape) for k, c in zip(pending, cast_out)})
    x = _ffn(x.reshape(batch * seq, D_MODEL), row(w["norm_ffn_pre"][0]), row(w["norm_ffn_post"][0]),
             w["w_ffn_in"], w["w_ffn_out"], 0).reshape(batch, seq, D_MODEL)
    attn_w = (w["b_sinks"], row(w["norm_mix_pre"][1]), row(w["kv_norm"]), w["w_b_q"], w["w_kv"],
              w["w_b_out"], row(w["norm_mix_post"][1]))
    ffn1 = (row(w["norm_ffn_pre"][1]), row(w["norm_ffn_post"][1]), w["w_ffn_in"], w["w_ffn_out"], 1)
    if cache_k is None:
        assert nb == 1 and cq == CHUNK
        x, kc, vc = _attn_pair_mixer(x, _rope_tables(pos, SOFTMAX_SCALE * LOG2_E), *attn_w, tt=tt,
                                     ffn=ffn1)
    else:
        assert tt == seq and cq == seq
        x, kc, vc = _attn_cached_mixer(x, cache_k, cache_v, _rope_tables(pos, SOFTMAX_SCALE),
                                       *attn_w, nb=nb)
        x = _ffn(x.reshape(batch * seq, D_MODEL), *ffn1).reshape(batch, seq, D_MODEL)
    return x, st, kc, vc, w


def kernel(x_prompt, x_sample, state_hgrn, cache_k, cache_v, norm_mix_pre, norm_mix_post, norm_ffn_pre, norm_ffn_post, w_ffn_in, w_ffn_out, w_a_in, a_lower_bound, a_out_norm, w_a_out, kv_norm, w_kv, w_b_q, b_sinks, w_b_out):
    w = dict(
        norm_mix_pre=norm_mix_pre, norm_mix_post=norm_mix_post,
        norm_ffn_pre=norm_ffn_pre, norm_ffn_post=norm_ffn_post,
        w_ffn_in=w_ffn_in, w_ffn_out=w_ffn_out,
        w_a_in=w_a_in[0].astype(BF16), a_lower_bound=a_lower_bound,
        a_out_norm=a_out_norm[0], w_a_out=w_a_out[0].astype(BF16),
        kv_norm=kv_norm, w_kv=w_kv, w_b_q=w_b_q[0], b_sinks=b_sinks[0], w_b_out=w_b_out[0],
    )
    bp, tp, _ = x_prompt.shape
    bs, ts, _ = x_sample.shape

    zero_state = jnp.zeros((bp, A_HEADS, A_DK, A_DV), F32)
    y_p, st_p, kc_p, vc_p, w = _trunk(
        x_prompt, jnp.arange(tp), zero_state, None, None, w,
        nb=1, tt=ROW_TILE, hgrn_chunk=HGRN_CHUNK, cq=CHUNK)

    nb_s = ROW_TILE // (2 * ts)
    y_s, st_s, kc_s, vc_s, _ = _trunk(
        x_sample, PAST_LEN + jnp.arange(ts), state_hgrn[0],
        cache_k.reshape(bs, WINDOW, KV_WIDTH), cache_v.reshape(bs, WINDOW, KV_WIDTH), w,
        nb=nb_s, tt=ts, hgrn_chunk=ts, cq=ts)

    cache4 = lambda a: a.reshape(a.shape[0], WINDOW, B_KV_HEADS, B_HEAD_DIM)
    return (y_p, y_s, st_p[None], st_s[None],
            cache4(kc_p), cache4(vc_p), cache4(kc_s), cache4(vc_s))
```
